```python
import jax, jax.numpy as jnp
from jax import lax
import numpy as np

D_MODEL = 2048
BATCH = 32
SEQ = 256
DEPTH = 1
DEC_BATCH = 2
DEC_SEQ = 1024
PAST_LEN = 512

GRID_W = 64
N_HEADS = 16
N_KV_HEADS = 4
HEAD_DIM = 128
KV_GROUP = N_HEADS // N_KV_HEADS
Q_BLOCK = 128
ROPE_THETA = 10000.0
D_RNN = D_MODEL
RNN_BLOCKS = 16
RNN_BLOCK_DIM = D_RNN // RNN_BLOCKS
CONV_WIDTH = 4
CONV_LEFT = 1
RG_C = 8.0
N_EXPERTS = 32
TOP_K = 4
D_FF_EXPERT = D_MODEL
SWIGLU_LIMIT = 7.0
SWIGLU_ALPHA = 1.702
MOE_ROW_BLOCK = 256
EPS = 1e-6
Q_COLS = N_HEADS * HEAD_DIM
KV_COLS = N_KV_HEADS * HEAD_DIM
IN_COLS = Q_COLS + 2 * KV_COLS + 2 * D_RNN + 2 * D_MODEL
IN_SPLITS = (Q_COLS, Q_COLS + KV_COLS, Q_COLS + 2 * KV_COLS, Q_COLS + 2 * KV_COLS + D_RNN,
             Q_COLS + 2 * KV_COLS + 2 * D_RNN, Q_COLS + 2 * KV_COLS + 2 * D_RNN + D_MODEL)

kernel_name = "hybrid_prefix_diffusion_gqa_rglru_moe_step"


def rms_norm(x, g):
    xf = x.astype(jnp.float32)
    y = xf * lax.rsqrt(jnp.mean(xf * xf, axis=-1, keepdims=True) + EPS)
    return (y * g.astype(jnp.float32)).astype(x.dtype)


def modulate(h, shift, scale):
    return (h * (1 + scale) + shift).astype(h.dtype)


def axial_rope(x):
    n_tok = x.shape[1]
    rows = n_tok // GRID_W
    row = jnp.broadcast_to(jnp.arange(rows, dtype=jnp.float32)[:, None], (rows, GRID_W)).reshape(-1)
    col = jnp.broadcast_to(jnp.arange(GRID_W, dtype=jnp.float32)[None, :], (rows, GRID_W)).reshape(-1)
    half = HEAD_DIM // 2
    nf = half // 2
    inv_freq = ROPE_THETA ** (-jnp.arange(nf, dtype=jnp.float32) / nf)
    xf = x.astype(jnp.float32)

    def rotate(xp, pos):
        ang = pos[:, None] * inv_freq[None, :]
        cos = jnp.cos(ang)[None, :, None, :]
        sin = jnp.sin(ang)[None, :, None, :]
        x1, x2 = xp[..., :nf], xp[..., nf:]
        return jnp.concatenate([x1 * cos - x2 * sin, x2 * cos + x1 * sin], axis=-1)

    out = jnp.concatenate([rotate(xf[..., :half], row), rotate(xf[..., half:], col)], axis=-1)
    return out.astype(x.dtype)


def block_attention(q, k, v):
    B, T = q.shape[0], q.shape[1]
    nb = T // Q_BLOCK
    qb = q.reshape(B, nb, Q_BLOCK, N_KV_HEADS, KV_GROUP, HEAD_DIM).transpose(1, 0, 2, 3, 4, 5)
    scale = HEAD_DIM ** -0.5

    def one_block(q_blk):
        s = jnp.einsum('bqkgd,bskd->bkgqs', q_blk, k).astype(jnp.float32) * scale
        p = jax.nn.softmax(s, axis=-1).astype(v.dtype)
        return jnp.einsum('bkgqs,bskd->bqkgd', p, v)

    o = lax.map(one_block, qb)
    return o.transpose(1, 0, 2, 3, 4, 5).reshape(B, T, Q_COLS)


def centred_dwconv(x, w, b):
    T = x.shape[1]
    xp = jnp.pad(x, ((0, 0), (CONV_LEFT, CONV_WIDTH - 1 - CONV_LEFT), (0, 0)))
    out = b + xp[:, 0:T] * w[0]
    for j in range(1, CONV_WIDTH):
        out = out + xp[:, j:j + T] * w[j]
    return out.astype(x.dtype)


def rglru_coeffs(xc, w_a, b_a, w_x, b_x, lam):
    B, T = xc.shape[0], xc.shape[1]
    xb = xc.reshape(B, T, RNN_BLOCKS, RNN_BLOCK_DIM)
    r = jax.nn.sigmoid(jnp.einsum('btni,nij->btnj', xb, w_a).reshape(B, T, D_RNN) + b_a)
    i = jax.nn.sigmoid(jnp.einsum('btni,nij->btnj', xb, w_x).reshape(B, T, D_RNN) + b_x)
    log_a = (-RG_C * r.astype(jnp.float32)) * jax.nn.softplus(-lam.astype(jnp.float32))
    a = jnp.exp(log_a)
    mult = jnp.sqrt(-jnp.expm1(2.0 * log_a))
    return a, mult * (i * xc).astype(jnp.float32)


def linear_scan(a, b, h0, reverse):
    h0 = h0.astype(jnp.float32)
    if reverse:
        b = b.at[:, -1].add(a[:, -1] * h0)
    else:
        b = b.at[:, 0].add(a[:, 0] * h0)

    def combine(left, right):
        a_l, b_l = left
        a_r, b_r = right
        return a_l * a_r, a_r * b_l + b_r

    _, h = lax.associative_scan(combine, (a, b), axis=1, reverse=reverse)
    return h


def moe_ffn(h, w_router, b_router, w_gate_up, b_gate_up, w_down, b_down):
    T = h.shape[0]
    n_assign = T * TOP_K
    logits = (h @ w_router + b_router).astype(jnp.float32)
    top_val, top_idx = lax.top_k(logits, TOP_K)
    gates = jax.nn.softmax(top_val, axis=-1).astype(h.dtype)
    e_flat = top_idx.reshape(-1).astype(jnp.int32)
    tok_flat = jnp.repeat(jnp.arange(T, dtype=jnp.int32), TOP_K)
    g_flat = gates.reshape(-1)
    order = jnp.argsort(e_flat)
    e_sorted = e_flat[order]
    counts = jnp.zeros((N_EXPERTS,), jnp.int32).at[e_flat].add(1)
    padded = ((counts + MOE_ROW_BLOCK - 1) // MOE_ROW_BLOCK) * MOE_ROW_BLOCK
    pad_end = jnp.cumsum(padded)
    pad_start = pad_end - padded
    sort_start = jnp.cumsum(counts) - counts
    rank = jnp.arange(n_assign, dtype=jnp.int32) - sort_start[e_sorted]
    dest = pad_start[e_sorted] + rank
    n_blocks = -(-n_assign // MOE_ROW_BLOCK) + N_EXPERTS
    n_rows = n_blocks * MOE_ROW_BLOCK
    row_tok = jnp.full((n_rows,), T, jnp.int32).at[dest].set(tok_flat[order])
    row_gate = jnp.zeros((n_rows,), h.dtype).at[dest].set(g_flat[order])
    block_start = jnp.arange(n_blocks, dtype=jnp.int32) * MOE_ROW_BLOCK
    block_expert = jnp.minimum(jnp.searchsorted(pad_end, block_start, side='right'), N_EXPERTS - 1)
    h_pad = jnp.concatenate([h, jnp.zeros((1, h.shape[1]), h.dtype)], axis=0)

    def expert_rows(args):
        tok_b, gate_b, e = args
        xb = h_pad[tok_b]
        gu = xb @ w_gate_up[e] + b_gate_up[e]
        glu = jnp.minimum(gu[:, :D_FF_EXPERT], SWIGLU_LIMIT)
        lin = jnp.clip(gu[:, D_FF_EXPERT:], -SWIGLU_LIMIT, SWIGLU_LIMIT)
        act = glu * jax.nn.sigmoid(SWIGLU_ALPHA * glu) * (lin + 1)
        return (act @ w_down[e] + b_down[e]) * gate_b[:, None]

    y_rows = lax.map(expert_rows, (row_tok.reshape(n_blocks, MOE_ROW_BLOCK),
                                   row_gate.reshape(n_blocks, MOE_ROW_BLOCK), block_expert))
    out = jax.ops.segment_sum(y_rows.reshape(n_rows, -1), row_tok, num_segments=T + 1)
    return out[:T].astype(h.dtype)


def trunk_layer(x, cond, p, k_ctx, v_ctx, h0_fwd, h0_bwd, latent):
    B, T, D = x.shape
    mod = jax.nn.silu(cond) @ p['w_mod'] + p['b_mod']
    sh1, sc1, gt1, sh2, sc2, gt2 = [m[:, None, :] for m in jnp.split(mod, 6, axis=-1)]
    h = modulate(rms_norm(x, p['g_pre_mix']), sh1, sc1)
    z = h @ p['w_in']
    q, k, v, xr, yr, ga, gr = jnp.split(z, IN_SPLITS, axis=-1)
    q = rms_norm(q.reshape(B, T, N_HEADS, HEAD_DIM), p['g_q_norm'])
    k = rms_norm(k.reshape(B, T, N_KV_HEADS, HEAD_DIM), p['g_k_norm'])
    v = v.reshape(B, T, N_KV_HEADS, HEAD_DIM)
    if latent:
        keys = jnp.concatenate([k_ctx.astype(x.dtype), axial_rope(k)], axis=1)
        vals = jnp.concatenate([v_ctx.astype(x.dtype), v], axis=1)
        attn = block_attention(axial_rope(q), keys, vals)
    else:
        attn = block_attention(q, k, v)
    xc = centred_dwconv(xr, p['conv_w'], p['conv_b'])
    a_f, b_f = rglru_coeffs(xc, p['rg_w_a'][0], p['rg_b_a'][0], p['rg_w_x'][0], p['rg_b_x'][0], p['rg_lambda'][0])
    a_b, b_b = rglru_coeffs(xc, p['rg_w_a'][1], p['rg_b_a'][1], p['rg_w_x'][1], p['rg_b_x'][1], p['rg_lambda'][1])
    h_f = linear_scan(a_f, b_f, h0_fwd, reverse=False)
    h_b = linear_scan(a_b, b_b, h0_bwd, reverse=True)
    rnn = (h_f + h_b).astype(x.dtype) * jax.nn.gelu(yr)
    merged = jax.nn.sigmoid(ga) * (attn @ p['w_o_attn']) + jax.nn.sigmoid(gr) * (rnn @ p['w_o_rnn'])
    x = x + gt1 * rms_norm(merged @ p['w_out'], p['g_post_mix'])
    h2 = modulate(rms_norm(x, p['g_pre_ffn']), sh2, sc2)
    ffn = moe_ffn(h2.reshape(B * T, D), p['w_router'], p['b_router'], p['w_gate_up'], p['b_gate_up'],
                  p['w_down'], p['b_down']).reshape(B, T, D)
    x = (x + gt2 * rms_norm(ffn, p['g_post_ffn'])).astype(x.dtype)
    if latent:
        return x
    return x, (k, v, h_f[:, -1].astype(x.dtype), h_b[:, 0].astype(x.dtype))


def setup_inputs(seed: int = 0) -> dict:
    key = jax.random.key(seed)
    keys = iter(jax.random.split(key, 40))

    def nrm(shape, s):
        return jax.random.normal(next(keys), shape, jnp.float32) * s

    def gain(shape):
        return 1.0 + nrm(shape, 0.01)

    a0 = jax.random.uniform(next(keys), (DEPTH, 2, D_RNN), jnp.float32, 0.9, 0.999)
    return {
        'x_prompt': nrm((BATCH, SEQ, D_MODEL), 1.0),
        'x_sample': nrm((DEC_BATCH, DEC_SEQ, D_MODEL), 1.0),
        'cache_k': nrm((DEC_BATCH, DEPTH, PAST_LEN, N_KV_HEADS, HEAD_DIM), 1.0),
        'cache_v': nrm((DEC_BATCH, DEPTH, PAST_LEN, N_KV_HEADS, HEAD_DIM), 1.0),
        'state_rnn_fwd': nrm((DEC_BATCH, DEPTH, D_RNN), 0.5),
        'state_rnn_bwd': nrm((DEC_BATCH, DEPTH, D_RNN), 0.5),
        'c': nrm((DEC_BATCH, D_MODEL), 1.0),
        'c_ctx': nrm((D_MODEL,), 1.0),
        'w_mod': nrm((DEPTH, D_MODEL, 6 * D_MODEL), 0.5 * D_MODEL ** -0.5),
        'b_mod': nrm((DEPTH, 6 * D_MODEL), 0.01),
        'g_pre_mix': gain((DEPTH, D_MODEL)),
        'w_in': nrm((DEPTH, D_MODEL, IN_COLS), D_MODEL ** -0.5),
        'g_q_norm': gain((DEPTH, HEAD_DIM)),
        'g_k_norm': gain((DEPTH, HEAD_DIM)),
        'conv_w': nrm((DEPTH, CONV_WIDTH, D_RNN), CONV_WIDTH ** -0.5),
        'conv_b': nrm((DEPTH, D_RNN), 0.01),
        'rg_w_a': nrm((DEPTH, 2, RNN_BLOCKS, RNN_BLOCK_DIM, RNN_BLOCK_DIM), RNN_BLOCK_DIM ** -0.5),
        'rg_b_a': nrm((DEPTH, 2, D_RNN), 0.01),
        'rg_w_x': nrm((DEPTH, 2, RNN_BLOCKS, RNN_BLOCK_DIM, RNN_BLOCK_DIM), RNN_BLOCK_DIM ** -0.5),
        'rg_b_x': nrm((DEPTH, 2, D_RNN), 0.01),
        'rg_lambda': jnp.log(a0) - jnp.log1p(-a0),
        'w_o_attn': nrm((DEPTH, Q_COLS, D_MODEL), Q_COLS ** -0.5),
        'w_o_rnn': nrm((DEPTH, D_RNN, D_MODEL), D_RNN ** -0.5),
        'w_out': nrm((DEPTH, D_MODEL, D_MODEL), D_MODEL ** -0.5),
        'g_post_mix': gain((DEPTH, D_MODEL)),
        'g_pre_ffn': gain((DEPTH, D_MODEL)),
        'w_router': nrm((DEPTH, D_MODEL, N_EXPERTS), D_MODEL ** -0.5),
        'b_router': nrm((DEPTH, N_EXPERTS), 0.01),
        'w_gate_up': nrm((DEPTH, N_EXPERTS, D_MODEL, 2 * D_FF_EXPERT), D_MODEL ** -0.5),
        'b_gate_up': nrm((DEPTH, N_EXPERTS, 2 * D_FF_EXPERT), 0.01),
        'w_down': nrm((DEPTH, N_EXPERTS, D_FF_EXPERT, D_MODEL), D_FF_EXPERT ** -0.5),
        'b_down': nrm((DEPTH, N_EXPERTS, D_MODEL), 0.01),
        'g_post_ffn': gain((DEPTH, D_MODEL)),
    }


def reference(x_prompt, x_sample, cache_k, cache_v, state_rnn_fwd, state_rnn_bwd, c, c_ctx,
              w_mod, b_mod, g_pre_mix, w_in, g_q_norm, g_k_norm, conv_w, conv_b,
              rg_w_a, rg_b_a, rg_w_x, rg_b_x, rg_lambda, w_o_attn, w_o_rnn, w_out, g_post_mix,
              g_pre_ffn, w_router, b_router, w_gate_up, b_gate_up, w_down, b_down, g_post_ffn):
    y_p = x_prompt
    y_s = x_sample
    zero_state = jnp.zeros((x_prompt.shape[0], D_RNN), x_prompt.dtype)
    ks, vs, hfs, hbs = [], [], [], []
    for l in range(DEPTH):
        p = {
            'w_mod': w_mod[l], 'b_mod': b_mod[l], 'g_pre_mix': g_pre_mix[l], 'w_in': w_in[l],
            'g_q_norm': g_q_norm[l], 'g_k_norm': g_k_norm[l], 'conv_w': conv_w[l], 'conv_b': conv_b[l],
            'rg_w_a': rg_w_a[l], 'rg_b_a': rg_b_a[l], 'rg_w_x': rg_w_x[l], 'rg_b_x': rg_b_x[l],
            'rg_lambda': rg_lambda[l], 'w_o_attn': w_o_attn[l], 'w_o_rnn': w_o_rnn[l], 'w_out': w_out[l],
            'g_post_mix': g_post_mix[l], 'g_pre_ffn': g_pre_ffn[l], 'w_router': w_router[l],
            'b_router': b_router[l], 'w_gate_up': w_gate_up[l], 'b_gate_up': b_gate_up[l],
            'w_down': w_down[l], 'b_down': b_down[l], 'g_post_ffn': g_post_ffn[l],
        }
        y_p, (k_l, v_l, hf_l, hb_l) = trunk_layer(y_p, c_ctx[None, :], p, None, None,
                                                 zero_state, zero_state, latent=False)
        ks.append(k_l)
        vs.append(v_l)
        hfs.append(hf_l)
        hbs.append(hb_l)
        y_s = trunk_layer(y_s, c, p, cache_k[:, l], cache_v[:, l],
                          state_rnn_fwd[:, l], state_rnn_bwd[:, l], latent=True)
    new_cache_k = jnp.stack(ks, axis=1)
    new_cache_v = jnp.stack(vs, axis=1)
    new_state_rnn_fwd = jnp.stack(hfs, axis=1)
    new_state_rnn_bwd = jnp.stack(hbs, axis=1)
    return (y_p, y_s, new_cache_k, new_cache_v, new_state_rnn_fwd, new_state_rnn_bwd)
```

```python
import functools

import jax
import jax.numpy as jnp
import numpy as np
from jax import lax
from jax.experimental import pallas as pl
from jax.experimental.pallas import tpu as pltpu

D_MODEL = 2048
N_CTX_SEQ = 32
CTX_LEN = 256
N_LAT_SEQ = 2
LAT_LEN = 1024
PAST_LEN = 512
N_CTX = N_CTX_SEQ * CTX_LEN
N_LAT = N_LAT_SEQ * LAT_LEN
N_TOK = N_CTX + N_LAT
GRID_W = 64
N_HEADS = 16
N_KV_HEADS = 4
HEAD_DIM = 128
KV_GROUP = N_HEADS // N_KV_HEADS
ROPE_THETA = 10000.0
RNN_BLOCKS = 16
RNN_BLOCK_DIM = 128
RG_C = 8.0
N_EXPERTS = 32
TOP_K = 4
D_FF = 2048
SWIGLU_LIMIT = 7.0
SWIGLU_ALPHA = 1.702
EPS = 1e-6
Q_COLS = N_HEADS * HEAD_DIM
KV_COLS = N_KV_HEADS * HEAD_DIM
IN_COLS = Q_COLS + 2 * KV_COLS + 4 * D_MODEL
COL_K = Q_COLS
COL_XR = Q_COLS + 2 * KV_COLS
COL_YR = COL_XR + D_MODEL
COL_GA = COL_YR + D_MODEL
COL_GR = COL_GA + D_MODEL

V7X_VMEM_BYTES = 64 * 1024 * 1024
VMEM_LIMIT = 56 * 1024 * 1024

ROW_TILE = 256
SUPER_TILES = 8
SUPER_ROWS = ROW_TILE * SUPER_TILES
N_ASSIGN = N_TOK * TOP_K
N_ROWS = N_ASSIGN + N_EXPERTS * ROW_TILE
N_ROW_TILES = N_ROWS // ROW_TILE
N_SUPER = N_ROW_TILES // SUPER_TILES + N_EXPERTS
FF_CHUNK = 512
N_FF_CHUNKS = D_FF // FF_CHUNK

BF16 = jnp.bfloat16
F32 = jnp.float32


def _params(semantics, vmem=VMEM_LIMIT):
    return pltpu.CompilerParams(dimension_semantics=semantics, vmem_limit_bytes=vmem)


def _rms_scale(x):
    return lax.rsqrt(jnp.mean(x * x, axis=-1, keepdims=True) + EPS)


def _sigmoid(x):
    return 1.0 / (1.0 + jnp.exp(-x))


def _mod_body(c_ref, w_ref, b_ref, o_ref):
    c = c_ref[...]
    a = (c * _sigmoid(c)).astype(BF16)
    o_ref[...] = jnp.dot(a, w_ref[...].astype(BF16), preferred_element_type=F32) + b_ref[...]


def modulation(cond8, w_mod, b_mod):
    tn = 1024
    n = w_mod.shape[1]
    return pl.pallas_call(
        _mod_body,
        grid=(n // tn,),
        in_specs=[
            pl.BlockSpec((8, D_MODEL), lambda j: (0, 0)),
            pl.BlockSpec((D_MODEL, tn), lambda j: (0, j)),
            pl.BlockSpec((1, tn), lambda j: (0, j)),
        ],
        out_specs=pl.BlockSpec((8, tn), lambda j: (0, j)),
        out_shape=jax.ShapeDtypeStruct((8, n), F32),
        compiler_params=_params(("arbitrary",)),
        name="modulation",
    )(cond8, w_mod, b_mod.reshape(1, n))


def _prenorm_body(x_ref, g_ref, sh_ref, sc_ref, o_ref):
    x = x_ref[...]
    y = x * _rms_scale(x) * g_ref[...]
    o_ref[...] = (y * (1.0 + sc_ref[...]) + sh_ref[...]).astype(o_ref.dtype)


def prenorm_modulate(x, g, shift, scale, group_of_block, tm):
    m = x.shape[0]
    gmap = lambda i: (group_of_block(i), 0, 0)
    return pl.pallas_call(
        _prenorm_body,
        grid=(m // tm,),
        in_specs=[
            pl.BlockSpec((tm, D_MODEL), lambda i: (i, 0)),
            pl.BlockSpec((1, D_MODEL), lambda i: (0, 0)),
            pl.BlockSpec((None, 1, D_MODEL), gmap),
            pl.BlockSpec((None, 1, D_MODEL), gmap),
        ],
        out_specs=pl.BlockSpec((tm, D_MODEL), lambda i: (i, 0)),
        out_shape=jax.ShapeDtypeStruct((m, D_MODEL), BF16),
        compiler_params=_params(("arbitrary",)),
        name="prenorm_modulate",
    )(x, g.reshape(1, D_MODEL), shift, scale)


def _inproj_body(h_ref, w_ref, o_ref, wbf_ref):
    @pl.when(pl.program_id(1) == 0)
    def _():
        wbf_ref[...] = w_ref[...].astype(BF16)

    o_ref[...] = jnp.dot(h_ref[...], wbf_ref[...], preferred_element_type=F32)


def in_projection(h, w_in):
    m = h.shape[0]
    tm, tn = 1024, 1024
    return pl.pallas_call(
        _inproj_body,
        grid=(IN_COLS // tn, m // tm),
        in_specs=[
            pl.BlockSpec((tm, D_MODEL), lambda j, i: (i, 0)),
            pl.BlockSpec((D_MODEL, tn), lambda j, i: (0, j)),
        ],
        out_specs=pl.BlockSpec((tm, tn), lambda j, i: (i, j)),
        out_shape=jax.ShapeDtypeStruct((m, IN_COLS), F32),
        scratch_shapes=[pltpu.VMEM((D_MODEL, tn), BF16)],
        compiler_params=_params(("arbitrary", "arbitrary")),
        name="in_projection",
    )(h, w_in)


def _rope(x, cos, sin_lo, sin_hi):
    return x * cos + pltpu.roll(x, 96, 1) * sin_lo + pltpu.roll(x, 32, 1) * sin_hi


def _head_norm(x, g):
    return x * _rms_scale(x) * g


def _softmax_pv(score_blocks, value_blocks):
    m = None
    for s in score_blocks:
        mi = jnp.max(s, axis=-1, keepdims=True)
        m = mi if m is None else jnp.maximum(m, mi)
    ps = [jnp.exp(s - m) for s in score_blocks]
    denom = None
    for p in ps:
        li = jnp.sum(p, axis=-1, keepdims=True)
        denom = li if denom is None else denom + li
    inv = 1.0 / denom
    out = None
    for p, v in zip(ps, value_blocks):
        o = jnp.dot((p * inv).astype(BF16), v, preferred_element_type=F32)
        out = o if out is None else out + o
    return out


def _attn_ctx_body(q_ref, kv_ref, gq_ref, gk_ref, o_ref, ko_ref, vo_ref):
    tq = q_ref.shape[0]
    scale = HEAD_DIM ** -0.5
    gq = gq_ref[...]
    gk = gk_ref[...]
    for g in range(N_KV_HEADS):
        kcols = slice(g * HEAD_DIM, (g + 1) * HEAD_DIM)
        kn = _head_norm(kv_ref[:, kcols], gk)
        ko_ref[:, kcols] = kn
        v = kv_ref[:, KV_COLS + g * HEAD_DIM:KV_COLS + (g + 1) * HEAD_DIM]
        vo_ref[:, kcols] = v
        qs = []
        for hh in range(KV_GROUP):
            h = g * KV_GROUP + hh
            qs.append(_head_norm(q_ref[:, h * HEAD_DIM:(h + 1) * HEAD_DIM], gq).astype(BF16))
        q4 = jnp.concatenate(qs, axis=0)
        s = lax.dot_general(q4, kn.astype(BF16), (((1,), (1,)), ((), ())),
                            preferred_element_type=F32) * scale
        o = _softmax_pv([s], [v.astype(BF16)])
        for hh in range(KV_GROUP):
            h = g * KV_GROUP + hh
            o_ref[:, h * HEAD_DIM:(h + 1) * HEAD_DIM] = o[hh * tq:(hh + 1) * tq].astype(o_ref.dtype)


def attention_ctx(z, g_q, g_k):
    nb = N_CTX_SEQ
    t = CTX_LEN
    return pl.pallas_call(
        _attn_ctx_body,
        grid=(nb,),
        in_specs=[
            pl.BlockSpec((t, Q_COLS), lambda b: (b, 0)),
            pl.BlockSpec((t, 2 * KV_COLS), lambda b: (b, COL_K // (2 * KV_COLS))),
            pl.BlockSpec((1, HEAD_DIM), lambda b: (0, 0)),
            pl.BlockSpec((1, HEAD_DIM), lambda b: (0, 0)),
        ],
        out_specs=[
            pl.BlockSpec((t, Q_COLS), lambda b: (b, 0)),
            pl.BlockSpec((t, KV_COLS), lambda b: (b, 0)),
            pl.BlockSpec((t, KV_COLS), lambda b: (b, 0)),
        ],
        out_shape=[
            jax.ShapeDtypeStruct((N_CTX, Q_COLS), BF16),
            jax.ShapeDtypeStruct((N_CTX, KV_COLS), F32),
            jax.ShapeDtypeStruct((N_CTX, KV_COLS), F32),
        ],
        compiler_params=_params(("arbitrary",)),
        name="attention_ctx",
    )(z, z, g_q.reshape(1, HEAD_DIM), g_k.reshape(1, HEAD_DIM))


def _attn_lat_body(q_ref, kv_ref, ck_ref, cv_ref, cos_ref, slo_ref, shi_ref, gq_ref, gk_ref,
                   o_ref, kr_ref):
    tq = q_ref.shape[0]
    qb = pl.program_id(1)
    scale = HEAD_DIM ** -0.5
    gq = gq_ref[...]

    @pl.when(qb == 0)
    def _():
        gk = gk_ref[...]
        for g in range(N_KV_HEADS):
            kcols = slice(g * HEAD_DIM, (g + 1) * HEAD_DIM)
            kn = _head_norm(kv_ref[:, kcols], gk)
            kr_ref[:, kcols] = _rope(kn, cos_ref[...], slo_ref[...], shi_ref[...]).astype(BF16)

    row0 = pl.multiple_of(qb * tq, tq)
    cos = cos_ref[pl.ds(row0, tq), :]
    slo = slo_ref[pl.ds(row0, tq), :]
    shi = shi_ref[pl.ds(row0, tq), :]
    for g in range(N_KV_HEADS):
        kcols = slice(g * HEAD_DIM, (g + 1) * HEAD_DIM)
        qs = []
        for hh in range(KV_GROUP):
            h = g * KV_GROUP + hh
            qn = _head_norm(q_ref[:, h * HEAD_DIM:(h + 1) * HEAD_DIM], gq)
            qs.append(_rope(qn, cos, slo, shi).astype(BF16))
        q4 = jnp.concatenate(qs, axis=0)
        dn = (((1,), (1,)), ((), ()))
        s_past = lax.dot_general(q4, ck_ref[:, kcols].astype(BF16), dn,
                                 preferred_element_type=F32) * scale
        s_new = lax.dot_general(q4, kr_ref[:, kcols], dn, preferred_element_type=F32) * scale
        v_past = cv_ref[:, kcols].astype(BF16)
        v_new = kv_ref[:, KV_COLS + g * HEAD_DIM:KV_COLS + (g + 1) * HEAD_DIM].astype(BF16)
        o = _softmax_pv([s_past, s_new], [v_past, v_new])
        for hh in range(KV_GROUP):
            h = g * KV_GROUP + hh
            o_ref[:, h * HEAD_DIM:(h + 1) * HEAD_DIM] = o[hh * tq:(hh + 1) * tq].astype(o_ref.dtype)


def attention_lat(z, cache_k, cache_v, rope_tabs, g_q, g_k):
    tq = 256
    nq = LAT_LEN // tq
    cos, slo, shi = rope_tabs
    tab = pl.BlockSpec((LAT_LEN, HEAD_DIM), lambda b, q: (0, 0))
    return pl.pallas_call(
        _attn_lat_body,
        grid=(N_LAT_SEQ, nq),
        in_specs=[
            pl.BlockSpec((tq, Q_COLS), lambda b, q: (b * nq + q, 0)),
            pl.BlockSpec((LAT_LEN, 2 * KV_COLS), lambda b, q: (b, COL_K // (2 * KV_COLS))),
            pl.BlockSpec((None, PAST_LEN, KV_COLS), lambda b, q: (b, 0, 0)),
            pl.BlockSpec((None, PAST_LEN, KV_COLS), lambda b, q: (b, 0, 0)),
            tab, tab, tab,
            pl.BlockSpec((1, HEAD_DIM), lambda b, q: (0, 0)),
            pl.BlockSpec((1, HEAD_DIM), lambda b, q: (0, 0)),
        ],
        out_specs=pl.BlockSpec((tq, Q_COLS), lambda b, q: (b * nq + q, 0)),
        out_shape=jax.ShapeDtypeStruct((N_LAT, Q_COLS), BF16),
        scratch_shapes=[pltpu.VMEM((LAT_LEN, KV_COLS), BF16)],
        compiler_params=_params(("arbitrary", "arbitrary")),
        name="attention_lat",
    )(z, z, cache_k, cache_v, cos, slo, shi, g_q.reshape(1, HEAD_DIM), g_k.reshape(1, HEAD_DIM))


def _rope_tables():
    t = np.arange(LAT_LEN)
    row = jnp.asarray(t // GRID_W, F32)
    col = jnp.asarray(t % GRID_W, F32)
    nf = HEAD_DIM // 4
    inv_freq = ROPE_THETA ** (-jnp.arange(nf, dtype=F32) / nf)
    ang_row = row[:, None] * inv_freq[None, :]
    ang_col = col[:, None] * inv_freq[None, :]
    ang = jnp.concatenate([ang_row, ang_row, ang_col, ang_col], axis=1)
    cos = jnp.cos(ang)
    sin = jnp.sin(ang)
    first = jnp.asarray((np.arange(HEAD_DIM) % (2 * nf)) < nf)[None, :]
    return cos, jnp.where(first, -sin, 0.0), jnp.where(first, 0.0, sin)


RNN_ROWS = 2048
RNN_COLS = 512
RNN_SUB = RNN_COLS // RNN_BLOCK_DIM


def _gelu_tanh(y):
    return 0.5 * y * (1.0 + jnp.tanh(0.7978845608028654 * (y + 0.044715 * (y * y * y))))


def _rglru_body(seq_len, xr_ref, yr_ref, cw_ref, cb_ref, wg_ref, bg_ref, lam_ref, h0f_ref, h0b_ref,
                o_ref, hf_ref, hb_ref, af_ref, bf_ref, ab_ref, bb_ref):
    n_seq = RNN_ROWS // seq_len
    x = xr_ref[...]
    t_in_seq = lax.broadcasted_iota(jnp.int32, (RNN_ROWS, 1), 0) & (seq_len - 1)
    zero = jnp.zeros_like(x)
    x_m1 = jnp.where(t_in_seq >= 1, pltpu.roll(x, 1, 0), zero)
    x_p1 = jnp.where(t_in_seq < seq_len - 1, pltpu.roll(x, RNN_ROWS - 1, 0), zero)
    x_p2 = jnp.where(t_in_seq < seq_len - 2, pltpu.roll(x, RNN_ROWS - 2, 0), zero)
    xc = (cb_ref[...] + x_m1 * cw_ref[0:1, :] + x * cw_ref[1:2, :]
          + x_p1 * cw_ref[2:3, :] + x_p2 * cw_ref[3:4, :])
    lam = lam_ref[...]
    softplus_neg = jnp.maximum(-lam, 0.0) + jnp.log(1.0 + jnp.exp(-jnp.abs(lam)))
    for n in range(RNN_SUB):
        cols = slice(n * RNN_BLOCK_DIM, (n + 1) * RNN_BLOCK_DIM)
        xn = xc[:, cols]
        pre = jnp.dot(xn.astype(BF16), wg_ref[n], preferred_element_type=F32) + bg_ref[n]
        for d, (a_ref, b_ref) in enumerate(((af_ref, bf_ref), (ab_ref, bb_ref))):
            r = _sigmoid(pre[:, (2 * d) * RNN_BLOCK_DIM:(2 * d + 1) * RNN_BLOCK_DIM])
            gate_in = _sigmoid(pre[:, (2 * d + 1) * RNN_BLOCK_DIM:(2 * d + 2) * RNN_BLOCK_DIM])
            log_a = (-RG_C * r) * softplus_neg[d:d + 1, cols]
            a = jnp.exp(log_a)
            a_ref[n] = a
            b_ref[n] = jnp.sqrt(1.0 - a * a) * (gate_in * xn)

    def step(t, carry):
        rows_f = pl.ds(t, n_seq, stride=seq_len)
        rows_b = pl.ds(seq_len - 1 - t, n_seq, stride=seq_len)
        out = []
        for n in range(RNN_SUB):
            hf = af_ref[n, rows_f, :] * carry[2 * n] + bf_ref[n, rows_f, :]
            hb = ab_ref[n, rows_b, :] * carry[2 * n + 1] + bb_ref[n, rows_b, :]
            bf_ref[n, rows_f, :] = hf
            bb_ref[n, rows_b, :] = hb
            out += [hf, hb]
        return tuple(out)

    init = []
    for n in range(RNN_SUB):
        cols = slice(n * RNN_BLOCK_DIM, (n + 1) * RNN_BLOCK_DIM)
        init += [h0f_ref[:, 0, cols], h0b_ref[:, 0, cols]]
    last = lax.fori_loop(0, seq_len, step, tuple(init), unroll=8)
    for n in range(RNN_SUB):
        cols = slice(n * RNN_BLOCK_DIM, (n + 1) * RNN_BLOCK_DIM)
        hf_ref[:, 0, cols] = last[2 * n]
        hb_ref[:, 0, cols] = last[2 * n + 1]
        o_ref[:, cols] = ((bf_ref[n] + bb_ref[n]) * _gelu_tanh(yr_ref[:, cols])).astype(o_ref.dtype)


def rglru_mixer(z, seq_len, conv_w, conv_b, w_gates, b_gates, lam, h0_f, h0_b):
    m = z.shape[0]
    n_seq_total = m // seq_len
    n_seq = RNN_ROWS // seq_len
    cblk = lambda base: (lambda r, c: (r, base // RNN_COLS + c))
    state_spec = pl.BlockSpec((n_seq, 1, RNN_COLS), lambda r, c: (r, 0, c))
    return pl.pallas_call(
        functools.partial(_rglru_body, seq_len),
        grid=(m // RNN_ROWS, D_MODEL // RNN_COLS),
        in_specs=[
            pl.BlockSpec((RNN_ROWS, RNN_COLS), cblk(COL_XR)),
            pl.BlockSpec((RNN_ROWS, RNN_COLS), cblk(COL_YR)),
            pl.BlockSpec((4, RNN_COLS), lambda r, c: (0, c)),
            pl.BlockSpec((1, RNN_COLS), lambda r, c: (0, c)),
            pl.BlockSpec((RNN_SUB, RNN_BLOCK_DIM, 4 * RNN_BLOCK_DIM), lambda r, c: (c, 0, 0)),
            pl.BlockSpec((RNN_SUB, 1, 4 * RNN_BLOCK_DIM), lambda r, c: (c, 0, 0)),
            pl.BlockSpec((2, RNN_COLS), lambda r, c: (0, c)),
            state_spec, state_spec,
        ],
        out_specs=[
            pl.BlockSpec((RNN_ROWS, RNN_COLS), lambda r, c: (r, c)),
            state_spec, state_spec,
        ],
        out_shape=[
            jax.ShapeDtypeStruct((m, D_MODEL), BF16),
            jax.ShapeDtypeStruct((n_seq_total, 1, D_MODEL), F32),
            jax.ShapeDtypeStruct((n_seq_total, 1, D_MODEL), F32),
        ],
        scratch_shapes=[pltpu.VMEM((RNN_SUB, RNN_ROWS, RNN_BLOCK_DIM), F32) for _ in range(4)],
        compiler_params=_params(("arbitrary", "arbitrary")),
        name="rglru_mixer_t%d" % seq_len,
    )(z, z, conv_w, conv_b.reshape(1, D_MODEL), w_gates, b_gates, lam, h0_f, h0_b)


def _merge_body(a_ref, r_ref, wa_ref, wr_ref, ga_ref, gr_ref, o_ref, wa_bf, wr_bf):
    @pl.when(pl.program_id(1) == 0)
    def _():
        wa_bf[...] = wa_ref[...].astype(BF16)
        wr_bf[...] = wr_ref[...].astype(BF16)

    pa = jnp.dot(a_ref[...], wa_bf[...], preferred_element_type=F32)
    pr = jnp.dot(r_ref[...], wr_bf[...], preferred_element_type=F32)
    o_ref[...] = (_sigmoid(ga_ref[...]) * pa + _sigmoid(gr_ref[...]) * pr).astype(o_ref.dtype)


def gated_merge(attn, rnn, z, w_o_attn, w_o_rnn):
    m = attn.shape[0]
    tm, tn = 512, 512
    return pl.pallas_call(
        _merge_body,
        grid=(D_MODEL // tn, m // tm),
        in_specs=[
            pl.BlockSpec((tm, Q_COLS), lambda j, i: (i, 0)),
            pl.BlockSpec((tm, D_MODEL), lambda j, i: (i, 0)),
            pl.BlockSpec((Q_COLS, tn), lambda j, i: (0, j)),
            pl.BlockSpec((D_MODEL, tn), lambda j, i: (0, j)),
            pl.BlockSpec((tm, tn), lambda j, i: (i, COL_GA // tn + j)),
            pl.BlockSpec((tm, tn), lambda j, i: (i, COL_GR // tn + j)),
        ],
        out_specs=pl.BlockSpec((tm, tn), lambda j, i: (i, j)),
        out_shape=jax.ShapeDtypeStruct((m, D_MODEL), BF16),
        scratch_shapes=[pltpu.VMEM((Q_COLS, tn), BF16), pltpu.VMEM((D_MODEL, tn), BF16)],
        compiler_params=_params(("arbitrary", "arbitrary")),
        name="gated_merge",
    )(attn, rnn, w_o_attn, w_o_rnn, z, z)


POST_TM = 512
POST_CTX_BLOCKS = N_CTX // POST_TM
LAT_BLOCKS_PER_SEQ = LAT_LEN // POST_TM


def _post_group(i):
    return jnp.where(i < POST_CTX_BLOCKS, 0, 1 + (i - POST_CTX_BLOCKS) // LAT_BLOCKS_PER_SEQ)


def _postmix_body(mc_ref, ml_ref, xc_ref, xl_ref, wo_ref, gpm_ref, gt1_ref, gpf_ref, sh2_ref, sc2_ref,
                  wr_ref, br_ref, x1_ref, h2_ref, e_ref, gate_ref, rank_ref, cnt_ref, carry_ref):
    i = pl.program_id(0)
    tm = POST_TM

    @pl.when(i == 0)
    def _():
        carry_ref[...] = jnp.zeros_like(carry_ref)

    is_ctx = i < POST_CTX_BLOCKS
    merged = jnp.where(is_ctx, mc_ref[...], ml_ref[...])
    x = jnp.where(is_ctx, xc_ref[...], xl_ref[...])
    o = jnp.dot(merged, wo_ref[...], preferred_element_type=F32)
    x1 = x + gt1_ref[...] * (o * _rms_scale(o) * gpm_ref[...])
    x1_ref[...] = x1
    h2 = (x1 * _rms_scale(x1) * gpf_ref[...]) * (1.0 + sc2_ref[...]) + sh2_ref[...]
    h2_ref[...] = h2

    logits = jnp.dot(h2.astype(BF16), wr_ref[...].astype(BF16), preferred_element_type=F32) + br_ref[...]
    lane = lax.broadcasted_iota(jnp.int32, (tm, N_EXPERTS), 1)
    work = logits
    chosen = jnp.zeros((tm, N_EXPERTS), F32)
    sels, vals, idxs = [], [], []
    for _ in range(TOP_K):
        mx = jnp.max(work, axis=-1, keepdims=True)
        idx = jnp.min(jnp.where(work == mx, lane, N_EXPERTS), axis=-1, keepdims=True)
        sel = lane == idx
        work = jnp.where(sel, -jnp.inf, work)
        chosen = jnp.where(sel, 1.0, chosen)
        sels.append(sel)
        vals.append(mx)
        idxs.append(idx)
    exps = [jnp.exp(v - vals[0]) for v in vals]
    inv = 1.0 / (exps[0] + exps[1] + exps[2] + exps[3])

    r_io = lax.broadcasted_iota(jnp.int32, (tm, tm), 0)
    c_io = lax.broadcasted_iota(jnp.int32, (tm, tm), 1)
    lower = jnp.where(c_io < r_io, 1.0, 0.0).astype(BF16)
    before = jnp.dot(lower, chosen.astype(BF16), preferred_element_type=F32) + carry_ref[...]
    carry_ref[...] = carry_ref[...] + jnp.sum(chosen, axis=0, keepdims=True)
    cnt_ref[...] = carry_ref[...]

    lane_k = lax.broadcasted_iota(jnp.int32, (tm, TOP_K), 1)
    e_out = jnp.zeros((tm, TOP_K), jnp.int32)
    g_out = jnp.zeros((tm, TOP_K), F32)
    r_out = jnp.zeros((tm, TOP_K), jnp.int32)
    for k in range(TOP_K):
        rk = jnp.sum(jnp.where(sels[k], before, 0.0), axis=-1, keepdims=True).astype(jnp.int32)
        e_out = jnp.where(lane_k == k, idxs[k], e_out)
        g_out = jnp.where(lane_k == k, exps[k] * inv, g_out)
        r_out = jnp.where(lane_k == k, rk, r_out)
    e_ref[...] = e_out
    gate_ref[...] = g_out
    rank_ref[...] = r_out


def post_mix_router(merged_ctx, merged_lat, x_ctx, x_lat, w_out_bf, g_post_mix, gt1, g_pre_ffn, sh2, sc2,
                    w_router, b_router):
    tm = POST_TM
    ctx_map = lambda i: (jnp.minimum(i, POST_CTX_BLOCKS - 1), 0)
    lat_map = lambda i: (jnp.maximum(i - POST_CTX_BLOCKS, 0), 0)
    gmap = lambda i: (_post_group(i), 0, 0)
    row = lambda i: (i, 0)
    const = lambda i: (0, 0)
    vec = pl.BlockSpec((1, D_MODEL), const)
    gvec = pl.BlockSpec((None, 1, D_MODEL), gmap)
    return pl.pallas_call(
        _postmix_body,
        grid=(N_TOK // tm,),
        in_specs=[
            pl.BlockSpec((tm, D_MODEL), ctx_map),
            pl.BlockSpec((tm, D_MODEL), lat_map),
            pl.BlockSpec((tm, D_MODEL), ctx_map),
            pl.BlockSpec((tm, D_MODEL), lat_map),
            pl.BlockSpec((D_MODEL, D_MODEL), const),
            vec, gvec, vec, gvec, gvec,
            pl.BlockSpec((D_MODEL, N_EXPERTS), const),
            pl.BlockSpec((1, N_EXPERTS), const),
        ],
        out_specs=[
            pl.BlockSpec((tm, D_MODEL), row),
            pl.BlockSpec((tm, D_MODEL), row),
            pl.BlockSpec((tm, TOP_K), row),
            pl.BlockSpec((tm, TOP_K), row),
            pl.BlockSpec((tm, TOP_K), row),
            pl.BlockSpec((1, N_EXPERTS), const),
        ],
        out_shape=[
            jax.ShapeDtypeStruct((N_TOK, D_MODEL), F32),
            jax.ShapeDtypeStruct((N_TOK, D_MODEL), F32),
            jax.ShapeDtypeStruct((N_TOK, TOP_K), jnp.int32),
            jax.ShapeDtypeStruct((N_TOK, TOP_K), F32),
            jax.ShapeDtypeStruct((N_TOK, TOP_K), jnp.int32),
            jax.ShapeDtypeStruct((1, N_EXPERTS), F32),
        ],
        scratch_shapes=[pltpu.VMEM((1, N_EXPERTS), F32)],
        compiler_params=_params(("arbitrary",)),
        name="post_mix_router",
    )(merged_ctx, merged_lat, x_ctx, x_lat, w_out_bf, g_post_mix.reshape(1, D_MODEL), gt1,
      g_pre_ffn.reshape(1, D_MODEL), sh2, sc2, w_router, b_router.reshape(1, N_EXPERTS))


def _row_gather_start(src_ref, dst_ref, idx_ref, n, sem):
    def issue(r, _):
        t = idx_ref[0, 0, r]
        pltpu.make_async_copy(src_ref.at[pl.ds(t, 1), :], dst_ref.at[pl.ds(r, 1), :], sem).start()
        return 0

    lax.fori_loop(0, n, issue, 0, unroll=8)


def _dispatch_body(tok_ref, tok_next_ref, h_ref, o_ref, stage_ref, sem_ref):
    r = pl.program_id(0)
    n = N_ROW_TILES
    slot = r % 2

    @pl.when(r == 0)
    def _():
        _row_gather_start(h_ref, stage_ref.at[0], tok_ref, ROW_TILE, sem_ref.at[0])

    @pl.when(r + 1 < n)
    def _():
        _row_gather_start(h_ref, stage_ref.at[1 - slot], tok_next_ref, ROW_TILE, sem_ref.at[1 - slot])

    pltpu.make_async_copy(h_ref.at[pl.ds(0, ROW_TILE), :], stage_ref.at[slot], sem_ref.at[slot]).wait()
    o_ref[...] = stage_ref[slot].astype(o_ref.dtype)


def dispatch_rows(h2, tok_sorted):
    tok3 = tok_sorted.reshape(N_ROW_TILES, 1, ROW_TILE)
    smem_blk = lambda f: pl.BlockSpec((1, 1, ROW_TILE), f, memory_space=pltpu.SMEM)
    return pl.pallas_call(
        _dispatch_body,
        grid=(N_ROW_TILES,),
        in_specs=[
            smem_blk(lambda r: (r, 0, 0)),
            smem_blk(lambda r: (jnp.minimum(r + 1, N_ROW_TILES - 1), 0, 0)),
            pl.BlockSpec(memory_space=pl.ANY),
        ],
        out_specs=pl.BlockSpec((ROW_TILE, D_MODEL), lambda r: (r, 0)),
        out_shape=jax.ShapeDtypeStruct((N_ROWS, D_MODEL), BF16),
        scratch_shapes=[pltpu.VMEM((2, ROW_TILE, D_MODEL), F32), pltpu.SemaphoreType.DMA((2,))],
        compiler_params=_params(("arbitrary",)),
        name="dispatch_rows",
    )(tok3, tok3, h2)


def _x_tile_copy(x_ref, xbuf_ref, sem, row_start, i):
    src = x_ref.at[pl.ds(pl.multiple_of(row_start + i * ROW_TILE, ROW_TILE), ROW_TILE), :]
    return pltpu.make_async_copy(src, xbuf_ref.at[pl.ds(i * ROW_TILE, ROW_TILE), :], sem)


def _moe_body(exp_ref, row_ref, nsub_ref, nzero_ref, x_ref, wg_ref, wl_ref, wd_ref, bg_ref, bl_ref, bd_ref, y_ref,
              xbuf_ref, act_ref, wg_bf, wl_bf, wd_bf, stage_ref, xsem, ysem):
    s = pl.program_id(0)
    j = pl.program_id(1)
    n_sub = nsub_ref[s]
    row_start = row_ref[s]

    def load_x(step):
        def issue(i, _):
            _x_tile_copy(x_ref, xbuf_ref, xsem, row_ref[step], i).start()
            return 0
        lax.fori_loop(0, nsub_ref[step], issue, 0)

    def wait_x(step):
        def wait(i, _):
            _x_tile_copy(x_ref, xbuf_ref, xsem, row_ref[step], i).wait()
            return 0
        lax.fori_loop(0, nsub_ref[step], wait, 0)

    @pl.when(jnp.logical_and(s == 0, j == 0))
    def _():
        load_x(0)

    @pl.when(j == 0)
    def _():
        wait_x(s)

    @pl.when(jnp.logical_and(j < N_FF_CHUNKS, n_sub > 0))
    def _():
        wg_bf[...] = wg_ref[...].astype(BF16)
        wl_bf[...] = wl_ref[...].astype(BF16)
        bg = bg_ref[...]
        bl = bl_ref[...]

        def sub(i, _):
            rows = pl.ds(pl.multiple_of(i * ROW_TILE, ROW_TILE), ROW_TILE)
            xt = xbuf_ref[rows, :]
            glu = jnp.minimum(jnp.dot(xt, wg_bf[...], preferred_element_type=F32) + bg, SWIGLU_LIMIT)
            lin = jnp.clip(jnp.dot(xt, wl_bf[...], preferred_element_type=F32) + bl,
                           -SWIGLU_LIMIT, SWIGLU_LIMIT)
            act = glu * _sigmoid(SWIGLU_ALPHA * glu) * (lin + 1.0)
            act_ref[j, rows, :] = act.astype(BF16)
            return 0

        lax.fori_loop(0, n_sub, sub, 0)

    @pl.when(jnp.logical_and(j == N_FF_CHUNKS, s + 1 < N_SUPER))
    def _():
        load_x(s + 1)

    for cc in range(N_FF_CHUNKS):
        @pl.when(jnp.logical_and(j == N_FF_CHUNKS + cc, n_sub > 0))
        def _(cc=cc):
            wd_bf[...] = wd_ref[...].astype(BF16)
            bd = bd_ref[...]

            def out_copy(i, slot):
                dst = y_ref.at[pl.ds(pl.multiple_of(row_start + i * ROW_TILE, ROW_TILE), ROW_TILE),
                               cc * FF_CHUNK:(cc + 1) * FF_CHUNK]
                return pltpu.make_async_copy(stage_ref.at[slot], dst, ysem.at[slot])

            def sub(i, _):
                rows = pl.ds(pl.multiple_of(i * ROW_TILE, ROW_TILE), ROW_TILE)
                slot = i % 2
                acc = bd
                for c in range(N_FF_CHUNKS):
                    acc = acc + jnp.dot(act_ref[c, rows, :], wd_bf[c * FF_CHUNK:(c + 1) * FF_CHUNK, :],
                                        preferred_element_type=F32)

                @pl.when(i >= 2)
                def _():
                    out_copy(i - 2, slot).wait()

                stage_ref[slot] = acc
                out_copy(i, slot).start()
                return 0

            lax.fori_loop(0, n_sub, sub, 0)

            @pl.when(n_sub >= 2)
            def _():
                out_copy(n_sub - 2, n_sub % 2).wait()

            out_copy(n_sub - 1, (n_sub - 1) % 2).wait()

    n_zero = nzero_ref[s]

    @pl.when(jnp.logical_and(j == 0, n_zero > 0))
    def _():
        stage_ref[0] = jnp.zeros((ROW_TILE, FF_CHUNK), F32)

        def zero_copy(i, cc):
            dst = y_ref.at[pl.ds(pl.multiple_of(row_start + i * ROW_TILE, ROW_TILE), ROW_TILE),
                           cc * FF_CHUNK:(cc + 1) * FF_CHUNK]
            return pltpu.make_async_copy(stage_ref.at[0], dst, ysem.at[0])

        def issue(i, _):
            for cc in range(N_FF_CHUNKS):
                zero_copy(i, cc).start()
            return 0

        def drain(i, _):
            for cc in range(N_FF_CHUNKS):
                zero_copy(i, cc).wait()
            return 0

        lax.fori_loop(0, n_zero, issue, 0)
        lax.fori_loop(0, n_zero, drain, 0)


def expert_mlp(x_sorted, sched, w_gate_up, b_gate_up, w_down, b_down):
    exp_of, row_of, nsub_of, nzero_of = sched
    last = N_FF_CHUNKS - 1
    up_of = lambda s, j, n: jnp.where(n[s] > 0, jnp.minimum(j, last), last)
    down_of = lambda s, j, n: jnp.where(n[s] > 0, jnp.maximum(j - N_FF_CHUNKS, 0), last)
    up_chunk = lambda s, j, e, r, n, z: (e[s], 0, up_of(s, j, n))
    lin_chunk = lambda s, j, e, r, n, z: (e[s], 0, N_FF_CHUNKS + up_of(s, j, n))
    down_chunk = lambda s, j, e, r, n, z: (e[s], 0, down_of(s, j, n))
    grid_spec = pltpu.PrefetchScalarGridSpec(
        num_scalar_prefetch=4,
        grid=(N_SUPER, 2 * N_FF_CHUNKS),
        in_specs=[
            pl.BlockSpec(memory_space=pl.ANY),
            pl.BlockSpec((None, D_MODEL, FF_CHUNK), up_chunk),
            pl.BlockSpec((None, D_MODEL, FF_CHUNK), lin_chunk),
            pl.BlockSpec((None, D_FF, FF_CHUNK), down_chunk),
            pl.BlockSpec((None, 1, FF_CHUNK), up_chunk),
            pl.BlockSpec((None, 1, FF_CHUNK), lin_chunk),
            pl.BlockSpec((None, 1, FF_CHUNK), down_chunk),
        ],
        out_specs=pl.BlockSpec(memory_space=pl.ANY),
        scratch_shapes=[
            pltpu.VMEM((SUPER_ROWS, D_MODEL), BF16),
            pltpu.VMEM((N_FF_CHUNKS, SUPER_ROWS, FF_CHUNK), BF16),
            pltpu.VMEM((D_MODEL, FF_CHUNK), BF16),
            pltpu.VMEM((D_MODEL, FF_CHUNK), BF16),
            pltpu.VMEM((D_FF, FF_CHUNK), BF16),
            pltpu.VMEM((2, ROW_TILE, FF_CHUNK), F32),
            pltpu.SemaphoreType.DMA(()),
            pltpu.SemaphoreType.DMA((2,)),
        ],
    )
    return pl.pallas_call(
        _moe_body,
        grid_spec=grid_spec,
        out_shape=jax.ShapeDtypeStruct((N_ROWS, D_MODEL), F32),
        compiler_params=_params(("arbitrary", "arbitrary"), vmem=58 * 1024 * 1024),
        name="expert_mlp",
    )(exp_of, row_of, nsub_of, nzero_of, x_sorted, w_gate_up, w_gate_up, w_down,
      b_gate_up.reshape(N_EXPERTS, 1, 2 * D_FF), b_gate_up.reshape(N_EXPERTS, 1, 2 * D_FF),
      b_down.reshape(N_EXPERTS, 1, D_MODEL))


COMB_TB = 256


def _combine_start(y_ref, ybuf_ref, pos_ref, sem):
    def issue(t, _):
        for k in range(TOP_K):
            p = pos_ref[0, 0, t * TOP_K + k]
            pltpu.make_async_copy(y_ref.at[pl.ds(p, 1), :], ybuf_ref.at[k, pl.ds(t, 1), :], sem).start()
        return 0

    lax.fori_loop(0, COMB_TB, issue, 0, unroll=2)


def _combine_body(n, pos_ref, pos_next_ref, y_ref, gate_ref, x1_ref, gt2_ref, g_ref, o_ref, ybuf_ref, sem_ref):
    i = pl.program_id(0)
    slot = i % 2

    @pl.when(i == 0)
    def _():
        _combine_start(y_ref, ybuf_ref.at[0], pos_ref, sem_ref.at[0])

    @pl.when(i + 1 < n)
    def _():
        _combine_start(y_ref, ybuf_ref.at[1 - slot], pos_next_ref, sem_ref.at[1 - slot])

    for k in range(TOP_K):
        pltpu.make_async_copy(y_ref.at[pl.ds(0, COMB_TB), :], ybuf_ref.at[slot, k], sem_ref.at[slot]).wait()
    gates = gate_ref[...]
    ffn = gates[:, 0:1] * ybuf_ref[slot, 0]
    for k in range(1, TOP_K):
        ffn = ffn + gates[:, k:k + 1] * ybuf_ref[slot, k]
    o_ref[...] = x1_ref[...] + gt2_ref[...] * (ffn * _rms_scale(ffn) * g_ref[...])


def combine_residual(y_sorted, pos, gates, x1, gt2, g_post_ffn, row_offset, n_rows, group_of_block):
    tb = COMB_TB
    nblk = n_rows // tb
    off = row_offset // tb
    pos3 = pos.reshape(N_TOK // tb, 1, tb * TOP_K)
    smem_blk = lambda f: pl.BlockSpec((1, 1, tb * TOP_K), f, memory_space=pltpu.SMEM)
    return pl.pallas_call(
        functools.partial(_combine_body, nblk),
        grid=(nblk,),
        in_specs=[
            smem_blk(lambda i: (off + i, 0, 0)),
            smem_blk(lambda i: (off + jnp.minimum(i + 1, nblk - 1), 0, 0)),
            pl.BlockSpec(memory_space=pl.ANY),
            pl.BlockSpec((tb, TOP_K), lambda i: (off + i, 0)),
            pl.BlockSpec((tb, D_MODEL), lambda i: (off + i, 0)),
            pl.BlockSpec((None, 1, D_MODEL), lambda i: (group_of_block(i), 0, 0)),
            pl.BlockSpec((1, D_MODEL), lambda i: (0, 0)),
        ],
        out_specs=pl.BlockSpec((tb, D_MODEL), lambda i: (i, 0)),
        out_shape=jax.ShapeDtypeStruct((n_rows, D_MODEL), F32),
        scratch_shapes=[pltpu.VMEM((2, TOP_K, tb, D_MODEL), F32), pltpu.SemaphoreType.DMA((2,))],
        compiler_params=_params(("arbitrary",)),
        name="combine_residual",
    )(pos3, pos3, y_sorted, gates, x1, gt2, g_post_ffn.reshape(1, D_MODEL))


def _routing_tables(e_idx, rank, counts_f):
    counts = counts_f.reshape(N_EXPERTS).astype(jnp.int32)
    n_tiles = (counts + ROW_TILE - 1) // ROW_TILE
    padded = n_tiles * ROW_TILE
    pad_end = jnp.cumsum(padded)
    pad_start = pad_end - padded
    pos = pad_start[e_idx] + rank
    tok_ids = jnp.repeat(jnp.arange(N_TOK, dtype=jnp.int32), TOP_K)
    tok_sorted = jnp.zeros((N_ROWS,), jnp.int32).at[pos.reshape(-1)].set(
        tok_ids, unique_indices=True, mode="drop")
    n_pass = (n_tiles + SUPER_TILES - 1) // SUPER_TILES
    pass_end = jnp.cumsum(n_pass)
    total = pass_end[-1]
    s = jnp.arange(N_SUPER, dtype=jnp.int32)
    s_eff = jnp.minimum(s, total - 1)
    e_of = jnp.minimum(jnp.searchsorted(pass_end, s_eff, side="right"), N_EXPERTS - 1).astype(jnp.int32)
    local = s_eff - (pass_end[e_of] - n_pass[e_of])
    row_of = pad_start[e_of] + local * SUPER_ROWS
    nsub = jnp.minimum(SUPER_TILES, n_tiles[e_of] - local * SUPER_TILES)
    nsub = jnp.where(s < total, nsub, 0).astype(jnp.int32)
    zero_row = pad_end[-1] + (s - total) * SUPER_ROWS
    nzero = jnp.clip((N_ROWS - zero_row) // ROW_TILE, 0, SUPER_TILES)
    nzero = jnp.where(s >= total, nzero, 0).astype(jnp.int32)
    row_of = jnp.where(s < total, row_of, jnp.minimum(zero_row, N_ROWS - ROW_TILE)).astype(jnp.int32)
    return pos.astype(jnp.int32), tok_sorted, (e_of, row_of, nsub, nzero)


def kernel(x_prompt, x_sample, cache_k, cache_v, state_rnn_fwd, state_rnn_bwd, c, c_ctx, w_mod, b_mod, g_pre_mix, w_in, g_q_norm, g_k_norm, conv_w, conv_b, rg_w_a, rg_b_a, rg_w_x, rg_b_x, rg_lambda, w_o_attn, w_o_rnn, w_out, g_post_mix, g_pre_ffn, w_router, b_router, w_gate_up, b_gate_up, w_down, b_down, g_post_ffn):
    l = 0
    x_ctx = x_prompt.reshape(N_CTX, D_MODEL)
    x_lat = x_sample.reshape(N_LAT, D_MODEL)

    cond8 = jnp.concatenate([c_ctx[None, :], c, jnp.zeros((8 - 1 - N_LAT_SEQ, D_MODEL), F32)], axis=0)
    mod = modulation(cond8, w_mod[l], b_mod[l])[:1 + N_LAT_SEQ].reshape(1 + N_LAT_SEQ, 6, 1, D_MODEL)
    sh1, sc1, gt1, sh2, sc2, gt2 = [mod[:, i] for i in range(6)]

    ctx_group = lambda i: 0
    lat_group_1024 = lambda i: 1 + i
    h_ctx = prenorm_modulate(x_ctx, g_pre_mix[l], sh1, sc1, ctx_group, 1024)
    h_lat = prenorm_modulate(x_lat, g_pre_mix[l], sh1, sc1, lat_group_1024, 1024)
    z_ctx = in_projection(h_ctx, w_in[l])
    z_lat = in_projection(h_lat, w_in[l])

    attn_ctx, k_new, v_new = attention_ctx(z_ctx, g_q_norm[l], g_k_norm[l])
    attn_lat = attention_lat(z_lat, cache_k[:, l].reshape(N_LAT_SEQ, PAST_LEN, KV_COLS),
                             cache_v[:, l].reshape(N_LAT_SEQ, PAST_LEN, KV_COLS),
                             _rope_tables(), g_q_norm[l], g_k_norm[l])

    def per_block(w):
        return w.reshape(2, RNN_BLOCKS, 1, RNN_BLOCK_DIM)

    w_gates = jnp.concatenate([rg_w_a[l, 0], rg_w_x[l, 0], rg_w_a[l, 1], rg_w_x[l, 1]], axis=-1).astype(BF16)
    ba, bx = per_block(rg_b_a[l]), per_block(rg_b_x[l])
    b_gates = jnp.concatenate([ba[0], bx[0], ba[1], bx[1]], axis=-1)
    zeros_state = jnp.zeros((N_CTX_SEQ, 1, D_MODEL), F32)
    rnn_ctx, hf_ctx, hb_ctx = rglru_mixer(z_ctx, CTX_LEN, conv_w[l], conv_b[l], w_gates, b_gates,
                                          rg_lambda[l], zeros_state, zeros_state)
    rnn_lat, _, _ = rglru_mixer(z_lat, LAT_LEN, conv_w[l], conv_b[l], w_gates, b_gates, rg_lambda[l],
                                state_rnn_fwd[:, l].reshape(N_LAT_SEQ, 1, D_MODEL),
                                state_rnn_bwd[:, l].reshape(N_LAT_SEQ, 1, D_MODEL))

    merged_ctx = gated_merge(attn_ctx, rnn_ctx, z_ctx, w_o_attn[l], w_o_rnn[l])
    merged_lat = gated_merge(attn_lat, rnn_lat, z_lat, w_o_attn[l], w_o_rnn[l])

    x1, h2, e_idx, gates, rank, counts = post_mix_router(
        merged_ctx, merged_lat, x_ctx, x_lat, w_out[l].astype(BF16), g_post_mix[l], gt1, g_pre_ffn[l],
        sh2, sc2, w_router[l], b_router[l])

    pos, tok_sorted, sched = _routing_tables(e_idx, rank, counts)
    x_sorted = dispatch_rows(h2, tok_sorted)
    y_sorted = expert_mlp(x_sorted, sched, w_gate_up[l], b_gate_up[l], w_down[l], b_down[l])

    y_ctx = combine_residual(y_sorted, pos, gates, x1, gt2, g_post_ffn[l], 0, N_CTX, ctx_group)
    y_lat = combine_residual(y_sorted, pos, gates, x1, gt2, g_post_ffn[l], N_CTX, N_LAT,
                             lambda i: 1 + i // (LAT_LEN // COMB_TB))

    return (y_ctx.reshape(N_CTX_SEQ, CTX_LEN, D_MODEL),
            y_lat.reshape(N_LAT_SEQ, LAT_LEN, D_MODEL),
            k_new.reshape(N_CTX_SEQ, 1, CTX_LEN, N_KV_HEADS, HEAD_DIM),
            v_new.reshape(N_CTX_SEQ, 1, CTX_LEN, N_KV_HEADS, HEAD_DIM),
            hf_ctx,
            hb_ctx)
```

```python
import functools

import jax
import jax.numpy as jnp
import numpy as np
from jax import lax
from jax.experimental import pallas as pl
from jax.experimental.pallas import tpu as pltpu

D_MODEL = 2048
N_CTX_SEQ = 32
CTX_LEN = 256
N_LAT_SEQ = 2
LAT_LEN = 1024
PAST_LEN = 512
N_CTX = N_CTX_SEQ * CTX_LEN
N_LAT = N_LAT_SEQ * LAT_LEN
N_TOK = N_CTX + N_LAT
GRID_W = 64
N_HEADS = 16
N_KV_HEADS = 4
HEAD_DIM = 128
KV_GROUP = N_HEADS // N_KV_HEADS
ROPE_THETA = 10000.0
RNN_BLOCKS = 16
RNN_BLOCK_DIM = 128
RG_C = 8.0
N_EXPERTS = 32
TOP_K = 4
D_FF = 2048
SWIGLU_LIMIT = 7.0
SWIGLU_ALPHA = 1.702
EPS = 1e-6
Q_COLS = N_HEADS * HEAD_DIM
KV_COLS = N_KV_HEADS * HEAD_DIM
IN_COLS = Q_COLS + 2 * KV_COLS + 4 * D_MODEL
COL_K = Q_COLS
COL_XR = Q_COLS + 2 * KV_COLS
COL_YR = COL_XR + D_MODEL
COL_GA = COL_YR + D_MODEL
COL_GR = COL_GA + D_MODEL

V7X_VMEM_BYTES = 64 * 1024 * 1024
VMEM_LIMIT = 56 * 1024 * 1024

ROW_TILE = 256
SUPER_TILES = 8
SUPER_ROWS = ROW_TILE * SUPER_TILES
N_ASSIGN = N_TOK * TOP_K
N_ROWS = N_ASSIGN + N_EXPERTS * ROW_TILE
N_ROW_TILES = N_ROWS // ROW_TILE
N_SUPER = N_ROW_TILES // SUPER_TILES + N_EXPERTS
FF_CHUNK = 512
N_FF_CHUNKS = D_FF // FF_CHUNK

BF16 = jnp.bfloat16
F32 = jnp.float32


def _params(semantics, vmem=VMEM_LIMIT):
    return pltpu.CompilerParams(dimension_semantics=semantics, vmem_limit_bytes=vmem)


def _rms_scale(x):
    return lax.rsqrt(jnp.mean(x * x, axis=-1, keepdims=True) + EPS)


def _sigmoid(x):
    return 1.0 / (1.0 + jnp.exp(-x))


def _mod_body(c_ref, w_ref, b_ref, o_ref):
    c = c_ref[...]
    a = (c * _sigmoid(c)).astype(BF16)
    o_ref[...] = jnp.dot(a, w_ref[...].astype(BF16), preferred_element_type=F32) + b_ref[...]


def modulation(cond8, w_mod, b_mod):
    tn = 1024
    n = w_mod.shape[1]
    return pl.pallas_call(
        _mod_body,
        grid=(n // tn,),
        in_specs=[
            pl.BlockSpec((8, D_MODEL), lambda j: (0, 0)),
            pl.BlockSpec((D_MODEL, tn), lambda j: (0, j)),
            pl.BlockSpec((1, tn), lambda j: (0, j)),
        ],
        out_specs=pl.BlockSpec((8, tn), lambda j: (0, j)),
        out_shape=jax.ShapeDtypeStruct((8, n), F32),
        compiler_params=_params(("arbitrary",)),
        name="modulation",
    )(cond8, w_mod, b_mod.reshape(1, n))


def _prenorm_body(x_ref, g_ref, sh_ref, sc_ref, o_ref):
    x = x_ref[...]
    y = x * _rms_scale(x) * g_ref[...]
    o_ref[...] = (y * (1.0 + sc_ref[...]) + sh_ref[...]).astype(o_ref.dtype)


def prenorm_modulate(x, g, shift, scale, group_of_block, tm):
    m = x.shape[0]
    gmap = lambda i: (group_of_block(i), 0, 0)
    return pl.pallas_call(
        _prenorm_body,
        grid=(m // tm,),
        in_specs=[
            pl.BlockSpec((tm, D_MODEL), lambda i: (i, 0)),
            pl.BlockSpec((1, D_MODEL), lambda i: (0, 0)),
            pl.BlockSpec((None, 1, D_MODEL), gmap),
            pl.BlockSpec((None, 1, D_MODEL), gmap),
        ],
        out_specs=pl.BlockSpec((tm, D_MODEL), lambda i: (i, 0)),
        out_shape=jax.ShapeDtypeStruct((m, D_MODEL), BF16),
        compiler_params=_params(("arbitrary",)),
        name="prenorm_modulate",
    )(x, g.reshape(1, D_MODEL), shift, scale)


def _inproj_body(h_ref, w_ref, o_ref, wbf_ref):
    @pl.when(pl.program_id(1) == 0)
    def _():
        wbf_ref[...] = w_ref[...].astype(BF16)

    o_ref[...] = jnp.dot(h_ref[...], wbf_ref[...], preferred_element_type=F32)


def in_projection(h, w_in):
    m = h.shape[0]
    tm, tn = 1024, 1024
    return pl.pallas_call(
        _inproj_body,
        grid=(IN_COLS // tn, m // tm),
        in_specs=[
            pl.BlockSpec((tm, D_MODEL), lambda j, i: (i, 0)),
            pl.BlockSpec((D_MODEL, tn), lambda j, i: (0, j)),
        ],
        out_specs=pl.BlockSpec((tm, tn), lambda j, i: (i, j)),
        out_shape=jax.ShapeDtypeStruct((m, IN_COLS), F32),
        scratch_shapes=[pltpu.VMEM((D_MODEL, tn), BF16)],
        compiler_params=_params(("arbitrary", "arbitrary")),
        name="in_projection",
    )(h, w_in)


def _rope(x, cos, sin_lo, sin_hi):
    return x * cos + pltpu.roll(x, 96, 1) * sin_lo + pltpu.roll(x, 32, 1) * sin_hi


def _head_norm(x, g):
    return x * _rms_scale(x) * g


def _softmax_pv(score_blocks, value_blocks):
    m = None
    for s in score_blocks:
        mi = jnp.max(s, axis=-1, keepdims=True)
        m = mi if m is None else jnp.maximum(m, mi)
    ps = [jnp.exp(s - m) for s in score_blocks]
    denom = None
    for p in ps:
        li = jnp.sum(p, axis=-1, keepdims=True)
        denom = li if denom is None else denom + li
    inv = 1.0 / denom
    out = None
    for p, v in zip(ps, value_blocks):
        o = jnp.dot((p * inv).astype(BF16), v, preferred_element_type=F32)
        out = o if out is None else out + o
    return out


def _attn_ctx_body(q_ref, kv_ref, gq_ref, gk_ref, o_ref, ko_ref, vo_ref):
    tq = q_ref.shape[0]
    scale = HEAD_DIM ** -0.5
    gq = gq_ref[...]
    gk = gk_ref[...]
    for g in range(N_KV_HEADS):
        kcols = slice(g * HEAD_DIM, (g + 1) * HEAD_DIM)
        kn = _head_norm(kv_ref[:, kcols], gk)
        v = kv_ref[:, KV_COLS + g * HEAD_DIM:KV_COLS + (g + 1) * HEAD_DIM]
        ko_ref[pl.ds(g, tq, stride=N_KV_HEADS), :] = kn
        vo_ref[pl.ds(g, tq, stride=N_KV_HEADS), :] = v
        qs = []
        for hh in range(KV_GROUP):
            h = g * KV_GROUP + hh
            qs.append(_head_norm(q_ref[:, h * HEAD_DIM:(h + 1) * HEAD_DIM], gq).astype(BF16))
        q4 = jnp.concatenate(qs, axis=0)
        s = lax.dot_general(q4, kn.astype(BF16), (((1,), (1,)), ((), ())),
                            preferred_element_type=F32) * scale
        o = _softmax_pv([s], [v.astype(BF16)])
        for hh in range(KV_GROUP):
            h = g * KV_GROUP + hh
            o_ref[:, h * HEAD_DIM:(h + 1) * HEAD_DIM] = o[hh * tq:(hh + 1) * tq].astype(o_ref.dtype)


def attention_ctx(z, g_q, g_k):
    nb = N_CTX_SEQ
    t = CTX_LEN
    return pl.pallas_call(
        _attn_ctx_body,
        grid=(nb,),
        in_specs=[
            pl.BlockSpec((t, Q_COLS), lambda b: (b, 0)),
            pl.BlockSpec((t, 2 * KV_COLS), lambda b: (b, COL_K // (2 * KV_COLS))),
            pl.BlockSpec((1, HEAD_DIM), lambda b: (0, 0)),
            pl.BlockSpec((1, HEAD_DIM), lambda b: (0, 0)),
        ],
        out_specs=[
            pl.BlockSpec((t, Q_COLS), lambda b: (b, 0)),
            pl.BlockSpec((t * N_KV_HEADS, HEAD_DIM), lambda b: (b, 0)),
            pl.BlockSpec((t * N_KV_HEADS, HEAD_DIM), lambda b: (b, 0)),
        ],
        out_shape=[
            jax.ShapeDtypeStruct((N_CTX, Q_COLS), BF16),
            jax.ShapeDtypeStruct((N_CTX * N_KV_HEADS, HEAD_DIM), F32),
            jax.ShapeDtypeStruct((N_CTX * N_KV_HEADS, HEAD_DIM), F32),
        ],
        compiler_params=_params(("arbitrary",)),
        name="attention_ctx",
    )(z, z, g_q.reshape(1, HEAD_DIM), g_k.reshape(1, HEAD_DIM))


def _attn_lat_body(q_ref, kv_ref, ck_ref, cv_ref, cos_ref, slo_ref, shi_ref, gq_ref, gk_ref,
                   o_ref, kr_ref):
    tq = q_ref.shape[0]
    qb = pl.program_id(1)
    scale = HEAD_DIM ** -0.5
    gq = gq_ref[...]

    @pl.when(qb == 0)
    def _():
        gk = gk_ref[...]
        for g in range(N_KV_HEADS):
            kcols = slice(g * HEAD_DIM, (g + 1) * HEAD_DIM)
            kn = _head_norm(kv_ref[:, kcols], gk)
            kr_ref[:, kcols] = _rope(kn, cos_ref[...], slo_ref[...], shi_ref[...]).astype(BF16)

    row0 = pl.multiple_of(qb * tq, tq)
    cos = cos_ref[pl.ds(row0, tq), :]
    slo = slo_ref[pl.ds(row0, tq), :]
    shi = shi_ref[pl.ds(row0, tq), :]
    for g in range(N_KV_HEADS):
        kcols = slice(g * HEAD_DIM, (g + 1) * HEAD_DIM)
        qs = []
        for hh in range(KV_GROUP):
            h = g * KV_GROUP + hh
            qn = _head_norm(q_ref[:, h * HEAD_DIM:(h + 1) * HEAD_DIM], gq)
            qs.append(_rope(qn, cos, slo, shi).astype(BF16))
        q4 = jnp.concatenate(qs, axis=0)
        dn = (((1,), (1,)), ((), ()))
        s_past = lax.dot_general(q4, ck_ref[:, kcols].astype(BF16), dn,
                                 preferred_element_type=F32) * scale
        s_new = lax.dot_general(q4, kr_ref[:, kcols], dn, preferred_element_type=F32) * scale
        v_past = cv_ref[:, kcols].astype(BF16)
        v_new = kv_ref[:, KV_COLS + g * HEAD_DIM:KV_COLS + (g + 1) * HEAD_DIM].astype(BF16)
        o = _softmax_pv([s_past, s_new], [v_past, v_new])
        for hh in range(KV_GROUP):
            h = g * KV_GROUP + hh
            o_ref[:, h * HEAD_DIM:(h + 1) * HEAD_DIM] = o[hh * tq:(hh + 1) * tq].astype(o_ref.dtype)


def attention_lat(z, cache_k, cache_v, rope_tabs, g_q, g_k):
    tq = 256
    nq = LAT_LEN // tq
    cos, slo, shi = rope_tabs
    tab = pl.BlockSpec((LAT_LEN, HEAD_DIM), lambda b, q: (0, 0))
    return pl.pallas_call(
        _attn_lat_body,
        grid=(N_LAT_SEQ, nq),
        in_specs=[
            pl.BlockSpec((tq, Q_COLS), lambda b, q: (b * nq + q, 0)),
            pl.BlockSpec((LAT_LEN, 2 * KV_COLS), lambda b, q: (b, COL_K // (2 * KV_COLS))),
            pl.BlockSpec((None, PAST_LEN, KV_COLS), lambda b, q: (b, 0, 0)),
            pl.BlockSpec((None, PAST_LEN, KV_COLS), lambda b, q: (b, 0, 0)),
            tab, tab, tab,
            pl.BlockSpec((1, HEAD_DIM), lambda b, q: (0, 0)),
            pl.BlockSpec((1, HEAD_DIM), lambda b, q: (0, 0)),
        ],
        out_specs=pl.BlockSpec((tq, Q_COLS), lambda b, q: (b * nq + q, 0)),
        out_shape=jax.ShapeDtypeStruct((N_LAT, Q_COLS), BF16),
        scratch_shapes=[pltpu.VMEM((LAT_LEN, KV_COLS), BF16)],
        compiler_params=_params(("arbitrary", "arbitrary")),
        name="attention_lat",
    )(z, z, cache_k, cache_v, cos, slo, shi, g_q.reshape(1, HEAD_DIM), g_k.reshape(1, HEAD_DIM))


def _rope_tables():
    t = np.arange(LAT_LEN)
    row = jnp.asarray(t // GRID_W, F32)
    col = jnp.asarray(t % GRID_W, F32)
    nf = HEAD_DIM // 4
    inv_freq = ROPE_THETA ** (-jnp.arange(nf, dtype=F32) / nf)
    ang_row = row[:, None] * inv_freq[None, :]
    ang_col = col[:, None] * inv_freq[None, :]
    ang = jnp.concatenate([ang_row, ang_row, ang_col, ang_col], axis=1)
    cos = jnp.cos(ang)
    sin = jnp.sin(ang)
    first = jnp.asarray((np.arange(HEAD_DIM) % (2 * nf)) < nf)[None, :]
    return cos, jnp.where(first, -sin, 0.0), jnp.where(first, 0.0, sin)


RNN_ROWS = 2048
RNN_COLS = 512
RNN_SUB = RNN_COLS // RNN_BLOCK_DIM


def _gelu_tanh(y):
    return 0.5 * y * (1.0 + jnp.tanh(0.7978845608028654 * (y + 0.044715 * (y * y * y))))


def _rglru_body(seq_len, xr_ref, yr_ref, cw_ref, cb_ref, wg_ref, bg_ref, lam_ref, h0f_ref, h0b_ref,
                o_ref, hf_ref, hb_ref, xs_ref, af_ref, bf_ref, ab_ref, bb_ref):
    n_seq = RNN_ROWS // seq_len
    for n in range(RNN_SUB):
        cols = slice(n * RNN_BLOCK_DIM, (n + 1) * RNN_BLOCK_DIM)
        for s in range(n_seq):
            xs_ref[n, pl.ds(s, seq_len, stride=n_seq), :] = xr_ref[s * seq_len:(s + 1) * seq_len, cols]

    row = lax.broadcasted_iota(jnp.int32, (RNN_ROWS, 1), 0)
    lam = lam_ref[...]
    softplus_neg = jnp.maximum(-lam, 0.0) + jnp.log(1.0 + jnp.exp(-jnp.abs(lam)))
    rate = softplus_neg * (-RG_C * 1.4426950408889634)
    for n in range(RNN_SUB):
        cols = slice(n * RNN_BLOCK_DIM, (n + 1) * RNN_BLOCK_DIM)
        x = xs_ref[n]
        x_m1 = jnp.where(row >= n_seq, pltpu.roll(x, n_seq, 0), 0.0)
        x_p1 = jnp.where(row < RNN_ROWS - n_seq, pltpu.roll(x, RNN_ROWS - n_seq, 0), 0.0)
        x_p2 = jnp.where(row < RNN_ROWS - 2 * n_seq, pltpu.roll(x, RNN_ROWS - 2 * n_seq, 0), 0.0)
        xn = (cb_ref[:, cols] + x_m1 * cw_ref[0:1, cols] + x * cw_ref[1:2, cols]
              + x_p1 * cw_ref[2:3, cols] + x_p2 * cw_ref[3:4, cols])
        pre = jnp.dot(xn.astype(BF16), wg_ref[n], preferred_element_type=F32) + bg_ref[n]
        for d, (a_ref, b_ref) in enumerate(((af_ref, bf_ref), (ab_ref, bb_ref))):
            r = 0.5 * jnp.tanh(0.5 * pre[:, (2 * d) * RNN_BLOCK_DIM:(2 * d + 1) * RNN_BLOCK_DIM]) + 0.5
            gate_in = 0.5 * jnp.tanh(
                0.5 * pre[:, (2 * d + 1) * RNN_BLOCK_DIM:(2 * d + 2) * RNN_BLOCK_DIM]) + 0.5
            a = jnp.exp2(r * rate[d:d + 1, cols])
            v = 1.0 - a * a
            a_ref[n] = a
            b_ref[n] = (v * lax.rsqrt(jnp.maximum(v, 1e-30))) * (gate_in * xn)

    def step(t, carry):
        rows_f = pl.ds(pl.multiple_of(t * n_seq, n_seq), n_seq)
        rows_b = pl.ds(pl.multiple_of((seq_len - 1 - t) * n_seq, n_seq), n_seq)
        out = []
        for n in range(RNN_SUB):
            hf = af_ref[n, rows_f, :] * carry[2 * n] + bf_ref[n, rows_f, :]
            hb = ab_ref[n, rows_b, :] * carry[2 * n + 1] + bb_ref[n, rows_b, :]
            bf_ref[n, rows_f, :] = hf
            bb_ref[n, rows_b, :] = hb
            out += [hf, hb]
        return tuple(out)

    init = []
    for n in range(RNN_SUB):
        cols = slice(n * RNN_BLOCK_DIM, (n + 1) * RNN_BLOCK_DIM)
        init += [h0f_ref[:, 0, cols], h0b_ref[:, 0, cols]]
    last = lax.fori_loop(0, seq_len, step, tuple(init), unroll=8)
    for n in range(RNN_SUB):
        cols = slice(n * RNN_BLOCK_DIM, (n + 1) * RNN_BLOCK_DIM)
        hf_ref[:, 0, cols] = last[2 * n]
        hb_ref[:, 0, cols] = last[2 * n + 1]
        bf_ref[n] = bf_ref[n] + bb_ref[n]
        for s in range(n_seq):
            rows = slice(s * seq_len, (s + 1) * seq_len)
            h_sum = bf_ref[n, pl.ds(s, seq_len, stride=n_seq), :]
            o_ref[rows, cols] = (h_sum * _gelu_tanh(yr_ref[rows, cols])).astype(o_ref.dtype)


def rglru_mixer(z, seq_len, conv_w, conv_b, w_gates, b_gates, lam, h0_f, h0_b):
    m = z.shape[0]
    n_seq_total = m // seq_len
    n_seq = RNN_ROWS // seq_len
    cblk = lambda base: (lambda r, c: (r, base // RNN_COLS + c))
    state_spec = pl.BlockSpec((n_seq, 1, RNN_COLS), lambda r, c: (r, 0, c))
    return pl.pallas_call(
        functools.partial(_rglru_body, seq_len),
        grid=(m // RNN_ROWS, D_MODEL // RNN_COLS),
        in_specs=[
            pl.BlockSpec((RNN_ROWS, RNN_COLS), cblk(COL_XR)),
            pl.BlockSpec((RNN_ROWS, RNN_COLS), cblk(COL_YR)),
            pl.BlockSpec((4, RNN_COLS), lambda r, c: (0, c)),
            pl.BlockSpec((1, RNN_COLS), lambda r, c: (0, c)),
            pl.BlockSpec((RNN_SUB, RNN_BLOCK_DIM, 4 * RNN_BLOCK_DIM), lambda r, c: (c, 0, 0)),
            pl.BlockSpec((RNN_SUB, 1, 4 * RNN_BLOCK_DIM), lambda r, c: (c, 0, 0)),
            pl.BlockSpec((2, RNN_COLS), lambda r, c: (0, c)),
            state_spec, state_spec,
        ],
        out_specs=[
            pl.BlockSpec((RNN_ROWS, RNN_COLS), lambda r, c: (r, c)),
            state_spec, state_spec,
        ],
        out_shape=[
            jax.ShapeDtypeStruct((m, D_MODEL), BF16),
            jax.ShapeDtypeStruct((n_seq_total, 1, D_MODEL), F32),
            jax.ShapeDtypeStruct((n_seq_total, 1, D_MODEL), F32),
        ],
        scratch_shapes=[pltpu.VMEM((RNN_SUB, RNN_ROWS, RNN_BLOCK_DIM), F32) for _ in range(5)],
        compiler_params=_params(("arbitrary", "arbitrary")),
        name="rglru_mixer_t%d" % seq_len,
    )(z, z, conv_w, conv_b.reshape(1, D_MODEL), w_gates, b_gates, lam, h0_f, h0_b)


def _merge_body(a_ref, r_ref, wa_ref, wr_ref, ga_ref, gr_ref, o_ref, wa_bf, wr_bf):
    @pl.when(pl.program_id(1) == 0)
    def _():
        wa_bf[...] = wa_ref[...].astype(BF16)
        wr_bf[...] = wr_ref[...].astype(BF16)

    pa = jnp.dot(a_ref[...], wa_bf[...], preferred_element_type=F32)
    pr = jnp.dot(r_ref[...], wr_bf[...], preferred_element_type=F32)
    o_ref[...] = (_sigmoid(ga_ref[...]) * pa + _sigmoid(gr_ref[...]) * pr).astype(o_ref.dtype)


def gated_merge(attn, rnn, z, w_o_attn, w_o_rnn):
    m = attn.shape[0]
    tm, tn = 512, 512
    return pl.pallas_call(
        _merge_body,
        grid=(D_MODEL // tn, m // tm),
        in_specs=[
            pl.BlockSpec((tm, Q_COLS), lambda j, i: (i, 0)),
            pl.BlockSpec((tm, D_MODEL), lambda j, i: (i, 0)),
            pl.BlockSpec((Q_COLS, tn), lambda j, i: (0, j)),
            pl.BlockSpec((D_MODEL, tn), lambda j, i: (0, j)),
            pl.BlockSpec((tm, tn), lambda j, i: (i, COL_GA // tn + j)),
            pl.BlockSpec((tm, tn), lambda j, i: (i, COL_GR // tn + j)),
        ],
        out_specs=pl.BlockSpec((tm, tn), lambda j, i: (i, j)),
        out_shape=jax.ShapeDtypeStruct((m, D_MODEL), BF16),
        scratch_shapes=[pltpu.VMEM((Q_COLS, tn), BF16), pltpu.VMEM((D_MODEL, tn), BF16)],
        compiler_params=_params(("arbitrary", "arbitrary")),
        name="gated_merge",
    )(attn, rnn, w_o_attn, w_o_rnn, z, z)


POST_TM = 512
POST_CTX_BLOCKS = N_CTX // POST_TM
LAT_BLOCKS_PER_SEQ = LAT_LEN // POST_TM


def _post_group(i):
    return jnp.where(i < POST_CTX_BLOCKS, 0, 1 + (i - POST_CTX_BLOCKS) // LAT_BLOCKS_PER_SEQ)


def _postmix_body(mc_ref, ml_ref, xc_ref, xl_ref, wo_ref, gpm_ref, gt1_ref, gpf_ref, sh2_ref, sc2_ref,
                  wr_ref, br_ref, x1_ref, h2_ref, e_ref, gate_ref, rank_ref, cnt_ref, carry_ref):
    i = pl.program_id(0)
    tm = POST_TM

    @pl.when(i == 0)
    def _():
        carry_ref[...] = jnp.zeros_like(carry_ref)

    is_ctx = i < POST_CTX_BLOCKS
    merged = jnp.where(is_ctx, mc_ref[...], ml_ref[...])
    x = jnp.where(is_ctx, xc_ref[...], xl_ref[...])
    o = jnp.dot(merged, wo_ref[...], preferred_element_type=F32)
    x1 = x + gt1_ref[...] * (o * _rms_scale(o) * gpm_ref[...])
    x1_ref[...] = x1
    h2 = (x1 * _rms_scale(x1) * gpf_ref[...]) * (1.0 + sc2_ref[...]) + sh2_ref[...]
    h2_ref[...] = h2

    logits = jnp.dot(h2.astype(BF16), wr_ref[...].astype(BF16), preferred_element_type=F32) + br_ref[...]
    lane = lax.broadcasted_iota(jnp.int32, (tm, N_EXPERTS), 1)
    work = logits
    chosen = jnp.zeros((tm, N_EXPERTS), F32)
    sels, vals, idxs = [], [], []
    for _ in range(TOP_K):
        mx = jnp.max(work, axis=-1, keepdims=True)
        idx = jnp.min(jnp.where(work == mx, lane, N_EXPERTS), axis=-1, keepdims=True)
        sel = lane == idx
        work = jnp.where(sel, -jnp.inf, work)
        chosen = jnp.where(sel, 1.0, chosen)
        sels.append(sel)
        vals.append(mx)
        idxs.append(idx)
    exps = [jnp.exp(v - vals[0]) for v in vals]
    inv = 1.0 / (exps[0] + exps[1] + exps[2] + exps[3])

    r_io = lax.broadcasted_iota(jnp.int32, (tm, tm), 0)
    c_io = lax.broadcasted_iota(jnp.int32, (tm, tm), 1)
    lower = jnp.where(c_io < r_io, 1.0, 0.0).astype(BF16)
    before = jnp.dot(lower, chosen.astype(BF16), preferred_element_type=F32) + carry_ref[...]
    carry_ref[...] = carry_ref[...] + jnp.sum(chosen, axis=0, keepdims=True)
    cnt_ref[...] = carry_ref[...]

    lane_k = lax.broadcasted_iota(jnp.int32, (tm, TOP_K), 1)
    e_out = jnp.zeros((tm, TOP_K), jnp.int32)
    g_out = jnp.zeros((tm, TOP_K), F32)
    r_out = jnp.zeros((tm, TOP_K), jnp.int32)
    for k in range(TOP_K):
        rk = jnp.sum(jnp.where(sels[k], before, 0.0), axis=-1, keepdims=True).astype(jnp.int32)
        e_out = jnp.where(lane_k == k, idxs[k], e_out)
        g_out = jnp.where(lane_k == k, exps[k] * inv, g_out)
        r_out = jnp.where(lane_k == k, rk, r_out)
    e_ref[...] = e_out
    gate_ref[...] = g_out
    rank_ref[...] = r_out


def post_mix_router(merged_ctx, merged_lat, x_ctx, x_lat, w_out_bf, g_post_mix, gt1, g_pre_ffn, sh2, sc2,
                    w_router, b_router):
    tm = POST_TM
    ctx_map = lambda i: (jnp.minimum(i, POST_CTX_BLOCKS - 1), 0)
    lat_map = lambda i: (jnp.maximum(i - POST_CTX_BLOCKS, 0), 0)
    gmap = lambda i: (_post_group(i), 0, 0)
    row = lambda i: (i, 0)
    const = lambda i: (0, 0)
    vec = pl.BlockSpec((1, D_MODEL), const)
    gvec = pl.BlockSpec((None, 1, D_MODEL), gmap)
    return pl.pallas_call(
        _postmix_body,
        grid=(N_TOK // tm,),
        in_specs=[
            pl.BlockSpec((tm, D_MODEL), ctx_map),
            pl.BlockSpec((tm, D_MODEL), lat_map),
            pl.BlockSpec((tm, D_MODEL), ctx_map),
            pl.BlockSpec((tm, D_MODEL), lat_map),
            pl.BlockSpec((D_MODEL, D_MODEL), const),
            vec, gvec, vec, gvec, gvec,
            pl.BlockSpec((D_MODEL, N_EXPERTS), const),
            pl.BlockSpec((1, N_EXPERTS), const),
        ],
        out_specs=[
            pl.BlockSpec((tm, D_MODEL), row),
            pl.BlockSpec((tm, D_MODEL), row),
            pl.BlockSpec((tm, TOP_K), row),
            pl.BlockSpec((tm, TOP_K), row),
            pl.BlockSpec((tm, TOP_K), row),
            pl.BlockSpec((1, N_EXPERTS), const),
        ],
        out_shape=[
            jax.ShapeDtypeStruct((N_TOK, D_MODEL), F32),
            jax.ShapeDtypeStruct((N_TOK, D_MODEL), F32),
            jax.ShapeDtypeStruct((N_TOK, TOP_K), jnp.int32),
            jax.ShapeDtypeStruct((N_TOK, TOP_K), F32),
            jax.ShapeDtypeStruct((N_TOK, TOP_K), jnp.int32),
            jax.ShapeDtypeStruct((1, N_EXPERTS), F32),
        ],
        scratch_shapes=[pltpu.VMEM((1, N_EXPERTS), F32)],
        compiler_params=_params(("arbitrary",)),
        name="post_mix_router",
    )(merged_ctx, merged_lat, x_ctx, x_lat, w_out_bf, g_post_mix.reshape(1, D_MODEL), gt1,
      g_pre_ffn.reshape(1, D_MODEL), sh2, sc2, w_router, b_router.reshape(1, N_EXPERTS))


LANE_BLOCKS = D_MODEL // 128
SUBLANES = 8


def _row_gather_start(src_ref, dst_ref, idx_ref, n, sem):
    def issue(r, _):
        t = idx_ref[0, 0, r]
        dst = dst_ref.at[lax.shift_right_logical(r, 3), :, jnp.bitwise_and(r, SUBLANES - 1), :]
        pltpu.make_async_copy(src_ref.at[t], dst, sem).start()
        return 0

    lax.fori_loop(0, n, issue, 0, unroll=8)


def _dispatch_body(tok_ref, tok_next_ref, h_ref, o_ref, stage_ref, sem_ref):
    r = pl.program_id(0)
    n = N_ROW_TILES
    slot = r % 2

    @pl.when(r == 0)
    def _():
        _row_gather_start(h_ref, stage_ref.at[0], tok_ref, ROW_TILE, sem_ref.at[0])

    @pl.when(r + 1 < n)
    def _():
        _row_gather_start(h_ref, stage_ref.at[1 - slot], tok_next_ref, ROW_TILE, sem_ref.at[1 - slot])

    pltpu.make_async_copy(stage_ref.at[slot], stage_ref.at[slot], sem_ref.at[slot]).wait()
    for c in range(LANE_BLOCKS):
        piece = stage_ref[slot, :, c, :, :].reshape(ROW_TILE, 128)
        o_ref[:, c * 128:(c + 1) * 128] = piece.astype(o_ref.dtype)


def dispatch_rows(h2, tok_sorted):
    h2 = h2.reshape(N_TOK, LANE_BLOCKS, 128)
    tok3 = tok_sorted.reshape(N_ROW_TILES, 1, ROW_TILE)
    smem_blk = lambda f: pl.BlockSpec((1, 1, ROW_TILE), f, memory_space=pltpu.SMEM)
    return pl.pallas_call(
        _dispatch_body,
        grid=(N_ROW_TILES,),
        in_specs=[
            smem_blk(lambda r: (r, 0, 0)),
            smem_blk(lambda r: (jnp.minimum(r + 1, N_ROW_TILES - 1), 0, 0)),
            pl.BlockSpec(memory_space=pl.ANY),
        ],
        out_specs=pl.BlockSpec((ROW_TILE, D_MODEL), lambda r: (r, 0)),
        out_shape=jax.ShapeDtypeStruct((N_ROWS, D_MODEL), BF16),
        scratch_shapes=[pltpu.VMEM((2, ROW_TILE // SUBLANES, LANE_BLOCKS, SUBLANES, 128), F32),
                        pltpu.SemaphoreType.DMA((2,))],
        compiler_params=_params(("arbitrary",)),
        name="dispatch_rows",
    )(tok3, tok3, h2)


def _x_tile_copy(x_ref, xbuf_ref, sem, row_start, i):
    src = x_ref.at[pl.ds(pl.multiple_of(row_start + i * ROW_TILE, ROW_TILE), ROW_TILE), :]
    return pltpu.make_async_copy(src, xbuf_ref.at[pl.ds(i * ROW_TILE, ROW_TILE), :], sem)


def _moe_body(exp_ref, row_ref, nsub_ref, nzero_ref, x_ref, wg_ref, wl_ref, wd_ref, bg_ref, bl_ref, bd_ref, y_ref,
              xbuf_ref, act_ref, wg_bf, wl_bf, wd_bf, stage_ref, pend_ref, xsem, ysem):
    s = pl.program_id(0)
    j = pl.program_id(1)
    n_sub = nsub_ref[s]
    row_start = row_ref[s]

    @pl.when(jnp.logical_and(s == 0, j == 0))
    def _():
        pend_ref[0] = 0
        pend_ref[1] = 0

    def drain_stage(slot):
        @pl.when(pend_ref[slot] == 1)
        def _():
            pltpu.make_async_copy(stage_ref.at[slot], stage_ref.at[slot], ysem.at[slot]).wait()
            pend_ref[slot] = 0

    def load_x(step):
        def issue(i, _):
            _x_tile_copy(x_ref, xbuf_ref, xsem, row_ref[step], i).start()
            return 0
        lax.fori_loop(0, nsub_ref[step], issue, 0)

    def wait_x(step):
        def wait(i, _):
            _x_tile_copy(x_ref, xbuf_ref, xsem, row_ref[step], i).wait()
            return 0
        lax.fori_loop(0, nsub_ref[step], wait, 0)

    @pl.when(jnp.logical_and(s == 0, j == 0))
    def _():
        load_x(0)

    @pl.when(j == 0)
    def _():
        wait_x(s)

    @pl.when(jnp.logical_and(j < N_FF_CHUNKS, n_sub > 0))
    def _():
        wg_bf[...] = wg_ref[...].astype(BF16)
        wl_bf[...] = wl_ref[...].astype(BF16)
        bg = bg_ref[...]
        bl = bl_ref[...]

        def sub(i, _):
            rows = pl.ds(pl.multiple_of(i * ROW_TILE, ROW_TILE), ROW_TILE)
            xt = xbuf_ref[rows, :]
            glu = jnp.minimum(jnp.dot(xt, wg_bf[...], preferred_element_type=F32) + bg, SWIGLU_LIMIT)
            lin = jnp.clip(jnp.dot(xt, wl_bf[...], preferred_element_type=F32) + bl,
                           -SWIGLU_LIMIT, SWIGLU_LIMIT)
            act = glu * _sigmoid(SWIGLU_ALPHA * glu) * (lin + 1.0)
            act_ref[j, rows, :] = act.astype(BF16)
            return 0

        lax.fori_loop(0, n_sub, sub, 0)

    @pl.when(jnp.logical_and(j == N_FF_CHUNKS, s + 1 < N_SUPER))
    def _():
        load_x(s + 1)

    for cc in range(N_FF_CHUNKS):
        @pl.when(jnp.logical_and(j == N_FF_CHUNKS + cc, n_sub > 0))
        def _(cc=cc):
            wd_bf[...] = wd_ref[...].astype(BF16)
            bd = bd_ref[...]

            def out_copy(i, slot):
                dst = y_ref.at[pl.ds(pl.multiple_of(row_start + i * ROW_TILE, ROW_TILE), ROW_TILE),
                               cc * FF_CHUNK:(cc + 1) * FF_CHUNK]
                return pltpu.make_async_copy(stage_ref.at[slot], dst, ysem.at[slot])

            def sub(i, _):
                rows = pl.ds(pl.multiple_of(i * ROW_TILE, ROW_TILE), ROW_TILE)
                slot = i % 2
                acc = bd
                for c in range(N_FF_CHUNKS):
                    acc = acc + jnp.dot(act_ref[c, rows, :], wd_bf[c * FF_CHUNK:(c + 1) * FF_CHUNK, :],
                                        preferred_element_type=F32)

                drain_stage(slot)
                stage_ref[slot] = acc
                out_copy(i, slot).start()
                pend_ref[slot] = 1
                return 0

            lax.fori_loop(0, n_sub, sub, 0)

    n_zero = nzero_ref[s]

    @pl.when(jnp.logical_and(j == 0, n_zero > 0))
    def _():
        drain_stage(0)
        stage_ref[0] = jnp.zeros((ROW_TILE, FF_CHUNK), F32)

        def zero_copy(i, cc):
            dst = y_ref.at[pl.ds(pl.multiple_of(row_start + i * ROW_TILE, ROW_TILE), ROW_TILE),
                           cc * FF_CHUNK:(cc + 1) * FF_CHUNK]
            return pltpu.make_async_copy(stage_ref.at[0], dst, ysem.at[0])

        def issue(i, _):
            for cc in range(N_FF_CHUNKS):
                zero_copy(i, cc).start()
            return 0

        def drain(i, _):
            for cc in range(N_FF_CHUNKS):
                zero_copy(i, cc).wait()
            return 0

        lax.fori_loop(0, n_zero, issue, 0)
        lax.fori_loop(0, n_zero, drain, 0)

    @pl.when(jnp.logical_and(s == N_SUPER - 1, j == 2 * N_FF_CHUNKS - 1))
    def _():
        drain_stage(0)
        drain_stage(1)


def expert_mlp(x_sorted, sched, w_gate_up, b_gate_up, w_down, b_down):
    exp_of, row_of, nsub_of, nzero_of = sched
    last = N_FF_CHUNKS - 1
    up_of = lambda s, j, n: jnp.where(n[s] > 0, jnp.minimum(j, last), last)
    down_of = lambda s, j, n: jnp.where(n[s] > 0, jnp.maximum(j - N_FF_CHUNKS, 0), last)
    up_chunk = lambda s, j, e, r, n, z: (e[s], 0, up_of(s, j, n))
    lin_chunk = lambda s, j, e, r, n, z: (e[s], 0, N_FF_CHUNKS + up_of(s, j, n))
    down_chunk = lambda s, j, e, r, n, z: (e[s], 0, down_of(s, j, n))
    grid_spec = pltpu.PrefetchScalarGridSpec(
        num_scalar_prefetch=4,
        grid=(N_SUPER, 2 * N_FF_CHUNKS),
        in_specs=[
            pl.BlockSpec(memory_space=pl.ANY),
            pl.BlockSpec((None, D_MODEL, FF_CHUNK), up_chunk),
            pl.BlockSpec((None, D_MODEL, FF_CHUNK), lin_chunk),
            pl.BlockSpec((None, D_FF, FF_CHUNK), down_chunk),
            pl.BlockSpec((None, 1, FF_CHUNK), up_chunk),
            pl.BlockSpec((None, 1, FF_CHUNK), lin_chunk),
            pl.BlockSpec((None, 1, FF_CHUNK), down_chunk),
        ],
        out_specs=pl.BlockSpec(memory_space=pl.ANY),
        scratch_shapes=[
            pltpu.VMEM((SUPER_ROWS, D_MODEL), BF16),
            pltpu.VMEM((N_FF_CHUNKS, SUPER_ROWS, FF_CHUNK), BF16),
            pltpu.VMEM((D_MODEL, FF_CHUNK), BF16),
            pltpu.VMEM((D_MODEL, FF_CHUNK), BF16),
            pltpu.VMEM((D_FF, FF_CHUNK), BF16),
            pltpu.VMEM((2, ROW_TILE, FF_CHUNK), F32),
            pltpu.SMEM((2,), jnp.int32),
            pltpu.SemaphoreType.DMA(()),
            pltpu.SemaphoreType.DMA((2,)),
        ],
    )
    return pl.pallas_call(
        _moe_body,
        grid_spec=grid_spec,
        out_shape=jax.ShapeDtypeStruct((N_ROWS, D_MODEL), F32),
        compiler_params=_params(("arbitrary", "arbitrary"), vmem=58 * 1024 * 1024),
        name="expert_mlp",
    )(exp_of, row_of, nsub_of, nzero_of, x_sorted, w_gate_up, w_gate_up, w_down,
      b_gate_up.reshape(N_EXPERTS, 1, 2 * D_FF), b_gate_up.reshape(N_EXPERTS, 1, 2 * D_FF),
      b_down.reshape(N_EXPERTS, 1, D_MODEL))


COMB_TB = 256


def _combine_start(y_ref, ybuf_ref, pos_ref, sem):
    def issue(t, _):
        for k in range(TOP_K):
            p = pos_ref[0, 0, t * TOP_K + k]
            pltpu.make_async_copy(y_ref.at[pl.ds(p, 1), :], ybuf_ref.at[k, pl.ds(t, 1), :], sem).start()
        return 0

    lax.fori_loop(0, COMB_TB, issue, 0, unroll=2)


def _combine_body(n, pos_ref, pos_next_ref, y_ref, gate_ref, x1_ref, gt2_ref, g_ref, o_ref, ybuf_ref, sem_ref):
    i = pl.program_id(0)
    slot = i % 2

    @pl.when(i == 0)
    def _():
        _combine_start(y_ref, ybuf_ref.at[0], pos_ref, sem_ref.at[0])

    @pl.when(i + 1 < n)
    def _():
        _combine_start(y_ref, ybuf_ref.at[1 - slot], pos_next_ref, sem_ref.at[1 - slot])

    for k in range(TOP_K):
        pltpu.make_async_copy(y_ref.at[pl.ds(0, COMB_TB), :], ybuf_ref.at[slot, k], sem_ref.at[slot]).wait()
    gates = gate_ref[...]
    ffn = gates[:, 0:1] * ybuf_ref[slot, 0]
    for k in range(1, TOP_K):
        ffn = ffn + gates[:, k:k + 1] * ybuf_ref[slot, k]
    o_ref[...] = x1_ref[...] + gt2_ref[...] * (ffn * _rms_scale(ffn) * g_ref[...])


def combine_residual(y_sorted, pos, gates, x1, gt2, g_post_ffn, row_offset, n_rows, group_of_block):
    tb = COMB_TB
    nblk = n_rows // tb
    off = row_offset // tb
    pos3 = pos.reshape(N_TOK // tb, 1, tb * TOP_K)
    smem_blk = lambda f: pl.BlockSpec((1, 1, tb * TOP_K), f, memory_space=pltpu.SMEM)
    return pl.pallas_call(
        functools.partial(_combine_body, nblk),
        grid=(nblk,),
        in_specs=[
            smem_blk(lambda i: (off + i, 0, 0)),
            smem_blk(lambda i: (off + jnp.minimum(i + 1, nblk - 1), 0, 0)),
            pl.BlockSpec(memory_space=pl.ANY),
            pl.BlockSpec((tb, TOP_K), lambda i: (off + i, 0)),
            pl.BlockSpec((tb, D_MODEL), lambda i: (off + i, 0)),
            pl.BlockSpec((None, 1, D_MODEL), lambda i: (group_of_block(i), 0, 0)),
            pl.BlockSpec((1, D_MODEL), lambda i: (0, 0)),
        ],
        out_specs=pl.BlockSpec((tb, D_MODEL), lambda i: (i, 0)),
        out_shape=jax.ShapeDtypeStruct((n_rows, D_MODEL), F32),
        scratch_shapes=[pltpu.VMEM((2, TOP_K, tb, D_MODEL), F32), pltpu.SemaphoreType.DMA((2,))],
        compiler_params=_params(("arbitrary",)),
        name="combine_residual",
    )(pos3, pos3, y_sorted, gates, x1, gt2, g_post_ffn.reshape(1, D_MODEL))


def _routing_tables(e_idx, rank, counts_f):
    counts = counts_f.reshape(N_EXPERTS).astype(jnp.int32)
    n_tiles = (counts + ROW_TILE - 1) // ROW_TILE
    padded = n_tiles * ROW_TILE
    pad_end = jnp.cumsum(padded)
    pad_start = pad_end - padded
    pos = pad_start[e_idx] + rank
    tok_ids = jnp.repeat(jnp.arange(N_TOK, dtype=jnp.int32), TOP_K)
    tok_sorted = jnp.zeros((N_ROWS,), jnp.int32).at[pos.reshape(-1)].set(
        tok_ids, unique_indices=True, mode="drop")
    n_pass = (n_tiles + SUPER_TILES - 1) // SUPER_TILES
    pass_end = jnp.cumsum(n_pass)
    total = pass_end[-1]
    s = jnp.arange(N_SUPER, dtype=jnp.int32)
    s_eff = jnp.minimum(s, total - 1)
    e_of = jnp.minimum(jnp.searchsorted(pass_end, s_eff, side="right"), N_EXPERTS - 1).astype(jnp.int32)
    local = s_eff - (pass_end[e_of] - n_pass[e_of])
    row_of = pad_start[e_of] + local * SUPER_ROWS
    nsub = jnp.minimum(SUPER_TILES, n_tiles[e_of] - local * SUPER_TILES)
    nsub = jnp.where(s < total, nsub, 0).astype(jnp.int32)
    zero_row = pad_end[-1] + (s - total) * SUPER_ROWS
    nzero = jnp.clip((N_ROWS - zero_row) // ROW_TILE, 0, SUPER_TILES)
    nzero = jnp.where(s >= total, nzero, 0).astype(jnp.int32)
    row_of = jnp.where(s < total, row_of, jnp.minimum(zero_row, N_ROWS - ROW_TILE)).astype(jnp.int32)
    return pos.astype(jnp.int32), tok_sorted, (e_of, row_of, nsub, nzero)


def kernel(x_prompt, x_sample, cache_k, cache_v, state_rnn_fwd, state_rnn_bwd, c, c_ctx, w_mod, b_mod, g_pre_mix, w_in, g_q_norm, g_k_norm, conv_w, conv_b, rg_w_a, rg_b_a, rg_w_x, rg_b_x, rg_lambda, w_o_attn, w_o_rnn, w_out, g_post_mix, g_pre_ffn, w_router, b_router, w_gate_up, b_gate_up, w_down, b_down, g_post_ffn):
    l = 0
    x_ctx = x_prompt.reshape(N_CTX, D_MODEL)
    x_lat = x_sample.reshape(N_LAT, D_MODEL)

    cond8 = jnp.concatenate([c_ctx[None, :], c, jnp.zeros((8 - 1 - N_LAT_SEQ, D_MODEL), F32)], axis=0)
    mod = modulation(cond8, w_mod[l], b_mod[l])[:1 + N_LAT_SEQ].reshape(1 + N_LAT_SEQ, 6, 1, D_MODEL)
    sh1, sc1, gt1, sh2, sc2, gt2 = [mod[:, i] for i in range(6)]

    ctx_group = lambda i: 0
    lat_group_1024 = lambda i: 1 + i
    h_ctx = prenorm_modulate(x_ctx, g_pre_mix[l], sh1, sc1, ctx_group, 1024)
    h_lat = prenorm_modulate(x_lat, g_pre_mix[l], sh1, sc1, lat_group_1024, 1024)
    z_ctx = in_projection(h_ctx, w_in[l])
    z_lat = in_projection(h_lat, w_in[l])

    attn_ctx, k_new, v_new = attention_ctx(z_ctx, g_q_norm[l], g_k_norm[l])
    attn_lat = attention_lat(z_lat, cache_k[:, l].reshape(N_LAT_SEQ, PAST_LEN, KV_COLS),
                             cache_v[:, l].reshape(N_LAT_SEQ, PAST_LEN, KV_COLS),
                             _rope_tables(), g_q_norm[l], g_k_norm[l])

    def per_block(w):
        return w.reshape(2, RNN_BLOCKS, 1, RNN_BLOCK_DIM)

    w_gates = jnp.concatenate([rg_w_a[l, 0], rg_w_x[l, 0], rg_w_a[l, 1], rg_w_x[l, 1]], axis=-1).astype(BF16)
    ba, bx = per_block(rg_b_a[l]), per_block(rg_b_x[l])
    b_gates = jnp.concatenate([ba[0], bx[0], ba[1], bx[1]], axis=-1)
    zeros_state = jnp.zeros((N_CTX_SEQ, 1, D_MODEL), F32)
    rnn_ctx, hf_ctx, hb_ctx = rglru_mixer(z_ctx, CTX_LEN, conv_w[l], conv_b[l], w_gates, b_gates,
                                          rg_lambda[l], zeros_state, zeros_state)
    rnn_lat, _, _ = rglru_mixer(z_lat, LAT_LEN, conv_w[l], conv_b[l], w_gates, b_gates, rg_lambda[l],
                                state_rnn_fwd[:, l].reshape(N_LAT_SEQ, 1, D_MODEL),
                                state_rnn_bwd[:, l].reshape(N_LAT_SEQ, 1, D_MODEL))

    merged_ctx = gated_merge(attn_ctx, rnn_ctx, z_ctx, w_o_attn[l], w_o_rnn[l])
    merged_lat = gated_merge(attn_lat, rnn_lat, z_lat, w_o_attn[l], w_o_rnn[l])

    x1, h2, e_idx, gates, rank, counts = post_mix_router(
        merged_ctx, merged_lat, x_ctx, x_lat, w_out[l].astype(BF16), g_post_mix[l], gt1, g_pre_ffn[l],
        sh2, sc2, w_router[l], b_router[l])

    pos, tok_sorted, sched = _routing_tables(e_idx, rank, counts)
    x_sorted = dispatch_rows(h2, tok_sorted)
    y_sorted = expert_mlp(x_sorted, sched, w_gate_up[l], b_gate_up[l], w_down[l], b_down[l])

    y_ctx = combine_residual(y_sorted, pos, gates, x1, gt2, g_post_ffn[l], 0, N_CTX, ctx_group)
    y_lat = combine_residual(y_sorted, pos, gates, x1, gt2, g_post_ffn[l], N_CTX, N_LAT,
                             lambda i: 1 + i // (LAT_LEN // COMB_TB))

    return (y_ctx.reshape(N_CTX_SEQ, CTX_LEN, D_MODEL),
            y_lat.reshape(N_LAT_SEQ, LAT_LEN, D_MODEL),
            k_new.reshape(N_CTX_SEQ, 1, CTX_LEN, N_KV_HEADS, HEAD_DIM),
            v_new.reshape(N_CTX_SEQ, 1, CTX_LEN, N_KV_HEADS, HEAD_DIM),
            hf_ctx,
            hb_ctx)
```

```python
import functools

import jax
import jax.numpy as jnp
import numpy as np
from jax import lax
from jax.experimental import pallas as pl
from jax.experimental.pallas import tpu as pltpu

D_MODEL = 2048
N_CTX_SEQ = 32
CTX_LEN = 256
N_LAT_SEQ = 2
LAT_LEN = 1024
PAST_LEN = 512
N_CTX = N_CTX_SEQ * CTX_LEN
N_LAT = N_LAT_SEQ * LAT_LEN
N_TOK = N_CTX + N_LAT
GRID_W = 64
N_HEADS = 16
N_KV_HEADS = 4
HEAD_DIM = 128
KV_GROUP = N_HEADS // N_KV_HEADS
ROPE_THETA = 10000.0
RNN_BLOCKS = 16
RNN_BLOCK_DIM = 128
RG_C = 8.0
N_EXPERTS = 32
TOP_K = 4
D_FF = 2048
SWIGLU_LIMIT = 7.0
SWIGLU_ALPHA = 1.702
EPS = 1e-6
Q_COLS = N_HEADS * HEAD_DIM
KV_COLS = N_KV_HEADS * HEAD_DIM
IN_COLS = Q_COLS + 2 * KV_COLS + 4 * D_MODEL
COL_K = Q_COLS
COL_XR = Q_COLS + 2 * KV_COLS
COL_YR = COL_XR + D_MODEL
COL_GA = COL_YR + D_MODEL
COL_GR = COL_GA + D_MODEL

V7X_VMEM_BYTES = 64 * 1024 * 1024
VMEM_LIMIT = 56 * 1024 * 1024

ROW_TILE = 256
SUPER_TILES = 8
SUPER_ROWS = ROW_TILE * SUPER_TILES
N_ASSIGN = N_TOK * TOP_K
N_ROWS = N_ASSIGN + N_EXPERTS * ROW_TILE
N_ROW_TILES = N_ROWS // ROW_TILE
N_SUPER = N_ROW_TILES // SUPER_TILES + N_EXPERTS
FF_CHUNK = 512
N_FF_CHUNKS = D_FF // FF_CHUNK

BF16 = jnp.bfloat16
F32 = jnp.float32


def _params(semantics, vmem=VMEM_LIMIT):
    return pltpu.CompilerParams(dimension_semantics=semantics, vmem_limit_bytes=vmem)


def _rms_scale(x):
    return lax.rsqrt(jnp.mean(x * x, axis=-1, keepdims=True) + EPS)


def _sigmoid(x):
    return 1.0 / (1.0 + jnp.exp(-x))


def _mod_body(c_ref, w_ref, b_ref, o_ref):
    c = c_ref[...]
    a = (c * _sigmoid(c)).astype(BF16)
    o_ref[...] = jnp.dot(a, w_ref[...].astype(BF16), preferred_element_type=F32) + b_ref[...]


def modulation(cond8, w_mod, b_mod):
    tn = 1024
    n = w_mod.shape[1]
    return pl.pallas_call(
        _mod_body,
        grid=(n // tn,),
        in_specs=[
            pl.BlockSpec((8, D_MODEL), lambda j: (0, 0)),
            pl.BlockSpec((D_MODEL, tn), lambda j: (0, j)),
            pl.BlockSpec((1, tn), lambda j: (0, j)),
        ],
        out_specs=pl.BlockSpec((8, tn), lambda j: (0, j)),
        out_shape=jax.ShapeDtypeStruct((8, n), F32),
        compiler_params=_params(("arbitrary",)),
        name="modulation",
    )(cond8, w_mod, b_mod.reshape(1, n))


def _prenorm_body(x_ref, g_ref, sh_ref, sc_ref, o_ref):
    x = x_ref[...]
    y = x * _rms_scale(x) * g_ref[...]
    o_ref[...] = (y * (1.0 + sc_ref[...]) + sh_ref[...]).astype(o_ref.dtype)


def prenorm_modulate(x, g, shift, scale, group_of_block, tm):
    m = x.shape[0]
    gmap = lambda i: (group_of_block(i), 0, 0)
    return pl.pallas_call(
        _prenorm_body,
        grid=(m // tm,),
        in_specs=[
            pl.BlockSpec((tm, D_MODEL), lambda i: (i, 0)),
            pl.BlockSpec((1, D_MODEL), lambda i: (0, 0)),
            pl.BlockSpec((None, 1, D_MODEL), gmap),
            pl.BlockSpec((None, 1, D_MODEL), gmap),
        ],
        out_specs=pl.BlockSpec((tm, D_MODEL), lambda i: (i, 0)),
        out_shape=jax.ShapeDtypeStruct((m, D_MODEL), BF16),
        compiler_params=_params(("arbitrary",)),
        name="prenorm_modulate",
    )(x, g.reshape(1, D_MODEL), shift, scale)


def _inproj_body(h_ref, w_ref, o_ref, wbf_ref):
    @pl.when(pl.program_id(1) == 0)
    def _():
        wbf_ref[...] = w_ref[...].astype(BF16)

    o_ref[...] = jnp.dot(h_ref[...], wbf_ref[...], preferred_element_type=F32)


def in_projection(h, w_in):
    m = h.shape[0]
    tm, tn = 1024, 1024
    return pl.pallas_call(
        _inproj_body,
        grid=(IN_COLS // tn, m // tm),
        in_specs=[
            pl.BlockSpec((tm, D_MODEL), lambda j, i: (i, 0)),
            pl.BlockSpec((D_MODEL, tn), lambda j, i: (0, j)),
        ],
        out_specs=pl.BlockSpec((tm, tn), lambda j, i: (i, j)),
        out_shape=jax.ShapeDtypeStruct((m, IN_COLS), F32),
        scratch_shapes=[pltpu.VMEM((D_MODEL, tn), BF16)],
        compiler_params=_params(("arbitrary", "arbitrary")),
        name="in_projection",
    )(h, w_in)


def _rope(x, cos, sin_lo, sin_hi):
    return x * cos + pltpu.roll(x, 96, 1) * sin_lo + pltpu.roll(x, 32, 1) * sin_hi


def _head_norm(x, g):
    return x * _rms_scale(x) * g


def _softmax_pv(score_blocks, value_blocks):
    m = None
    for s in score_blocks:
        mi = jnp.max(s, axis=-1, keepdims=True)
        m = mi if m is None else jnp.maximum(m, mi)
    ps = [jnp.exp(s - m) for s in score_blocks]
    denom = None
    for p in ps:
        li = jnp.sum(p, axis=-1, keepdims=True)
        denom = li if denom is None else denom + li
    inv = 1.0 / denom
    out = None
    for p, v in zip(ps, value_blocks):
        o = jnp.dot((p * inv).astype(BF16), v, preferred_element_type=F32)
        out = o if out is None else out + o
    return out


def _attn_ctx_body(q_ref, kv_ref, gq_ref, gk_ref, o_ref, ko_ref, vo_ref):
    tq = q_ref.shape[0]
    scale = HEAD_DIM ** -0.5
    gq = gq_ref[...]
    gk = gk_ref[...]
    for g in range(N_KV_HEADS):
        kcols = slice(g * HEAD_DIM, (g + 1) * HEAD_DIM)
        kn = _head_norm(kv_ref[:, kcols], gk)
        v = kv_ref[:, KV_COLS + g * HEAD_DIM:KV_COLS + (g + 1) * HEAD_DIM]
        ko_ref[pl.ds(g, tq, stride=N_KV_HEADS), :] = kn
        vo_ref[pl.ds(g, tq, stride=N_KV_HEADS), :] = v
        qs = []
        for hh in range(KV_GROUP):
            h = g * KV_GROUP + hh
            qs.append(_head_norm(q_ref[:, h * HEAD_DIM:(h + 1) * HEAD_DIM], gq).astype(BF16))
        q4 = jnp.concatenate(qs, axis=0)
        s = lax.dot_general(q4, kn.astype(BF16), (((1,), (1,)), ((), ())),
                            preferred_element_type=F32) * scale
        o = _softmax_pv([s], [v.astype(BF16)])
        for hh in range(KV_GROUP):
            h = g * KV_GROUP + hh
            o_ref[:, h * HEAD_DIM:(h + 1) * HEAD_DIM] = o[hh * tq:(hh + 1) * tq].astype(o_ref.dtype)


def attention_ctx(z, g_q, g_k):
    nb = N_CTX_SEQ
    t = CTX_LEN
    return pl.pallas_call(
        _attn_ctx_body,
        grid=(nb,),
        in_specs=[
            pl.BlockSpec((t, Q_COLS), lambda b: (b, 0)),
            pl.BlockSpec((t, 2 * KV_COLS), lambda b: (b, COL_K // (2 * KV_COLS))),
            pl.BlockSpec((1, HEAD_DIM), lambda b: (0, 0)),
            pl.BlockSpec((1, HEAD_DIM), lambda b: (0, 0)),
        ],
        out_specs=[
            pl.BlockSpec((t, Q_COLS), lambda b: (b, 0)),
            pl.BlockSpec((t * N_KV_HEADS, HEAD_DIM), lambda b: (b, 0)),
            pl.BlockSpec((t * N_KV_HEADS, HEAD_DIM), lambda b: (b, 0)),
        ],
        out_shape=[
            jax.ShapeDtypeStruct((N_CTX, Q_COLS), BF16),
            jax.ShapeDtypeStruct((N_CTX * N_KV_HEADS, HEAD_DIM), F32),
            jax.ShapeDtypeStruct((N_CTX * N_KV_HEADS, HEAD_DIM), F32),
        ],
        compiler_params=_params(("arbitrary",)),
        name="attention_ctx",
    )(z, z, g_q.reshape(1, HEAD_DIM), g_k.reshape(1, HEAD_DIM))


def _attn_lat_body(q_ref, kv_ref, ck_ref, cv_ref, cos_ref, slo_ref, shi_ref, gq_ref, gk_ref,
                   o_ref, kr_ref):
    tq = q_ref.shape[0]
    qb = pl.program_id(1)
    scale = HEAD_DIM ** -0.5
    gq = gq_ref[...]

    @pl.when(qb == 0)
    def _():
        gk = gk_ref[...]
        for g in range(N_KV_HEADS):
            kcols = slice(g * HEAD_DIM, (g + 1) * HEAD_DIM)
            kn = _head_norm(kv_ref[:, kcols], gk)
            kr_ref[:, kcols] = _rope(kn, cos_ref[...], slo_ref[...], shi_ref[...]).astype(BF16)

    row0 = pl.multiple_of(qb * tq, tq)
    cos = cos_ref[pl.ds(row0, tq), :]
    slo = slo_ref[pl.ds(row0, tq), :]
    shi = shi_ref[pl.ds(row0, tq), :]
    for g in range(N_KV_HEADS):
        kcols = slice(g * HEAD_DIM, (g + 1) * HEAD_DIM)
        qs = []
        for hh in range(KV_GROUP):
            h = g * KV_GROUP + hh
            qn = _head_norm(q_ref[:, h * HEAD_DIM:(h + 1) * HEAD_DIM], gq)
            qs.append(_rope(qn, cos, slo, shi).astype(BF16))
        q4 = jnp.concatenate(qs, axis=0)
        dn = (((1,), (1,)), ((), ()))
        s_past = lax.dot_general(q4, ck_ref[:, kcols].astype(BF16), dn,
                                 preferred_element_type=F32) * scale
        s_new = lax.dot_general(q4, kr_ref[:, kcols], dn, preferred_element_type=F32) * scale
        v_past = cv_ref[:, kcols].astype(BF16)
        v_new = kv_ref[:, KV_COLS + g * HEAD_DIM:KV_COLS + (g + 1) * HEAD_DIM].astype(BF16)
        o = _softmax_pv([s_past, s_new], [v_past, v_new])
        for hh in range(KV_GROUP):
            h = g * KV_GROUP + hh
            o_ref[:, h * HEAD_DIM:(h + 1) * HEAD_DIM] = o[hh * tq:(hh + 1) * tq].astype(o_ref.dtype)


def attention_lat(z, cache_k, cache_v, rope_tabs, g_q, g_k):
    tq = 256
    nq = LAT_LEN // tq
    cos, slo, shi = rope_tabs
    tab = pl.BlockSpec((LAT_LEN, HEAD_DIM), lambda b, q: (0, 0))
    return pl.pallas_call(
        _attn_lat_body,
        grid=(N_LAT_SEQ, nq),
        in_specs=[
            pl.BlockSpec((tq, Q_COLS), lambda b, q: (b * nq + q, 0)),
            pl.BlockSpec((LAT_LEN, 2 * KV_COLS), lambda b, q: (b, COL_K // (2 * KV_COLS))),
            pl.BlockSpec((None, PAST_LEN, KV_COLS), lambda b, q: (b, 0, 0)),
            pl.BlockSpec((None, PAST_LEN, KV_COLS), lambda b, q: (b, 0, 0)),
            tab, tab, tab,
            pl.BlockSpec((1, HEAD_DIM), lambda b, q: (0, 0)),
            pl.BlockSpec((1, HEAD_DIM), lambda b, q: (0, 0)),
        ],
        out_specs=pl.BlockSpec((tq, Q_COLS), lambda b, q: (b * nq + q, 0)),
        out_shape=jax.ShapeDtypeStruct((N_LAT, Q_COLS), BF16),
        scratch_shapes=[pltpu.VMEM((LAT_LEN, KV_COLS), BF16)],
        compiler_params=_params(("arbitrary", "arbitrary")),
        name="attention_lat",
    )(z, z, cache_k, cache_v, cos, slo, shi, g_q.reshape(1, HEAD_DIM), g_k.reshape(1, HEAD_DIM))


def _rope_tables():
    t = np.arange(LAT_LEN)
    row = jnp.asarray(t // GRID_W, F32)
    col = jnp.asarray(t % GRID_W, F32)
    nf = HEAD_DIM // 4
    inv_freq = ROPE_THETA ** (-jnp.arange(nf, dtype=F32) / nf)
    ang_row = row[:, None] * inv_freq[None, :]
    ang_col = col[:, None] * inv_freq[None, :]
    ang = jnp.concatenate([ang_row, ang_row, ang_col, ang_col], axis=1)
    cos = jnp.cos(ang)
    sin = jnp.sin(ang)
    first = jnp.asarray((np.arange(HEAD_DIM) % (2 * nf)) < nf)[None, :]
    return cos, jnp.where(first, -sin, 0.0), jnp.where(first, 0.0, sin)


RNN_ROWS = 2048
RNN_COLS = 512
RNN_SUB = RNN_COLS // RNN_BLOCK_DIM


def _gelu_tanh(y):
    return 0.5 * y * (1.0 + jnp.tanh(0.7978845608028654 * (y + 0.044715 * (y * y * y))))


def _rglru_body(seq_len, xr_ref, yr_ref, cw_ref, cb_ref, wg_ref, bg_ref, lam_ref, h0f_ref, h0b_ref,
                o_ref, hf_ref, hb_ref, xs_ref, af_ref, bf_ref, ab_ref, bb_ref):
    n_seq = RNN_ROWS // seq_len
    for n in range(RNN_SUB):
        cols = slice(n * RNN_BLOCK_DIM, (n + 1) * RNN_BLOCK_DIM)
        for s in range(n_seq):
            xs_ref[n, pl.ds(s, seq_len, stride=n_seq), :] = xr_ref[s * seq_len:(s + 1) * seq_len, cols]

    row = lax.broadcasted_iota(jnp.int32, (RNN_ROWS, 1), 0)
    lam = lam_ref[...]
    softplus_neg = jnp.maximum(-lam, 0.0) + jnp.log(1.0 + jnp.exp(-jnp.abs(lam)))
    rate = softplus_neg * (-RG_C * 1.4426950408889634)
    for n in range(RNN_SUB):
        cols = slice(n * RNN_BLOCK_DIM, (n + 1) * RNN_BLOCK_DIM)
        x = xs_ref[n]
        x_m1 = jnp.where(row >= n_seq, pltpu.roll(x, n_seq, 0), 0.0)
        x_p1 = jnp.where(row < RNN_ROWS - n_seq, pltpu.roll(x, RNN_ROWS - n_seq, 0), 0.0)
        x_p2 = jnp.where(row < RNN_ROWS - 2 * n_seq, pltpu.roll(x, RNN_ROWS - 2 * n_seq, 0), 0.0)
        xn = (cb_ref[:, cols] + x_m1 * cw_ref[0:1, cols] + x * cw_ref[1:2, cols]
              + x_p1 * cw_ref[2:3, cols] + x_p2 * cw_ref[3:4, cols])
        pre = jnp.dot(xn.astype(BF16), wg_ref[n], preferred_element_type=F32) + bg_ref[n]
        for d, (a_ref, b_ref) in enumerate(((af_ref, bf_ref), (ab_ref, bb_ref))):
            r = 0.5 * jnp.tanh(0.5 * pre[:, (2 * d) * RNN_BLOCK_DIM:(2 * d + 1) * RNN_BLOCK_DIM]) + 0.5
            gate_in = 0.5 * jnp.tanh(
                0.5 * pre[:, (2 * d + 1) * RNN_BLOCK_DIM:(2 * d + 2) * RNN_BLOCK_DIM]) + 0.5
            a = jnp.exp2(r * rate[d:d + 1, cols])
            v = 1.0 - a * a
            a_ref[n] = a
            b_ref[n] = (v * lax.rsqrt(jnp.maximum(v, 1e-30))) * (gate_in * xn)

    def step(t, carry):
        rows_f = pl.ds(pl.multiple_of(t * n_seq, n_seq), n_seq)
        rows_b = pl.ds(pl.multiple_of((seq_len - 1 - t) * n_seq, n_seq), n_seq)
        out = []
        for n in range(RNN_SUB):
            hf = af_ref[n, rows_f, :] * carry[2 * n] + bf_ref[n, rows_f, :]
            hb = ab_ref[n, rows_b, :] * carry[2 * n + 1] + bb_ref[n, rows_b, :]
            bf_ref[n, rows_f, :] = hf
            bb_ref[n, rows_b, :] = hb
            out += [hf, hb]
        return tuple(out)

    init = []
    for n in range(RNN_SUB):
        cols = slice(n * RNN_BLOCK_DIM, (n + 1) * RNN_BLOCK_DIM)
        init += [h0f_ref[:, 0, cols], h0b_ref[:, 0, cols]]
    last = lax.fori_loop(0, seq_len, step, tuple(init), unroll=8)
    for n in range(RNN_SUB):
        cols = slice(n * RNN_BLOCK_DIM, (n + 1) * RNN_BLOCK_DIM)
        hf_ref[:, 0, cols] = last[2 * n]
        hb_ref[:, 0, cols] = last[2 * n + 1]
        bf_ref[n] = bf_ref[n] + bb_ref[n]
        for s in range(n_seq):
            rows = slice(s * seq_len, (s + 1) * seq_len)
            h_sum = bf_ref[n, pl.ds(s, seq_len, stride=n_seq), :]
            o_ref[rows, cols] = (h_sum * _gelu_tanh(yr_ref[rows, cols])).astype(o_ref.dtype)


def rglru_mixer(z, seq_len, conv_w, conv_b, w_gates, b_gates, lam, h0_f, h0_b):
    m = z.shape[0]
    n_seq_total = m // seq_len
    n_seq = RNN_ROWS // seq_len
    cblk = lambda base: (lambda r, c: (r, base // RNN_COLS + c))
    state_spec = pl.BlockSpec((n_seq, 1, RNN_COLS), lambda r, c: (r, 0, c))
    return pl.pallas_call(
        functools.partial(_rglru_body, seq_len),
        grid=(m // RNN_ROWS, D_MODEL // RNN_COLS),
        in_specs=[
            pl.BlockSpec((RNN_ROWS, RNN_COLS), cblk(COL_XR)),
            pl.BlockSpec((RNN_ROWS, RNN_COLS), cblk(COL_YR)),
            pl.BlockSpec((4, RNN_COLS), lambda r, c: (0, c)),
            pl.BlockSpec((1, RNN_COLS), lambda r, c: (0, c)),
            pl.BlockSpec((RNN_SUB, RNN_BLOCK_DIM, 4 * RNN_BLOCK_DIM), lambda r, c: (c, 0, 0)),
            pl.BlockSpec((RNN_SUB, 1, 4 * RNN_BLOCK_DIM), lambda r, c: (c, 0, 0)),
            pl.BlockSpec((2, RNN_COLS), lambda r, c: (0, c)),
            state_spec, state_spec,
        ],
        out_specs=[
            pl.BlockSpec((RNN_ROWS, RNN_COLS), lambda r, c: (r, c)),
            state_spec, state_spec,
        ],
        out_shape=[
            jax.ShapeDtypeStruct((m, D_MODEL), BF16),
            jax.ShapeDtypeStruct((n_seq_total, 1, D_MODEL), F32),
            jax.ShapeDtypeStruct((n_seq_total, 1, D_MODEL), F32),
        ],
        scratch_shapes=[pltpu.VMEM((RNN_SUB, RNN_ROWS, RNN_BLOCK_DIM), F32) for _ in range(5)],
        compiler_params=_params(("arbitrary", "arbitrary")),
        name="rglru_mixer_t%d" % seq_len,
    )(z, z, conv_w, conv_b.reshape(1, D_MODEL), w_gates, b_gates, lam, h0_f, h0_b)


def _merge_body(a_ref, r_ref, wa_ref, wr_ref, ga_ref, gr_ref, o_ref, wa_bf, wr_bf):
    @pl.when(pl.program_id(1) == 0)
    def _():
        wa_bf[...] = wa_ref[...].astype(BF16)
        wr_bf[...] = wr_ref[...].astype(BF16)

    pa = jnp.dot(a_ref[...], wa_bf[...], preferred_element_type=F32)
    pr = jnp.dot(r_ref[...], wr_bf[...], preferred_element_type=F32)
    o_ref[...] = (_sigmoid(ga_ref[...]) * pa + _sigmoid(gr_ref[...]) * pr).astype(o_ref.dtype)


def gated_merge(attn, rnn, z, w_o_attn, w_o_rnn):
    m = attn.shape[0]
    tm, tn = 1024, 512
    return pl.pallas_call(
        _merge_body,
        grid=(D_MODEL // tn, m // tm),
        in_specs=[
            pl.BlockSpec((tm, Q_COLS), lambda j, i: (i, 0)),
            pl.BlockSpec((tm, D_MODEL), lambda j, i: (i, 0)),
            pl.BlockSpec((Q_COLS, tn), lambda j, i: (0, j)),
            pl.BlockSpec((D_MODEL, tn), lambda j, i: (0, j)),
            pl.BlockSpec((tm, tn), lambda j, i: (i, COL_GA // tn + j)),
            pl.BlockSpec((tm, tn), lambda j, i: (i, COL_GR // tn + j)),
        ],
        out_specs=pl.BlockSpec((tm, tn), lambda j, i: (i, j)),
        out_shape=jax.ShapeDtypeStruct((m, D_MODEL), BF16),
        scratch_shapes=[pltpu.VMEM((Q_COLS, tn), BF16), pltpu.VMEM((D_MODEL, tn), BF16)],
        compiler_params=_params(("arbitrary", "arbitrary")),
        name="gated_merge",
    )(attn, rnn, w_o_attn, w_o_rnn, z, z)


POST_TM = 512
HALF_D = D_MODEL // 2
POST_CTX_BLOCKS = N_CTX // POST_TM
LAT_BLOCKS_PER_SEQ = LAT_LEN // POST_TM


def _post_group(i):
    return jnp.where(i < POST_CTX_BLOCKS, 0, 1 + (i - POST_CTX_BLOCKS) // LAT_BLOCKS_PER_SEQ)


def _postmix_body(mc_ref, ml_ref, xc_ref, xl_ref, wo_ref, gpm_ref, gt1_ref, gpf_ref, sh2_ref, sc2_ref,
                  wr_ref, br_ref, x1_ref, h2_ref, e_ref, gate_ref, rank_ref, cnt_ref, carry_ref):
    i = pl.program_id(0)
    tm = POST_TM

    @pl.when(i == 0)
    def _():
        carry_ref[...] = jnp.zeros_like(carry_ref)

    is_ctx = i < POST_CTX_BLOCKS
    merged = jnp.where(is_ctx, mc_ref[...], ml_ref[...])
    x = jnp.where(is_ctx, xc_ref[...], xl_ref[...])
    o = jnp.dot(merged, wo_ref[...], preferred_element_type=F32)
    x1 = x + gt1_ref[...] * (o * _rms_scale(o) * gpm_ref[...])
    x1_ref[...] = x1
    h2 = (x1 * _rms_scale(x1) * gpf_ref[...]) * (1.0 + sc2_ref[...]) + sh2_ref[...]
    h2_bf = h2.astype(BF16)
    bits = lax.bitcast_convert_type(h2_bf.astype(F32), jnp.uint32)
    h2_ref[...] = (lax.shift_right_logical(bits[:, :HALF_D], jnp.uint32(16))
                   | (bits[:, HALF_D:] & jnp.uint32(0xFFFF0000)))

    logits = jnp.dot(h2_bf, wr_ref[...].astype(BF16), preferred_element_type=F32) + br_ref[...]
    lane = lax.broadcasted_iota(jnp.int32, (tm, N_EXPERTS), 1)
    work = logits
    chosen = jnp.zeros((tm, N_EXPERTS), F32)
    sels, vals, idxs = [], [], []
    for _ in range(TOP_K):
        mx = jnp.max(work, axis=-1, keepdims=True)
        idx = jnp.min(jnp.where(work == mx, lane, N_EXPERTS), axis=-1, keepdims=True)
        sel = lane == idx
        work = jnp.where(sel, -jnp.inf, work)
        chosen = jnp.where(sel, 1.0, chosen)
        sels.append(sel)
        vals.append(mx)
        idxs.append(idx)
    exps = [jnp.exp(v - vals[0]) for v in vals]
    inv = 1.0 / (exps[0] + exps[1] + exps[2] + exps[3])

    r_io = lax.broadcasted_iota(jnp.int32, (tm, tm), 0)
    c_io = lax.broadcasted_iota(jnp.int32, (tm, tm), 1)
    lower = jnp.where(c_io < r_io, 1.0, 0.0).astype(BF16)
    before = jnp.dot(lower, chosen.astype(BF16), preferred_element_type=F32) + carry_ref[...]
    carry_ref[...] = carry_ref[...] + jnp.sum(chosen, axis=0, keepdims=True)
    cnt_ref[...] = carry_ref[...]

    lane_k = lax.broadcasted_iota(jnp.int32, (tm, TOP_K), 1)
    e_out = jnp.zeros((tm, TOP_K), jnp.int32)
    g_out = jnp.zeros((tm, TOP_K), F32)
    r_out = jnp.zeros((tm, TOP_K), jnp.int32)
    for k in range(TOP_K):
        rk = jnp.sum(jnp.where(sels[k], before, 0.0), axis=-1, keepdims=True).astype(jnp.int32)
        e_out = jnp.where(lane_k == k, idxs[k], e_out)
        g_out = jnp.where(lane_k == k, exps[k] * inv, g_out)
        r_out = jnp.where(lane_k == k, rk, r_out)
    e_ref[...] = e_out
    gate_ref[...] = g_out
    rank_ref[...] = r_out


def post_mix_router(merged_ctx, merged_lat, x_ctx, x_lat, w_out_bf, g_post_mix, gt1, g_pre_ffn, sh2, sc2,
                    w_router, b_router):
    tm = POST_TM
    ctx_map = lambda i: (jnp.minimum(i, POST_CTX_BLOCKS - 1), 0)
    lat_map = lambda i: (jnp.maximum(i - POST_CTX_BLOCKS, 0), 0)
    gmap = lambda i: (_post_group(i), 0, 0)
    row = lambda i: (i, 0)
    const = lambda i: (0, 0)
    vec = pl.BlockSpec((1, D_MODEL), const)
    gvec = pl.BlockSpec((None, 1, D_MODEL), gmap)
    return pl.pallas_call(
        _postmix_body,
        grid=(N_TOK // tm,),
        in_specs=[
            pl.BlockSpec((tm, D_MODEL), ctx_map),
            pl.BlockSpec((tm, D_MODEL), lat_map),
            pl.BlockSpec((tm, D_MODEL), ctx_map),
            pl.BlockSpec((tm, D_MODEL), lat_map),
            pl.BlockSpec((D_MODEL, D_MODEL), const),
            vec, gvec, vec, gvec, gvec,
            pl.BlockSpec((D_MODEL, N_EXPERTS), const),
            pl.BlockSpec((1, N_EXPERTS), const),
        ],
        out_specs=[
            pl.BlockSpec((tm, D_MODEL), row),
            pl.BlockSpec((tm, HALF_D), row),
            pl.BlockSpec((tm, TOP_K), row),
            pl.BlockSpec((tm, TOP_K), row),
            pl.BlockSpec((tm, TOP_K), row),
            pl.BlockSpec((1, N_EXPERTS), const),
        ],
        out_shape=[
            jax.ShapeDtypeStruct((N_TOK, D_MODEL), F32),
            jax.ShapeDtypeStruct((N_TOK, HALF_D), jnp.uint32),
            jax.ShapeDtypeStruct((N_TOK, TOP_K), jnp.int32),
            jax.ShapeDtypeStruct((N_TOK, TOP_K), F32),
            jax.ShapeDtypeStruct((N_TOK, TOP_K), jnp.int32),
            jax.ShapeDtypeStruct((1, N_EXPERTS), F32),
        ],
        scratch_shapes=[pltpu.VMEM((1, N_EXPERTS), F32)],
        compiler_params=_params(("arbitrary",)),
        name="post_mix_router",
    )(merged_ctx, merged_lat, x_ctx, x_lat, w_out_bf, g_post_mix.reshape(1, D_MODEL), gt1,
      g_pre_ffn.reshape(1, D_MODEL), sh2, sc2, w_router, b_router.reshape(1, N_EXPERTS))


GATHER_ROWS = ROW_TILE // N_FF_CHUNKS


def _unpack_rows(words):
    lo = lax.bitcast_convert_type(lax.shift_left(words, jnp.uint32(16)), F32).astype(BF16)
    hi = lax.bitcast_convert_type(words & jnp.uint32(0xFFFF0000), F32).astype(BF16)
    return jnp.concatenate([lo, hi], axis=1)


def _moe_body(exp_ref, row_ref, nsub_ref, nzero_ref, tok_ref, h_ref, wg_ref, wl_ref, wd_ref, bg_ref, bl_ref,
              bd_ref, y_ref, xbuf_ref, act_ref, wg_bf, wl_bf, wd_bf, stage_ref, idx_ref, pend_ref,
              xsem, isem, ysem):
    s = pl.program_id(0)
    j = pl.program_id(1)
    n_sub = nsub_ref[s]
    row_start = row_ref[s]
    n_next = nsub_ref[jnp.minimum(s + 1, N_SUPER - 1)]

    def idx_copy(p):
        tile0 = pl.multiple_of(row_ref[p], ROW_TILE) // ROW_TILE
        return pltpu.make_async_copy(tok_ref.at[pl.ds(tile0, SUPER_TILES)], idx_ref.at[p % 2],
                                     isem.at[p % 2])

    def gather_row(slot, tile, col, r):
        t = idx_ref[slot, tile, 0, col]
        pltpu.make_async_copy(h_ref.at[pl.ds(t, 1), :], xbuf_ref.at[pl.ds(r, 1), :], xsem).start()

    def gather_range(p, first, last):
        def issue(r, _):
            gather_row(p % 2, lax.shift_right_logical(r, 8), jnp.bitwise_and(r, ROW_TILE - 1), r)
            return 0

        lax.fori_loop(first, last, issue, 0)

    def wait_rows(count):
        @pl.when(count > 0)
        def _():
            pltpu.make_async_copy(h_ref.at[pl.ds(0, count), :], xbuf_ref.at[pl.ds(0, count), :], xsem).wait()

    @pl.when(jnp.logical_and(s == 0, j == 0))
    def _():
        pend_ref[0] = 0
        pend_ref[1] = 0

    def drain_stage(slot):
        @pl.when(pend_ref[slot] == 1)
        def _():
            pltpu.make_async_copy(stage_ref.at[slot], stage_ref.at[slot], ysem.at[slot]).wait()
            pend_ref[slot] = 0

    @pl.when(jnp.logical_and(s == 0, j == 0))
    def _():
        idx_copy(0).start()
        idx_copy(0).wait()
        gather_range(0, 0, n_sub * ROW_TILE)
        wait_rows(n_sub * ROW_TILE)

    @pl.when(jnp.logical_and(s > 0, j == 0))
    def _():
        wait_rows(jnp.maximum(nsub_ref[jnp.maximum(s - 1, 0)], n_sub) * ROW_TILE)

    @pl.when(jnp.logical_and(j == 0, s + 1 < N_SUPER))
    def _():
        idx_copy(s + 1).start()

    @pl.when(jnp.logical_and(j == N_FF_CHUNKS, s + 1 < N_SUPER))
    def _():
        idx_copy(s + 1).wait()

        @pl.when(n_sub == 0)
        def _():
            gather_range(s + 1, 0, n_next * ROW_TILE)

    @pl.when(jnp.logical_and(j < N_FF_CHUNKS, n_sub > 0))
    def _():
        wg_bf[...] = wg_ref[...].astype(BF16)
        wl_bf[...] = wl_ref[...].astype(BF16)
        bg = bg_ref[...]
        bl = bl_ref[...]

        def sub(i, _):
            rows = pl.ds(pl.multiple_of(i * ROW_TILE, ROW_TILE), ROW_TILE)
            xt = _unpack_rows(xbuf_ref[rows, :])
            glu = jnp.minimum(jnp.dot(xt, wg_bf[...], preferred_element_type=F32) + bg, SWIGLU_LIMIT)
            lin = jnp.clip(jnp.dot(xt, wl_bf[...], preferred_element_type=F32) + bl,
                           -SWIGLU_LIMIT, SWIGLU_LIMIT)
            act = glu * _sigmoid(SWIGLU_ALPHA * glu) * (lin + 1.0)
            act_ref[j, rows, :] = act.astype(BF16)
            return 0

        lax.fori_loop(0, n_sub, sub, 0)

    for cc in range(N_FF_CHUNKS):
        @pl.when(jnp.logical_and(j == N_FF_CHUNKS + cc, n_sub > 0))
        def _(cc=cc):
            wd_bf[...] = wd_ref[...].astype(BF16)
            bd = bd_ref[...]
            next_slot = (s + 1) % 2

            def out_copy(i, slot):
                dst = y_ref.at[pl.ds(pl.multiple_of(row_start + i * ROW_TILE, ROW_TILE), ROW_TILE),
                               cc * FF_CHUNK:(cc + 1) * FF_CHUNK]
                return pltpu.make_async_copy(stage_ref.at[slot], dst, ysem.at[slot])

            def sub(i, _):
                rows = pl.ds(pl.multiple_of(i * ROW_TILE, ROW_TILE), ROW_TILE)
                slot = i % 2
                acc = bd
                for c in range(N_FF_CHUNKS):
                    acc = acc + jnp.dot(act_ref[c, rows, :], wd_bf[c * FF_CHUNK:(c + 1) * FF_CHUNK, :],
                                        preferred_element_type=F32)
                first = (cc * n_sub + i) * GATHER_ROWS
                tile = lax.shift_right_logical(first, 8)
                col = jnp.bitwise_and(first, ROW_TILE - 1)
                for g in range(GATHER_ROWS):
                    gather_row(next_slot, tile, col + g, first + g)

                drain_stage(slot)
                stage_ref[slot] = acc
                out_copy(i, slot).start()
                pend_ref[slot] = 1
                return 0

            lax.fori_loop(0, n_sub, sub, 0)
            if cc == N_FF_CHUNKS - 1:
                gather_range(s + 1, n_sub * ROW_TILE, n_next * ROW_TILE)

    n_zero = nzero_ref[s]

    @pl.when(jnp.logical_and(j == 0, n_zero > 0))
    def _():
        drain_stage(0)
        stage_ref[0] = jnp.zeros((ROW_TILE, FF_CHUNK), F32)

        def zero_copy(i, cc):
            dst = y_ref.at[pl.ds(pl.multiple_of(row_start + i * ROW_TILE, ROW_TILE), ROW_TILE),
                           cc * FF_CHUNK:(cc + 1) * FF_CHUNK]
            return pltpu.make_async_copy(stage_ref.at[0], dst, ysem.at[0])

        def issue(i, _):
            for cc in range(N_FF_CHUNKS):
                zero_copy(i, cc).start()
            return 0

        def drain(i, _):
            for cc in range(N_FF_CHUNKS):
                zero_copy(i, cc).wait()
            return 0

        lax.fori_loop(0, n_zero, issue, 0)
        lax.fori_loop(0, n_zero, drain, 0)

    @pl.when(jnp.logical_and(s == N_SUPER - 1, j == 2 * N_FF_CHUNKS - 1))
    def _():
        drain_stage(0)
        drain_stage(1)


def expert_mlp(h_packed, tok_sorted, sched, w_gate_up, b_gate_up, w_down, b_down):
    exp_of, row_of, nsub_of, nzero_of = sched
    last = N_FF_CHUNKS - 1
    up_of = lambda s, j, n: jnp.where(n[s] > 0, jnp.minimum(j, last), last)
    down_of = lambda s, j, n: jnp.where(n[s] > 0, jnp.maximum(j - N_FF_CHUNKS, 0), last)
    up_chunk = lambda s, j, e, r, n, z: (e[s], 0, up_of(s, j, n))
    lin_chunk = lambda s, j, e, r, n, z: (e[s], 0, N_FF_CHUNKS + up_of(s, j, n))
    down_chunk = lambda s, j, e, r, n, z: (e[s], 0, down_of(s, j, n))
    grid_spec = pltpu.PrefetchScalarGridSpec(
        num_scalar_prefetch=4,
        grid=(N_SUPER, 2 * N_FF_CHUNKS),
        in_specs=[
            pl.BlockSpec(memory_space=pl.ANY),
            pl.BlockSpec(memory_space=pl.ANY),
            pl.BlockSpec((None, D_MODEL, FF_CHUNK), up_chunk),
            pl.BlockSpec((None, D_MODEL, FF_CHUNK), lin_chunk),
            pl.BlockSpec((None, D_FF, FF_CHUNK), down_chunk),
            pl.BlockSpec((None, 1, FF_CHUNK), up_chunk),
            pl.BlockSpec((None, 1, FF_CHUNK), lin_chunk),
            pl.BlockSpec((None, 1, FF_CHUNK), down_chunk),
        ],
        out_specs=pl.BlockSpec(memory_space=pl.ANY),
        scratch_shapes=[
            pltpu.VMEM((SUPER_ROWS, HALF_D), jnp.uint32),
            pltpu.VMEM((N_FF_CHUNKS, SUPER_ROWS, FF_CHUNK), BF16),
            pltpu.VMEM((D_MODEL, FF_CHUNK), BF16),
            pltpu.VMEM((D_MODEL, FF_CHUNK), BF16),
            pltpu.VMEM((D_FF, FF_CHUNK), BF16),
            pltpu.VMEM((2, ROW_TILE, FF_CHUNK), F32),
            pltpu.SMEM((2, SUPER_TILES, 1, ROW_TILE), jnp.int32),
            pltpu.SMEM((2,), jnp.int32),
            pltpu.SemaphoreType.DMA(()),
            pltpu.SemaphoreType.DMA((2,)),
            pltpu.SemaphoreType.DMA((2,)),
        ],
    )
    tok_tiles = jnp.concatenate([tok_sorted.reshape(N_ROW_TILES, 1, ROW_TILE),
                                 jnp.zeros((SUPER_TILES, 1, ROW_TILE), jnp.int32)], axis=0)
    return pl.pallas_call(
        _moe_body,
        grid_spec=grid_spec,
        out_shape=jax.ShapeDtypeStruct((N_ROWS, D_MODEL), F32),
        compiler_params=_params(("arbitrary", "arbitrary"), vmem=58 * 1024 * 1024),
        name="expert_mlp",
    )(exp_of, row_of, nsub_of, nzero_of, tok_tiles, h_packed, w_gate_up, w_gate_up, w_down,
      b_gate_up.reshape(N_EXPERTS, 1, 2 * D_FF), b_gate_up.reshape(N_EXPERTS, 1, 2 * D_FF),
      b_down.reshape(N_EXPERTS, 1, D_MODEL))


COMB_TB = 256


def _combine_start(y_ref, ybuf_ref, pos_ref, sem):
    def issue(t, _):
        for k in range(TOP_K):
            p = pos_ref[0, 0, t * TOP_K + k]
            pltpu.make_async_copy(y_ref.at[pl.ds(p, 1), :], ybuf_ref.at[k, pl.ds(t, 1), :], sem).start()
        return 0

    lax.fori_loop(0, COMB_TB, issue, 0, unroll=4)


def _combine_body(n, pos_ref, pos_next_ref, y_ref, gate_ref, x1_ref, gt2_ref, g_ref, o_ref, ybuf_ref, sem_ref):
    i = pl.program_id(0)
    slot = i % 2

    @pl.when(i == 0)
    def _():
        _combine_start(y_ref, ybuf_ref.at[0], pos_ref, sem_ref.at[0])

    @pl.when(i + 1 < n)
    def _():
        _combine_start(y_ref, ybuf_ref.at[1 - slot], pos_next_ref, sem_ref.at[1 - slot])

    for k in range(TOP_K):
        pltpu.make_async_copy(y_ref.at[pl.ds(0, COMB_TB), :], ybuf_ref.at[slot, k], sem_ref.at[slot]).wait()
    gates = gate_ref[...]
    ffn = gates[:, 0:1] * ybuf_ref[slot, 0]
    for k in range(1, TOP_K):
        ffn = ffn + gates[:, k:k + 1] * ybuf_ref[slot, k]
    o_ref[...] = x1_ref[...] + gt2_ref[...] * (ffn * _rms_scale(ffn) * g_ref[...])


def combine_residual(y_sorted, pos, gates, x1, gt2, g_post_ffn, row_offset, n_rows, group_of_block):
    tb = COMB_TB
    nblk = n_rows // tb
    off = row_offset // tb
    pos3 = pos.reshape(N_TOK // tb, 1, tb * TOP_K)
    smem_blk = lambda f: pl.BlockSpec((1, 1, tb * TOP_K), f, memory_space=pltpu.SMEM)
    return pl.pallas_call(
        functools.partial(_combine_body, nblk),
        grid=(nblk,),
        in_specs=[
            smem_blk(lambda i: (off + i, 0, 0)),
            smem_blk(lambda i: (off + jnp.minimum(i + 1, nblk - 1), 0, 0)),
            pl.BlockSpec(memory_space=pl.ANY),
            pl.BlockSpec((tb, TOP_K), lambda i: (off + i, 0)),
            pl.BlockSpec((tb, D_MODEL), lambda i: (off + i, 0)),
            pl.BlockSpec((None, 1, D_MODEL), lambda i: (group_of_block(i), 0, 0)),
            pl.BlockSpec((1, D_MODEL), lambda i: (0, 0)),
        ],
        out_specs=pl.BlockSpec((tb, D_MODEL), lambda i: (i, 0)),
        out_shape=jax.ShapeDtypeStruct((n_rows, D_MODEL), F32),
        scratch_shapes=[pltpu.VMEM((2, TOP_K, tb, D_MODEL), F32), pltpu.SemaphoreType.DMA((2,))],
        compiler_params=_params(("arbitrary",)),
        name="combine_residual",
    )(pos3, pos3, y_sorted, gates, x1, gt2, g_post_ffn.reshape(1, D_MODEL))


def _routing_tables(e_idx, rank, counts_f):
    counts = counts_f.reshape(N_EXPERTS).astype(jnp.int32)
    n_tiles = (counts + ROW_TILE - 1) // ROW_TILE
    padded = n_tiles * ROW_TILE
    pad_end = jnp.cumsum(padded)
    pad_start = pad_end - padded
    pos = pad_start[e_idx] + rank
    tok_ids = jnp.repeat(jnp.arange(N_TOK, dtype=jnp.int32), TOP_K)
    tok_sorted = jnp.zeros((N_ROWS,), jnp.int32).at[pos.reshape(-1)].set(
        tok_ids, unique_indices=True, mode="drop")
    n_pass = (n_tiles + SUPER_TILES - 1) // SUPER_TILES
    pass_end = jnp.cumsum(n_pass)
    total = pass_end[-1]
    s = jnp.arange(N_SUPER, dtype=jnp.int32)
    s_eff = jnp.minimum(s, total - 1)
    e_of = jnp.minimum(jnp.searchsorted(pass_end, s_eff, side="right"), N_EXPERTS - 1).astype(jnp.int32)
    local = s_eff - (pass_end[e_of] - n_pass[e_of])
    row_of = pad_start[e_of] + local * SUPER_ROWS
    nsub = jnp.minimum(SUPER_TILES, n_tiles[e_of] - local * SUPER_TILES)
    nsub = jnp.where(s < total, nsub, 0).astype(jnp.int32)
    zero_row = pad_end[-1] + (s - total) * SUPER_ROWS
    nzero = jnp.clip((N_ROWS - zero_row) // ROW_TILE, 0, SUPER_TILES)
    nzero = jnp.where(s >= total, nzero, 0).astype(jnp.int32)
    row_of = jnp.where(s < total, row_of, jnp.minimum(zero_row, N_ROWS - ROW_TILE)).astype(jnp.int32)
    return pos.astype(jnp.int32), tok_sorted, (e_of, row_of, nsub, nzero)


def kernel(x_prompt, x_sample, cache_k, cache_v, state_rnn_fwd, state_rnn_bwd, c, c_ctx, w_mod, b_mod, g_pre_mix, w_in, g_q_norm, g_k_norm, conv_w, conv_b, rg_w_a, rg_b_a, rg_w_x, rg_b_x, rg_lambda, w_o_attn, w_o_rnn, w_out, g_post_mix, g_pre_ffn, w_router, b_router, w_gate_up, b_gate_up, w_down, b_down, g_post_ffn):
    l = 0
    x_ctx = x_prompt.reshape(N_CTX, D_MODEL)
    x_lat = x_sample.reshape(N_LAT, D_MODEL)

    cond8 = jnp.concatenate([c_ctx[None, :], c, jnp.zeros((8 - 1 - N_LAT_SEQ, D_MODEL), F32)], axis=0)
    mod = modulation(cond8, w_mod[l], b_mod[l])[:1 + N_LAT_SEQ].reshape(1 + N_LAT_SEQ, 6, 1, D_MODEL)
    sh1, sc1, gt1, sh2, sc2, gt2 = [mod[:, i] for i in range(6)]

    ctx_group = lambda i: 0
    lat_group_1024 = lambda i: 1 + i
    h_ctx = prenorm_modulate(x_ctx, g_pre_mix[l], sh1, sc1, ctx_group, 1024)
    h_lat = prenorm_modulate(x_lat, g_pre_mix[l], sh1, sc1, lat_group_1024, 1024)
    z_ctx = in_projection(h_ctx, w_in[l])
    z_lat = in_projection(h_lat, w_in[l])

    attn_ctx, k_new, v_new = attention_ctx(z_ctx, g_q_norm[l], g_k_norm[l])
    attn_lat = attention_lat(z_lat, cache_k[:, l].reshape(N_LAT_SEQ, PAST_LEN, KV_COLS),
                             cache_v[:, l].reshape(N_LAT_SEQ, PAST_LEN, KV_COLS),
                             _rope_tables(), g_q_norm[l], g_k_norm[l])

    def per_block(w):
        return w.reshape(2, RNN_BLOCKS, 1, RNN_BLOCK_DIM)

    w_gates = jnp.concatenate([rg_w_a[l, 0], rg_w_x[l, 0], rg_w_a[l, 1], rg_w_x[l, 1]], axis=-1).astype(BF16)
    ba, bx = per_block(rg_b_a[l]), per_block(rg_b_x[l])
    b_gates = jnp.concatenate([ba[0], bx[0], ba[1], bx[1]], axis=-1)
    zeros_state = jnp.zeros((N_CTX_SEQ, 1, D_MODEL), F32)
    rnn_ctx, hf_ctx, hb_ctx = rglru_mixer(z_ctx, CTX_LEN, conv_w[l], conv_b[l], w_gates, b_gates,
                                          rg_lambda[l], zeros_state, zeros_state)
    rnn_lat, _, _ = rglru_mixer(z_lat, LAT_LEN, conv_w[l], conv_b[l], w_gates, b_gates, rg_lambda[l],
                                state_rnn_fwd[:, l].reshape(N_LAT_SEQ, 1, D_MODEL),
                                state_rnn_bwd[:, l].reshape(N_LAT_SEQ, 1, D_MODEL))

    merged_ctx = gated_merge(attn_ctx, rnn_ctx, z_ctx, w_o_attn[l], w_o_rnn[l])
    merged_lat = gated_merge(attn_lat, rnn_lat, z_lat, w_o_attn[l], w_o_rnn[l])

    x1, h2, e_idx, gates, rank, counts = post_mix_router(
        merged_ctx, merged_lat, x_ctx, x_lat, w_out[l].astype(BF16), g_post_mix[l], gt1, g_pre_ffn[l],
        sh2, sc2, w_router[l], b_router[l])

    pos, tok_sorted, sched = _routing_tables(e_idx, rank, counts)
    y_sorted = expert_mlp(h2, tok_sorted, sched, w_gate_up[l], b_gate_up[l], w_down[l], b_down[l])

    y_ctx = combine_residual(y_sorted, pos, gates, x1, gt2, g_post_ffn[l], 0, N_CTX, ctx_group)
    y_lat = combine_residual(y_sorted, pos, gates, x1, gt2, g_post_ffn[l], N_CTX, N_LAT,
                             lambda i: 1 + i // (LAT_LEN // COMB_TB))

    return (y_ctx.reshape(N_CTX_SEQ, CTX_LEN, D_MODEL),
            y_lat.reshape(N_LAT_SEQ, LAT_LEN, D_MODEL),
            k_new.reshape(N_CTX_SEQ, 1, CTX_LEN, N_KV_HEADS, HEAD_DIM),
            v_new.reshape(N_CTX_SEQ, 1, CTX_LEN, N_KV_HEADS, HEAD_DIM),
            hf_ctx,
            hb_ctx)
```

```python
import functools

import jax
import jax.numpy as jnp
import numpy as np
from jax import lax
from jax.experimental import pallas as pl
from jax.experimental.pallas import tpu as pltpu

D_MODEL = 2048
N_CTX_SEQ = 32
CTX_LEN = 256
N_LAT_SEQ = 2
LAT_LEN = 1024
PAST_LEN = 512
N_CTX = N_CTX_SEQ * CTX_LEN
N_LAT = N_LAT_SEQ * LAT_LEN
N_TOK = N_CTX + N_LAT
GRID_W = 64
N_HEADS = 16
N_KV_HEADS = 4
HEAD_DIM = 128
KV_GROUP = N_HEADS // N_KV_HEADS
ROPE_THETA = 10000.0
RNN_BLOCKS = 16
RNN_BLOCK_DIM = 128
RG_C = 8.0
N_EXPERTS = 32
TOP_K = 4
D_FF = 2048
SWIGLU_LIMIT = 7.0
SWIGLU_ALPHA = 1.702
EPS = 1e-6
Q_COLS = N_HEADS * HEAD_DIM
KV_COLS = N_KV_HEADS * HEAD_DIM
IN_COLS = Q_COLS + 2 * KV_COLS + 4 * D_MODEL
COL_K = Q_COLS
COL_XR = Q_COLS + 2 * KV_COLS
COL_YR = COL_XR + D_MODEL
COL_GA = COL_YR + D_MODEL
COL_GR = COL_GA + D_MODEL

V7X_VMEM_BYTES = 64 * 1024 * 1024
VMEM_LIMIT = 56 * 1024 * 1024

ROW_TILE = 256
SUPER_TILES = 8
SUPER_ROWS = ROW_TILE * SUPER_TILES
N_ASSIGN = N_TOK * TOP_K
N_ROWS = N_ASSIGN + N_EXPERTS * ROW_TILE
N_ROW_TILES = N_ROWS // ROW_TILE
N_SUPER = N_ROW_TILES // SUPER_TILES + N_EXPERTS
FF_CHUNK = 512
N_FF_CHUNKS = D_FF // FF_CHUNK

BF16 = jnp.bfloat16
F32 = jnp.float32


def _params(semantics, vmem=VMEM_LIMIT):
    return pltpu.CompilerParams(dimension_semantics=semantics, vmem_limit_bytes=vmem)


def _rms_scale(x):
    return lax.rsqrt(jnp.mean(x * x, axis=-1, keepdims=True) + EPS)


def _sigmoid(x):
    return 1.0 / (1.0 + jnp.exp(-x))


def _mod_body(c_ref, w_ref, b_ref, o_ref):
    c = c_ref[...]
    a = (c * _sigmoid(c)).astype(BF16)
    o_ref[...] = jnp.dot(a, w_ref[...].astype(BF16), preferred_element_type=F32) + b_ref[...]


def modulation(cond8, w_mod, b_mod):
    tn = 1024
    n = w_mod.shape[1]
    return pl.pallas_call(
        _mod_body,
        grid=(n // tn,),
        in_specs=[
            pl.BlockSpec((8, D_MODEL), lambda j: (0, 0)),
            pl.BlockSpec((D_MODEL, tn), lambda j: (0, j)),
            pl.BlockSpec((1, tn), lambda j: (0, j)),
        ],
        out_specs=pl.BlockSpec((8, tn), lambda j: (0, j)),
        out_shape=jax.ShapeDtypeStruct((8, n), F32),
        compiler_params=_params(("arbitrary",)),
        name="modulation",
    )(cond8, w_mod, b_mod.reshape(1, n))


def _prenorm_body(x_ref, g_ref, sh_ref, sc_ref, o_ref):
    x = x_ref[...]
    y = x * _rms_scale(x) * g_ref[...]
    o_ref[...] = (y * (1.0 + sc_ref[...]) + sh_ref[...]).astype(o_ref.dtype)


def prenorm_modulate(x, g, shift, scale, group_of_block, tm):
    m = x.shape[0]
    gmap = lambda i: (group_of_block(i), 0, 0)
    return pl.pallas_call(
        _prenorm_body,
        grid=(m // tm,),
        in_specs=[
            pl.BlockSpec((tm, D_MODEL), lambda i: (i, 0)),
            pl.BlockSpec((1, D_MODEL), lambda i: (0, 0)),
            pl.BlockSpec((None, 1, D_MODEL), gmap),
            pl.BlockSpec((None, 1, D_MODEL), gmap),
        ],
        out_specs=pl.BlockSpec((tm, D_MODEL), lambda i: (i, 0)),
        out_shape=jax.ShapeDtypeStruct((m, D_MODEL), BF16),
        compiler_params=_params(("arbitrary",)),
        name="prenorm_modulate",
    )(x, g.reshape(1, D_MODEL), shift, scale)


def _inproj_body(h_ref, w_ref, o_ref, wbf_ref):
    @pl.when(pl.program_id(1) == 0)
    def _():
        wbf_ref[...] = w_ref[...].astype(BF16)

    o_ref[...] = jnp.dot(h_ref[...], wbf_ref[...], preferred_element_type=F32)


def in_projection(h, w_in):
    m = h.shape[0]
    tm, tn = 1024, 1024
    return pl.pallas_call(
        _inproj_body,
        grid=(IN_COLS // tn, m // tm),
        in_specs=[
            pl.BlockSpec((tm, D_MODEL), lambda j, i: (i, 0)),
            pl.BlockSpec((D_MODEL, tn), lambda j, i: (0, j)),
        ],
        out_specs=pl.BlockSpec((tm, tn), lambda j, i: (i, j)),
        out_shape=jax.ShapeDtypeStruct((m, IN_COLS), F32),
        scratch_shapes=[pltpu.VMEM((D_MODEL, tn), BF16)],
        compiler_params=_params(("arbitrary", "arbitrary")),
        name="in_projection",
    )(h, w_in)


def _rope(x, cos, sin_lo, sin_hi):
    return x * cos + pltpu.roll(x, 96, 1) * sin_lo + pltpu.roll(x, 32, 1) * sin_hi


def _head_norm(x, g):
    return x * _rms_scale(x) * g


def _softmax_pv(score_blocks, value_blocks):
    m = None
    for s in score_blocks:
        mi = jnp.max(s, axis=-1, keepdims=True)
        m = mi if m is None else jnp.maximum(m, mi)
    ps = [jnp.exp(s - m) for s in score_blocks]
    denom = None
    for p in ps:
        li = jnp.sum(p, axis=-1, keepdims=True)
        denom = li if denom is None else denom + li
    inv = 1.0 / denom
    out = None
    for p, v in zip(ps, value_blocks):
        o = jnp.dot((p * inv).astype(BF16), v, preferred_element_type=F32)
        out = o if out is None else out + o
    return out


def _attn_ctx_body(q_ref, kv_ref, gq_ref, gk_ref, o_ref, ko_ref, vo_ref):
    tq = q_ref.shape[0]
    scale = HEAD_DIM ** -0.5
    gq = gq_ref[...]
    gk = gk_ref[...]
    for g in range(N_KV_HEADS):
        kcols = slice(g * HEAD_DIM, (g + 1) * HEAD_DIM)
        kn = _head_norm(kv_ref[:, kcols], gk)
        v = kv_ref[:, KV_COLS + g * HEAD_DIM:KV_COLS + (g + 1) * HEAD_DIM]
        ko_ref[pl.ds(g, tq, stride=N_KV_HEADS), :] = kn
        vo_ref[pl.ds(g, tq, stride=N_KV_HEADS), :] = v
        qs = []
        for hh in range(KV_GROUP):
            h = g * KV_GROUP + hh
            qs.append(_head_norm(q_ref[:, h * HEAD_DIM:(h + 1) * HEAD_DIM], gq).astype(BF16))
        q4 = jnp.concatenate(qs, axis=0)
        s = lax.dot_general(q4, kn.astype(BF16), (((1,), (1,)), ((), ())),
                            preferred_element_type=F32) * scale
        o = _softmax_pv([s], [v.astype(BF16)])
        for hh in range(KV_GROUP):
            h = g * KV_GROUP + hh
            o_ref[:, h * HEAD_DIM:(h + 1) * HEAD_DIM] = o[hh * tq:(hh + 1) * tq].astype(o_ref.dtype)


def attention_ctx(z, g_q, g_k):
    nb = N_CTX_SEQ
    t = CTX_LEN
    return pl.pallas_call(
        _attn_ctx_body,
        grid=(nb,),
        in_specs=[
            pl.BlockSpec((t, Q_COLS), lambda b: (b, 0)),
            pl.BlockSpec((t, 2 * KV_COLS), lambda b: (b, COL_K // (2 * KV_COLS))),
            pl.BlockSpec((1, HEAD_DIM), lambda b: (0, 0)),
            pl.BlockSpec((1, HEAD_DIM), lambda b: (0, 0)),
        ],
        out_specs=[
            pl.BlockSpec((t, Q_COLS), lambda b: (b, 0)),
            pl.BlockSpec((t * N_KV_HEADS, HEAD_DIM), lambda b: (b, 0)),
            pl.BlockSpec((t * N_KV_HEADS, HEAD_DIM), lambda b: (b, 0)),
        ],
        out_shape=[
            jax.ShapeDtypeStruct((N_CTX, Q_COLS), BF16),
            jax.ShapeDtypeStruct((N_CTX * N_KV_HEADS, HEAD_DIM), F32),
            jax.ShapeDtypeStruct((N_CTX * N_KV_HEADS, HEAD_DIM), F32),
        ],
        compiler_params=_params(("arbitrary",)),
        name="attention_ctx",
    )(z, z, g_q.reshape(1, HEAD_DIM), g_k.reshape(1, HEAD_DIM))


def _attn_lat_body(q_ref, kv_ref, ck_ref, cv_ref, cos_ref, slo_ref, shi_ref, gq_ref, gk_ref,
                   o_ref, kr_ref):
    tq = q_ref.shape[0]
    qb = pl.program_id(1)
    scale = HEAD_DIM ** -0.5
    gq = gq_ref[...]

    @pl.when(qb == 0)
    def _():
        gk = gk_ref[...]
        for g in range(N_KV_HEADS):
            kcols = slice(g * HEAD_DIM, (g + 1) * HEAD_DIM)
            kn = _head_norm(kv_ref[:, kcols], gk)
            kr_ref[:, kcols] = _rope(kn, cos_ref[...], slo_ref[...], shi_ref[...]).astype(BF16)

    row0 = pl.multiple_of(qb * tq, tq)
    cos = cos_ref[pl.ds(row0, tq), :]
    slo = slo_ref[pl.ds(row0, tq), :]
    shi = shi_ref[pl.ds(row0, tq), :]
    for g in range(N_KV_HEADS):
        kcols = slice(g * HEAD_DIM, (g + 1) * HEAD_DIM)
        qs = []
        for hh in range(KV_GROUP):
            h = g * KV_GROUP + hh
            qn = _head_norm(q_ref[:, h * HEAD_DIM:(h + 1) * HEAD_DIM], gq)
            qs.append(_rope(qn, cos, slo, shi).astype(BF16))
        q4 = jnp.concatenate(qs, axis=0)
        dn = (((1,), (1,)), ((), ()))
        s_past = lax.dot_general(q4, ck_ref[:, kcols].astype(BF16), dn,
                                 preferred_element_type=F32) * scale
        s_new = lax.dot_general(q4, kr_ref[:, kcols], dn, preferred_element_type=F32) * scale
        v_past = cv_ref[:, kcols].astype(BF16)
        v_new = kv_ref[:, KV_COLS + g * HEAD_DIM:KV_COLS + (g + 1) * HEAD_DIM].astype(BF16)
        o = _softmax_pv([s_past, s_new], [v_past, v_new])
        for hh in range(KV_GROUP):
            h = g * KV_GROUP + hh
            o_ref[:, h * HEAD_DIM:(h + 1) * HEAD_DIM] = o[hh * tq:(hh + 1) * tq].astype(o_ref.dtype)


def attention_lat(z, cache_k, cache_v, rope_tabs, g_q, g_k):
    tq = 256
    nq = LAT_LEN // tq
    cos, slo, shi = rope_tabs
    tab = pl.BlockSpec((LAT_LEN, HEAD_DIM), lambda b, q: (0, 0))
    return pl.pallas_call(
        _attn_lat_body,
        grid=(N_LAT_SEQ, nq),
        in_specs=[
            pl.BlockSpec((tq, Q_COLS), lambda b, q: (b * nq + q, 0)),
            pl.BlockSpec((LAT_LEN, 2 * KV_COLS), lambda b, q: (b, COL_K // (2 * KV_COLS))),
            pl.BlockSpec((None, PAST_LEN, KV_COLS), lambda b, q: (b, 0, 0)),
            pl.BlockSpec((None, PAST_LEN, KV_COLS), lambda b, q: (b, 0, 0)),
            tab, tab, tab,
            pl.BlockSpec((1, HEAD_DIM), lambda b, q: (0, 0)),
            pl.BlockSpec((1, HEAD_DIM), lambda b, q: (0, 0)),
        ],
        out_specs=pl.BlockSpec((tq, Q_COLS), lambda b, q: (b * nq + q, 0)),
        out_shape=jax.ShapeDtypeStruct((N_LAT, Q_COLS), BF16),
        scratch_shapes=[pltpu.VMEM((LAT_LEN, KV_COLS), BF16)],
        compiler_params=_params(("arbitrary", "arbitrary")),
        name="attention_lat",
    )(z, z, cache_k, cache_v, cos, slo, shi, g_q.reshape(1, HEAD_DIM), g_k.reshape(1, HEAD_DIM))


def _rope_tables():
    t = np.arange(LAT_LEN)
    row = jnp.asarray(t // GRID_W, F32)
    col = jnp.asarray(t % GRID_W, F32)
    nf = HEAD_DIM // 4
    inv_freq = ROPE_THETA ** (-jnp.arange(nf, dtype=F32) / nf)
    ang_row = row[:, None] * inv_freq[None, :]
    ang_col = col[:, None] * inv_freq[None, :]
    ang = jnp.concatenate([ang_row, ang_row, ang_col, ang_col], axis=1)
    cos = jnp.cos(ang)
    sin = jnp.sin(ang)
    first = jnp.asarray((np.arange(HEAD_DIM) % (2 * nf)) < nf)[None, :]
    return cos, jnp.where(first, -sin, 0.0), jnp.where(first, 0.0, sin)


RNN_ROWS = 2048
RNN_COLS = 512
RNN_SUB = RNN_COLS // RNN_BLOCK_DIM


def _gelu_tanh(y):
    return 0.5 * y * (1.0 + jnp.tanh(0.7978845608028654 * (y + 0.044715 * (y * y * y))))


def _rglru_body(seq_len, xr_ref, yr_ref, cw_ref, cb_ref, wg_ref, bg_ref, lam_ref, h0f_ref, h0b_ref,
                o_ref, hf_ref, hb_ref, xs_ref, af_ref, bf_ref, ab_ref, bb_ref):
    n_seq = RNN_ROWS // seq_len
    for n in range(RNN_SUB):
        cols = slice(n * RNN_BLOCK_DIM, (n + 1) * RNN_BLOCK_DIM)
        for s in range(n_seq):
            xs_ref[n, pl.ds(s, seq_len, stride=n_seq), :] = xr_ref[s * seq_len:(s + 1) * seq_len, cols]

    row = lax.broadcasted_iota(jnp.int32, (RNN_ROWS, 1), 0)
    lam = lam_ref[...]
    softplus_neg = jnp.maximum(-lam, 0.0) + jnp.log(1.0 + jnp.exp(-jnp.abs(lam)))
    rate = softplus_neg * (-RG_C * 1.4426950408889634)
    for n in range(RNN_SUB):
        cols = slice(n * RNN_BLOCK_DIM, (n + 1) * RNN_BLOCK_DIM)
        x = xs_ref[n]
        x_m1 = jnp.where(row >= n_seq, pltpu.roll(x, n_seq, 0), 0.0)
        x_p1 = jnp.where(row < RNN_ROWS - n_seq, pltpu.roll(x, RNN_ROWS - n_seq, 0), 0.0)
        x_p2 = jnp.where(row < RNN_ROWS - 2 * n_seq, pltpu.roll(x, RNN_ROWS - 2 * n_seq, 0), 0.0)
        xn = (cb_ref[:, cols] + x_m1 * cw_ref[0:1, cols] + x * cw_ref[1:2, cols]
              + x_p1 * cw_ref[2:3, cols] + x_p2 * cw_ref[3:4, cols])
        pre = jnp.dot(xn.astype(BF16), wg_ref[n], preferred_element_type=F32) + bg_ref[n]
        for d, (a_ref, b_ref) in enumerate(((af_ref, bf_ref), (ab_ref, bb_ref))):
            r = 0.5 * jnp.tanh(0.5 * pre[:, (2 * d) * RNN_BLOCK_DIM:(2 * d + 1) * RNN_BLOCK_DIM]) + 0.5
            gate_in = 0.5 * jnp.tanh(
                0.5 * pre[:, (2 * d + 1) * RNN_BLOCK_DIM:(2 * d + 2) * RNN_BLOCK_DIM]) + 0.5
            a = jnp.exp2(r * rate[d:d + 1, cols])
            v = 1.0 - a * a
            a_ref[n] = a
            b_ref[n] = (v * lax.rsqrt(jnp.maximum(v, 1e-30))) * (gate_in * xn)

    def step(t, carry):
        rows_f = pl.ds(pl.multiple_of(t * n_seq, n_seq), n_seq)
        rows_b = pl.ds(pl.multiple_of((seq_len - 1 - t) * n_seq, n_seq), n_seq)
        out = []
        for n in range(RNN_SUB):
            hf = af_ref[n, rows_f, :] * carry[2 * n] + bf_ref[n, rows_f, :]
            hb = ab_ref[n, rows_b, :] * carry[2 * n + 1] + bb_ref[n, rows_b, :]
            bf_ref[n, rows_f, :] = hf
            bb_ref[n, rows_b, :] = hb
            out += [hf, hb]
        return tuple(out)

    init = []
    for n in range(RNN_SUB):
        cols = slice(n * RNN_BLOCK_DIM, (n + 1) * RNN_BLOCK_DIM)
        init += [h0f_ref[:, 0, cols], h0b_ref[:, 0, cols]]
    last = lax.fori_loop(0, seq_len, step, tuple(init), unroll=8)
    for n in range(RNN_SUB):
        cols = slice(n * RNN_BLOCK_DIM, (n + 1) * RNN_BLOCK_DIM)
        hf_ref[:, 0, cols] = last[2 * n]
        hb_ref[:, 0, cols] = last[2 * n + 1]
        bf_ref[n] = bf_ref[n] + bb_ref[n]
        for s in range(n_seq):
            rows = slice(s * seq_len, (s + 1) * seq_len)
            h_sum = bf_ref[n, pl.ds(s, seq_len, stride=n_seq), :]
            o_ref[rows, cols] = (h_sum * _gelu_tanh(yr_ref[rows, cols])).astype(o_ref.dtype)


def rglru_mixer(z, seq_len, conv_w, conv_b, w_gates, b_gates, lam, h0_f, h0_b):
    m = z.shape[0]
    n_seq_total = m // seq_len
    n_seq = RNN_ROWS // seq_len
    cblk = lambda base: (lambda r, c: (r, base // RNN_COLS + c))
    state_spec = pl.BlockSpec((n_seq, 1, RNN_COLS), lambda r, c: (r, 0, c))
    return pl.pallas_call(
        functools.partial(_rglru_body, seq_len),
        grid=(m // RNN_ROWS, D_MODEL // RNN_COLS),
        in_specs=[
            pl.BlockSpec((RNN_ROWS, RNN_COLS), cblk(COL_XR)),
            pl.BlockSpec((RNN_ROWS, RNN_COLS), cblk(COL_YR)),
            pl.BlockSpec((4, RNN_COLS), lambda r, c: (0, c)),
            pl.BlockSpec((1, RNN_COLS), lambda r, c: (0, c)),
            pl.BlockSpec((RNN_SUB, RNN_BLOCK_DIM, 4 * RNN_BLOCK_DIM), lambda r, c: (c, 0, 0)),
            pl.BlockSpec((RNN_SUB, 1, 4 * RNN_BLOCK_DIM), lambda r, c: (c, 0, 0)),
            pl.BlockSpec((2, RNN_COLS), lambda r, c: (0, c)),
            state_spec, state_spec,
        ],
        out_specs=[
            pl.BlockSpec((RNN_ROWS, RNN_COLS), lambda r, c: (r, c)),
            state_spec, state_spec,
        ],
        out_shape=[
            jax.ShapeDtypeStruct((m, D_MODEL), BF16),
            jax.ShapeDtypeStruct((n_seq_total, 1, D_MODEL), F32),
            jax.ShapeDtypeStruct((n_seq_total, 1, D_MODEL), F32),
        ],
        scratch_shapes=[pltpu.VMEM((RNN_SUB, RNN_ROWS, RNN_BLOCK_DIM), F32) for _ in range(5)],
        compiler_params=_params(("arbitrary", "arbitrary")),
        name="rglru_mixer_t%d" % seq_len,
    )(z, z, conv_w, conv_b.reshape(1, D_MODEL), w_gates, b_gates, lam, h0_f, h0_b)


def _merge_body(a_ref, r_ref, wa_ref, wr_ref, ga_ref, gr_ref, o_ref, wa_bf, wr_bf):
    @pl.when(pl.program_id(1) == 0)
    def _():
        wa_bf[...] = wa_ref[...].astype(BF16)
        wr_bf[...] = wr_ref[...].astype(BF16)

    pa = jnp.dot(a_ref[...], wa_bf[...], preferred_element_type=F32)
    pr = jnp.dot(r_ref[...], wr_bf[...], preferred_element_type=F32)
    o_ref[...] = (_sigmoid(ga_ref[...]) * pa + _sigmoid(gr_ref[...]) * pr).astype(o_ref.dtype)


def gated_merge(attn, rnn, z, w_o_attn, w_o_rnn):
    m = attn.shape[0]
    tm, tn = 1024, 512
    return pl.pallas_call(
        _merge_body,
        grid=(D_MODEL // tn, m // tm),
        in_specs=[
            pl.BlockSpec((tm, Q_COLS), lambda j, i: (i, 0)),
            pl.BlockSpec((tm, D_MODEL), lambda j, i: (i, 0)),
            pl.BlockSpec((Q_COLS, tn), lambda j, i: (0, j)),
            pl.BlockSpec((D_MODEL, tn), lambda j, i: (0, j)),
            pl.BlockSpec((tm, tn), lambda j, i: (i, COL_GA // tn + j)),
            pl.BlockSpec((tm, tn), lambda j, i: (i, COL_GR // tn + j)),
        ],
        out_specs=pl.BlockSpec((tm, tn), lambda j, i: (i, j)),
        out_shape=jax.ShapeDtypeStruct((m, D_MODEL), BF16),
        scratch_shapes=[pltpu.VMEM((Q_COLS, tn), BF16), pltpu.VMEM((D_MODEL, tn), BF16)],
        compiler_params=_params(("arbitrary", "arbitrary")),
        name="gated_merge",
    )(attn, rnn, w_o_attn, w_o_rnn, z, z)


POST_TM = 512
HALF_D = D_MODEL // 2
WORD_ROWS = HALF_D // 128
POST_CTX_BLOCKS = N_CTX // POST_TM
LAT_BLOCKS_PER_SEQ = LAT_LEN // POST_TM


def _post_group(i):
    return jnp.where(i < POST_CTX_BLOCKS, 0, 1 + (i - POST_CTX_BLOCKS) // LAT_BLOCKS_PER_SEQ)


def _postmix_body(mc_ref, ml_ref, xc_ref, xl_ref, wo_ref, gpm_ref, gt1_ref, gpf_ref, sh2_ref, sc2_ref,
                  wr_ref, br_ref, x1_ref, h2_ref, e_ref, gate_ref, rank_ref, cnt_ref, carry_ref):
    i = pl.program_id(0)
    tm = POST_TM

    @pl.when(i == 0)
    def _():
        carry_ref[...] = jnp.zeros_like(carry_ref)

    is_ctx = i < POST_CTX_BLOCKS
    merged = jnp.where(is_ctx, mc_ref[...], ml_ref[...])
    x = jnp.where(is_ctx, xc_ref[...], xl_ref[...])
    o = jnp.dot(merged, wo_ref[...], preferred_element_type=F32)
    x1 = x + gt1_ref[...] * (o * _rms_scale(o) * gpm_ref[...])
    x1_ref[...] = x1
    h2 = (x1 * _rms_scale(x1) * gpf_ref[...]) * (1.0 + sc2_ref[...]) + sh2_ref[...]
    h2_bf = h2.astype(BF16)
    bits = lax.bitcast_convert_type(h2_bf.astype(F32), jnp.uint32)
    words = (lax.shift_right_logical(bits[:, :HALF_D], jnp.uint32(16))
             | (bits[:, HALF_D:] & jnp.uint32(0xFFFF0000)))
    for c in range(WORD_ROWS):
        h2_ref[pl.ds(c, tm, stride=WORD_ROWS), :] = words[:, c * 128:(c + 1) * 128]

    logits = jnp.dot(h2_bf, wr_ref[...].astype(BF16), preferred_element_type=F32) + br_ref[...]
    lane = lax.broadcasted_iota(jnp.int32, (tm, N_EXPERTS), 1)
    work = logits
    chosen = jnp.zeros((tm, N_EXPERTS), F32)
    sels, vals, idxs = [], [], []
    for _ in range(TOP_K):
        mx = jnp.max(work, axis=-1, keepdims=True)
        idx = jnp.min(jnp.where(work == mx, lane, N_EXPERTS), axis=-1, keepdims=True)
        sel = lane == idx
        work = jnp.where(sel, -jnp.inf, work)
        chosen = jnp.where(sel, 1.0, chosen)
        sels.append(sel)
        vals.append(mx)
        idxs.append(idx)
    exps = [jnp.exp(v - vals[0]) for v in vals]
    inv = 1.0 / (exps[0] + exps[1] + exps[2] + exps[3])

    r_io = lax.broadcasted_iota(jnp.int32, (tm, tm), 0)
    c_io = lax.broadcasted_iota(jnp.int32, (tm, tm), 1)
    lower = jnp.where(c_io < r_io, 1.0, 0.0).astype(BF16)
    before = jnp.dot(lower, chosen.astype(BF16), preferred_element_type=F32) + carry_ref[...]
    carry_ref[...] = carry_ref[...] + jnp.sum(chosen, axis=0, keepdims=True)
    cnt_ref[...] = carry_ref[...]

    lane_k = lax.broadcasted_iota(jnp.int32, (tm, TOP_K), 1)
    e_out = jnp.zeros((tm, TOP_K), jnp.int32)
    g_out = jnp.zeros((tm, TOP_K), F32)
    r_out = jnp.zeros((tm, TOP_K), jnp.int32)
    for k in range(TOP_K):
        rk = jnp.sum(jnp.where(sels[k], before, 0.0), axis=-1, keepdims=True).astype(jnp.int32)
        e_out = jnp.where(lane_k == k, idxs[k], e_out)
        g_out = jnp.where(lane_k == k, exps[k] * inv, g_out)
        r_out = jnp.where(lane_k == k, rk, r_out)
    e_ref[...] = e_out
    gate_ref[...] = g_out
    rank_ref[...] = r_out


def post_mix_router(merged_ctx, merged_lat, x_ctx, x_lat, w_out_bf, g_post_mix, gt1, g_pre_ffn, sh2, sc2,
                    w_router, b_router):
    tm = POST_TM
    ctx_map = lambda i: (jnp.minimum(i, POST_CTX_BLOCKS - 1), 0)
    lat_map = lambda i: (jnp.maximum(i - POST_CTX_BLOCKS, 0), 0)
    gmap = lambda i: (_post_group(i), 0, 0)
    row = lambda i: (i, 0)
    const = lambda i: (0, 0)
    vec = pl.BlockSpec((1, D_MODEL), const)
    gvec = pl.BlockSpec((None, 1, D_MODEL), gmap)
    return pl.pallas_call(
        _postmix_body,
        grid=(N_TOK // tm,),
        in_specs=[
            pl.BlockSpec((tm, D_MODEL), ctx_map),
            pl.BlockSpec((tm, D_MODEL), lat_map),
            pl.BlockSpec((tm, D_MODEL), ctx_map),
            pl.BlockSpec((tm, D_MODEL), lat_map),
            pl.BlockSpec((D_MODEL, D_MODEL), const),
            vec, gvec, vec, gvec, gvec,
            pl.BlockSpec((D_MODEL, N_EXPERTS), const),
            pl.BlockSpec((1, N_EXPERTS), const),
        ],
        out_specs=[
            pl.BlockSpec((tm, D_MODEL), row),
            pl.BlockSpec((tm * WORD_ROWS, 128), row),
            pl.BlockSpec((tm, TOP_K), row),
            pl.BlockSpec((tm, TOP_K), row),
            pl.BlockSpec((tm, TOP_K), row),
            pl.BlockSpec((1, N_EXPERTS), const),
        ],
        out_shape=[
            jax.ShapeDtypeStruct((N_TOK, D_MODEL), F32),
            jax.ShapeDtypeStruct((N_TOK * WORD_ROWS, 128), jnp.uint32),
            jax.ShapeDtypeStruct((N_TOK, TOP_K), jnp.int32),
            jax.ShapeDtypeStruct((N_TOK, TOP_K), F32),
            jax.ShapeDtypeStruct((N_TOK, TOP_K), jnp.int32),
            jax.ShapeDtypeStruct((1, N_EXPERTS), F32),
        ],
        scratch_shapes=[pltpu.VMEM((1, N_EXPERTS), F32)],
        compiler_params=_params(("arbitrary",)),
        name="post_mix_router",
    )(merged_ctx, merged_lat, x_ctx, x_lat, w_out_bf, g_post_mix.reshape(1, D_MODEL), gt1,
      g_pre_ffn.reshape(1, D_MODEL), sh2, sc2, w_router, b_router.reshape(1, N_EXPERTS))


GATHER_ROWS = ROW_TILE // N_FF_CHUNKS


def _unpack_tile(xbuf_ref, i):
    base = pl.multiple_of(i * (ROW_TILE * WORD_ROWS), ROW_TILE * WORD_ROWS)
    lo, hi = [], []
    for c in range(WORD_ROWS):
        words = xbuf_ref[pl.ds(base + c, ROW_TILE, stride=WORD_ROWS), :]
        lo.append(lax.bitcast_convert_type(lax.shift_left(words, jnp.uint32(16)), F32).astype(BF16))
        hi.append(lax.bitcast_convert_type(words & jnp.uint32(0xFFFF0000), F32).astype(BF16))
    return jnp.concatenate(lo + hi, axis=1)


def _moe_body(exp_ref, row_ref, nsub_ref, nzero_ref, tok_ref, h_ref, wg_ref, wl_ref, wd_ref, bg_ref, bl_ref,
              bd_ref, y_ref, xbuf_ref, act_ref, wg_bf, wl_bf, wd_bf, stage_ref, idx_ref, pend_ref,
              xsem, isem, ysem):
    s = pl.program_id(0)
    j = pl.program_id(1)
    n_sub = nsub_ref[s]
    row_start = row_ref[s]
    n_next = nsub_ref[jnp.minimum(s + 1, N_SUPER - 1)]

    def idx_copy(p):
        tile0 = pl.multiple_of(row_ref[p], ROW_TILE) // ROW_TILE
        return pltpu.make_async_copy(tok_ref.at[pl.ds(tile0, SUPER_TILES)], idx_ref.at[p % 2],
                                     isem.at[p % 2])

    def gather_row(slot, tile, col, r):
        t = idx_ref[slot, tile, 0, col]
        src = h_ref.at[pl.ds(pl.multiple_of(t * WORD_ROWS, WORD_ROWS), WORD_ROWS), :]
        dst = xbuf_ref.at[pl.ds(pl.multiple_of(r * WORD_ROWS, WORD_ROWS), WORD_ROWS), :]
        pltpu.make_async_copy(src, dst, xsem).start()

    def gather_range(p, first, last):
        def issue(r, _):
            gather_row(p % 2, lax.shift_right_logical(r, 8), jnp.bitwise_and(r, ROW_TILE - 1), r)
            return 0

        lax.fori_loop(first, last, issue, 0)

    def wait_rows(count):
        @pl.when(count > 0)
        def _():
            n = pl.multiple_of(count * WORD_ROWS, ROW_TILE * WORD_ROWS)
            pltpu.make_async_copy(h_ref.at[pl.ds(0, n), :], xbuf_ref.at[pl.ds(0, n), :], xsem).wait()

    @pl.when(jnp.logical_and(s == 0, j == 0))
    def _():
        pend_ref[0] = 0
        pend_ref[1] = 0

    def drain_stage(slot):
        @pl.when(pend_ref[slot] == 1)
        def _():
            pltpu.make_async_copy(stage_ref.at[slot], stage_ref.at[slot], ysem.at[slot]).wait()
            pend_ref[slot] = 0

    @pl.when(jnp.logical_and(s == 0, j == 0))
    def _():
        idx_copy(0).start()
        idx_copy(0).wait()
        gather_range(0, 0, n_sub * ROW_TILE)
        wait_rows(n_sub * ROW_TILE)

    @pl.when(jnp.logical_and(s > 0, j == 0))
    def _():
        wait_rows(jnp.maximum(nsub_ref[jnp.maximum(s - 1, 0)], n_sub) * ROW_TILE)

    @pl.when(jnp.logical_and(j == 0, s + 1 < N_SUPER))
    def _():
        idx_copy(s + 1).start()

    @pl.when(jnp.logical_and(j == N_FF_CHUNKS, s + 1 < N_SUPER))
    def _():
        idx_copy(s + 1).wait()

        @pl.when(n_sub == 0)
        def _():
            gather_range(s + 1, 0, n_next * ROW_TILE)

    @pl.when(jnp.logical_and(j < N_FF_CHUNKS, n_sub > 0))
    def _():
        wg_bf[...] = wg_ref[...].astype(BF16)
        wl_bf[...] = wl_ref[...].astype(BF16)
        bg = bg_ref[...]
        bl = bl_ref[...]

        def sub(i, _):
            rows = pl.ds(pl.multiple_of(i * ROW_TILE, ROW_TILE), ROW_TILE)
            xt = _unpack_tile(xbuf_ref, i)
            glu = jnp.minimum(jnp.dot(xt, wg_bf[...], preferred_element_type=F32) + bg, SWIGLU_LIMIT)
            lin = jnp.clip(jnp.dot(xt, wl_bf[...], preferred_element_type=F32) + bl,
                           -SWIGLU_LIMIT, SWIGLU_LIMIT)
            act = glu * _sigmoid(SWIGLU_ALPHA * glu) * (lin + 1.0)
            act_ref[j, rows, :] = act.astype(BF16)
            return 0

        lax.fori_loop(0, n_sub, sub, 0)

    for cc in range(N_FF_CHUNKS):
        @pl.when(jnp.logical_and(j == N_FF_CHUNKS + cc, n_sub > 0))
        def _(cc=cc):
            wd_bf[...] = wd_ref[...].astype(BF16)
            bd = bd_ref[...]
            next_slot = (s + 1) % 2

            def out_copy(i, slot):
                dst = y_ref.at[pl.ds(pl.multiple_of(row_start + i * ROW_TILE, ROW_TILE), ROW_TILE),
                               cc * FF_CHUNK:(cc + 1) * FF_CHUNK]
                return pltpu.make_async_copy(stage_ref.at[slot], dst, ysem.at[slot])

            def sub(i, _):
                rows = pl.ds(pl.multiple_of(i * ROW_TILE, ROW_TILE), ROW_TILE)
                slot = i % 2
                acc = bd
                for c in range(N_FF_CHUNKS):
                    acc = acc + jnp.dot(act_ref[c, rows, :], wd_bf[c * FF_CHUNK:(c + 1) * FF_CHUNK, :],
                                        preferred_element_type=F32)
                first = (cc * n_sub + i) * GATHER_ROWS
                tile = lax.shift_right_logical(first, 8)
                col = jnp.bitwise_and(first, ROW_TILE - 1)
                for g in range(GATHER_ROWS):
                    gather_row(next_slot, tile, col + g, first + g)

                drain_stage(slot)
                stage_ref[slot] = acc
                out_copy(i, slot).start()
                pend_ref[slot] = 1
                return 0

            lax.fori_loop(0, n_sub, sub, 0)
            if cc == N_FF_CHUNKS - 1:
                gather_range(s + 1, n_sub * ROW_TILE, n_next * ROW_TILE)

    n_zero = nzero_ref[s]

    @pl.when(jnp.logical_and(j == 0, n_zero > 0))
    def _():
        drain_stage(0)
        stage_ref[0] = jnp.zeros((ROW_TILE, FF_CHUNK), F32)

        def zero_copy(i, cc):
            dst = y_ref.at[pl.ds(pl.multiple_of(row_start + i * ROW_TILE, ROW_TILE), ROW_TILE),
                           cc * FF_CHUNK:(cc + 1) * FF_CHUNK]
            return pltpu.make_async_copy(stage_ref.at[0], dst, ysem.at[0])

        def issue(i, _):
            for cc in range(N_FF_CHUNKS):
                zero_copy(i, cc).start()
            return 0

        def drain(i, _):
            for cc in range(N_FF_CHUNKS):
                zero_copy(i, cc).wait()
            return 0

        lax.fori_loop(0, n_zero, issue, 0)
        lax.fori_loop(0, n_zero, drain, 0)

    @pl.when(jnp.logical_and(s == N_SUPER - 1, j == 2 * N_FF_CHUNKS - 1))
    def _():
        drain_stage(0)
        drain_stage(1)


def expert_mlp(h_packed, tok_sorted, sched, w_gate_up, b_gate_up, w_down, b_down):
    exp_of, row_of, nsub_of, nzero_of = sched
    last = N_FF_CHUNKS - 1
    up_of = lambda s, j, n: jnp.where(n[s] > 0, jnp.minimum(j, last), last)
    down_of = lambda s, j, n: jnp.where(n[s] > 0, jnp.maximum(j - N_FF_CHUNKS, 0), last)
    up_chunk = lambda s, j, e, r, n, z: (e[s], 0, up_of(s, j, n))
    lin_chunk = lambda s, j, e, r, n, z: (e[s], 0, N_FF_CHUNKS + up_of(s, j, n))
    down_chunk = lambda s, j, e, r, n, z: (e[s], 0, down_of(s, j, n))
    grid_spec = pltpu.PrefetchScalarGridSpec(
        num_scalar_prefetch=4,
        grid=(N_SUPER, 2 * N_FF_CHUNKS),
        in_specs=[
            pl.BlockSpec(memory_space=pl.ANY),
            pl.BlockSpec(memory_space=pl.ANY),
            pl.BlockSpec((None, D_MODEL, FF_CHUNK), up_chunk),
            pl.BlockSpec((None, D_MODEL, FF_CHUNK), lin_chunk),
            pl.BlockSpec((None, D_FF, FF_CHUNK), down_chunk),
            pl.BlockSpec((None, 1, FF_CHUNK), up_chunk),
            pl.BlockSpec((None, 1, FF_CHUNK), lin_chunk),
            pl.BlockSpec((None, 1, FF_CHUNK), down_chunk),
        ],
        out_specs=pl.BlockSpec(memory_space=pl.ANY),
        scratch_shapes=[
            pltpu.VMEM((SUPER_ROWS * WORD_ROWS, 128), jnp.uint32),
            pltpu.VMEM((N_FF_CHUNKS, SUPER_ROWS, FF_CHUNK), BF16),
            pltpu.VMEM((D_MODEL, FF_CHUNK), BF16),
            pltpu.VMEM((D_MODEL, FF_CHUNK), BF16),
            pltpu.VMEM((D_FF, FF_CHUNK), BF16),
            pltpu.VMEM((2, ROW_TILE, FF_CHUNK), F32),
            pltpu.SMEM((2, SUPER_TILES, 1, ROW_TILE), jnp.int32),
            pltpu.SMEM((2,), jnp.int32),
            pltpu.SemaphoreType.DMA(()),
            pltpu.SemaphoreType.DMA((2,)),
            pltpu.SemaphoreType.DMA((2,)),
        ],
    )
    tok_tiles = jnp.concatenate([tok_sorted.reshape(N_ROW_TILES, 1, ROW_TILE),
                                 jnp.zeros((SUPER_TILES, 1, ROW_TILE), jnp.int32)], axis=0)
    return pl.pallas_call(
        _moe_body,
        grid_spec=grid_spec,
        out_shape=jax.ShapeDtypeStruct((N_ROWS, D_MODEL), F32),
        compiler_params=_params(("arbitrary", "arbitrary"), vmem=58 * 1024 * 1024),
        name="expert_mlp",
    )(exp_of, row_of, nsub_of, nzero_of, tok_tiles, h_packed, w_gate_up, w_gate_up, w_down,
      b_gate_up.reshape(N_EXPERTS, 1, 2 * D_FF), b_gate_up.reshape(N_EXPERTS, 1, 2 * D_FF),
      b_down.reshape(N_EXPERTS, 1, D_MODEL))


COMB_TB = 256


def _combine_start(y_ref, ybuf_ref, pos_ref, sem):
    def issue(t, _):
        for k in range(TOP_K):
            p = pos_ref[0, 0, t * TOP_K + k]
            pltpu.make_async_copy(y_ref.at[pl.ds(p, 1), :], ybuf_ref.at[k, pl.ds(t, 1), :], sem).start()
        return 0

    lax.fori_loop(0, COMB_TB, issue, 0, unroll=4)


def _combine_body(n, pos_ref, pos_next_ref, y_ref, gate_ref, x1_ref, gt2_ref, g_ref, o_ref, ybuf_ref, sem_ref):
    i = pl.program_id(0)
    slot = i % 2

    @pl.when(i == 0)
    def _():
        _combine_start(y_ref, ybuf_ref.at[0], pos_ref, sem_ref.at[0])

    @pl.when(i + 1 < n)
    def _():
        _combine_start(y_ref, ybuf_ref.at[1 - slot], pos_next_ref, sem_ref.at[1 - slot])

    for k in range(TOP_K):
        pltpu.make_async_copy(y_ref.at[pl.ds(0, COMB_TB), :], ybuf_ref.at[slot, k], sem_ref.at[slot]).wait()
    gates = gate_ref[...]
    ffn = gates[:, 0:1] * ybuf_ref[slot, 0]
    for k in range(1, TOP_K):
        ffn = ffn + gates[:, k:k + 1] * ybuf_ref[slot, k]
    o_ref[...] = x1_ref[...] + gt2_ref[...] * (ffn * _rms_scale(ffn) * g_ref[...])


def combine_residual(y_sorted, pos, gates, x1, gt2, g_post_ffn, row_offset, n_rows, group_of_block):
    tb = COMB_TB
    nblk = n_rows // tb
    off = row_offset // tb
    pos3 = pos.reshape(N_TOK // tb, 1, tb * TOP_K)
    smem_blk = lambda f: pl.BlockSpec((1, 1, tb * TOP_K), f, memory_space=pltpu.SMEM)
    return pl.pallas_call(
        functools.partial(_combine_body, nblk),
        grid=(nblk,),
        in_specs=[
            smem_blk(lambda i: (off + i, 0, 0)),
            smem_blk(lambda i: (off + jnp.minimum(i + 1, nblk - 1), 0, 0)),
            pl.BlockSpec(memory_space=pl.ANY),
            pl.BlockSpec((tb, TOP_K), lambda i: (off + i, 0)),
            pl.BlockSpec((tb, D_MODEL), lambda i: (off + i, 0)),
            pl.BlockSpec((None, 1, D_MODEL), lambda i: (group_of_block(i), 0, 0)),
            pl.BlockSpec((1, D_MODEL), lambda i: (0, 0)),
        ],
        out_specs=pl.BlockSpec((tb, D_MODEL), lambda i: (i, 0)),
        out_shape=jax.ShapeDtypeStruct((n_rows, D_MODEL), F32),
        scratch_shapes=[pltpu.VMEM((2, TOP_K, tb, D_MODEL), F32), pltpu.SemaphoreType.DMA((2,))],
        compiler_params=_params(("arbitrary",)),
        name="combine_residual",
    )(pos3, pos3, y_sorted, gates, x1, gt2, g_post_ffn.reshape(1, D_MODEL))


def _routing_tables(e_idx, rank, counts_f):
    counts = counts_f.reshape(N_EXPERTS).astype(jnp.int32)
    n_tiles = (counts + ROW_TILE - 1) // ROW_TILE
    padded = n_tiles * ROW_TILE
    pad_end = jnp.cumsum(padded)
    pad_start = pad_end - padded
    pos = pad_start[e_idx] + rank
    tok_ids = jnp.repeat(jnp.arange(N_TOK, dtype=jnp.int32), TOP_K)
    tok_sorted = jnp.zeros((N_ROWS,), jnp.int32).at[pos.reshape(-1)].set(
        tok_ids, unique_indices=True, mode="drop")
    n_pass = (n_tiles + SUPER_TILES - 1) // SUPER_TILES
    pass_end = jnp.cumsum(n_pass)
    total = pass_end[-1]
    s = jnp.arange(N_SUPER, dtype=jnp.int32)
    s_eff = jnp.minimum(s, total - 1)
    e_of = jnp.minimum(jnp.searchsorted(pass_end, s_eff, side="right"), N_EXPERTS - 1).astype(jnp.int32)
    local = s_eff - (pass_end[e_of] - n_pass[e_of])
    row_of = pad_start[e_of] + local * SUPER_ROWS
    nsub = jnp.minimum(SUPER_TILES, n_tiles[e_of] - local * SUPER_TILES)
    nsub = jnp.where(s < total, nsub, 0).astype(jnp.int32)
    zero_row = pad_end[-1] + (s - total) * SUPER_ROWS
    nzero = jnp.clip((N_ROWS - zero_row) // ROW_TILE, 0, SUPER_TILES)
    nzero = jnp.where(s >= total, nzero, 0).astype(jnp.int32)
    row_of = jnp.where(s < total, row_of, jnp.minimum(zero_row, N_ROWS - ROW_TILE)).astype(jnp.int32)
    return pos.astype(jnp.int32), tok_sorted, (e_of, row_of, nsub, nzero)


def kernel(x_prompt, x_sample, cache_k, cache_v, state_rnn_fwd, state_rnn_bwd, c, c_ctx, w_mod, b_mod, g_pre_mix, w_in, g_q_norm, g_k_norm, conv_w, conv_b, rg_w_a, rg_b_a, rg_w_x, rg_b_x, rg_lambda, w_o_attn, w_o_rnn, w_out, g_post_mix, g_pre_ffn, w_router, b_router, w_gate_up, b_gate_up, w_down, b_down, g_post_ffn):
    l = 0
    x_ctx = x_prompt.reshape(N_CTX, D_MODEL)
    x_lat = x_sample.reshape(N_LAT, D_MODEL)

    cond8 = jnp.concatenate([c_ctx[None, :], c, jnp.zeros((8 - 1 - N_LAT_SEQ, D_MODEL), F32)], axis=0)
    mod = modulation(cond8, w_mod[l], b_mod[l])[:1 + N_LAT_SEQ].reshape(1 + N_LAT_SEQ, 6, 1, D_MODEL)
    sh1, sc1, gt1, sh2, sc2, gt2 = [mod[:, i] for i in range(6)]

    ctx_group = lambda i: 0
    lat_group_1024 = lambda i: 1 + i
    h_ctx = prenorm_modulate(x_ctx, g_pre_mix[l], sh1, sc1, ctx_group, 1024)
    h_lat = prenorm_modulate(x_lat, g_pre_mix[l], sh1, sc1, lat_group_1024, 1024)
    z_ctx = in_projection(h_ctx, w_in[l])
    z_lat = in_projection(h_lat, w_in[l])

    attn_ctx, k_new, v_new = attention_ctx(z_ctx, g_q_norm[l], g_k_norm[l])
    attn_lat = attention_lat(z_lat, cache_k[:, l].reshape(N_LAT_SEQ, PAST_LEN, KV_COLS),
                             cache_v[:, l].reshape(N_LAT_SEQ, PAST_LEN, KV_COLS),
                             _rope_tables(), g_q_norm[l], g_k_norm[l])

    def per_block(w):
        return w.reshape(2, RNN_BLOCKS, 1, RNN_BLOCK_DIM)

    w_gates = jnp.concatenate([rg_w_a[l, 0], rg_w_x[l, 0], rg_w_a[l, 1], rg_w_x[l, 1]], axis=-1).astype(BF16)
    ba, bx = per_block(rg_b_a[l]), per_block(rg_b_x[l])
    b_gates = jnp.concatenate([ba[0], bx[0], ba[1], bx[1]], axis=-1)
    zeros_state = jnp.zeros((N_CTX_SEQ, 1, D_MODEL), F32)
    rnn_ctx, hf_ctx, hb_ctx = rglru_mixer(z_ctx, CTX_LEN, conv_w[l], conv_b[l], w_gates, b_gates,
                                          rg_lambda[l], zeros_state, zeros_state)
    rnn_lat, _, _ = rglru_mixer(z_lat, LAT_LEN, conv_w[l], conv_b[l], w_gates, b_gates, rg_lambda[l],
                                state_rnn_fwd[:, l].reshape(N_LAT_SEQ, 1, D_MODEL),
                                state_rnn_bwd[:, l].reshape(N_LAT_SEQ, 1, D_MODEL))

    merged_ctx = gated_merge(attn_ctx, rnn_ctx, z_ctx, w_o_attn[l], w_o_rnn[l])
    merged_lat = gated_merge(attn_lat, rnn_lat, z_lat, w_o_attn[l], w_o_rnn[l])

    x1, h2, e_idx, gates, rank, counts = post_mix_router(
        merged_ctx, merged_lat, x_ctx, x_lat, w_out[l].astype(BF16), g_post_mix[l], gt1, g_pre_ffn[l],
        sh2, sc2, w_router[l], b_router[l])

    pos, tok_sorted, sched = _routing_tables(e_idx, rank, counts)
    y_sorted = expert_mlp(h2, tok_sorted, sched, w_gate_up[l], b_gate_up[l], w_down[l], b_down[l])

    y_ctx = combine_residual(y_sorted, pos, gates, x1, gt2, g_post_ffn[l], 0, N_CTX, ctx_group)
    y_lat = combine_residual(y_sorted, pos, gates, x1, gt2, g_post_ffn[l], N_CTX, N_LAT,
                             lambda i: 1 + i // (LAT_LEN // COMB_TB))

    return (y_ctx.reshape(N_CTX_SEQ, CTX_LEN, D_MODEL),
            y_lat.reshape(N_LAT_SEQ, LAT_LEN, D_MODEL),
            k_new.reshape(N_CTX_SEQ, 1, CTX_LEN, N_KV_HEADS, HEAD_DIM),
            v_new.reshape(N_CTX_SEQ, 1, CTX_LEN, N_KV_HEADS, HEAD_DIM),
            hf_ctx,
            hb_ctx)
```

```python
import functools

import jax
import jax.numpy as jnp
import numpy as np
from jax import lax
from jax.experimental import pallas as pl
from jax.experimental.pallas import tpu as pltpu

D_MODEL = 2048
N_CTX_SEQ = 32
CTX_LEN = 256
N_LAT_SEQ = 2
LAT_LEN = 1024
PAST_LEN = 512
N_CTX = N_CTX_SEQ * CTX_LEN
N_LAT = N_LAT_SEQ * LAT_LEN
N_TOK = N_CTX + N_LAT
GRID_W = 64
N_HEADS = 16
N_KV_HEADS = 4
HEAD_DIM = 128
KV_GROUP = N_HEADS // N_KV_HEADS
ROPE_THETA = 10000.0
RNN_BLOCKS = 16
RNN_BLOCK_DIM = 128
RG_C = 8.0
N_EXPERTS = 32
TOP_K = 4
D_FF = 2048
SWIGLU_LIMIT = 7.0
SWIGLU_ALPHA = 1.702
EPS = 1e-6
Q_COLS = N_HEADS * HEAD_DIM
KV_COLS = N_KV_HEADS * HEAD_DIM
IN_COLS = Q_COLS + 2 * KV_COLS + 4 * D_MODEL
COL_K = Q_COLS
COL_XR = Q_COLS + 2 * KV_COLS
COL_YR = COL_XR + D_MODEL
COL_GA = COL_YR + D_MODEL
COL_GR = COL_GA + D_MODEL

V7X_VMEM_BYTES = 64 * 1024 * 1024
VMEM_LIMIT = 56 * 1024 * 1024
EXPERT_VMEM_LIMIT = 60 * 1024 * 1024

ROW_TILE = 256
SUPER_TILES = 8
SUPER_ROWS = ROW_TILE * SUPER_TILES
N_ASSIGN = N_TOK * TOP_K
N_ROWS = N_ASSIGN + N_EXPERTS * ROW_TILE
N_ROW_TILES = N_ROWS // ROW_TILE
N_SUPER = N_ROW_TILES // SUPER_TILES + N_EXPERTS
FF_CHUNK = 512
N_FF_CHUNKS = D_FF // FF_CHUNK

BF16 = jnp.bfloat16
F32 = jnp.float32


def _params(semantics, vmem=VMEM_LIMIT):
    return pltpu.CompilerParams(dimension_semantics=semantics, vmem_limit_bytes=vmem)


def _rms_scale(x):
    return lax.rsqrt(jnp.mean(x * x, axis=-1, keepdims=True) + EPS)


def _sigmoid(x):
    return 1.0 / (1.0 + jnp.exp(-x))


def _mod_body(c_ref, w_ref, b_ref, o_ref):
    c = c_ref[...]
    a = (c * _sigmoid(c)).astype(BF16)
    o_ref[...] = jnp.dot(a, w_ref[...].astype(BF16), preferred_element_type=F32) + b_ref[...]


def modulation(cond8, w_mod, b_mod):
    tn = 1024
    n = w_mod.shape[1]
    return pl.pallas_call(
        _mod_body,
        grid=(n // tn,),
        in_specs=[
            pl.BlockSpec((8, D_MODEL), lambda j: (0, 0)),
            pl.BlockSpec((D_MODEL, tn), lambda j: (0, j)),
            pl.BlockSpec((1, tn), lambda j: (0, j)),
        ],
        out_specs=pl.BlockSpec((8, tn), lambda j: (0, j)),
        out_shape=jax.ShapeDtypeStruct((8, n), F32),
        compiler_params=_params(("arbitrary",)),
        name="modulation",
    )(cond8, w_mod, b_mod.reshape(1, n))


def _prenorm_body(x_ref, g_ref, sh_ref, sc_ref, o_ref):
    x = x_ref[...]
    y = x * _rms_scale(x) * g_ref[...]
    o_ref[...] = (y * (1.0 + sc_ref[...]) + sh_ref[...]).astype(o_ref.dtype)


def prenorm_modulate(x, g, shift, scale, group_of_block, tm):
    m = x.shape[0]
    gmap = lambda i: (group_of_block(i), 0, 0)
    return pl.pallas_call(
        _prenorm_body,
        grid=(m // tm,),
        in_specs=[
            pl.BlockSpec((tm, D_MODEL), lambda i: (i, 0)),
            pl.BlockSpec((1, D_MODEL), lambda i: (0, 0)),
            pl.BlockSpec((None, 1, D_MODEL), gmap),
            pl.BlockSpec((None, 1, D_MODEL), gmap),
        ],
        out_specs=pl.BlockSpec((tm, D_MODEL), lambda i: (i, 0)),
        out_shape=jax.ShapeDtypeStruct((m, D_MODEL), BF16),
        compiler_params=_params(("arbitrary",)),
        name="prenorm_modulate",
    )(x, g.reshape(1, D_MODEL), shift, scale)


def _inproj_body(h_ref, w_ref, o_ref, wbf_ref):
    @pl.when(pl.program_id(1) == 0)
    def _():
        wbf_ref[...] = w_ref[...].astype(BF16)

    o_ref[...] = jnp.dot(h_ref[...], wbf_ref[...], preferred_element_type=F32)


def in_projection(h, w_in):
    m = h.shape[0]
    tm, tn = 1024, 1024
    return pl.pallas_call(
        _inproj_body,
        grid=(IN_COLS // tn, m // tm),
        in_specs=[
            pl.BlockSpec((tm, D_MODEL), lambda j, i: (i, 0)),
            pl.BlockSpec((D_MODEL, tn), lambda j, i: (0, j)),
        ],
        out_specs=pl.BlockSpec((tm, tn), lambda j, i: (i, j)),
        out_shape=jax.ShapeDtypeStruct((m, IN_COLS), F32),
        scratch_shapes=[pltpu.VMEM((D_MODEL, tn), BF16)],
        compiler_params=_params(("arbitrary", "arbitrary")),
        name="in_projection",
    )(h, w_in)


def _rope(x, cos, sin_lo, sin_hi):
    return x * cos + pltpu.roll(x, 96, 1) * sin_lo + pltpu.roll(x, 32, 1) * sin_hi


def _head_norm(x, g):
    return x * _rms_scale(x) * g


def _softmax_pv(score_blocks, value_blocks):
    m = None
    for s in score_blocks:
        mi = jnp.max(s, axis=-1, keepdims=True)
        m = mi if m is None else jnp.maximum(m, mi)
    ps = [jnp.exp(s - m) for s in score_blocks]
    denom = None
    for p in ps:
        li = jnp.sum(p, axis=-1, keepdims=True)
        denom = li if denom is None else denom + li
    inv = 1.0 / denom
    out = None
    for p, v in zip(ps, value_blocks):
        o = jnp.dot((p * inv).astype(BF16), v, preferred_element_type=F32)
        out = o if out is None else out + o
    return out


def _attn_ctx_body(q_ref, kv_ref, gq_ref, gk_ref, o_ref, ko_ref, vo_ref):
    tq = q_ref.shape[0]
    scale = HEAD_DIM ** -0.5
    gq = gq_ref[...]
    gk = gk_ref[...]
    for g in range(N_KV_HEADS):
        kcols = slice(g * HEAD_DIM, (g + 1) * HEAD_DIM)
        kn = _head_norm(kv_ref[:, kcols], gk)
        v = kv_ref[:, KV_COLS + g * HEAD_DIM:KV_COLS + (g + 1) * HEAD_DIM]
        ko_ref[pl.ds(g, tq, stride=N_KV_HEADS), :] = kn
        vo_ref[pl.ds(g, tq, stride=N_KV_HEADS), :] = v
        qs = []
        for hh in range(KV_GROUP):
            h = g * KV_GROUP + hh
            qs.append(_head_norm(q_ref[:, h * HEAD_DIM:(h + 1) * HEAD_DIM], gq).astype(BF16))
        q4 = jnp.concatenate(qs, axis=0)
        s = lax.dot_general(q4, kn.astype(BF16), (((1,), (1,)), ((), ())),
                            preferred_element_type=F32) * scale
        o = _softmax_pv([s], [v.astype(BF16)])
        for hh in range(KV_GROUP):
            h = g * KV_GROUP + hh
            o_ref[:, h * HEAD_DIM:(h + 1) * HEAD_DIM] = o[hh * tq:(hh + 1) * tq].astype(o_ref.dtype)


def attention_ctx(z, g_q, g_k):
    nb = N_CTX_SEQ
    t = CTX_LEN
    return pl.pallas_call(
        _attn_ctx_body,
        grid=(nb,),
        in_specs=[
            pl.BlockSpec((t, Q_COLS), lambda b: (b, 0)),
            pl.BlockSpec((t, 2 * KV_COLS), lambda b: (b, COL_K // (2 * KV_COLS))),
            pl.BlockSpec((1, HEAD_DIM), lambda b: (0, 0)),
            pl.BlockSpec((1, HEAD_DIM), lambda b: (0, 0)),
        ],
        out_specs=[
            pl.BlockSpec((t, Q_COLS), lambda b: (b, 0)),
            pl.BlockSpec((t * N_KV_HEADS, HEAD_DIM), lambda b: (b, 0)),
            pl.BlockSpec((t * N_KV_HEADS, HEAD_DIM), lambda b: (b, 0)),
        ],
        out_shape=[
            jax.ShapeDtypeStruct((N_CTX, Q_COLS), BF16),
            jax.ShapeDtypeStruct((N_CTX * N_KV_HEADS, HEAD_DIM), F32),
            jax.ShapeDtypeStruct((N_CTX * N_KV_HEADS, HEAD_DIM), F32),
        ],
        compiler_params=_params(("arbitrary",)),
        name="attention_ctx",
    )(z, z, g_q.reshape(1, HEAD_DIM), g_k.reshape(1, HEAD_DIM))


def _attn_lat_body(q_ref, kv_ref, ck_ref, cv_ref, cos_ref, slo_ref, shi_ref, gq_ref, gk_ref,
                   o_ref, kr_ref):
    tq = q_ref.shape[0]
    qb = pl.program_id(1)
    scale = HEAD_DIM ** -0.5
    gq = gq_ref[...]

    @pl.when(qb == 0)
    def _():
        gk = gk_ref[...]
        for g in range(N_KV_HEADS):
            kcols = slice(g * HEAD_DIM, (g + 1) * HEAD_DIM)
            kn = _head_norm(kv_ref[:, kcols], gk)
            kr_ref[:, kcols] = _rope(kn, cos_ref[...], slo_ref[...], shi_ref[...]).astype(BF16)

    row0 = pl.multiple_of(qb * tq, tq)
    cos = cos_ref[pl.ds(row0, tq), :]
    slo = slo_ref[pl.ds(row0, tq), :]
    shi = shi_ref[pl.ds(row0, tq), :]
    for g in range(N_KV_HEADS):
        kcols = slice(g * HEAD_DIM, (g + 1) * HEAD_DIM)
        qs = []
        for hh in range(KV_GROUP):
            h = g * KV_GROUP + hh
            qn = _head_norm(q_ref[:, h * HEAD_DIM:(h + 1) * HEAD_DIM], gq)
            qs.append(_rope(qn, cos, slo, shi).astype(BF16))
        q4 = jnp.concatenate(qs, axis=0)
        dn = (((1,), (1,)), ((), ()))
        s_past = lax.dot_general(q4, ck_ref[:, kcols].astype(BF16), dn,
                                 preferred_element_type=F32) * scale
        s_new = lax.dot_general(q4, kr_ref[:, kcols], dn, preferred_element_type=F32) * scale
        v_past = cv_ref[:, kcols].astype(BF16)
        v_new = kv_ref[:, KV_COLS + g * HEAD_DIM:KV_COLS + (g + 1) * HEAD_DIM].astype(BF16)
        o = _softmax_pv([s_past, s_new], [v_past, v_new])
        for hh in range(KV_GROUP):
            h = g * KV_GROUP + hh
            o_ref[:, h * HEAD_DIM:(h + 1) * HEAD_DIM] = o[hh * tq:(hh + 1) * tq].astype(o_ref.dtype)


def attention_lat(z, cache_k, cache_v, rope_tabs, g_q, g_k):
    tq = 256
    nq = LAT_LEN // tq
    cos, slo, shi = rope_tabs
    tab = pl.BlockSpec((LAT_LEN, HEAD_DIM), lambda b, q: (0, 0))
    return pl.pallas_call(
        _attn_lat_body,
        grid=(N_LAT_SEQ, nq),
        in_specs=[
            pl.BlockSpec((tq, Q_COLS), lambda b, q: (b * nq + q, 0)),
            pl.BlockSpec((LAT_LEN, 2 * KV_COLS), lambda b, q: (b, COL_K // (2 * KV_COLS))),
            pl.BlockSpec((None, PAST_LEN, KV_COLS), lambda b, q: (b, 0, 0)),
            pl.BlockSpec((None, PAST_LEN, KV_COLS), lambda b, q: (b, 0, 0)),
            tab, tab, tab,
            pl.BlockSpec((1, HEAD_DIM), lambda b, q: (0, 0)),
            pl.BlockSpec((1, HEAD_DIM), lambda b, q: (0, 0)),
        ],
        out_specs=pl.BlockSpec((tq, Q_COLS), lambda b, q: (b * nq + q, 0)),
        out_shape=jax.ShapeDtypeStruct((N_LAT, Q_COLS), BF16),
        scratch_shapes=[pltpu.VMEM((LAT_LEN, KV_COLS), BF16)],
        compiler_params=_params(("arbitrary", "arbitrary")),
        name="attention_lat",
    )(z, z, cache_k, cache_v, cos, slo, shi, g_q.reshape(1, HEAD_DIM), g_k.reshape(1, HEAD_DIM))


def _rope_tables():
    t = np.arange(LAT_LEN)
    row = jnp.asarray(t // GRID_W, F32)
    col = jnp.asarray(t % GRID_W, F32)
    nf = HEAD_DIM // 4
    inv_freq = ROPE_THETA ** (-jnp.arange(nf, dtype=F32) / nf)
    ang_row = row[:, None] * inv_freq[None, :]
    ang_col = col[:, None] * inv_freq[None, :]
    ang = jnp.concatenate([ang_row, ang_row, ang_col, ang_col], axis=1)
    cos = jnp.cos(ang)
    sin = jnp.sin(ang)
    first = jnp.asarray((np.arange(HEAD_DIM) % (2 * nf)) < nf)[None, :]
    return cos, jnp.where(first, -sin, 0.0), jnp.where(first, 0.0, sin)


RNN_ROWS = 2048
RNN_COLS = 512
RNN_SUB = RNN_COLS // RNN_BLOCK_DIM


def _gelu_tanh(y):
    return 0.5 * y * (1.0 + jnp.tanh(0.7978845608028654 * (y + 0.044715 * (y * y * y))))


def _rglru_body(seq_len, xr_ref, yr_ref, cw_ref, cb_ref, wg_ref, bg_ref, lam_ref, h0f_ref, h0b_ref,
                o_ref, hf_ref, hb_ref, xs_ref, af_ref, bf_ref, ab_ref, bb_ref):
    n_seq = RNN_ROWS // seq_len
    for n in range(RNN_SUB):
        cols = slice(n * RNN_BLOCK_DIM, (n + 1) * RNN_BLOCK_DIM)
        for s in range(n_seq):
            xs_ref[n, pl.ds(s, seq_len, stride=n_seq), :] = xr_ref[s * seq_len:(s + 1) * seq_len, cols]

    row = lax.broadcasted_iota(jnp.int32, (RNN_ROWS, 1), 0)
    lam = lam_ref[...]
    softplus_neg = jnp.maximum(-lam, 0.0) + jnp.log(1.0 + jnp.exp(-jnp.abs(lam)))
    rate = softplus_neg * (-RG_C * 1.4426950408889634)
    for n in range(RNN_SUB):
        cols = slice(n * RNN_BLOCK_DIM, (n + 1) * RNN_BLOCK_DIM)
        x = xs_ref[n]
        x_m1 = jnp.where(row >= n_seq, pltpu.roll(x, n_seq, 0), 0.0)
        x_p1 = jnp.where(row < RNN_ROWS - n_seq, pltpu.roll(x, RNN_ROWS - n_seq, 0), 0.0)
        x_p2 = jnp.where(row < RNN_ROWS - 2 * n_seq, pltpu.roll(x, RNN_ROWS - 2 * n_seq, 0), 0.0)
        xn = (cb_ref[:, cols] + x_m1 * cw_ref[0:1, cols] + x * cw_ref[1:2, cols]
              + x_p1 * cw_ref[2:3, cols] + x_p2 * cw_ref[3:4, cols])
        pre = jnp.dot(xn.astype(BF16), wg_ref[n], preferred_element_type=F32) + bg_ref[n]
        for d, (a_ref, b_ref) in enumerate(((af_ref, bf_ref), (ab_ref, bb_ref))):
            r = 0.5 * jnp.tanh(0.5 * pre[:, (2 * d) * RNN_BLOCK_DIM:(2 * d + 1) * RNN_BLOCK_DIM]) + 0.5
            gate_in = 0.5 * jnp.tanh(
                0.5 * pre[:, (2 * d + 1) * RNN_BLOCK_DIM:(2 * d + 2) * RNN_BLOCK_DIM]) + 0.5
            a = jnp.exp2(r * rate[d:d + 1, cols])
            v = 1.0 - a * a
            a_ref[n] = a
            b_ref[n] = (v * lax.rsqrt(jnp.maximum(v, 1e-30))) * (gate_in * xn)

    def step(t, carry):
        rows_f = pl.ds(pl.multiple_of(t * n_seq, n_seq), n_seq)
        rows_b = pl.ds(pl.multiple_of((seq_len - 1 - t) * n_seq, n_seq), n_seq)
        out = []
        for n in range(RNN_SUB):
            hf = af_ref[n, rows_f, :] * carry[2 * n] + bf_ref[n, rows_f, :]
            hb = ab_ref[n, rows_b, :] * carry[2 * n + 1] + bb_ref[n, rows_b, :]
            bf_ref[n, rows_f, :] = hf
            bb_ref[n, rows_b, :] = hb
            out += [hf, hb]
        return tuple(out)

    init = []
    for n in range(RNN_SUB):
        cols = slice(n * RNN_BLOCK_DIM, (n + 1) * RNN_BLOCK_DIM)
        init += [h0f_ref[:, 0, cols], h0b_ref[:, 0, cols]]
    last = lax.fori_loop(0, seq_len, step, tuple(init), unroll=8)
    for n in range(RNN_SUB):
        cols = slice(n * RNN_BLOCK_DIM, (n + 1) * RNN_BLOCK_DIM)
        hf_ref[:, 0, cols] = last[2 * n]
        hb_ref[:, 0, cols] = last[2 * n + 1]
        bf_ref[n] = bf_ref[n] + bb_ref[n]
        for s in range(n_seq):
            rows = slice(s * seq_len, (s + 1) * seq_len)
            h_sum = bf_ref[n, pl.ds(s, seq_len, stride=n_seq), :]
            o_ref[rows, cols] = (h_sum * _gelu_tanh(yr_ref[rows, cols])).astype(o_ref.dtype)


def rglru_mixer(z, seq_len, conv_w, conv_b, w_gates, b_gates, lam, h0_f, h0_b):
    m = z.shape[0]
    n_seq_total = m // seq_len
    n_seq = RNN_ROWS // seq_len
    cblk = lambda base: (lambda r, c: (r, base // RNN_COLS + c))
    state_spec = pl.BlockSpec((n_seq, 1, RNN_COLS), lambda r, c: (r, 0, c))
    return pl.pallas_call(
        functools.partial(_rglru_body, seq_len),
        grid=(m // RNN_ROWS, D_MODEL // RNN_COLS),
        in_specs=[
            pl.BlockSpec((RNN_ROWS, RNN_COLS), cblk(COL_XR)),
            pl.BlockSpec((RNN_ROWS, RNN_COLS), cblk(COL_YR)),
            pl.BlockSpec((4, RNN_COLS), lambda r, c: (0, c)),
            pl.BlockSpec((1, RNN_COLS), lambda r, c: (0, c)),
            pl.BlockSpec((RNN_SUB, RNN_BLOCK_DIM, 4 * RNN_BLOCK_DIM), lambda r, c: (c, 0, 0)),
            pl.BlockSpec((RNN_SUB, 1, 4 * RNN_BLOCK_DIM), lambda r, c: (c, 0, 0)),
            pl.BlockSpec((2, RNN_COLS), lambda r, c: (0, c)),
            state_spec, state_spec,
        ],
        out_specs=[
            pl.BlockSpec((RNN_ROWS, RNN_COLS), lambda r, c: (r, c)),
            state_spec, state_spec,
        ],
        out_shape=[
            jax.ShapeDtypeStruct((m, D_MODEL), BF16),
            jax.ShapeDtypeStruct((n_seq_total, 1, D_MODEL), F32),
            jax.ShapeDtypeStruct((n_seq_total, 1, D_MODEL), F32),
        ],
        scratch_shapes=[pltpu.VMEM((RNN_SUB, RNN_ROWS, RNN_BLOCK_DIM), F32) for _ in range(5)],
        compiler_params=_params(("arbitrary", "arbitrary")),
        name="rglru_mixer_t%d" % seq_len,
    )(z, z, conv_w, conv_b.reshape(1, D_MODEL), w_gates, b_gates, lam, h0_f, h0_b)


def _merge_body(a_ref, r_ref, wa_ref, wr_ref, ga_ref, gr_ref, o_ref, wa_bf, wr_bf):
    @pl.when(pl.program_id(1) == 0)
    def _():
        wa_bf[...] = wa_ref[...].astype(BF16)
        wr_bf[...] = wr_ref[...].astype(BF16)

    pa = jnp.dot(a_ref[...], wa_bf[...], preferred_element_type=F32)
    pr = jnp.dot(r_ref[...], wr_bf[...], preferred_element_type=F32)
    o_ref[...] = (_sigmoid(ga_ref[...]) * pa + _sigmoid(gr_ref[...]) * pr).astype(o_ref.dtype)


def gated_merge(attn, rnn, z, w_o_attn, w_o_rnn):
    m = attn.shape[0]
    tm, tn = 1024, 512
    return pl.pallas_call(
        _merge_body,
        grid=(D_MODEL // tn, m // tm),
        in_specs=[
            pl.BlockSpec((tm, Q_COLS), lambda j, i: (i, 0)),
            pl.BlockSpec((tm, D_MODEL), lambda j, i: (i, 0)),
            pl.BlockSpec((Q_COLS, tn), lambda j, i: (0, j)),
            pl.BlockSpec((D_MODEL, tn), lambda j, i: (0, j)),
            pl.BlockSpec((tm, tn), lambda j, i: (i, COL_GA // tn + j)),
            pl.BlockSpec((tm, tn), lambda j, i: (i, COL_GR // tn + j)),
        ],
        out_specs=pl.BlockSpec((tm, tn), lambda j, i: (i, j)),
        out_shape=jax.ShapeDtypeStruct((m, D_MODEL), BF16),
        scratch_shapes=[pltpu.VMEM((Q_COLS, tn), BF16), pltpu.VMEM((D_MODEL, tn), BF16)],
        compiler_params=_params(("arbitrary", "arbitrary")),
        name="gated_merge",
    )(attn, rnn, w_o_attn, w_o_rnn, z, z)


POST_TM = 512
HALF_D = D_MODEL // 2
WORD_ROWS = HALF_D // 128
POST_CTX_BLOCKS = N_CTX // POST_TM
LAT_BLOCKS_PER_SEQ = LAT_LEN // POST_TM


def _post_group(i):
    return jnp.where(i < POST_CTX_BLOCKS, 0, 1 + (i - POST_CTX_BLOCKS) // LAT_BLOCKS_PER_SEQ)


def _postmix_body(mc_ref, ml_ref, xc_ref, xl_ref, wo_ref, gpm_ref, gt1_ref, gpf_ref, sh2_ref, sc2_ref,
                  wr_ref, br_ref, x1_ref, h2_ref, e_ref, gate_ref, rank_ref, cnt_ref, carry_ref):
    i = pl.program_id(0)
    tm = POST_TM

    @pl.when(i == 0)
    def _():
        carry_ref[...] = jnp.zeros_like(carry_ref)

    is_ctx = i < POST_CTX_BLOCKS
    merged = jnp.where(is_ctx, mc_ref[...], ml_ref[...])
    x = jnp.where(is_ctx, xc_ref[...], xl_ref[...])
    o = jnp.dot(merged, wo_ref[...], preferred_element_type=F32)
    x1 = x + gt1_ref[...] * (o * _rms_scale(o) * gpm_ref[...])
    x1_ref[...] = x1
    h2 = (x1 * _rms_scale(x1) * gpf_ref[...]) * (1.0 + sc2_ref[...]) + sh2_ref[...]
    h2_bf = h2.astype(BF16)
    bits = lax.bitcast_convert_type(h2_bf.astype(F32), jnp.uint32)
    words = (lax.shift_right_logical(bits[:, :HALF_D], jnp.uint32(16))
             | (bits[:, HALF_D:] & jnp.uint32(0xFFFF0000)))
    for c in range(WORD_ROWS):
        h2_ref[pl.ds(c, tm, stride=WORD_ROWS), :] = words[:, c * 128:(c + 1) * 128]

    logits = jnp.dot(h2_bf, wr_ref[...].astype(BF16), preferred_element_type=F32) + br_ref[...]
    lane = lax.broadcasted_iota(jnp.int32, (tm, N_EXPERTS), 1)
    work = logits
    chosen = jnp.zeros((tm, N_EXPERTS), F32)
    sels, vals, idxs = [], [], []
    for _ in range(TOP_K):
        mx = jnp.max(work, axis=-1, keepdims=True)
        idx = jnp.min(jnp.where(work == mx, lane, N_EXPERTS), axis=-1, keepdims=True)
        sel = lane == idx
        work = jnp.where(sel, -jnp.inf, work)
        chosen = jnp.where(sel, 1.0, chosen)
        sels.append(sel)
        vals.append(mx)
        idxs.append(idx)
    exps = [jnp.exp(v - vals[0]) for v in vals]
    inv = 1.0 / (exps[0] + exps[1] + exps[2] + exps[3])

    r_io = lax.broadcasted_iota(jnp.int32, (tm, tm), 0)
    c_io = lax.broadcasted_iota(jnp.int32, (tm, tm), 1)
    lower = jnp.where(c_io < r_io, 1.0, 0.0).astype(BF16)
    before = jnp.dot(lower, chosen.astype(BF16), preferred_element_type=F32) + carry_ref[...]
    carry_ref[...] = carry_ref[...] + jnp.sum(chosen, axis=0, keepdims=True)
    cnt_ref[...] = carry_ref[...]

    lane_k = lax.broadcasted_iota(jnp.int32, (tm, TOP_K), 1)
    e_out = jnp.zeros((tm, TOP_K), jnp.int32)
    g_out = jnp.zeros((tm, TOP_K), F32)
    r_out = jnp.zeros((tm, TOP_K), jnp.int32)
    for k in range(TOP_K):
        rk = jnp.sum(jnp.where(sels[k], before, 0.0), axis=-1, keepdims=True).astype(jnp.int32)
        e_out = jnp.where(lane_k == k, idxs[k], e_out)
        g_out = jnp.where(lane_k == k, exps[k] * inv, g_out)
        r_out = jnp.where(lane_k == k, rk, r_out)
    e_ref[...] = e_out
    gate_ref[...] = g_out
    rank_ref[...] = r_out


def post_mix_router(merged_ctx, merged_lat, x_ctx, x_lat, w_out_bf, g_post_mix, gt1, g_pre_ffn, sh2, sc2,
                    w_router, b_router):
    tm = POST_TM
    ctx_map = lambda i: (jnp.minimum(i, POST_CTX_BLOCKS - 1), 0)
    lat_map = lambda i: (jnp.maximum(i - POST_CTX_BLOCKS, 0), 0)
    gmap = lambda i: (_post_group(i), 0, 0)
    row = lambda i: (i, 0)
    const = lambda i: (0, 0)
    vec = pl.BlockSpec((1, D_MODEL), const)
    gvec = pl.BlockSpec((None, 1, D_MODEL), gmap)
    return pl.pallas_call(
        _postmix_body,
        grid=(N_TOK // tm,),
        in_specs=[
            pl.BlockSpec((tm, D_MODEL), ctx_map),
            pl.BlockSpec((tm, D_MODEL), lat_map),
            pl.BlockSpec((tm, D_MODEL), ctx_map),
            pl.BlockSpec((tm, D_MODEL), lat_map),
            pl.BlockSpec((D_MODEL, D_MODEL), const),
            vec, gvec, vec, gvec, gvec,
            pl.BlockSpec((D_MODEL, N_EXPERTS), const),
            pl.BlockSpec((1, N_EXPERTS), const),
        ],
        out_specs=[
            pl.BlockSpec((tm, D_MODEL), row),
            pl.BlockSpec((tm * WORD_ROWS, 128), row),
            pl.BlockSpec((tm, TOP_K), row),
            pl.BlockSpec((tm, TOP_K), row),
            pl.BlockSpec((tm, TOP_K), row),
            pl.BlockSpec((1, N_EXPERTS), const),
        ],
        out_shape=[
            jax.ShapeDtypeStruct((N_TOK, D_MODEL), F32),
            jax.ShapeDtypeStruct((N_TOK * WORD_ROWS, 128), jnp.uint32),
            jax.ShapeDtypeStruct((N_TOK, TOP_K), jnp.int32),
            jax.ShapeDtypeStruct((N_TOK, TOP_K), F32),
            jax.ShapeDtypeStruct((N_TOK, TOP_K), jnp.int32),
            jax.ShapeDtypeStruct((1, N_EXPERTS), F32),
        ],
        scratch_shapes=[pltpu.VMEM((1, N_EXPERTS), F32)],
        compiler_params=_params(("arbitrary",)),
        name="post_mix_router",
    )(merged_ctx, merged_lat, x_ctx, x_lat, w_out_bf, g_post_mix.reshape(1, D_MODEL), gt1,
      g_pre_ffn.reshape(1, D_MODEL), sh2, sc2, w_router, b_router.reshape(1, N_EXPERTS))


GATHER_ROWS = ROW_TILE // (2 * N_FF_CHUNKS)
GATHER_PRIORITY = 1


def _unpack_tile(xbuf_ref, slot, i):
    base = pl.multiple_of(i * (ROW_TILE * WORD_ROWS), ROW_TILE * WORD_ROWS)
    lo, hi = [], []
    for c in range(WORD_ROWS):
        words = xbuf_ref[slot, pl.ds(base + c, ROW_TILE, stride=WORD_ROWS), :]
        lo.append(lax.bitcast_convert_type(lax.shift_left(words, jnp.uint32(16)), F32).astype(BF16))
        hi.append(lax.bitcast_convert_type(words & jnp.uint32(0xFFFF0000), F32).astype(BF16))
    return jnp.concatenate(lo + hi, axis=1)


def _moe_body(exp_ref, row_ref, nsub_ref, nzero_ref, tok_ref, h_ref, wg_ref, wl_ref, wd_ref, bg_ref, bl_ref,
              bd_ref, y_ref, xbuf_ref, act_ref, wg_bf, wl_bf, wd_bf, stage_ref, idx_ref, pend_ref,
              xsem, isem, ysem):
    s = pl.program_id(0)
    j = pl.program_id(1)
    n_sub = nsub_ref[s]
    row_start = row_ref[s]
    n_next = nsub_ref[jnp.minimum(s + 1, N_SUPER - 1)]

    def idx_copy(p):
        tile0 = pl.multiple_of(row_ref[p], ROW_TILE) // ROW_TILE
        return pltpu.make_async_copy(tok_ref.at[pl.ds(tile0, SUPER_TILES)], idx_ref.at[p % 2],
                                     isem.at[p % 2])

    def gather_row(slot, tile, col, r):
        t = idx_ref[slot, tile, 0, col]
        src = h_ref.at[pl.ds(pl.multiple_of(t * WORD_ROWS, WORD_ROWS), WORD_ROWS), :]
        dst = xbuf_ref.at[slot, pl.ds(pl.multiple_of(r * WORD_ROWS, WORD_ROWS), WORD_ROWS), :]
        pltpu.make_async_copy(src, dst, xsem.at[slot]).start(priority=GATHER_PRIORITY)

    def gather_chunk(step, i):
        first = (step * n_sub + i) * GATHER_ROWS
        tile = lax.shift_right_logical(first, 8)
        col = jnp.bitwise_and(first, ROW_TILE - 1)
        for g in range(GATHER_ROWS):
            gather_row((s + 1) % 2, tile, col + g, first + g)

    def gather_range(p, first, last):
        def issue(r, _):
            gather_row(p % 2, lax.shift_right_logical(r, 8), jnp.bitwise_and(r, ROW_TILE - 1), r)
            return 0

        lax.fori_loop(first, last, issue, 0)

    def wait_rows(slot, count):
        @pl.when(count > 0)
        def _():
            n = pl.multiple_of(count * WORD_ROWS, ROW_TILE * WORD_ROWS)
            pltpu.make_async_copy(h_ref.at[pl.ds(0, n), :], xbuf_ref.at[slot, pl.ds(0, n), :],
                                  xsem.at[slot]).wait()

    @pl.when(jnp.logical_and(s == 0, j == 0))
    def _():
        pend_ref[0] = 0
        pend_ref[1] = 0

    def drain_stage(slot):
        @pl.when(pend_ref[slot] == 1)
        def _():
            pltpu.make_async_copy(stage_ref.at[slot], stage_ref.at[slot], ysem.at[slot]).wait()
            pend_ref[slot] = 0

    @pl.when(jnp.logical_and(s == 0, j == 0))
    def _():
        idx_copy(0).start()
        idx_copy(0).wait()
        gather_range(0, 0, n_sub * ROW_TILE)
        wait_rows(0, n_sub * ROW_TILE)
        idx_copy(1).start()

    @pl.when(jnp.logical_and(s > 0, j == 0))
    def _():
        wait_rows(s % 2, jnp.maximum(nsub_ref[jnp.maximum(s - 1, 0)], n_sub) * ROW_TILE)

    @pl.when(jnp.logical_and(j == 0, s + 1 < N_SUPER))
    def _():
        idx_copy(s + 1).wait()

    @pl.when(jnp.logical_and(j == 0, s + 2 < N_SUPER))
    def _():
        idx_copy(s + 2).start()

    @pl.when(jnp.logical_and(j == 0, jnp.logical_and(n_sub == 0, s + 1 < N_SUPER)))
    def _():
        gather_range(s + 1, 0, n_next * ROW_TILE)

    @pl.when(jnp.logical_and(j < N_FF_CHUNKS, n_sub > 0))
    def _():
        wg_bf[...] = wg_ref[...].astype(BF16)
        wl_bf[...] = wl_ref[...].astype(BF16)
        bg = bg_ref[...]
        bl = bl_ref[...]

        def sub(i, _):
            rows = pl.ds(pl.multiple_of(i * ROW_TILE, ROW_TILE), ROW_TILE)
            xt = _unpack_tile(xbuf_ref, s % 2, i)
            glu = jnp.minimum(jnp.dot(xt, wg_bf[...], preferred_element_type=F32) + bg, SWIGLU_LIMIT)
            lin = jnp.clip(jnp.dot(xt, wl_bf[...], preferred_element_type=F32) + bl,
                           -SWIGLU_LIMIT, SWIGLU_LIMIT)
            gather_chunk(j, i)
            act = glu * _sigmoid(SWIGLU_ALPHA * glu) * (lin + 1.0)
            act_ref[j, rows, :] = act.astype(BF16)
            return 0

        lax.fori_loop(0, n_sub, sub, 0)

    for cc in range(N_FF_CHUNKS):
        @pl.when(jnp.logical_and(j == N_FF_CHUNKS + cc, n_sub > 0))
        def _(cc=cc):
            wd_bf[...] = wd_ref[...].astype(BF16)
            bd = bd_ref[...]

            def out_copy(i, slot):
                dst = y_ref.at[pl.ds(pl.multiple_of(row_start + i * ROW_TILE, ROW_TILE), ROW_TILE),
                               cc * FF_CHUNK:(cc + 1) * FF_CHUNK]
                return pltpu.make_async_copy(stage_ref.at[slot], dst, ysem.at[slot])

            def sub(i, _):
                rows = pl.ds(pl.multiple_of(i * ROW_TILE, ROW_TILE), ROW_TILE)
                slot = i % 2
                acc = bd
                for c in range(N_FF_CHUNKS):
                    acc = acc + jnp.dot(act_ref[c, rows, :], wd_bf[c * FF_CHUNK:(c + 1) * FF_CHUNK, :],
                                        preferred_element_type=F32)
                gather_chunk(N_FF_CHUNKS + cc, i)

                drain_stage(slot)
                stage_ref[slot] = acc
                out_copy(i, slot).start()
                pend_ref[slot] = 1
                return 0

            lax.fori_loop(0, n_sub, sub, 0)
            if cc == N_FF_CHUNKS - 1:
                gather_range(s + 1, n_sub * ROW_TILE, n_next * ROW_TILE)

    n_zero = nzero_ref[s]

    @pl.when(jnp.logical_and(j == 0, n_zero > 0))
    def _():
        drain_stage(0)
        stage_ref[0] = jnp.zeros((ROW_TILE, FF_CHUNK), F32)

        def zero_copy(i, cc):
            dst = y_ref.at[pl.ds(pl.multiple_of(row_start + i * ROW_TILE, ROW_TILE), ROW_TILE),
                           cc * FF_CHUNK:(cc + 1) * FF_CHUNK]
            return pltpu.make_async_copy(stage_ref.at[0], dst, ysem.at[0])

        def issue(i, _):
            for cc in range(N_FF_CHUNKS):
                zero_copy(i, cc).start()
            return 0

        def drain(i, _):
            for cc in range(N_FF_CHUNKS):
                zero_copy(i, cc).wait()
            return 0

        lax.fori_loop(0, n_zero, issue, 0)
        lax.fori_loop(0, n_zero, drain, 0)

    @pl.when(jnp.logical_and(s == N_SUPER - 1, j == 2 * N_FF_CHUNKS - 1))
    def _():
        drain_stage(0)
        drain_stage(1)


def expert_mlp(h_packed, tok_sorted, sched, w_gate_up, b_gate_up, w_down, b_down):
    exp_of, row_of, nsub_of, nzero_of = sched
    last = N_FF_CHUNKS - 1
    up_of = lambda s, j, n: jnp.where(n[s] > 0, jnp.minimum(j, last), last)
    down_of = lambda s, j, n: jnp.where(n[s] > 0, jnp.maximum(j - N_FF_CHUNKS, 0), last)
    up_chunk = lambda s, j, e, r, n, z: (e[s], 0, up_of(s, j, n))
    lin_chunk = lambda s, j, e, r, n, z: (e[s], 0, N_FF_CHUNKS + up_of(s, j, n))
    down_chunk = lambda s, j, e, r, n, z: (e[s], 0, down_of(s, j, n))
    grid_spec = pltpu.PrefetchScalarGridSpec(
        num_scalar_prefetch=4,
        grid=(N_SUPER, 2 * N_FF_CHUNKS),
        in_specs=[
            pl.BlockSpec(memory_space=pl.ANY),
            pl.BlockSpec(memory_space=pl.ANY),
            pl.BlockSpec((None, D_MODEL, FF_CHUNK), up_chunk),
            pl.BlockSpec((None, D_MODEL, FF_CHUNK), lin_chunk),
            pl.BlockSpec((None, D_FF, FF_CHUNK), down_chunk),
            pl.BlockSpec((None, 1, FF_CHUNK), up_chunk),
            pl.BlockSpec((None, 1, FF_CHUNK), lin_chunk),
            pl.BlockSpec((None, 1, FF_CHUNK), down_chunk),
        ],
        out_specs=pl.BlockSpec(memory_space=pl.ANY),
        scratch_shapes=[
            pltpu.VMEM((2, SUPER_ROWS * WORD_ROWS, 128), jnp.uint32),
            pltpu.VMEM((N_FF_CHUNKS, SUPER_ROWS, FF_CHUNK), BF16),
            pltpu.VMEM((D_MODEL, FF_CHUNK), BF16),
            pltpu.VMEM((D_MODEL, FF_CHUNK), BF16),
            pltpu.VMEM((D_FF, FF_CHUNK), BF16),
            pltpu.VMEM((2, ROW_TILE, FF_CHUNK), F32),
            pltpu.SMEM((2, SUPER_TILES, 1, ROW_TILE), jnp.int32),
            pltpu.SMEM((2,), jnp.int32),
            pltpu.SemaphoreType.DMA((2,)),
            pltpu.SemaphoreType.DMA((2,)),
            pltpu.SemaphoreType.DMA((2,)),
        ],
    )
    tok_tiles = jnp.concatenate([tok_sorted.reshape(N_ROW_TILES, 1, ROW_TILE),
                                 jnp.zeros((SUPER_TILES, 1, ROW_TILE), jnp.int32)], axis=0)
    return pl.pallas_call(
        _moe_body,
        grid_spec=grid_spec,
        out_shape=jax.ShapeDtypeStruct((N_ROWS, D_MODEL), F32),
        compiler_params=_params(("arbitrary", "arbitrary"), vmem=EXPERT_VMEM_LIMIT),
        name="expert_mlp",
    )(exp_of, row_of, nsub_of, nzero_of, tok_tiles, h_packed, w_gate_up, w_gate_up, w_down,
      b_gate_up.reshape(N_EXPERTS, 1, 2 * D_FF), b_gate_up.reshape(N_EXPERTS, 1, 2 * D_FF),
      b_down.reshape(N_EXPERTS, 1, D_MODEL))


COMB_TB = 256


def _combine_start(y_ref, ybuf_ref, pos_ref, sem):
    def issue(t, _):
        for k in range(TOP_K):
            p = pos_ref[0, 0, t * TOP_K + k]
            pltpu.make_async_copy(y_ref.at[pl.ds(p, 1), :], ybuf_ref.at[k, pl.ds(t, 1), :], sem).start()
        return 0

    lax.fori_loop(0, COMB_TB, issue, 0, unroll=4)


def _combine_body(n, pos_ref, pos_next_ref, y_ref, gate_ref, x1_ref, gt2_ref, g_ref, o_ref, ybuf_ref, sem_ref):
    i = pl.program_id(0)
    slot = i % 2

    @pl.when(i == 0)
    def _():
        _combine_start(y_ref, ybuf_ref.at[0], pos_ref, sem_ref.at[0])

    @pl.when(i + 1 < n)
    def _():
        _combine_start(y_ref, ybuf_ref.at[1 - slot], pos_next_ref, sem_ref.at[1 - slot])

    for k in range(TOP_K):
        pltpu.make_async_copy(y_ref.at[pl.ds(0, COMB_TB), :], ybuf_ref.at[slot, k], sem_ref.at[slot]).wait()
    gates = gate_ref[...]
    ffn = gates[:, 0:1] * ybuf_ref[slot, 0]
    for k in range(1, TOP_K):
        ffn = ffn + gates[:, k:k + 1] * ybuf_ref[slot, k]
    o_ref[...] = x1_ref[...] + gt2_ref[...] * (ffn * _rms_scale(ffn) * g_ref[...])


def combine_residual(y_sorted, pos, gates, x1, gt2, g_post_ffn, row_offset, n_rows, group_of_block):
    tb = COMB_TB
    nblk = n_rows // tb
    off = row_offset // tb
    pos3 = pos.reshape(N_TOK // tb, 1, tb * TOP_K)
    smem_blk = lambda f: pl.BlockSpec((1, 1, tb * TOP_K), f, memory_space=pltpu.SMEM)
    return pl.pallas_call(
        functools.partial(_combine_body, nblk),
        grid=(nblk,),
        in_specs=[
            smem_blk(lambda i: (off + i, 0, 0)),
            smem_blk(lambda i: (off + jnp.minimum(i + 1, nblk - 1), 0, 0)),
            pl.BlockSpec(memory_space=pl.ANY),
            pl.BlockSpec((tb, TOP_K), lambda i: (off + i, 0)),
            pl.BlockSpec((tb, D_MODEL), lambda i: (off + i, 0)),
            pl.BlockSpec((None, 1, D_MODEL), lambda i: (group_of_block(i), 0, 0)),
            pl.BlockSpec((1, D_MODEL), lambda i: (0, 0)),
        ],
        out_specs=pl.BlockSpec((tb, D_MODEL), lambda i: (i, 0)),
        out_shape=jax.ShapeDtypeStruct((n_rows, D_MODEL), F32),
        scratch_shapes=[pltpu.VMEM((2, TOP_K, tb, D_MODEL), F32), pltpu.SemaphoreType.DMA((2,))],
        compiler_params=_params(("arbitrary",)),
        name="combine_residual",
    )(pos3, pos3, y_sorted, gates, x1, gt2, g_post_ffn.reshape(1, D_MODEL))


def _routing_tables(e_idx, rank, counts_f):
    counts = counts_f.reshape(N_EXPERTS).astype(jnp.int32)
    n_tiles = (counts + ROW_TILE - 1) // ROW_TILE
    padded = n_tiles * ROW_TILE
    pad_end = jnp.cumsum(padded)
    pad_start = pad_end - padded
    pos = pad_start[e_idx] + rank
    tok_ids = jnp.repeat(jnp.arange(N_TOK, dtype=jnp.int32), TOP_K)
    tok_sorted = jnp.zeros((N_ROWS,), jnp.int32).at[pos.reshape(-1)].set(
        tok_ids, unique_indices=True, mode="drop")
    n_pass = (n_tiles + SUPER_TILES - 1) // SUPER_TILES
    pass_end = jnp.cumsum(n_pass)
    total = pass_end[-1]
    s = jnp.arange(N_SUPER, dtype=jnp.int32)
    s_eff = jnp.minimum(s, total - 1)
    e_of = jnp.minimum(jnp.searchsorted(pass_end, s_eff, side="right"), N_EXPERTS - 1).astype(jnp.int32)
    local = s_eff - (pass_end[e_of] - n_pass[e_of])
    row_of = pad_start[e_of] + local * SUPER_ROWS
    nsub = jnp.minimum(SUPER_TILES, n_tiles[e_of] - local * SUPER_TILES)
    nsub = jnp.where(s < total, nsub, 0).astype(jnp.int32)
    zero_row = pad_end[-1] + (s - total) * SUPER_ROWS
    nzero = jnp.clip((N_ROWS - zero_row) // ROW_TILE, 0, SUPER_TILES)
    nzero = jnp.where(s >= total, nzero, 0).astype(jnp.int32)
    row_of = jnp.where(s < total, row_of, jnp.minimum(zero_row, N_ROWS - ROW_TILE)).astype(jnp.int32)
    return pos.astype(jnp.int32), tok_sorted, (e_of, row_of, nsub, nzero)


def kernel(x_prompt, x_sample, cache_k, cache_v, state_rnn_fwd, state_rnn_bwd, c, c_ctx, w_mod, b_mod, g_pre_mix, w_in, g_q_norm, g_k_norm, conv_w, conv_b, rg_w_a, rg_b_a, rg_w_x, rg_b_x, rg_lambda, w_o_attn, w_o_rnn, w_out, g_post_mix, g_pre_ffn, w_router, b_router, w_gate_up, b_gate_up, w_down, b_down, g_post_ffn):
    l = 0
    x_ctx = x_prompt.reshape(N_CTX, D_MODEL)
    x_lat = x_sample.reshape(N_LAT, D_MODEL)

    cond8 = jnp.concatenate([c_ctx[None, :], c, jnp.zeros((8 - 1 - N_LAT_SEQ, D_MODEL), F32)], axis=0)
    mod = modulation(cond8, w_mod[l], b_mod[l])[:1 + N_LAT_SEQ].reshape(1 + N_LAT_SEQ, 6, 1, D_MODEL)
    sh1, sc1, gt1, sh2, sc2, gt2 = [mod[:, i] for i in range(6)]

    ctx_group = lambda i: 0
    lat_group_1024 = lambda i: 1 + i
    h_ctx = prenorm_modulate(x_ctx, g_pre_mix[l], sh1, sc1, ctx_group, 1024)
    h_lat = prenorm_modulate(x_lat, g_pre_mix[l], sh1, sc1, lat_group_1024, 1024)
    z_ctx = in_projection(h_ctx, w_in[l])
    z_lat = in_projection(h_lat, w_in[l])

    attn_ctx, k_new, v_new = attention_ctx(z_ctx, g_q_norm[l], g_k_norm[l])
    attn_lat = attention_lat(z_lat, cache_k[:, l].reshape(N_LAT_SEQ, PAST_LEN, KV_COLS),
                             cache_v[:, l].reshape(N_LAT_SEQ, PAST_LEN, KV_COLS),
                             _rope_tables(), g_q_norm[l], g_k_norm[l])

    def per_block(w):
        return w.reshape(2, RNN_BLOCKS, 1, RNN_BLOCK_DIM)

    w_gates = jnp.concatenate([rg_w_a[l, 0], rg_w_x[l, 0], rg_w_a[l, 1], rg_w_x[l, 1]], axis=-1).astype(BF16)
    ba, bx = per_block(rg_b_a[l]), per_block(rg_b_x[l])
    b_gates = jnp.concatenate([ba[0], bx[0], ba[1], bx[1]], axis=-1)
    zeros_state = jnp.zeros((N_CTX_SEQ, 1, D_MODEL), F32)
    rnn_ctx, hf_ctx, hb_ctx = rglru_mixer(z_ctx, CTX_LEN, conv_w[l], conv_b[l], w_gates, b_gates,
                                          rg_lambda[l], zeros_state, zeros_state)
    rnn_lat, _, _ = rglru_mixer(z_lat, LAT_LEN, conv_w[l], conv_b[l], w_gates, b_gates, rg_lambda[l],
                                state_rnn_fwd[:, l].reshape(N_LAT_SEQ, 1, D_MODEL),
                                state_rnn_bwd[:, l].reshape(N_LAT_SEQ, 1, D_MODEL))

    merged_ctx = gated_merge(attn_ctx, rnn_ctx, z_ctx, w_o_attn[l], w_o_rnn[l])
    merged_lat = gated_merge(attn_lat, rnn_lat, z_lat, w_o_attn[l], w_o_rnn[l])

    x1, h2, e_idx, gates, rank, counts = post_mix_router(
        merged_ctx, merged_lat, x_ctx, x_lat, w_out[l].astype(BF16), g_post_mix[l], gt1, g_pre_ffn[l],
        sh2, sc2, w_router[l], b_router[l])

    pos, tok_sorted, sched = _routing_tables(e_idx, rank, counts)
    y_sorted = expert_mlp(h2, tok_sorted, sched, w_gate_up[l], b_gate_up[l], w_down[l], b_down[l])

    y_ctx = combine_residual(y_sorted, pos, gates, x1, gt2, g_post_ffn[l], 0, N_CTX, ctx_group)
    y_lat = combine_residual(y_sorted, pos, gates, x1, gt2, g_post_ffn[l], N_CTX, N_LAT,
                             lambda i: 1 + i // (LAT_LEN // COMB_TB))

    return (y_ctx.reshape(N_CTX_SEQ, CTX_LEN, D_MODEL),
            y_lat.reshape(N_LAT_SEQ, LAT_LEN, D_MODEL),
            k_new.reshape(N_CTX_SEQ, 1, CTX_LEN, N_KV_HEADS, HEAD_DIM),
            v_new.reshape(N_CTX_SEQ, 1, CTX_LEN, N_KV_HEADS, HEAD_DIM),
            hf_ctx,
            hb_ctx)
```

```python
import functools

import jax
import jax.numpy as jnp
import numpy as np
from jax import lax
from jax.experimental import pallas as pl
from jax.experimental.pallas import tpu as pltpu

D_MODEL = 2048
N_CTX_SEQ = 32
CTX_LEN = 256
N_LAT_SEQ = 2
LAT_LEN = 1024
PAST_LEN = 512
N_CTX = N_CTX_SEQ * CTX_LEN
N_LAT = N_LAT_SEQ * LAT_LEN
N_TOK = N_CTX + N_LAT
GRID_W = 64
N_HEADS = 16
N_KV_HEADS = 4
HEAD_DIM = 128
KV_GROUP = N_HEADS // N_KV_HEADS
ROPE_THETA = 10000.0
RNN_BLOCKS = 16
RNN_BLOCK_DIM = 128
RG_C = 8.0
N_EXPERTS = 32
TOP_K = 4
D_FF = 2048
SWIGLU_LIMIT = 7.0
SWIGLU_ALPHA = 1.702
EPS = 1e-6
Q_COLS = N_HEADS * HEAD_DIM
KV_COLS = N_KV_HEADS * HEAD_DIM
IN_COLS = Q_COLS + 2 * KV_COLS + 4 * D_MODEL
COL_K = Q_COLS
COL_XR = Q_COLS + 2 * KV_COLS
COL_YR = COL_XR + D_MODEL
COL_GA = COL_YR + D_MODEL
COL_GR = COL_GA + D_MODEL

V7X_VMEM_BYTES = 64 * 1024 * 1024
VMEM_LIMIT = 56 * 1024 * 1024
EXPERT_VMEM_LIMIT = 60 * 1024 * 1024

ROW_TILE = 256
SUPER_TILES = 8
SUPER_ROWS = ROW_TILE * SUPER_TILES
N_ASSIGN = N_TOK * TOP_K
N_ROWS = N_ASSIGN + N_EXPERTS * ROW_TILE
N_ROW_TILES = N_ROWS // ROW_TILE
N_SUPER = N_ROW_TILES // SUPER_TILES + N_EXPERTS
FF_CHUNK = 512
N_FF_CHUNKS = D_FF // FF_CHUNK

BF16 = jnp.bfloat16
F32 = jnp.float32


def _params(semantics, vmem=VMEM_LIMIT):
    return pltpu.CompilerParams(dimension_semantics=semantics, vmem_limit_bytes=vmem)


def _rms_scale(x):
    return lax.rsqrt(jnp.mean(x * x, axis=-1, keepdims=True) + EPS)


def _sigmoid(x):
    return 1.0 / (1.0 + jnp.exp(-x))


def _mod_body(c_ref, w_ref, b_ref, o_ref):
    c = c_ref[...]
    a = (c * _sigmoid(c)).astype(BF16)
    o_ref[...] = jnp.dot(a, w_ref[...].astype(BF16), preferred_element_type=F32) + b_ref[...]


def modulation(cond8, w_mod, b_mod):
    tn = 1024
    n = w_mod.shape[1]
    return pl.pallas_call(
        _mod_body,
        grid=(n // tn,),
        in_specs=[
            pl.BlockSpec((8, D_MODEL), lambda j: (0, 0)),
            pl.BlockSpec((D_MODEL, tn), lambda j: (0, j)),
            pl.BlockSpec((1, tn), lambda j: (0, j)),
        ],
        out_specs=pl.BlockSpec((8, tn), lambda j: (0, j)),
        out_shape=jax.ShapeDtypeStruct((8, n), F32),
        compiler_params=_params(("arbitrary",)),
        name="modulation",
    )(cond8, w_mod, b_mod.reshape(1, n))


def _prenorm_body(x_ref, g_ref, sh_ref, sc_ref, o_ref):
    x = x_ref[...]
    y = x * _rms_scale(x) * g_ref[...]
    o_ref[...] = (y * (1.0 + sc_ref[...]) + sh_ref[...]).astype(o_ref.dtype)


def prenorm_modulate(x, g, shift, scale, group_of_block, tm):
    m = x.shape[0]
    gmap = lambda i: (group_of_block(i), 0, 0)
    return pl.pallas_call(
        _prenorm_body,
        grid=(m // tm,),
        in_specs=[
            pl.BlockSpec((tm, D_MODEL), lambda i: (i, 0)),
            pl.BlockSpec((1, D_MODEL), lambda i: (0, 0)),
            pl.BlockSpec((None, 1, D_MODEL), gmap),
            pl.BlockSpec((None, 1, D_MODEL), gmap),
        ],
        out_specs=pl.BlockSpec((tm, D_MODEL), lambda i: (i, 0)),
        out_shape=jax.ShapeDtypeStruct((m, D_MODEL), BF16),
        compiler_params=_params(("arbitrary",)),
        name="prenorm_modulate",
    )(x, g.reshape(1, D_MODEL), shift, scale)


def _inproj_body(h_ref, w_ref, o_ref, wbf_ref):
    @pl.when(pl.program_id(1) == 0)
    def _():
        wbf_ref[...] = w_ref[...].astype(BF16)

    o_ref[...] = jnp.dot(h_ref[...], wbf_ref[...], preferred_element_type=F32)


def in_projection(h, w_in):
    m = h.shape[0]
    tm, tn = 1024, 1024
    return pl.pallas_call(
        _inproj_body,
        grid=(IN_COLS // tn, m // tm),
        in_specs=[
            pl.BlockSpec((tm, D_MODEL), lambda j, i: (i, 0)),
            pl.BlockSpec((D_MODEL, tn), lambda j, i: (0, j)),
        ],
        out_specs=pl.BlockSpec((tm, tn), lambda j, i: (i, j)),
        out_shape=jax.ShapeDtypeStruct((m, IN_COLS), F32),
        scratch_shapes=[pltpu.VMEM((D_MODEL, tn), BF16)],
        compiler_params=_params(("arbitrary", "arbitrary")),
        name="in_projection",
    )(h, w_in)


def _rope(x, cos, sin_lo, sin_hi):
    return x * cos + pltpu.roll(x, 96, 1) * sin_lo + pltpu.roll(x, 32, 1) * sin_hi


def _head_norm(x, g):
    return x * _rms_scale(x) * g


def _softmax_pv(score_blocks, value_blocks):
    m = None
    for s in score_blocks:
        mi = jnp.max(s, axis=-1, keepdims=True)
        m = mi if m is None else jnp.maximum(m, mi)
    ps = [jnp.exp(s - m) for s in score_blocks]
    denom = None
    for p in ps:
        li = jnp.sum(p, axis=-1, keepdims=True)
        denom = li if denom is None else denom + li
    inv = 1.0 / denom
    out = None
    for p, v in zip(ps, value_blocks):
        o = jnp.dot((p * inv).astype(BF16), v, preferred_element_type=F32)
        out = o if out is None else out + o
    return out


def _attn_ctx_body(q_ref, kv_ref, gq_ref, gk_ref, o_ref, ko_ref, vo_ref):
    tq = q_ref.shape[0]
    scale = HEAD_DIM ** -0.5
    gq = gq_ref[...]
    gk = gk_ref[...]
    for g in range(N_KV_HEADS):
        kcols = slice(g * HEAD_DIM, (g + 1) * HEAD_DIM)
        kn = _head_norm(kv_ref[:, kcols], gk)
        v = kv_ref[:, KV_COLS + g * HEAD_DIM:KV_COLS + (g + 1) * HEAD_DIM]
        ko_ref[pl.ds(g, tq, stride=N_KV_HEADS), :] = kn
        vo_ref[pl.ds(g, tq, stride=N_KV_HEADS), :] = v
        qs = []
        for hh in range(KV_GROUP):
            h = g * KV_GROUP + hh
            qs.append(_head_norm(q_ref[:, h * HEAD_DIM:(h + 1) * HEAD_DIM], gq).astype(BF16))
        q4 = jnp.concatenate(qs, axis=0)
        s = lax.dot_general(q4, kn.astype(BF16), (((1,), (1,)), ((), ())),
                            preferred_element_type=F32) * scale
        o = _softmax_pv([s], [v.astype(BF16)])
        for hh in range(KV_GROUP):
            h = g * KV_GROUP + hh
            o_ref[:, h * HEAD_DIM:(h + 1) * HEAD_DIM] = o[hh * tq:(hh + 1) * tq].astype(o_ref.dtype)


def attention_ctx(z, g_q, g_k):
    nb = N_CTX_SEQ
    t = CTX_LEN
    return pl.pallas_call(
        _attn_ctx_body,
        grid=(nb,),
        in_specs=[
            pl.BlockSpec((t, Q_COLS), lambda b: (b, 0)),
            pl.BlockSpec((t, 2 * KV_COLS), lambda b: (b, COL_K // (2 * KV_COLS))),
            pl.BlockSpec((1, HEAD_DIM), lambda b: (0, 0)),
            pl.BlockSpec((1, HEAD_DIM), lambda b: (0, 0)),
        ],
        out_specs=[
            pl.BlockSpec((t, Q_COLS), lambda b: (b, 0)),
            pl.BlockSpec((t * N_KV_HEADS, HEAD_DIM), lambda b: (b, 0)),
            pl.BlockSpec((t * N_KV_HEADS, HEAD_DIM), lambda b: (b, 0)),
        ],
        out_shape=[
            jax.ShapeDtypeStruct((N_CTX, Q_COLS), BF16),
            jax.ShapeDtypeStruct((N_CTX * N_KV_HEADS, HEAD_DIM), F32),
            jax.ShapeDtypeStruct((N_CTX * N_KV_HEADS, HEAD_DIM), F32),
        ],
        compiler_params=_params(("arbitrary",)),
        name="attention_ctx",
    )(z, z, g_q.reshape(1, HEAD_DIM), g_k.reshape(1, HEAD_DIM))


def _attn_lat_body(q_ref, kv_ref, ck_ref, cv_ref, cos_ref, slo_ref, shi_ref, gq_ref, gk_ref,
                   o_ref, kr_ref):
    tq = q_ref.shape[0]
    qb = pl.program_id(1)
    scale = HEAD_DIM ** -0.5
    gq = gq_ref[...]

    @pl.when(qb == 0)
    def _():
        gk = gk_ref[...]
        for g in range(N_KV_HEADS):
            kcols = slice(g * HEAD_DIM, (g + 1) * HEAD_DIM)
            kn = _head_norm(kv_ref[:, kcols], gk)
            kr_ref[:, kcols] = _rope(kn, cos_ref[...], slo_ref[...], shi_ref[...]).astype(BF16)

    row0 = pl.multiple_of(qb * tq, tq)
    cos = cos_ref[pl.ds(row0, tq), :]
    slo = slo_ref[pl.ds(row0, tq), :]
    shi = shi_ref[pl.ds(row0, tq), :]
    for g in range(N_KV_HEADS):
        kcols = slice(g * HEAD_DIM, (g + 1) * HEAD_DIM)
        qs = []
        for hh in range(KV_GROUP):
            h = g * KV_GROUP + hh
            qn = _head_norm(q_ref[:, h * HEAD_DIM:(h + 1) * HEAD_DIM], gq)
            qs.append(_rope(qn, cos, slo, shi).astype(BF16))
        q4 = jnp.concatenate(qs, axis=0)
        dn = (((1,), (1,)), ((), ()))
        s_past = lax.dot_general(q4, ck_ref[:, kcols].astype(BF16), dn,
                                 preferred_element_type=F32) * scale
        s_new = lax.dot_general(q4, kr_ref[:, kcols], dn, preferred_element_type=F32) * scale
        v_past = cv_ref[:, kcols].astype(BF16)
        v_new = kv_ref[:, KV_COLS + g * HEAD_DIM:KV_COLS + (g + 1) * HEAD_DIM].astype(BF16)
        o = _softmax_pv([s_past, s_new], [v_past, v_new])
        for hh in range(KV_GROUP):
            h = g * KV_GROUP + hh
            o_ref[:, h * HEAD_DIM:(h + 1) * HEAD_DIM] = o[hh * tq:(hh + 1) * tq].astype(o_ref.dtype)


def attention_lat(z, cache_k, cache_v, rope_tabs, g_q, g_k):
    tq = 256
    nq = LAT_LEN // tq
    cos, slo, shi = rope_tabs
    tab = pl.BlockSpec((LAT_LEN, HEAD_DIM), lambda b, q: (0, 0))
    return pl.pallas_call(
        _attn_lat_body,
        grid=(N_LAT_SEQ, nq),
        in_specs=[
            pl.BlockSpec((tq, Q_COLS), lambda b, q: (b * nq + q, 0)),
            pl.BlockSpec((LAT_LEN, 2 * KV_COLS), lambda b, q: (b, COL_K // (2 * KV_COLS))),
            pl.BlockSpec((None, PAST_LEN, KV_COLS), lambda b, q: (b, 0, 0)),
            pl.BlockSpec((None, PAST_LEN, KV_COLS), lambda b, q: (b, 0, 0)),
            tab, tab, tab,
            pl.BlockSpec((1, HEAD_DIM), lambda b, q: (0, 0)),
            pl.BlockSpec((1, HEAD_DIM), lambda b, q: (0, 0)),
        ],
        out_specs=pl.BlockSpec((tq, Q_COLS), lambda b, q: (b * nq + q, 0)),
        out_shape=jax.ShapeDtypeStruct((N_LAT, Q_COLS), BF16),
        scratch_shapes=[pltpu.VMEM((LAT_LEN, KV_COLS), BF16)],
        compiler_params=_params(("arbitrary", "arbitrary")),
        name="attention_lat",
    )(z, z, cache_k, cache_v, cos, slo, shi, g_q.reshape(1, HEAD_DIM), g_k.reshape(1, HEAD_DIM))


def _rope_tables():
    t = np.arange(LAT_LEN)
    row = jnp.asarray(t // GRID_W, F32)
    col = jnp.asarray(t % GRID_W, F32)
    nf = HEAD_DIM // 4
    inv_freq = ROPE_THETA ** (-jnp.arange(nf, dtype=F32) / nf)
    ang_row = row[:, None] * inv_freq[None, :]
    ang_col = col[:, None] * inv_freq[None, :]
    ang = jnp.concatenate([ang_row, ang_row, ang_col, ang_col], axis=1)
    cos = jnp.cos(ang)
    sin = jnp.sin(ang)
    first = jnp.asarray((np.arange(HEAD_DIM) % (2 * nf)) < nf)[None, :]
    return cos, jnp.where(first, -sin, 0.0), jnp.where(first, 0.0, sin)


RNN_ROWS = 2048
RNN_COLS = 512
RNN_SUB = RNN_COLS // RNN_BLOCK_DIM


def _gelu_tanh(y):
    return 0.5 * y * (1.0 + jnp.tanh(0.7978845608028654 * (y + 0.044715 * (y * y * y))))


def _rglru_body(seq_len, xr_ref, yr_ref, cw_ref, cb_ref, wg_ref, bg_ref, lam_ref, h0f_ref, h0b_ref,
                o_ref, hf_ref, hb_ref, xs_ref, af_ref, bf_ref, ab_ref, bb_ref):
    n_seq = RNN_ROWS // seq_len
    for n in range(RNN_SUB):
        cols = slice(n * RNN_BLOCK_DIM, (n + 1) * RNN_BLOCK_DIM)
        for s in range(n_seq):
            xs_ref[n, pl.ds(s, seq_len, stride=n_seq), :] = xr_ref[s * seq_len:(s + 1) * seq_len, cols]

    row = lax.broadcasted_iota(jnp.int32, (RNN_ROWS, 1), 0)
    lam = lam_ref[...]
    softplus_neg = jnp.maximum(-lam, 0.0) + jnp.log(1.0 + jnp.exp(-jnp.abs(lam)))
    rate = softplus_neg * (-RG_C * 1.4426950408889634)
    for n in range(RNN_SUB):
        cols = slice(n * RNN_BLOCK_DIM, (n + 1) * RNN_BLOCK_DIM)
        x = xs_ref[n]
        x_m1 = jnp.where(row >= n_seq, pltpu.roll(x, n_seq, 0), 0.0)
        x_p1 = jnp.where(row < RNN_ROWS - n_seq, pltpu.roll(x, RNN_ROWS - n_seq, 0), 0.0)
        x_p2 = jnp.where(row < RNN_ROWS - 2 * n_seq, pltpu.roll(x, RNN_ROWS - 2 * n_seq, 0), 0.0)
        xn = (cb_ref[:, cols] + x_m1 * cw_ref[0:1, cols] + x * cw_ref[1:2, cols]
              + x_p1 * cw_ref[2:3, cols] + x_p2 * cw_ref[3:4, cols])
        pre = jnp.dot(xn.astype(BF16), wg_ref[n], preferred_element_type=F32) + bg_ref[n]
        for d, (a_ref, b_ref) in enumerate(((af_ref, bf_ref), (ab_ref, bb_ref))):
            r = 0.5 * jnp.tanh(0.5 * pre[:, (2 * d) * RNN_BLOCK_DIM:(2 * d + 1) * RNN_BLOCK_DIM]) + 0.5
            gate_in = 0.5 * jnp.tanh(
                0.5 * pre[:, (2 * d + 1) * RNN_BLOCK_DIM:(2 * d + 2) * RNN_BLOCK_DIM]) + 0.5
            a = jnp.exp2(r * rate[d:d + 1, cols])
            v = 1.0 - a * a
            a_ref[n] = a
            b_ref[n] = (v * lax.rsqrt(jnp.maximum(v, 1e-30))) * (gate_in * xn)

    def step(t, carry):
        rows_f = pl.ds(pl.multiple_of(t * n_seq, n_seq), n_seq)
        rows_b = pl.ds(pl.multiple_of((seq_len - 1 - t) * n_seq, n_seq), n_seq)
        out = []
        for n in range(RNN_SUB):
            hf = af_ref[n, rows_f, :] * carry[2 * n] + bf_ref[n, rows_f, :]
            hb = ab_ref[n, rows_b, :] * carry[2 * n + 1] + bb_ref[n, rows_b, :]
            bf_ref[n, rows_f, :] = hf
            bb_ref[n, rows_b, :] = hb
            out += [hf, hb]
        return tuple(out)

    init = []
    for n in range(RNN_SUB):
        cols = slice(n * RNN_BLOCK_DIM, (n + 1) * RNN_BLOCK_DIM)
        init += [h0f_ref[:, 0, cols], h0b_ref[:, 0, cols]]
    last = lax.fori_loop(0, seq_len, step, tuple(init), unroll=8)
    for n in range(RNN_SUB):
        cols = slice(n * RNN_BLOCK_DIM, (n + 1) * RNN_BLOCK_DIM)
        hf_ref[:, 0, cols] = last[2 * n]
        hb_ref[:, 0, cols] = last[2 * n + 1]
        bf_ref[n] = bf_ref[n] + bb_ref[n]
        for s in range(n_seq):
            rows = slice(s * seq_len, (s + 1) * seq_len)
            h_sum = bf_ref[n, pl.ds(s, seq_len, stride=n_seq), :]
            o_ref[rows, cols] = (h_sum * _gelu_tanh(yr_ref[rows, cols])).astype(o_ref.dtype)


def rglru_mixer(z, seq_len, conv_w, conv_b, w_gates, b_gates, lam, h0_f, h0_b):
    m = z.shape[0]
    n_seq_total = m // seq_len
    n_seq = RNN_ROWS // seq_len
    cblk = lambda base: (lambda r, c: (r, base // RNN_COLS + c))
    state_spec = pl.BlockSpec((n_seq, 1, RNN_COLS), lambda r, c: (r, 0, c))
    return pl.pallas_call(
        functools.partial(_rglru_body, seq_len),
        grid=(m // RNN_ROWS, D_MODEL // RNN_COLS),
        in_specs=[
            pl.BlockSpec((RNN_ROWS, RNN_COLS), cblk(COL_XR)),
            pl.BlockSpec((RNN_ROWS, RNN_COLS), cblk(COL_YR)),
            pl.BlockSpec((4, RNN_COLS), lambda r, c: (0, c)),
            pl.BlockSpec((1, RNN_COLS), lambda r, c: (0, c)),
            pl.BlockSpec((RNN_SUB, RNN_BLOCK_DIM, 4 * RNN_BLOCK_DIM), lambda r, c: (c, 0, 0)),
            pl.BlockSpec((RNN_SUB, 1, 4 * RNN_BLOCK_DIM), lambda r, c: (c, 0, 0)),
            pl.BlockSpec((2, RNN_COLS), lambda r, c: (0, c)),
            state_spec, state_spec,
        ],
        out_specs=[
            pl.BlockSpec((RNN_ROWS, RNN_COLS), lambda r, c: (r, c)),
            state_spec, state_spec,
        ],
        out_shape=[
            jax.ShapeDtypeStruct((m, D_MODEL), BF16),
            jax.ShapeDtypeStruct((n_seq_total, 1, D_MODEL), F32),
            jax.ShapeDtypeStruct((n_seq_total, 1, D_MODEL), F32),
        ],
        scratch_shapes=[pltpu.VMEM((RNN_SUB, RNN_ROWS, RNN_BLOCK_DIM), F32) for _ in range(5)],
        compiler_params=_params(("arbitrary", "arbitrary")),
        name="rglru_mixer_t%d" % seq_len,
    )(z, z, conv_w, conv_b.reshape(1, D_MODEL), w_gates, b_gates, lam, h0_f, h0_b)


def _merge_body(a_ref, r_ref, wa_ref, wr_ref, ga_ref, gr_ref, o_ref, wa_bf, wr_bf):
    @pl.when(pl.program_id(1) == 0)
    def _():
        wa_bf[...] = wa_ref[...].astype(BF16)
        wr_bf[...] = wr_ref[...].astype(BF16)

    pa = jnp.dot(a_ref[...], wa_bf[...], preferred_element_type=F32)
    pr = jnp.dot(r_ref[...], wr_bf[...], preferred_element_type=F32)
    o_ref[...] = (_sigmoid(ga_ref[...]) * pa + _sigmoid(gr_ref[...]) * pr).astype(o_ref.dtype)


def gated_merge(attn, rnn, z, w_o_attn, w_o_rnn):
    m = attn.shape[0]
    tm, tn = 1024, 512
    return pl.pallas_call(
        _merge_body,
        grid=(D_MODEL // tn, m // tm),
        in_specs=[
            pl.BlockSpec((tm, Q_COLS), lambda j, i: (i, 0)),
            pl.BlockSpec((tm, D_MODEL), lambda j, i: (i, 0)),
            pl.BlockSpec((Q_COLS, tn), lambda j, i: (0, j)),
            pl.BlockSpec((D_MODEL, tn), lambda j, i: (0, j)),
            pl.BlockSpec((tm, tn), lambda j, i: (i, COL_GA // tn + j)),
            pl.BlockSpec((tm, tn), lambda j, i: (i, COL_GR // tn + j)),
        ],
        out_specs=pl.BlockSpec((tm, tn), lambda j, i: (i, j)),
        out_shape=jax.ShapeDtypeStruct((m, D_MODEL), BF16),
        scratch_shapes=[pltpu.VMEM((Q_COLS, tn), BF16), pltpu.VMEM((D_MODEL, tn), BF16)],
        compiler_params=_params(("arbitrary", "arbitrary")),
        name="gated_merge",
    )(attn, rnn, w_o_attn, w_o_rnn, z, z)


POST_TM = 512
HALF_D = D_MODEL // 2
WORD_ROWS = HALF_D // 128
SUBLANES = 8
POST_CTX_BLOCKS = N_CTX // POST_TM
LAT_BLOCKS_PER_SEQ = LAT_LEN // POST_TM


def _post_group(i):
    return jnp.where(i < POST_CTX_BLOCKS, 0, 1 + (i - POST_CTX_BLOCKS) // LAT_BLOCKS_PER_SEQ)


def _postmix_body(mc_ref, ml_ref, xc_ref, xl_ref, wo_ref, gpm_ref, gt1_ref, gpf_ref, sh2_ref, sc2_ref,
                  wr_ref, br_ref, x1_ref, h2_ref, e_ref, gate_ref, rank_ref, cnt_ref, carry_ref):
    i = pl.program_id(0)
    tm = POST_TM

    @pl.when(i == 0)
    def _():
        carry_ref[...] = jnp.zeros_like(carry_ref)

    is_ctx = i < POST_CTX_BLOCKS
    merged = jnp.where(is_ctx, mc_ref[...], ml_ref[...])
    x = jnp.where(is_ctx, xc_ref[...], xl_ref[...])
    o = jnp.dot(merged, wo_ref[...], preferred_element_type=F32)
    x1 = x + gt1_ref[...] * (o * _rms_scale(o) * gpm_ref[...])
    x1_ref[...] = x1
    h2 = (x1 * _rms_scale(x1) * gpf_ref[...]) * (1.0 + sc2_ref[...]) + sh2_ref[...]
    h2_bf = h2.astype(BF16)
    bits = lax.bitcast_convert_type(h2_bf.astype(F32), jnp.uint32)
    words = (lax.shift_right_logical(bits[:, :HALF_D], jnp.uint32(16))
             | (bits[:, HALF_D:] & jnp.uint32(0xFFFF0000)))
    for c in range(WORD_ROWS):
        h2_ref[pl.ds(c, tm, stride=WORD_ROWS), :] = words[:, c * 128:(c + 1) * 128]

    logits = jnp.dot(h2_bf, wr_ref[...].astype(BF16), preferred_element_type=F32) + br_ref[...]
    lane = lax.broadcasted_iota(jnp.int32, (tm, N_EXPERTS), 1)
    work = logits
    chosen = jnp.zeros((tm, N_EXPERTS), F32)
    sels, vals, idxs = [], [], []
    for _ in range(TOP_K):
        mx = jnp.max(work, axis=-1, keepdims=True)
        idx = jnp.min(jnp.where(work == mx, lane, N_EXPERTS), axis=-1, keepdims=True)
        sel = lane == idx
        work = jnp.where(sel, -jnp.inf, work)
        chosen = jnp.where(sel, 1.0, chosen)
        sels.append(sel)
        vals.append(mx)
        idxs.append(idx)
    exps = [jnp.exp(v - vals[0]) for v in vals]
    inv = 1.0 / (exps[0] + exps[1] + exps[2] + exps[3])

    r_io = lax.broadcasted_iota(jnp.int32, (tm, tm), 0)
    c_io = lax.broadcasted_iota(jnp.int32, (tm, tm), 1)
    lower = jnp.where(c_io < r_io, 1.0, 0.0).astype(BF16)
    before = jnp.dot(lower, chosen.astype(BF16), preferred_element_type=F32) + carry_ref[...]
    carry_ref[...] = carry_ref[...] + jnp.sum(chosen, axis=0, keepdims=True)
    cnt_ref[...] = carry_ref[...]

    lane_k = lax.broadcasted_iota(jnp.int32, (tm, TOP_K), 1)
    e_out = jnp.zeros((tm, TOP_K), jnp.int32)
    g_out = jnp.zeros((tm, TOP_K), F32)
    r_out = jnp.zeros((tm, TOP_K), jnp.int32)
    for k in range(TOP_K):
        rk = jnp.sum(jnp.where(sels[k], before, 0.0), axis=-1, keepdims=True).astype(jnp.int32)
        e_out = jnp.where(lane_k == k, idxs[k], e_out)
        g_out = jnp.where(lane_k == k, exps[k] * inv, g_out)
        r_out = jnp.where(lane_k == k, rk, r_out)
    e_ref[...] = e_out
    gate_ref[...] = g_out
    rank_ref[...] = r_out


def post_mix_router(merged_ctx, merged_lat, x_ctx, x_lat, w_out_bf, g_post_mix, gt1, g_pre_ffn, sh2, sc2,
                    w_router, b_router):
    tm = POST_TM
    ctx_map = lambda i: (jnp.minimum(i, POST_CTX_BLOCKS - 1), 0)
    lat_map = lambda i: (jnp.maximum(i - POST_CTX_BLOCKS, 0), 0)
    gmap = lambda i: (_post_group(i), 0, 0)
    row = lambda i: (i, 0)
    const = lambda i: (0, 0)
    vec = pl.BlockSpec((1, D_MODEL), const)
    gvec = pl.BlockSpec((None, 1, D_MODEL), gmap)
    return pl.pallas_call(
        _postmix_body,
        grid=(N_TOK // tm,),
        in_specs=[
            pl.BlockSpec((tm, D_MODEL), ctx_map),
            pl.BlockSpec((tm, D_MODEL), lat_map),
            pl.BlockSpec((tm, D_MODEL), ctx_map),
            pl.BlockSpec((tm, D_MODEL), lat_map),
            pl.BlockSpec((D_MODEL, D_MODEL), const),
            vec, gvec, vec, gvec, gvec,
            pl.BlockSpec((D_MODEL, N_EXPERTS), const),
            pl.BlockSpec((1, N_EXPERTS), const),
        ],
        out_specs=[
            pl.BlockSpec((tm, D_MODEL), row),
            pl.BlockSpec((tm * WORD_ROWS, 128), row),
            pl.BlockSpec((tm, TOP_K), row),
            pl.BlockSpec((tm, TOP_K), row),
            pl.BlockSpec((tm, TOP_K), row),
            pl.BlockSpec((1, N_EXPERTS), const),
        ],
        out_shape=[
            jax.ShapeDtypeStruct((N_TOK, D_MODEL), F32),
            jax.ShapeDtypeStruct((N_TOK * WORD_ROWS, 128), jnp.uint32),
            jax.ShapeDtypeStruct((N_TOK, TOP_K), jnp.int32),
            jax.ShapeDtypeStruct((N_TOK, TOP_K), F32),
            jax.ShapeDtypeStruct((N_TOK, TOP_K), jnp.int32),
            jax.ShapeDtypeStruct((1, N_EXPERTS), F32),
        ],
        scratch_shapes=[pltpu.VMEM((1, N_EXPERTS), F32)],
        compiler_params=_params(("arbitrary",)),
        name="post_mix_router",
    )(merged_ctx, merged_lat, x_ctx, x_lat, w_out_bf, g_post_mix.reshape(1, D_MODEL), gt1,
      g_pre_ffn.reshape(1, D_MODEL), sh2, sc2, w_router, b_router.reshape(1, N_EXPERTS))


GATHER_ROWS = ROW_TILE // (2 * N_FF_CHUNKS)
GATHER_PRIORITY = 1


def _unpack_tile(xbuf_ref, slot, i):
    groups = ROW_TILE // SUBLANES
    base = pl.multiple_of(i * groups, groups)
    lo, hi = [], []
    for c in range(WORD_ROWS):
        words = xbuf_ref[slot, pl.ds(base, groups), c, :, :].reshape(ROW_TILE, 128)
        lo.append(lax.bitcast_convert_type(lax.shift_left(words, jnp.uint32(16)), F32).astype(BF16))
        hi.append(lax.bitcast_convert_type(words & jnp.uint32(0xFFFF0000), F32).astype(BF16))
    return jnp.concatenate(lo + hi, axis=1)


def _for_tiles(n_tiles, body):
    def one(i, _):
        body(i)
        return 0

    lax.fori_loop(0, n_tiles, one, 0)


def _moe_body(exp_ref, row_ref, nsub_ref, nzero_ref, tok_ref, h_ref, wg_ref, wl_ref, wd_ref, bg_ref, bl_ref,
              bd_ref, y_ref, xbuf_ref, act_ref, wg_bf, wl_bf, wd_bf, stage_ref, idx_ref, pend_ref,
              xsem, isem, ysem):
    s = pl.program_id(0)
    j = pl.program_id(1)
    n_sub = nsub_ref[s]
    row_start = row_ref[s]
    n_next = nsub_ref[jnp.minimum(s + 1, N_SUPER - 1)]

    def idx_copy(p):
        tile0 = pl.multiple_of(row_ref[p], ROW_TILE) // ROW_TILE
        return pltpu.make_async_copy(tok_ref.at[pl.ds(tile0, SUPER_TILES)], idx_ref.at[p % 2],
                                     isem.at[p % 2])

    def gather_row(slot, tile, col, r):
        t = idx_ref[slot, tile, 0, col]
        src = h_ref.at[pl.ds(pl.multiple_of(t * WORD_ROWS, WORD_ROWS), WORD_ROWS), :]
        dst = xbuf_ref.at[slot, lax.shift_right_logical(r, 3), :, jnp.bitwise_and(r, SUBLANES - 1), :]
        pltpu.make_async_copy(src, dst, xsem.at[slot]).start(priority=GATHER_PRIORITY)

    def gather_chunk(step, i):
        first = (step * n_sub + i) * GATHER_ROWS
        tile = lax.shift_right_logical(first, 8)
        col = jnp.bitwise_and(first, ROW_TILE - 1)
        for g in range(GATHER_ROWS):
            gather_row((s + 1) % 2, tile, col + g, first + g)

    def gather_range(p, first, last):
        def issue(r, _):
            gather_row(p % 2, lax.shift_right_logical(r, 8), jnp.bitwise_and(r, ROW_TILE - 1), r)
            return 0

        lax.fori_loop(first, last, issue, 0)

    def wait_rows(slot, count):
        @pl.when(count > 0)
        def _():
            window = xbuf_ref.at[slot, pl.ds(0, lax.shift_right_logical(count, 3))]
            pltpu.make_async_copy(window, window, xsem.at[slot]).wait()

    @pl.when(jnp.logical_and(s == 0, j == 0))
    def _():
        pend_ref[0] = 0
        pend_ref[1] = 0

    def drain_stage(slot):
        @pl.when(pend_ref[slot] == 1)
        def _():
            pltpu.make_async_copy(stage_ref.at[slot], stage_ref.at[slot], ysem.at[slot]).wait()
            pend_ref[slot] = 0

    @pl.when(jnp.logical_and(s == 0, j == 0))
    def _():
        idx_copy(0).start()
        idx_copy(0).wait()
        gather_range(0, 0, n_sub * ROW_TILE)
        wait_rows(0, n_sub * ROW_TILE)
        idx_copy(1).start()

    @pl.when(jnp.logical_and(s > 0, j == 0))
    def _():
        wait_rows(s % 2, jnp.maximum(nsub_ref[jnp.maximum(s - 1, 0)], n_sub) * ROW_TILE)

    @pl.when(jnp.logical_and(j == 0, s + 1 < N_SUPER))
    def _():
        idx_copy(s + 1).wait()

    @pl.when(jnp.logical_and(j == 0, s + 2 < N_SUPER))
    def _():
        idx_copy(s + 2).start()

    @pl.when(jnp.logical_and(j == 0, jnp.logical_and(n_sub == 0, s + 1 < N_SUPER)))
    def _():
        gather_range(s + 1, 0, n_next * ROW_TILE)

    @pl.when(jnp.logical_and(j < N_FF_CHUNKS, n_sub > 0))
    def _():
        wg_bf[...] = wg_ref[...].astype(BF16)
        wl_bf[...] = wl_ref[...].astype(BF16)
        bg = bg_ref[...]
        bl = bl_ref[...]

        def up_tile(i):
            rows = pl.ds(pl.multiple_of(i * ROW_TILE, ROW_TILE), ROW_TILE)
            xt = _unpack_tile(xbuf_ref, s % 2, i)
            glu = jnp.minimum(jnp.dot(xt, wg_bf[...], preferred_element_type=F32) + bg, SWIGLU_LIMIT)
            lin = jnp.clip(jnp.dot(xt, wl_bf[...], preferred_element_type=F32) + bl,
                           -SWIGLU_LIMIT, SWIGLU_LIMIT)
            gather_chunk(j, i)
            act = glu * _sigmoid(SWIGLU_ALPHA * glu) * (lin + 1.0)
            act_ref[j, rows, :] = act.astype(BF16)

        _for_tiles(n_sub, up_tile)

    for cc in range(N_FF_CHUNKS):
        @pl.when(jnp.logical_and(j == N_FF_CHUNKS + cc, n_sub > 0))
        def _(cc=cc):
            wd_bf[...] = wd_ref[...].astype(BF16)
            bd = bd_ref[...]

            def out_copy(i, slot):
                dst = y_ref.at[pl.ds(pl.multiple_of(row_start + i * ROW_TILE, ROW_TILE), ROW_TILE),
                               cc * FF_CHUNK:(cc + 1) * FF_CHUNK]
                return pltpu.make_async_copy(stage_ref.at[slot], dst, ysem.at[slot])

            def down_tile(i):
                rows = pl.ds(pl.multiple_of(i * ROW_TILE, ROW_TILE), ROW_TILE)
                slot = i % 2
                acc = bd
                for c in range(N_FF_CHUNKS):
                    acc = acc + jnp.dot(act_ref[c, rows, :], wd_bf[c * FF_CHUNK:(c + 1) * FF_CHUNK, :],
                                        preferred_element_type=F32)
                gather_chunk(N_FF_CHUNKS + cc, i)
                drain_stage(slot)
                stage_ref[slot] = acc
                out_copy(i, slot).start()
                pend_ref[slot] = 1

            _for_tiles(n_sub, down_tile)
            if cc == N_FF_CHUNKS - 1:
                gather_range(s + 1, n_sub * ROW_TILE, n_next * ROW_TILE)

    n_zero = nzero_ref[s]

    @pl.when(jnp.logical_and(j == 0, n_zero > 0))
    def _():
        drain_stage(0)
        stage_ref[0] = jnp.zeros((ROW_TILE, FF_CHUNK), F32)

        def zero_copy(i, cc):
            dst = y_ref.at[pl.ds(pl.multiple_of(row_start + i * ROW_TILE, ROW_TILE), ROW_TILE),
                           cc * FF_CHUNK:(cc + 1) * FF_CHUNK]
            return pltpu.make_async_copy(stage_ref.at[0], dst, ysem.at[0])

        def issue(i, _):
            for cc in range(N_FF_CHUNKS):
                zero_copy(i, cc).start()
            return 0

        def drain(i, _):
            for cc in range(N_FF_CHUNKS):
                zero_copy(i, cc).wait()
            return 0

        lax.fori_loop(0, n_zero, issue, 0)
        lax.fori_loop(0, n_zero, drain, 0)

    @pl.when(jnp.logical_and(s == N_SUPER - 1, j == 2 * N_FF_CHUNKS - 1))
    def _():
        drain_stage(0)
        drain_stage(1)


def expert_mlp(h_packed, tok_sorted, sched, w_gate_up, b_gate_up, w_down, b_down):
    exp_of, row_of, nsub_of, nzero_of = sched
    last = N_FF_CHUNKS - 1
    up_of = lambda s, j, n: jnp.where(n[s] > 0, jnp.minimum(j, last), last)
    down_of = lambda s, j, n: jnp.where(n[s] > 0, jnp.maximum(j - N_FF_CHUNKS, 0), last)
    up_chunk = lambda s, j, e, r, n, z: (e[s], 0, up_of(s, j, n))
    lin_chunk = lambda s, j, e, r, n, z: (e[s], 0, N_FF_CHUNKS + up_of(s, j, n))
    down_chunk = lambda s, j, e, r, n, z: (e[s], 0, down_of(s, j, n))
    grid_spec = pltpu.PrefetchScalarGridSpec(
        num_scalar_prefetch=4,
        grid=(N_SUPER, 2 * N_FF_CHUNKS),
        in_specs=[
            pl.BlockSpec(memory_space=pl.ANY),
            pl.BlockSpec(memory_space=pl.ANY),
            pl.BlockSpec((None, D_MODEL, FF_CHUNK), up_chunk),
            pl.BlockSpec((None, D_MODEL, FF_CHUNK), lin_chunk),
            pl.BlockSpec((None, D_FF, FF_CHUNK), down_chunk),
            pl.BlockSpec((None, 1, FF_CHUNK), up_chunk),
            pl.BlockSpec((None, 1, FF_CHUNK), lin_chunk),
            pl.BlockSpec((None, 1, FF_CHUNK), down_chunk),
        ],
        out_specs=pl.BlockSpec(memory_space=pl.ANY),
        scratch_shapes=[
            pltpu.VMEM((2, SUPER_ROWS // SUBLANES, WORD_ROWS, SUBLANES, 128), jnp.uint32),
            pltpu.VMEM((N_FF_CHUNKS, SUPER_ROWS, FF_CHUNK), BF16),
            pltpu.VMEM((D_MODEL, FF_CHUNK), BF16),
            pltpu.VMEM((D_MODEL, FF_CHUNK), BF16),
            pltpu.VMEM((D_FF, FF_CHUNK), BF16),
            pltpu.VMEM((2, ROW_TILE, FF_CHUNK), F32),
            pltpu.SMEM((2, SUPER_TILES, 1, ROW_TILE), jnp.int32),
            pltpu.SMEM((2,), jnp.int32),
            pltpu.SemaphoreType.DMA((2,)),
            pltpu.SemaphoreType.DMA((2,)),
            pltpu.SemaphoreType.DMA((2,)),
        ],
    )
    tok_tiles = jnp.concatenate([tok_sorted.reshape(N_ROW_TILES, 1, ROW_TILE),
                                 jnp.zeros((SUPER_TILES, 1, ROW_TILE), jnp.int32)], axis=0)
    return pl.pallas_call(
        _moe_body,
        grid_spec=grid_spec,
        out_shape=jax.ShapeDtypeStruct((N_ROWS, D_MODEL), F32),
        compiler_params=_params(("arbitrary", "arbitrary"), vmem=EXPERT_VMEM_LIMIT),
        name="expert_mlp",
    )(exp_of, row_of, nsub_of, nzero_of, tok_tiles, h_packed, w_gate_up, w_gate_up, w_down,
      b_gate_up.reshape(N_EXPERTS, 1, 2 * D_FF), b_gate_up.reshape(N_EXPERTS, 1, 2 * D_FF),
      b_down.reshape(N_EXPERTS, 1, D_MODEL))


COMB_TB = 256


def _combine_start(y_ref, ybuf_ref, pos_ref, sem):
    def issue(t, _):
        for k in range(TOP_K):
            p = pos_ref[0, 0, t * TOP_K + k]
            pltpu.make_async_copy(y_ref.at[pl.ds(p, 1), :], ybuf_ref.at[k, pl.ds(t, 1), :], sem).start()
        return 0

    lax.fori_loop(0, COMB_TB, issue, 0, unroll=4)


def _combine_body(n, pos_ref, pos_next_ref, y_ref, gate_ref, x1_ref, gt2_ref, g_ref, o_ref, ybuf_ref, sem_ref):
    i = pl.program_id(0)
    slot = i % 2

    @pl.when(i == 0)
    def _():
        _combine_start(y_ref, ybuf_ref.at[0], pos_ref, sem_ref.at[0])

    @pl.when(i + 1 < n)
    def _():
        _combine_start(y_ref, ybuf_ref.at[1 - slot], pos_next_ref, sem_ref.at[1 - slot])

    for k in range(TOP_K):
        pltpu.make_async_copy(y_ref.at[pl.ds(0, COMB_TB), :], ybuf_ref.at[slot, k], sem_ref.at[slot]).wait()
    gates = gate_ref[...]
    ffn = gates[:, 0:1] * ybuf_ref[slot, 0]
    for k in range(1, TOP_K):
        ffn = ffn + gates[:, k:k + 1] * ybuf_ref[slot, k]
    o_ref[...] = x1_ref[...] + gt2_ref[...] * (ffn * _rms_scale(ffn) * g_ref[...])


def combine_residual(y_sorted, pos, gates, x1, gt2, g_post_ffn, row_offset, n_rows, group_of_block):
    tb = COMB_TB
    nblk = n_rows // tb
    off = row_offset // tb
    pos3 = pos.reshape(N_TOK // tb, 1, tb * TOP_K)
    smem_blk = lambda f: pl.BlockSpec((1, 1, tb * TOP_K), f, memory_space=pltpu.SMEM)
    return pl.pallas_call(
        functools.partial(_combine_body, nblk),
        grid=(nblk,),
        in_specs=[
            smem_blk(lambda i: (off + i, 0, 0)),
            smem_blk(lambda i: (off + jnp.minimum(i + 1, nblk - 1), 0, 0)),
            pl.BlockSpec(memory_space=pl.ANY),
            pl.BlockSpec((tb, TOP_K), lambda i: (off + i, 0)),
            pl.BlockSpec((tb, D_MODEL), lambda i: (off + i, 0)),
            pl.BlockSpec((None, 1, D_MODEL), lambda i: (group_of_block(i), 0, 0)),
            pl.BlockSpec((1, D_MODEL), lambda i: (0, 0)),
        ],
        out_specs=pl.BlockSpec((tb, D_MODEL), lambda i: (i, 0)),
        out_shape=jax.ShapeDtypeStruct((n_rows, D_MODEL), F32),
        scratch_shapes=[pltpu.VMEM((2, TOP_K, tb, D_MODEL), F32), pltpu.SemaphoreType.DMA((2,))],
        compiler_params=_params(("arbitrary",)),
        name="combine_residual",
    )(pos3, pos3, y_sorted, gates, x1, gt2, g_post_ffn.reshape(1, D_MODEL))


def _routing_tables(e_idx, rank, counts_f):
    counts = counts_f.reshape(N_EXPERTS).astype(jnp.int32)
    n_tiles = (counts + ROW_TILE - 1) // ROW_TILE
    padded = n_tiles * ROW_TILE
    pad_end = jnp.cumsum(padded)
    pad_start = pad_end - padded
    pos = pad_start[e_idx] + rank
    tok_ids = jnp.repeat(jnp.arange(N_TOK, dtype=jnp.int32), TOP_K)
    tok_sorted = jnp.zeros((N_ROWS,), jnp.int32).at[pos.reshape(-1)].set(
        tok_ids, unique_indices=True, mode="drop")
    n_pass = (n_tiles + SUPER_TILES - 1) // SUPER_TILES
    pass_end = jnp.cumsum(n_pass)
    total = pass_end[-1]
    s = jnp.arange(N_SUPER, dtype=jnp.int32)
    s_eff = jnp.minimum(s, total - 1)
    e_of = jnp.minimum(jnp.searchsorted(pass_end, s_eff, side="right"), N_EXPERTS - 1).astype(jnp.int32)
    local = s_eff - (pass_end[e_of] - n_pass[e_of])
    row_of = pad_start[e_of] + local * SUPER_ROWS
    nsub = jnp.minimum(SUPER_TILES, n_tiles[e_of] - local * SUPER_TILES)
    nsub = jnp.where(s < total, nsub, 0).astype(jnp.int32)
    zero_row = pad_end[-1] + (s - total) * SUPER_ROWS
    nzero = jnp.clip((N_ROWS - zero_row) // ROW_TILE, 0, SUPER_TILES)
    nzero = jnp.where(s >= total, nzero, 0).astype(jnp.int32)
    row_of = jnp.where(s < total, row_of, jnp.minimum(zero_row, N_ROWS - ROW_TILE)).astype(jnp.int32)
    return pos.astype(jnp.int32), tok_sorted, (e_of, row_of, nsub, nzero)


def kernel(x_prompt, x_sample, cache_k, cache_v, state_rnn_fwd, state_rnn_bwd, c, c_ctx, w_mod, b_mod, g_pre_mix, w_in, g_q_norm, g_k_norm, conv_w, conv_b, rg_w_a, rg_b_a, rg_w_x, rg_b_x, rg_lambda, w_o_attn, w_o_rnn, w_out, g_post_mix, g_pre_ffn, w_router, b_router, w_gate_up, b_gate_up, w_down, b_down, g_post_ffn):
    l = 0
    x_ctx = x_prompt.reshape(N_CTX, D_MODEL)
    x_lat = x_sample.reshape(N_LAT, D_MODEL)

    cond8 = jnp.concatenate([c_ctx[None, :], c, jnp.zeros((8 - 1 - N_LAT_SEQ, D_MODEL), F32)], axis=0)
    mod = modulation(cond8, w_mod[l], b_mod[l])[:1 + N_LAT_SEQ].reshape(1 + N_LAT_SEQ, 6, 1, D_MODEL)
    sh1, sc1, gt1, sh2, sc2, gt2 = [mod[:, i] for i in range(6)]

    ctx_group = lambda i: 0
    lat_group_1024 = lambda i: 1 + i
    h_ctx = prenorm_modulate(x_ctx, g_pre_mix[l], sh1, sc1, ctx_group, 1024)
    h_lat = prenorm_modulate(x_lat, g_pre_mix[l], sh1, sc1, lat_group_1024, 1024)
    z_ctx = in_projection(h_ctx, w_in[l])
    z_lat = in_projection(h_lat, w_in[l])

    attn_ctx, k_new, v_new = attention_ctx(z_ctx, g_q_norm[l], g_k_norm[l])
    attn_lat = attention_lat(z_lat, cache_k[:, l].reshape(N_LAT_SEQ, PAST_LEN, KV_COLS),
                             cache_v[:, l].reshape(N_LAT_SEQ, PAST_LEN, KV_COLS),
                             _rope_tables(), g_q_norm[l], g_k_norm[l])

    def per_block(w):
        return w.reshape(2, RNN_BLOCKS, 1, RNN_BLOCK_DIM)

    w_gates = jnp.concatenate([rg_w_a[l, 0], rg_w_x[l, 0], rg_w_a[l, 1], rg_w_x[l, 1]], axis=-1).astype(BF16)
    ba, bx = per_block(rg_b_a[l]), per_block(rg_b_x[l])
    b_gates = jnp.concatenate([ba[0], bx[0], ba[1], bx[1]], axis=-1)
    zeros_state = jnp.zeros((N_CTX_SEQ, 1, D_MODEL), F32)
    rnn_ctx, hf_ctx, hb_ctx = rglru_mixer(z_ctx, CTX_LEN, conv_w[l], conv_b[l], w_gates, b_gates,
                                          rg_lambda[l], zeros_state, zeros_state)
    rnn_lat, _, _ = rglru_mixer(z_lat, LAT_LEN, conv_w[l], conv_b[l], w_gates, b_gates, rg_lambda[l],
                                state_rnn_fwd[:, l].reshape(N_LAT_SEQ, 1, D_MODEL),
                                state_rnn_bwd[:, l].reshape(N_LAT_SEQ, 1, D_MODEL))

    merged_ctx = gated_merge(attn_ctx, rnn_ctx, z_ctx, w_o_attn[l], w_o_rnn[l])
    merged_lat = gated_merge(attn_lat, rnn_lat, z_lat, w_o_attn[l], w_o_rnn[l])

    x1, h2, e_idx, gates, rank, counts = post_mix_router(
        merged_ctx, merged_lat, x_ctx, x_lat, w_out[l].astype(BF16), g_post_mix[l], gt1, g_pre_ffn[l],
        sh2, sc2, w_router[l], b_router[l])

    pos, tok_sorted, sched = _routing_tables(e_idx, rank, counts)
    y_sorted = expert_mlp(h2, tok_sorted, sched, w_gate_up[l], b_gate_up[l], w_down[l], b_down[l])

    y_ctx = combine_residual(y_sorted, pos, gates, x1, gt2, g_post_ffn[l], 0, N_CTX, ctx_group)
    y_lat = combine_residual(y_sorted, pos, gates, x1, gt2, g_post_ffn[l], N_CTX, N_LAT,
                             lambda i: 1 + i // (LAT_LEN // COMB_TB))

    return (y_ctx.reshape(N_CTX_SEQ, CTX_LEN, D_MODEL),
            y_lat.reshape(N_LAT_SEQ, LAT_LEN, D_MODEL),
            k_new.reshape(N_CTX_SEQ, 1, CTX_LEN, N_KV_HEADS, HEAD_DIM),
            v_new.reshape(N_CTX_SEQ, 1, CTX_LEN, N_KV_HEADS, HEAD_DIM),
            hf_ctx,
            hb_ctx)
```

```python
import functools

import jax
import jax.numpy as jnp
import numpy as np
from jax import lax
from jax.experimental import pallas as pl
from jax.experimental.pallas import tpu as pltpu

D_MODEL = 2048
N_CTX_SEQ = 32
CTX_LEN = 256
N_LAT_SEQ = 2
LAT_LEN = 1024
PAST_LEN = 512
N_CTX = N_CTX_SEQ * CTX_LEN
N_LAT = N_LAT_SEQ * LAT_LEN
N_TOK = N_CTX + N_LAT
GRID_W = 64
N_HEADS = 16
N_KV_HEADS = 4
HEAD_DIM = 128
KV_GROUP = N_HEADS // N_KV_HEADS
ROPE_THETA = 10000.0
RNN_BLOCKS = 16
RNN_BLOCK_DIM = 128
RG_C = 8.0
N_EXPERTS = 32
TOP_K = 4
D_FF = 2048
SWIGLU_LIMIT = 7.0
SWIGLU_ALPHA = 1.702
EPS = 1e-6
Q_COLS = N_HEADS * HEAD_DIM
KV_COLS = N_KV_HEADS * HEAD_DIM
IN_COLS = Q_COLS + 2 * KV_COLS + 4 * D_MODEL
COL_K = Q_COLS
COL_XR = Q_COLS + 2 * KV_COLS
COL_YR = COL_XR + D_MODEL
COL_GA = COL_YR + D_MODEL
COL_GR = COL_GA + D_MODEL

V7X_VMEM_BYTES = 64 * 1024 * 1024
VMEM_LIMIT = 56 * 1024 * 1024
EXPERT_VMEM_LIMIT = 60 * 1024 * 1024

ROW_TILE = 256
SUPER_TILES = 8
SUPER_ROWS = ROW_TILE * SUPER_TILES
N_ASSIGN = N_TOK * TOP_K
N_ROWS = N_ASSIGN + N_EXPERTS * ROW_TILE
N_ROW_TILES = N_ROWS // ROW_TILE
N_SUPER = N_ROW_TILES // SUPER_TILES + N_EXPERTS
FF_CHUNK = 512
N_FF_CHUNKS = D_FF // FF_CHUNK

BF16 = jnp.bfloat16
F32 = jnp.float32


def _params(semantics, vmem=VMEM_LIMIT):
    return pltpu.CompilerParams(dimension_semantics=semantics, vmem_limit_bytes=vmem)


def _rms_scale(x):
    return lax.rsqrt(jnp.mean(x * x, axis=-1, keepdims=True) + EPS)


def _sigmoid(x):
    return 1.0 / (1.0 + jnp.exp(-x))


def _mod_body(c_ref, w_ref, b_ref, o_ref):
    c = c_ref[...]
    a = (c * _sigmoid(c)).astype(BF16)
    o_ref[...] = jnp.dot(a, w_ref[...].astype(BF16), preferred_element_type=F32) + b_ref[...]


def modulation(cond8, w_mod, b_mod):
    tn = 1024
    n = w_mod.shape[1]
    return pl.pallas_call(
        _mod_body,
        grid=(n // tn,),
        in_specs=[
            pl.BlockSpec((8, D_MODEL), lambda j: (0, 0)),
            pl.BlockSpec((D_MODEL, tn), lambda j: (0, j)),
            pl.BlockSpec((1, tn), lambda j: (0, j)),
        ],
        out_specs=pl.BlockSpec((8, tn), lambda j: (0, j)),
        out_shape=jax.ShapeDtypeStruct((8, n), F32),
        compiler_params=_params(("arbitrary",)),
        name="modulation",
    )(cond8, w_mod, b_mod.reshape(1, n))


def _prenorm_body(x_ref, g_ref, sh_ref, sc_ref, o_ref):
    x = x_ref[...]
    y = x * _rms_scale(x) * g_ref[...]
    o_ref[...] = (y * (1.0 + sc_ref[...]) + sh_ref[...]).astype(o_ref.dtype)


def prenorm_modulate(x, g, shift, scale, group_of_block, tm):
    m = x.shape[0]
    gmap = lambda i: (group_of_block(i), 0, 0)
    return pl.pallas_call(
        _prenorm_body,
        grid=(m // tm,),
        in_specs=[
            pl.BlockSpec((tm, D_MODEL), lambda i: (i, 0)),
            pl.BlockSpec((1, D_MODEL), lambda i: (0, 0)),
            pl.BlockSpec((None, 1, D_MODEL), gmap),
            pl.BlockSpec((None, 1, D_MODEL), gmap),
        ],
        out_specs=pl.BlockSpec((tm, D_MODEL), lambda i: (i, 0)),
        out_shape=jax.ShapeDtypeStruct((m, D_MODEL), BF16),
        compiler_params=_params(("arbitrary",)),
        name="prenorm_modulate",
    )(x, g.reshape(1, D_MODEL), shift, scale)


def _inproj_body(h_ref, w_ref, o_ref, wbf_ref):
    @pl.when(pl.program_id(1) == 0)
    def _():
        wbf_ref[...] = w_ref[...].astype(BF16)

    o_ref[...] = jnp.dot(h_ref[...], wbf_ref[...], preferred_element_type=F32)


def in_projection(h, w_in):
    m = h.shape[0]
    tm, tn = 1024, 1024
    return pl.pallas_call(
        _inproj_body,
        grid=(IN_COLS // tn, m // tm),
        in_specs=[
            pl.BlockSpec((tm, D_MODEL), lambda j, i: (i, 0)),
            pl.BlockSpec((D_MODEL, tn), lambda j, i: (0, j)),
        ],
        out_specs=pl.BlockSpec((tm, tn), lambda j, i: (i, j)),
        out_shape=jax.ShapeDtypeStruct((m, IN_COLS), F32),
        scratch_shapes=[pltpu.VMEM((D_MODEL, tn), BF16)],
        compiler_params=_params(("arbitrary", "arbitrary")),
        name="in_projection",
    )(h, w_in)


def _rope(x, cos, sin_lo, sin_hi):
    return x * cos + pltpu.roll(x, 96, 1) * sin_lo + pltpu.roll(x, 32, 1) * sin_hi


def _head_norm(x, g):
    return x * _rms_scale(x) * g


def _softmax_pv(score_blocks, value_blocks):
    m = None
    for s in score_blocks:
        mi = jnp.max(s, axis=-1, keepdims=True)
        m = mi if m is None else jnp.maximum(m, mi)
    ps = [jnp.exp(s - m) for s in score_blocks]
    denom = None
    for p in ps:
        li = jnp.sum(p, axis=-1, keepdims=True)
        denom = li if denom is None else denom + li
    inv = 1.0 / denom
    out = None
    for p, v in zip(ps, value_blocks):
        o = jnp.dot((p * inv).astype(BF16), v, preferred_element_type=F32)
        out = o if out is None else out + o
    return out


def _attn_ctx_body(q_ref, kv_ref, gq_ref, gk_ref, o_ref, ko_ref, vo_ref):
    tq = q_ref.shape[0]
    scale = HEAD_DIM ** -0.5
    gq = gq_ref[...]
    gk = gk_ref[...]
    for g in range(N_KV_HEADS):
        kcols = slice(g * HEAD_DIM, (g + 1) * HEAD_DIM)
        kn = _head_norm(kv_ref[:, kcols], gk)
        v = kv_ref[:, KV_COLS + g * HEAD_DIM:KV_COLS + (g + 1) * HEAD_DIM]
        ko_ref[pl.ds(g, tq, stride=N_KV_HEADS), :] = kn
        vo_ref[pl.ds(g, tq, stride=N_KV_HEADS), :] = v
        qs = []
        for hh in range(KV_GROUP):
            h = g * KV_GROUP + hh
            qs.append(_head_norm(q_ref[:, h * HEAD_DIM:(h + 1) * HEAD_DIM], gq).astype(BF16))
        q4 = jnp.concatenate(qs, axis=0)
        s = lax.dot_general(q4, kn.astype(BF16), (((1,), (1,)), ((), ())),
                            preferred_element_type=F32) * scale
        o = _softmax_pv([s], [v.astype(BF16)])
        for hh in range(KV_GROUP):
            h = g * KV_GROUP + hh
            o_ref[:, h * HEAD_DIM:(h + 1) * HEAD_DIM] = o[hh * tq:(hh + 1) * tq].astype(o_ref.dtype)


def attention_ctx(z, g_q, g_k):
    nb = N_CTX_SEQ
    t = CTX_LEN
    return pl.pallas_call(
        _attn_ctx_body,
        grid=(nb,),
        in_specs=[
            pl.BlockSpec((t, Q_COLS), lambda b: (b, 0)),
            pl.BlockSpec((t, 2 * KV_COLS), lambda b: (b, COL_K // (2 * KV_COLS))),
            pl.BlockSpec((1, HEAD_DIM), lambda b: (0, 0)),
            pl.BlockSpec((1, HEAD_DIM), lambda b: (0, 0)),
        ],
        out_specs=[
            pl.BlockSpec((t, Q_COLS), lambda b: (b, 0)),
            pl.BlockSpec((t * N_KV_HEADS, HEAD_DIM), lambda b: (b, 0)),
            pl.BlockSpec((t * N_KV_HEADS, HEAD_DIM), lambda b: (b, 0)),
        ],
        out_shape=[
            jax.ShapeDtypeStruct((N_CTX, Q_COLS), BF16),
            jax.ShapeDtypeStruct((N_CTX * N_KV_HEADS, HEAD_DIM), F32),
            jax.ShapeDtypeStruct((N_CTX * N_KV_HEADS, HEAD_DIM), F32),
        ],
        compiler_params=_params(("arbitrary",)),
        name="attention_ctx",
    )(z, z, g_q.reshape(1, HEAD_DIM), g_k.reshape(1, HEAD_DIM))


def _attn_lat_body(q_ref, kv_ref, ck_ref, cv_ref, cos_ref, slo_ref, shi_ref, gq_ref, gk_ref,
                   o_ref, kr_ref):
    tq = q_ref.shape[0]
    qb = pl.program_id(1)
    scale = HEAD_DIM ** -0.5
    gq = gq_ref[...]

    @pl.when(qb == 0)
    def _():
        gk = gk_ref[...]
        for g in range(N_KV_HEADS):
            kcols = slice(g * HEAD_DIM, (g + 1) * HEAD_DIM)
            kn = _head_norm(kv_ref[:, kcols], gk)
            kr_ref[:, kcols] = _rope(kn, cos_ref[...], slo_ref[...], shi_ref[...]).astype(BF16)

    row0 = pl.multiple_of(qb * tq, tq)
    cos = cos_ref[pl.ds(row0, tq), :]
    slo = slo_ref[pl.ds(row0, tq), :]
    shi = shi_ref[pl.ds(row0, tq), :]
    for g in range(N_KV_HEADS):
        kcols = slice(g * HEAD_DIM, (g + 1) * HEAD_DIM)
        qs = []
        for hh in range(KV_GROUP):
            h = g * KV_GROUP + hh
            qn = _head_norm(q_ref[:, h * HEAD_DIM:(h + 1) * HEAD_DIM], gq)
            qs.append(_rope(qn, cos, slo, shi).astype(BF16))
        q4 = jnp.concatenate(qs, axis=0)
        dn = (((1,), (1,)), ((), ()))
        s_past = lax.dot_general(q4, ck_ref[:, kcols].astype(BF16), dn,
                                 preferred_element_type=F32) * scale
        s_new = lax.dot_general(q4, kr_ref[:, kcols], dn, preferred_element_type=F32) * scale
        v_past = cv_ref[:, kcols].astype(BF16)
        v_new = kv_ref[:, KV_COLS + g * HEAD_DIM:KV_COLS + (g + 1) * HEAD_DIM].astype(BF16)
        o = _softmax_pv([s_past, s_new], [v_past, v_new])
        for hh in range(KV_GROUP):
            h = g * KV_GROUP + hh
            o_ref[:, h * HEAD_DIM:(h + 1) * HEAD_DIM] = o[hh * tq:(hh + 1) * tq].astype(o_ref.dtype)


def attention_lat(z, cache_k, cache_v, rope_tabs, g_q, g_k):
    tq = 256
    nq = LAT_LEN // tq
    cos, slo, shi = rope_tabs
    tab = pl.BlockSpec((LAT_LEN, HEAD_DIM), lambda b, q: (0, 0))
    return pl.pallas_call(
        _attn_lat_body,
        grid=(N_LAT_SEQ, nq),
        in_specs=[
            pl.BlockSpec((tq, Q_COLS), lambda b, q: (b * nq + q, 0)),
            pl.BlockSpec((LAT_LEN, 2 * KV_COLS), lambda b, q: (b, COL_K // (2 * KV_COLS))),
            pl.BlockSpec((None, PAST_LEN, KV_COLS), lambda b, q: (b, 0, 0)),
            pl.BlockSpec((None, PAST_LEN, KV_COLS), lambda b, q: (b, 0, 0)),
            tab, tab, tab,
            pl.BlockSpec((1, HEAD_DIM), lambda b, q: (0, 0)),
            pl.BlockSpec((1, HEAD_DIM), lambda b, q: (0, 0)),
        ],
        out_specs=pl.BlockSpec((tq, Q_COLS), lambda b, q: (b * nq + q, 0)),
        out_shape=jax.ShapeDtypeStruct((N_LAT, Q_COLS), BF16),
        scratch_shapes=[pltpu.VMEM((LAT_LEN, KV_COLS), BF16)],
        compiler_params=_params(("arbitrary", "arbitrary")),
        name="attention_lat",
    )(z, z, cache_k, cache_v, cos, slo, shi, g_q.reshape(1, HEAD_DIM), g_k.reshape(1, HEAD_DIM))


def _rope_tables():
    t = np.arange(LAT_LEN)
    row = jnp.asarray(t // GRID_W, F32)
    col = jnp.asarray(t % GRID_W, F32)
    nf = HEAD_DIM // 4
    inv_freq = ROPE_THETA ** (-jnp.arange(nf, dtype=F32) / nf)
    ang_row = row[:, None] * inv_freq[None, :]
    ang_col = col[:, None] * inv_freq[None, :]
    ang = jnp.concatenate([ang_row, ang_row, ang_col, ang_col], axis=1)
    cos = jnp.cos(ang)
    sin = jnp.sin(ang)
    first = jnp.asarray((np.arange(HEAD_DIM) % (2 * nf)) < nf)[None, :]
    return cos, jnp.where(first, -sin, 0.0), jnp.where(first, 0.0, sin)


RNN_ROWS = 2048
RNN_COLS = 512
RNN_SUB = RNN_COLS // RNN_BLOCK_DIM


def _gelu_tanh(y):
    return 0.5 * y * (1.0 + jnp.tanh(0.7978845608028654 * (y + 0.044715 * (y * y * y))))


def _rglru_body(seq_len, xr_ref, yr_ref, cw_ref, cb_ref, wg_ref, bg_ref, lam_ref, h0f_ref, h0b_ref,
                o_ref, hf_ref, hb_ref, xs_ref, af_ref, bf_ref, ab_ref, bb_ref):
    n_seq = RNN_ROWS // seq_len
    for n in range(RNN_SUB):
        cols = slice(n * RNN_BLOCK_DIM, (n + 1) * RNN_BLOCK_DIM)
        for s in range(n_seq):
            xs_ref[n, pl.ds(s, seq_len, stride=n_seq), :] = xr_ref[s * seq_len:(s + 1) * seq_len, cols]

    row = lax.broadcasted_iota(jnp.int32, (RNN_ROWS, 1), 0)
    lam = lam_ref[...]
    softplus_neg = jnp.maximum(-lam, 0.0) + jnp.log(1.0 + jnp.exp(-jnp.abs(lam)))
    rate = softplus_neg * (-RG_C * 1.4426950408889634)
    for n in range(RNN_SUB):
        cols = slice(n * RNN_BLOCK_DIM, (n + 1) * RNN_BLOCK_DIM)
        x = xs_ref[n]
        x_m1 = jnp.where(row >= n_seq, pltpu.roll(x, n_seq, 0), 0.0)
        x_p1 = jnp.where(row < RNN_ROWS - n_seq, pltpu.roll(x, RNN_ROWS - n_seq, 0), 0.0)
        x_p2 = jnp.where(row < RNN_ROWS - 2 * n_seq, pltpu.roll(x, RNN_ROWS - 2 * n_seq, 0), 0.0)
        xn = (cb_ref[:, cols] + x_m1 * cw_ref[0:1, cols] + x * cw_ref[1:2, cols]
              + x_p1 * cw_ref[2:3, cols] + x_p2 * cw_ref[3:4, cols])
        pre = jnp.dot(xn.astype(BF16), wg_ref[n], preferred_element_type=F32) + bg_ref[n]
        for d, (a_ref, b_ref) in enumerate(((af_ref, bf_ref), (ab_ref, bb_ref))):
            r = 0.5 * jnp.tanh(0.5 * pre[:, (2 * d) * RNN_BLOCK_DIM:(2 * d + 1) * RNN_BLOCK_DIM]) + 0.5
            gate_in = 0.5 * jnp.tanh(
                0.5 * pre[:, (2 * d + 1) * RNN_BLOCK_DIM:(2 * d + 2) * RNN_BLOCK_DIM]) + 0.5
            a = jnp.exp2(r * rate[d:d + 1, cols])
            v = 1.0 - a * a
            a_ref[n] = a
            b_ref[n] = (v * lax.rsqrt(jnp.maximum(v, 1e-30))) * (gate_in * xn)

    def step(t, carry):
        rows_f = pl.ds(pl.multiple_of(t * n_seq, n_seq), n_seq)
        rows_b = pl.ds(pl.multiple_of((seq_len - 1 - t) * n_seq, n_seq), n_seq)
        out = []
        for n in range(RNN_SUB):
            hf = af_ref[n, rows_f, :] * carry[2 * n] + bf_ref[n, rows_f, :]
            hb = ab_ref[n, rows_b, :] * carry[2 * n + 1] + bb_ref[n, rows_b, :]
            bf_ref[n, rows_f, :] = hf
            bb_ref[n, rows_b, :] = hb
            out += [hf, hb]
        return tuple(out)

    init = []
    for n in range(RNN_SUB):
        cols = slice(n * RNN_BLOCK_DIM, (n + 1) * RNN_BLOCK_DIM)
        init += [h0f_ref[:, 0, cols], h0b_ref[:, 0, cols]]
    last = lax.fori_loop(0, seq_len, step, tuple(init), unroll=8)
    for n in range(RNN_SUB):
        cols = slice(n * RNN_BLOCK_DIM, (n + 1) * RNN_BLOCK_DIM)
        hf_ref[:, 0, cols] = last[2 * n]
        hb_ref[:, 0, cols] = last[2 * n + 1]
        bf_ref[n] = bf_ref[n] + bb_ref[n]
        for s in range(n_seq):
            rows = slice(s * seq_len, (s + 1) * seq_len)
            h_sum = bf_ref[n, pl.ds(s, seq_len, stride=n_seq), :]
            o_ref[rows, cols] = (h_sum * _gelu_tanh(yr_ref[rows, cols])).astype(o_ref.dtype)


def rglru_mixer(z, seq_len, conv_w, conv_b, w_gates, b_gates, lam, h0_f, h0_b):
    m = z.shape[0]
    n_seq_total = m // seq_len
    n_seq = RNN_ROWS // seq_len
    cblk = lambda base: (lambda r, c: (r, base // RNN_COLS + c))
    state_spec = pl.BlockSpec((n_seq, 1, RNN_COLS), lambda r, c: (r, 0, c))
    return pl.pallas_call(
        functools.partial(_rglru_body, seq_len),
        grid=(m // RNN_ROWS, D_MODEL // RNN_COLS),
        in_specs=[
            pl.BlockSpec((RNN_ROWS, RNN_COLS), cblk(COL_XR)),
            pl.BlockSpec((RNN_ROWS, RNN_COLS), cblk(COL_YR)),
            pl.BlockSpec((4, RNN_COLS), lambda r, c: (0, c)),
            pl.BlockSpec((1, RNN_COLS), lambda r, c: (0, c)),
            pl.BlockSpec((RNN_SUB, RNN_BLOCK_DIM, 4 * RNN_BLOCK_DIM), lambda r, c: (c, 0, 0)),
            pl.BlockSpec((RNN_SUB, 1, 4 * RNN_BLOCK_DIM), lambda r, c: (c, 0, 0)),
            pl.BlockSpec((2, RNN_COLS), lambda r, c: (0, c)),
            state_spec, state_spec,
        ],
        out_specs=[
            pl.BlockSpec((RNN_ROWS, RNN_COLS), lambda r, c: (r, c)),
            state_spec, state_spec,
        ],
        out_shape=[
            jax.ShapeDtypeStruct((m, D_MODEL), BF16),
            jax.ShapeDtypeStruct((n_seq_total, 1, D_MODEL), F32),
            jax.ShapeDtypeStruct((n_seq_total, 1, D_MODEL), F32),
        ],
        scratch_shapes=[pltpu.VMEM((RNN_SUB, RNN_ROWS, RNN_BLOCK_DIM), F32) for _ in range(5)],
        compiler_params=_params(("arbitrary", "arbitrary")),
        name="rglru_mixer_t%d" % seq_len,
    )(z, z, conv_w, conv_b.reshape(1, D_MODEL), w_gates, b_gates, lam, h0_f, h0_b)


def _merge_body(a_ref, r_ref, wa_ref, wr_ref, ga_ref, gr_ref, o_ref, wa_bf, wr_bf):
    @pl.when(pl.program_id(1) == 0)
    def _():
        wa_bf[...] = wa_ref[...].astype(BF16)
        wr_bf[...] = wr_ref[...].astype(BF16)

    pa = jnp.dot(a_ref[...], wa_bf[...], preferred_element_type=F32)
    pr = jnp.dot(r_ref[...], wr_bf[...], preferred_element_type=F32)
    o_ref[...] = (_sigmoid(ga_ref[...]) * pa + _sigmoid(gr_ref[...]) * pr).astype(o_ref.dtype)


def gated_merge(attn, rnn, z, w_o_attn, w_o_rnn):
    m = attn.shape[0]
    tm, tn = 1024, 512
    return pl.pallas_call(
        _merge_body,
        grid=(D_MODEL // tn, m // tm),
        in_specs=[
            pl.BlockSpec((tm, Q_COLS), lambda j, i: (i, 0)),
            pl.BlockSpec((tm, D_MODEL), lambda j, i: (i, 0)),
            pl.BlockSpec((Q_COLS, tn), lambda j, i: (0, j)),
            pl.BlockSpec((D_MODEL, tn), lambda j, i: (0, j)),
            pl.BlockSpec((tm, tn), lambda j, i: (i, COL_GA // tn + j)),
            pl.BlockSpec((tm, tn), lambda j, i: (i, COL_GR // tn + j)),
        ],
        out_specs=pl.BlockSpec((tm, tn), lambda j, i: (i, j)),
        out_shape=jax.ShapeDtypeStruct((m, D_MODEL), BF16),
        scratch_shapes=[pltpu.VMEM((Q_COLS, tn), BF16), pltpu.VMEM((D_MODEL, tn), BF16)],
        compiler_params=_params(("arbitrary", "arbitrary")),
        name="gated_merge",
    )(attn, rnn, w_o_attn, w_o_rnn, z, z)


POST_TM = 512
POST_SPLIT = 1
HALF_D = D_MODEL // 2
WORD_ROWS = HALF_D // 128
SUBLANES = 8
POST_CTX_BLOCKS = N_CTX // POST_TM
LAT_BLOCKS_PER_SEQ = LAT_LEN // POST_TM


def _post_group(i):
    return jnp.where(i < POST_CTX_BLOCKS, 0, 1 + (i - POST_CTX_BLOCKS) // LAT_BLOCKS_PER_SEQ)


def _postmix_body(mc_ref, ml_ref, xc_ref, xl_ref, wo_ref, gpm_ref, gt1_ref, gpf_ref, sh2_ref, sc2_ref,
                  wr_ref, br_ref, x1_ref, h2_ref, e_ref, gate_ref, rank_ref, cnt_ref, carry_ref):
    i = pl.program_id(0)
    tm = POST_TM

    @pl.when(i == 0)
    def _():
        carry_ref[...] = jnp.zeros_like(carry_ref)

    is_ctx = i < POST_CTX_BLOCKS
    th = tm // POST_SPLIT
    r_io = lax.broadcasted_iota(jnp.int32, (th, th), 0)
    c_io = lax.broadcasted_iota(jnp.int32, (th, th), 1)
    lower = jnp.where(c_io < r_io, 1.0, 0.0).astype(BF16)
    lane = lax.broadcasted_iota(jnp.int32, (th, N_EXPERTS), 1)
    lane_k = lax.broadcasted_iota(jnp.int32, (th, TOP_K), 1)
    wr = wr_ref[...].astype(BF16)
    carry = carry_ref[...]
    for part in range(POST_SPLIT):
        rows = slice(part * th, (part + 1) * th)
        merged = jnp.where(is_ctx, mc_ref[rows, :], ml_ref[rows, :])
        x = jnp.where(is_ctx, xc_ref[rows, :], xl_ref[rows, :])
        o = jnp.dot(merged, wo_ref[...], preferred_element_type=F32)
        x1 = x + gt1_ref[...] * (o * _rms_scale(o) * gpm_ref[...])
        x1_ref[rows, :] = x1
        h2 = (x1 * _rms_scale(x1) * gpf_ref[...]) * (1.0 + sc2_ref[...]) + sh2_ref[...]
        h2_bf = h2.astype(BF16)
        bits = lax.bitcast_convert_type(h2_bf.astype(F32), jnp.uint32)
        words = (lax.shift_right_logical(bits[:, :HALF_D], jnp.uint32(16))
                 | (bits[:, HALF_D:] & jnp.uint32(0xFFFF0000)))
        for c in range(WORD_ROWS):
            h2_ref[pl.ds(part * th * WORD_ROWS + c, th, stride=WORD_ROWS), :] = words[:, c * 128:(c + 1) * 128]

        logits = jnp.dot(h2_bf, wr, preferred_element_type=F32) + br_ref[...]
        work = logits
        chosen = jnp.zeros((th, N_EXPERTS), F32)
        sels, vals, idxs = [], [], []
        for _ in range(TOP_K):
            mx = jnp.max(work, axis=-1, keepdims=True)
            idx = jnp.min(jnp.where(work == mx, lane, N_EXPERTS), axis=-1, keepdims=True)
            sel = lane == idx
            work = jnp.where(sel, -jnp.inf, work)
            chosen = jnp.where(sel, 1.0, chosen)
            sels.append(sel)
            vals.append(mx)
            idxs.append(idx)
        exps = [jnp.exp(v - vals[0]) for v in vals]
        inv = 1.0 / (exps[0] + exps[1] + exps[2] + exps[3])

        before = jnp.dot(lower, chosen.astype(BF16), preferred_element_type=F32) + carry
        carry = carry + jnp.sum(chosen, axis=0, keepdims=True)

        e_out = jnp.zeros((th, TOP_K), jnp.int32)
        g_out = jnp.zeros((th, TOP_K), F32)
        r_out = jnp.zeros((th, TOP_K), jnp.int32)
        for k in range(TOP_K):
            rk = jnp.sum(jnp.where(sels[k], before, 0.0), axis=-1, keepdims=True).astype(jnp.int32)
            e_out = jnp.where(lane_k == k, idxs[k], e_out)
            g_out = jnp.where(lane_k == k, exps[k] * inv, g_out)
            r_out = jnp.where(lane_k == k, rk, r_out)
        e_ref[rows, :] = e_out
        gate_ref[rows, :] = g_out
        rank_ref[rows, :] = r_out
    carry_ref[...] = carry
    cnt_ref[...] = carry


def post_mix_router(merged_ctx, merged_lat, x_ctx, x_lat, w_out_bf, g_post_mix, gt1, g_pre_ffn, sh2, sc2,
                    w_router, b_router):
    tm = POST_TM
    ctx_map = lambda i: (jnp.minimum(i, POST_CTX_BLOCKS - 1), 0)
    lat_map = lambda i: (jnp.maximum(i - POST_CTX_BLOCKS, 0), 0)
    gmap = lambda i: (_post_group(i), 0, 0)
    row = lambda i: (i, 0)
    const = lambda i: (0, 0)
    vec = pl.BlockSpec((1, D_MODEL), const)
    gvec = pl.BlockSpec((None, 1, D_MODEL), gmap)
    return pl.pallas_call(
        _postmix_body,
        grid=(N_TOK // tm,),
        in_specs=[
            pl.BlockSpec((tm, D_MODEL), ctx_map),
            pl.BlockSpec((tm, D_MODEL), lat_map),
            pl.BlockSpec((tm, D_MODEL), ctx_map),
            pl.BlockSpec((tm, D_MODEL), lat_map),
            pl.BlockSpec((D_MODEL, D_MODEL), const),
            vec, gvec, vec, gvec, gvec,
            pl.BlockSpec((D_MODEL, N_EXPERTS), const),
            pl.BlockSpec((1, N_EXPERTS), const),
        ],
        out_specs=[
            pl.BlockSpec((tm, D_MODEL), row),
            pl.BlockSpec((tm * WORD_ROWS, 128), row),
            pl.BlockSpec((tm, TOP_K), row),
            pl.BlockSpec((tm, TOP_K), row),
            pl.BlockSpec((tm, TOP_K), row),
            pl.BlockSpec((1, N_EXPERTS), const),
        ],
        out_shape=[
            jax.ShapeDtypeStruct((N_TOK, D_MODEL), F32),
            jax.ShapeDtypeStruct((N_TOK * WORD_ROWS, 128), jnp.uint32),
            jax.ShapeDtypeStruct((N_TOK, TOP_K), jnp.int32),
            jax.ShapeDtypeStruct((N_TOK, TOP_K), F32),
            jax.ShapeDtypeStruct((N_TOK, TOP_K), jnp.int32),
            jax.ShapeDtypeStruct((1, N_EXPERTS), F32),
        ],
        scratch_shapes=[pltpu.VMEM((1, N_EXPERTS), F32)],
        compiler_params=_params(("arbitrary",)),
        name="post_mix_router",
    )(merged_ctx, merged_lat, x_ctx, x_lat, w_out_bf, g_post_mix.reshape(1, D_MODEL), gt1,
      g_pre_ffn.reshape(1, D_MODEL), sh2, sc2, w_router, b_router.reshape(1, N_EXPERTS))


GATHER_ROWS = ROW_TILE // (2 * N_FF_CHUNKS)
GATHER_PRIORITY = 1


def _unpack_tile(xbuf_ref, slot, i):
    base = pl.multiple_of(i * (ROW_TILE * WORD_ROWS), ROW_TILE * WORD_ROWS)
    lo, hi = [], []
    for c in range(WORD_ROWS):
        words = xbuf_ref[slot, pl.ds(base + c, ROW_TILE, stride=WORD_ROWS), :]
        lo.append(lax.bitcast_convert_type(lax.shift_left(words, jnp.uint32(16)), F32).astype(BF16))
        hi.append(lax.bitcast_convert_type(words & jnp.uint32(0xFFFF0000), F32).astype(BF16))
    return jnp.concatenate(lo + hi, axis=1)


def _for_tiles(n_tiles, body):
    def one(i, _):
        body(i)
        return 0

    lax.fori_loop(0, n_tiles, one, 0)


def _moe_body(exp_ref, row_ref, nsub_ref, nzero_ref, tok_ref, h_ref, wg_ref, wl_ref, wd_ref, bg_ref, bl_ref,
              bd_ref, y_ref, xbuf_ref, act_ref, wg_bf, wl_bf, wd_bf, stage_ref, idx_ref, pend_ref,
              xsem, isem, ysem):
    s = pl.program_id(0)
    j = pl.program_id(1)
    n_sub = nsub_ref[s]
    row_start = row_ref[s]
    n_next = nsub_ref[jnp.minimum(s + 1, N_SUPER - 1)]

    def idx_copy(p):
        tile0 = pl.multiple_of(row_ref[p], ROW_TILE) // ROW_TILE
        return pltpu.make_async_copy(tok_ref.at[pl.ds(tile0, SUPER_TILES)], idx_ref.at[p % 2],
                                     isem.at[p % 2])

    def gather_row(slot, tile, col, r):
        t = idx_ref[slot, tile, 0, col]
        src = h_ref.at[pl.ds(pl.multiple_of(t * WORD_ROWS, WORD_ROWS), WORD_ROWS), :]
        dst = xbuf_ref.at[slot, pl.ds(pl.multiple_of(r * WORD_ROWS, WORD_ROWS), WORD_ROWS), :]
        pltpu.make_async_copy(src, dst, xsem.at[slot]).start(priority=GATHER_PRIORITY)

    def gather_chunk(step, i):
        first = (step * n_sub + i) * GATHER_ROWS
        tile = lax.shift_right_logical(first, 8)
        col = jnp.bitwise_and(first, ROW_TILE - 1)
        for g in range(GATHER_ROWS):
            gather_row((s + 1) % 2, tile, col + g, first + g)

    def gather_range(p, first, last):
        def issue(r, _):
            gather_row(p % 2, lax.shift_right_logical(r, 8), jnp.bitwise_and(r, ROW_TILE - 1), r)
            return 0

        lax.fori_loop(first, last, issue, 0)

    def wait_rows(slot, count):
        @pl.when(count > 0)
        def _():
            n = pl.multiple_of(count * WORD_ROWS, ROW_TILE * WORD_ROWS)
            window = xbuf_ref.at[slot, pl.ds(0, n), :]
            pltpu.make_async_copy(window, window, xsem.at[slot]).wait()

    @pl.when(jnp.logical_and(s == 0, j == 0))
    def _():
        pend_ref[0] = 0
        pend_ref[1] = 0

    def drain_stage(slot):
        @pl.when(pend_ref[slot] == 1)
        def _():
            pltpu.make_async_copy(stage_ref.at[slot], stage_ref.at[slot], ysem.at[slot]).wait()
            pend_ref[slot] = 0

    @pl.when(jnp.logical_and(s == 0, j == 0))
    def _():
        idx_copy(0).start()
        idx_copy(0).wait()
        gather_range(0, 0, n_sub * ROW_TILE)
        wait_rows(0, n_sub * ROW_TILE)
        idx_copy(1).start()

    @pl.when(jnp.logical_and(s > 0, j == 0))
    def _():
        wait_rows(s % 2, jnp.maximum(nsub_ref[jnp.maximum(s - 1, 0)], n_sub) * ROW_TILE)

    @pl.when(jnp.logical_and(j == 0, s + 1 < N_SUPER))
    def _():
        idx_copy(s + 1).wait()

    @pl.when(jnp.logical_and(j == 0, s + 2 < N_SUPER))
    def _():
        idx_copy(s + 2).start()

    @pl.when(jnp.logical_and(j == 0, jnp.logical_and(n_sub == 0, s + 1 < N_SUPER)))
    def _():
        gather_range(s + 1, 0, n_next * ROW_TILE)

    @pl.when(jnp.logical_and(j < N_FF_CHUNKS, n_sub > 0))
    def _():
        wg_bf[...] = wg_ref[...].astype(BF16)
        wl_bf[...] = wl_ref[...].astype(BF16)
        bg = bg_ref[...]
        bl = bl_ref[...]

        def up_tile(i):
            rows = pl.ds(pl.multiple_of(i * ROW_TILE, ROW_TILE), ROW_TILE)
            xt = _unpack_tile(xbuf_ref, s % 2, i)
            glu = jnp.minimum(jnp.dot(xt, wg_bf[...], preferred_element_type=F32) + bg, SWIGLU_LIMIT)
            lin = jnp.clip(jnp.dot(xt, wl_bf[...], preferred_element_type=F32) + bl,
                           -SWIGLU_LIMIT, SWIGLU_LIMIT)
            gather_chunk(j, i)
            act = glu * _sigmoid(SWIGLU_ALPHA * glu) * (lin + 1.0)
            act_ref[j, rows, :] = act.astype(BF16)

        _for_tiles(n_sub, up_tile)

    for cc in range(N_FF_CHUNKS):
        @pl.when(jnp.logical_and(j == N_FF_CHUNKS + cc, n_sub > 0))
        def _(cc=cc):
            wd_bf[...] = wd_ref[...].astype(BF16)
            bd = bd_ref[...]

            def out_copy(i, slot):
                dst = y_ref.at[pl.ds(pl.multiple_of(row_start + i * ROW_TILE, ROW_TILE), ROW_TILE),
                               cc * FF_CHUNK:(cc + 1) * FF_CHUNK]
                return pltpu.make_async_copy(stage_ref.at[slot], dst, ysem.at[slot])

            def down_tile(i):
                rows = pl.ds(pl.multiple_of(i * ROW_TILE, ROW_TILE), ROW_TILE)
                slot = i % 2
                acc = bd
                for c in range(N_FF_CHUNKS):
                    acc = acc + jnp.dot(act_ref[c, rows, :], wd_bf[c * FF_CHUNK:(c + 1) * FF_CHUNK, :],
                                        preferred_element_type=F32)
                gather_chunk(N_FF_CHUNKS + cc, i)
                drain_stage(slot)
                stage_ref[slot] = acc
                out_copy(i, slot).start()
                pend_ref[slot] = 1

            _for_tiles(n_sub, down_tile)
            if cc == N_FF_CHUNKS - 1:
                gather_range(s + 1, n_sub * ROW_TILE, n_next * ROW_TILE)

    n_zero = nzero_ref[s]

    @pl.when(jnp.logical_and(j == 0, n_zero > 0))
    def _():
        drain_stage(0)
        stage_ref[0] = jnp.zeros((ROW_TILE, FF_CHUNK), F32)

        def zero_copy(i, cc):
            dst = y_ref.at[pl.ds(pl.multiple_of(row_start + i * ROW_TILE, ROW_TILE), ROW_TILE),
                           cc * FF_CHUNK:(cc + 1) * FF_CHUNK]
            return pltpu.make_async_copy(stage_ref.at[0], dst, ysem.at[0])

        def issue(i, _):
            for cc in range(N_FF_CHUNKS):
                zero_copy(i, cc).start()
            return 0

        def drain(i, _):
            for cc in range(N_FF_CHUNKS):
                zero_copy(i, cc).wait()
            return 0

        lax.fori_loop(0, n_zero, issue, 0)
        lax.fori_loop(0, n_zero, drain, 0)

    @pl.when(jnp.logical_and(s == N_SUPER - 1, j == 2 * N_FF_CHUNKS - 1))
    def _():
        drain_stage(0)
        drain_stage(1)


def expert_mlp(h_packed, tok_sorted, sched, w_gate_up, b_gate_up, w_down, b_down):
    exp_of, row_of, nsub_of, nzero_of = sched
    last = N_FF_CHUNKS - 1
    up_of = lambda s, j, n: jnp.where(n[s] > 0, jnp.minimum(j, last), last)
    down_of = lambda s, j, n: jnp.where(n[s] > 0, jnp.maximum(j - N_FF_CHUNKS, 0), last)
    up_chunk = lambda s, j, e, r, n, z: (e[s], 0, up_of(s, j, n))
    lin_chunk = lambda s, j, e, r, n, z: (e[s], 0, N_FF_CHUNKS + up_of(s, j, n))
    down_chunk = lambda s, j, e, r, n, z: (e[s], 0, down_of(s, j, n))
    grid_spec = pltpu.PrefetchScalarGridSpec(
        num_scalar_prefetch=4,
        grid=(N_SUPER, 2 * N_FF_CHUNKS),
        in_specs=[
            pl.BlockSpec(memory_space=pl.ANY),
            pl.BlockSpec(memory_space=pl.ANY),
            pl.BlockSpec((None, D_MODEL, FF_CHUNK), up_chunk),
            pl.BlockSpec((None, D_MODEL, FF_CHUNK), lin_chunk),
            pl.BlockSpec((None, D_FF, FF_CHUNK), down_chunk),
            pl.BlockSpec((None, 1, FF_CHUNK), up_chunk),
            pl.BlockSpec((None, 1, FF_CHUNK), lin_chunk),
            pl.BlockSpec((None, 1, FF_CHUNK), down_chunk),
        ],
        out_specs=pl.BlockSpec(memory_space=pl.ANY),
        scratch_shapes=[
            pltpu.VMEM((2, SUPER_ROWS * WORD_ROWS, 128), jnp.uint32),
            pltpu.VMEM((N_FF_CHUNKS, SUPER_ROWS, FF_CHUNK), BF16),
            pltpu.VMEM((D_MODEL, FF_CHUNK), BF16),
            pltpu.VMEM((D_MODEL, FF_CHUNK), BF16),
            pltpu.VMEM((D_FF, FF_CHUNK), BF16),
            pltpu.VMEM((2, ROW_TILE, FF_CHUNK), F32),
            pltpu.SMEM((2, SUPER_TILES, 1, ROW_TILE), jnp.int32),
            pltpu.SMEM((2,), jnp.int32),
            pltpu.SemaphoreType.DMA((2,)),
            pltpu.SemaphoreType.DMA((2,)),
            pltpu.SemaphoreType.DMA((2,)),
        ],
    )
    tok_tiles = jnp.concatenate([tok_sorted.reshape(N_ROW_TILES, 1, ROW_TILE),
                                 jnp.zeros((SUPER_TILES, 1, ROW_TILE), jnp.int32)], axis=0)
    return pl.pallas_call(
        _moe_body,
        grid_spec=grid_spec,
        out_shape=jax.ShapeDtypeStruct((N_ROWS, D_MODEL), F32),
        compiler_params=_params(("arbitrary", "arbitrary"), vmem=EXPERT_VMEM_LIMIT),
        name="expert_mlp",
    )(exp_of, row_of, nsub_of, nzero_of, tok_tiles, h_packed, w_gate_up, w_gate_up, w_down,
      b_gate_up.reshape(N_EXPERTS, 1, 2 * D_FF), b_gate_up.reshape(N_EXPERTS, 1, 2 * D_FF),
      b_down.reshape(N_EXPERTS, 1, D_MODEL))


COMB_TB = 256


def _combine_start(y_ref, ybuf_ref, pos_ref, sem):
    def issue(t, _):
        for k in range(TOP_K):
            p = pos_ref[0, 0, t * TOP_K + k]
            pltpu.make_async_copy(y_ref.at[pl.ds(p, 1), :], ybuf_ref.at[k, pl.ds(t, 1), :], sem).start()
        return 0

    lax.fori_loop(0, COMB_TB, issue, 0, unroll=4)


def _combine_body(n, pos_ref, pos_next_ref, y_ref, gate_ref, x1_ref, gt2_ref, g_ref, o_ref, ybuf_ref, sem_ref):
    i = pl.program_id(0)
    slot = i % 2

    @pl.when(i == 0)
    def _():
        _combine_start(y_ref, ybuf_ref.at[0], pos_ref, sem_ref.at[0])

    @pl.when(i + 1 < n)
    def _():
        _combine_start(y_ref, ybuf_ref.at[1 - slot], pos_next_ref, sem_ref.at[1 - slot])

    for k in range(TOP_K):
        pltpu.make_async_copy(y_ref.at[pl.ds(0, COMB_TB), :], ybuf_ref.at[slot, k], sem_ref.at[slot]).wait()
    gates = gate_ref[...]
    ffn = gates[:, 0:1] * ybuf_ref[slot, 0]
    for k in range(1, TOP_K):
        ffn = ffn + gates[:, k:k + 1] * ybuf_ref[slot, k]
    o_ref[...] = x1_ref[...] + gt2_ref[...] * (ffn * _rms_scale(ffn) * g_ref[...])


def combine_residual(y_sorted, pos, gates, x1, gt2, g_post_ffn, row_offset, n_rows, group_of_block):
    tb = COMB_TB
    nblk = n_rows // tb
    off = row_offset // tb
    pos3 = pos.reshape(N_TOK // tb, 1, tb * TOP_K)
    smem_blk = lambda f: pl.BlockSpec((1, 1, tb * TOP_K), f, memory_space=pltpu.SMEM)
    return pl.pallas_call(
        functools.partial(_combine_body, nblk),
        grid=(nblk,),
        in_specs=[
            smem_blk(lambda i: (off + i, 0, 0)),
            smem_blk(lambda i: (off + jnp.minimum(i + 1, nblk - 1), 0, 0)),
            pl.BlockSpec(memory_space=pl.ANY),
            pl.BlockSpec((tb, TOP_K), lambda i: (off + i, 0)),
            pl.BlockSpec((tb, D_MODEL), lambda i: (off + i, 0)),
            pl.BlockSpec((None, 1, D_MODEL), lambda i: (group_of_block(i), 0, 0)),
            pl.BlockSpec((1, D_MODEL), lambda i: (0, 0)),
        ],
        out_specs=pl.BlockSpec((tb, D_MODEL), lambda i: (i, 0)),
        out_shape=jax.ShapeDtypeStruct((n_rows, D_MODEL), F32),
        scratch_shapes=[pltpu.VMEM((2, TOP_K, tb, D_MODEL), F32), pltpu.SemaphoreType.DMA((2,))],
        compiler_params=_params(("arbitrary",)),
        name="combine_residual",
    )(pos3, pos3, y_sorted, gates, x1, gt2, g_post_ffn.reshape(1, D_MODEL))


INV_CHUNK = 4096


def _row_tokens_body(pos_ref, zeros_ref, o_ref, sem):
    i = pl.program_id(0)

    @pl.when(i == 0)
    def _():
        cp = pltpu.make_async_copy(zeros_ref, o_ref, sem)
        cp.start()
        cp.wait()

    base = i * INV_CHUNK

    def put(r, _):
        o_ref[pos_ref[r]] = lax.shift_right_logical(base + r, 2)
        return 0

    lax.fori_loop(0, INV_CHUNK, put, 0, unroll=8)


def row_tokens(pos):
    return pl.pallas_call(
        _row_tokens_body,
        grid=(N_ASSIGN // INV_CHUNK,),
        in_specs=[
            pl.BlockSpec((INV_CHUNK,), lambda i: (i,), memory_space=pltpu.SMEM),
            pl.BlockSpec(memory_space=pl.ANY),
        ],
        out_specs=pl.BlockSpec(memory_space=pltpu.SMEM),
        out_shape=jax.ShapeDtypeStruct((N_ROWS,), jnp.int32),
        scratch_shapes=[pltpu.SemaphoreType.DMA(())],
        compiler_params=_params(("arbitrary",)),
        name="row_tokens",
    )(pos.reshape(N_ASSIGN), jnp.zeros((N_ROWS,), jnp.int32))


def _routing_tables(e_idx, rank, counts_f):
    counts = counts_f.reshape(N_EXPERTS).astype(jnp.int32)
    n_tiles = (counts + ROW_TILE - 1) // ROW_TILE
    padded = n_tiles * ROW_TILE
    pad_end = jnp.cumsum(padded)
    pad_start = pad_end - padded
    pos = (pad_start[e_idx] + rank).astype(jnp.int32)
    tok_sorted = row_tokens(pos)
    n_pass = (n_tiles + SUPER_TILES - 1) // SUPER_TILES
    pass_end = jnp.cumsum(n_pass)
    total = pass_end[-1]
    s = jnp.arange(N_SUPER, dtype=jnp.int32)
    s_eff = jnp.minimum(s, total - 1)
    e_of = jnp.minimum(jnp.searchsorted(pass_end, s_eff, side="right"), N_EXPERTS - 1).astype(jnp.int32)
    local = s_eff - (pass_end[e_of] - n_pass[e_of])
    row_of = pad_start[e_of] + local * SUPER_ROWS
    nsub = jnp.minimum(SUPER_TILES, n_tiles[e_of] - local * SUPER_TILES)
    nsub = jnp.where(s < total, nsub, 0).astype(jnp.int32)
    zero_row = pad_end[-1] + (s - total) * SUPER_ROWS
    nzero = jnp.clip((N_ROWS - zero_row) // ROW_TILE, 0, SUPER_TILES)
    nzero = jnp.where(s >= total, nzero, 0).astype(jnp.int32)
    row_of = jnp.where(s < total, row_of, jnp.minimum(zero_row, N_ROWS - ROW_TILE)).astype(jnp.int32)
    return pos.astype(jnp.int32), tok_sorted, (e_of, row_of, nsub, nzero)


def kernel(x_prompt, x_sample, cache_k, cache_v, state_rnn_fwd, state_rnn_bwd, c, c_ctx, w_mod, b_mod, g_pre_mix, w_in, g_q_norm, g_k_norm, conv_w, conv_b, rg_w_a, rg_b_a, rg_w_x, rg_b_x, rg_lambda, w_o_attn, w_o_rnn, w_out, g_post_mix, g_pre_ffn, w_router, b_router, w_gate_up, b_gate_up, w_down, b_down, g_post_ffn):
    l = 0
    x_ctx = x_prompt.reshape(N_CTX, D_MODEL)
    x_lat = x_sample.reshape(N_LAT, D_MODEL)

    cond8 = jnp.concatenate([c_ctx[None, :], c, jnp.zeros((8 - 1 - N_LAT_SEQ, D_MODEL), F32)], axis=0)
    mod = modulation(cond8, w_mod[l], b_mod[l])[:1 + N_LAT_SEQ].reshape(1 + N_LAT_SEQ, 6, 1, D_MODEL)
    sh1, sc1, gt1, sh2, sc2, gt2 = [mod[:, i] for i in range(6)]

    ctx_group = lambda i: 0
    lat_group_1024 = lambda i: 1 + i
    h_ctx = prenorm_modulate(x_ctx, g_pre_mix[l], sh1, sc1, ctx_group, 1024)
    h_lat = prenorm_modulate(x_lat, g_pre_mix[l], sh1, sc1, lat_group_1024, 1024)
    z_ctx = in_projection(h_ctx, w_in[l])
    z_lat = in_projection(h_lat, w_in[l])

    attn_ctx, k_new, v_new = attention_ctx(z_ctx, g_q_norm[l], g_k_norm[l])
    attn_lat = attention_lat(z_lat, cache_k[:, l].reshape(N_LAT_SEQ, PAST_LEN, KV_COLS),
                             cache_v[:, l].reshape(N_LAT_SEQ, PAST_LEN, KV_COLS),
                             _rope_tables(), g_q_norm[l], g_k_norm[l])

    def per_block(w):
        return w.reshape(2, RNN_BLOCKS, 1, RNN_BLOCK_DIM)

    w_gates = jnp.concatenate([rg_w_a[l, 0], rg_w_x[l, 0], rg_w_a[l, 1], rg_w_x[l, 1]], axis=-1).astype(BF16)
    ba, bx = per_block(rg_b_a[l]), per_block(rg_b_x[l])
    b_gates = jnp.concatenate([ba[0], bx[0], ba[1], bx[1]], axis=-1)
    zeros_state = jnp.zeros((N_CTX_SEQ, 1, D_MODEL), F32)
    rnn_ctx, hf_ctx, hb_ctx = rglru_mixer(z_ctx, CTX_LEN, conv_w[l], conv_b[l], w_gates, b_gates,
                                          rg_lambda[l], zeros_state, zeros_state)
    rnn_lat, _, _ = rglru_mixer(z_lat, LAT_LEN, conv_w[l], conv_b[l], w_gates, b_gates, rg_lambda[l],
                                state_rnn_fwd[:, l].reshape(N_LAT_SEQ, 1, D_MODEL),
                                state_rnn_bwd[:, l].reshape(N_LAT_SEQ, 1, D_MODEL))

    merged_ctx = gated_merge(attn_ctx, rnn_ctx, z_ctx, w_o_attn[l], w_o_rnn[l])
    merged_lat = gated_merge(attn_lat, rnn_lat, z_lat, w_o_attn[l], w_o_rnn[l])

    x1, h2, e_idx, gates, rank, counts = post_mix_router(
        merged_ctx, merged_lat, x_ctx, x_lat, w_out[l].astype(BF16), g_post_mix[l], gt1, g_pre_ffn[l],
        sh2, sc2, w_router[l], b_router[l])

    pos, tok_sorted, sched = _routing_tables(e_idx, rank, counts)
    y_sorted = expert_mlp(h2, tok_sorted, sched, w_gate_up[l], b_gate_up[l], w_down[l], b_down[l])

    y_ctx = combine_residual(y_sorted, pos, gates, x1, gt2, g_post_ffn[l], 0, N_CTX, ctx_group)
    y_lat = combine_residual(y_sorted, pos, gates, x1, gt2, g_post_ffn[l], N_CTX, N_LAT,
                             lambda i: 1 + i // (LAT_LEN // COMB_TB))

    return (y_ctx.reshape(N_CTX_SEQ, CTX_LEN, D_MODEL),
            y_lat.reshape(N_LAT_SEQ, LAT_LEN, D_MODEL),
            k_new.reshape(N_CTX_SEQ, 1, CTX_LEN, N_KV_HEADS, HEAD_DIM),
            v_new.reshape(N_CTX_SEQ, 1, CTX_LEN, N_KV_HEADS, HEAD_DIM),
            hf_ctx,
            hb_ctx)
```

```python
import functools

import jax
import jax.numpy as jnp
import numpy as np
from jax import lax
from jax.experimental import pallas as pl
from jax.experimental.pallas import tpu as pltpu

D_MODEL = 2048
N_CTX_SEQ = 32
CTX_LEN = 256
N_LAT_SEQ = 2
LAT_LEN = 1024
PAST_LEN = 512
N_CTX = N_CTX_SEQ * CTX_LEN
N_LAT = N_LAT_SEQ * LAT_LEN
N_TOK = N_CTX + N_LAT
GRID_W = 64
N_HEADS = 16
N_KV_HEADS = 4
HEAD_DIM = 128
KV_GROUP = N_HEADS // N_KV_HEADS
ROPE_THETA = 10000.0
RNN_BLOCKS = 16
RNN_BLOCK_DIM = 128
RG_C = 8.0
N_EXPERTS = 32
TOP_K = 4
D_FF = 2048
SWIGLU_LIMIT = 7.0
SWIGLU_ALPHA = 1.702
EPS = 1e-6
Q_COLS = N_HEADS * HEAD_DIM
KV_COLS = N_KV_HEADS * HEAD_DIM
IN_COLS = Q_COLS + 2 * KV_COLS + 4 * D_MODEL
COL_K = Q_COLS
COL_XR = Q_COLS + 2 * KV_COLS
COL_YR = COL_XR + D_MODEL
COL_GA = COL_YR + D_MODEL
COL_GR = COL_GA + D_MODEL

V7X_VMEM_BYTES = 64 * 1024 * 1024
VMEM_LIMIT = 56 * 1024 * 1024
EXPERT_VMEM_LIMIT = 60 * 1024 * 1024

ROW_TILE = 256
SUPER_TILES = 8
SUPER_ROWS = ROW_TILE * SUPER_TILES
N_ASSIGN = N_TOK * TOP_K
N_ROWS = N_ASSIGN + N_EXPERTS * ROW_TILE
N_ROW_TILES = N_ROWS // ROW_TILE
N_SUPER = N_ROW_TILES // SUPER_TILES + N_EXPERTS
FF_CHUNK = 512
N_FF_CHUNKS = D_FF // FF_CHUNK

BF16 = jnp.bfloat16
F32 = jnp.float32


def _params(semantics, vmem=VMEM_LIMIT):
    return pltpu.CompilerParams(dimension_semantics=semantics, vmem_limit_bytes=vmem)


def _rms_scale(x):
    return lax.rsqrt(jnp.mean(x * x, axis=-1, keepdims=True) + EPS)


def _sigmoid(x):
    return 1.0 / (1.0 + jnp.exp(-x))


def _mod_body(c_ref, w_ref, b_ref, o_ref):
    c = c_ref[...]
    a = (c * _sigmoid(c)).astype(BF16)
    o_ref[...] = jnp.dot(a, w_ref[...].astype(BF16), preferred_element_type=F32) + b_ref[...]


def modulation(cond8, w_mod, b_mod):
    tn = 1024
    n = w_mod.shape[1]
    return pl.pallas_call(
        _mod_body,
        grid=(n // tn,),
        in_specs=[
            pl.BlockSpec((8, D_MODEL), lambda j: (0, 0)),
            pl.BlockSpec((D_MODEL, tn), lambda j: (0, j)),
            pl.BlockSpec((1, tn), lambda j: (0, j)),
        ],
        out_specs=pl.BlockSpec((8, tn), lambda j: (0, j)),
        out_shape=jax.ShapeDtypeStruct((8, n), F32),
        compiler_params=_params(("arbitrary",)),
        name="modulation",
    )(cond8, w_mod, b_mod.reshape(1, n))


def _prenorm_body(x_ref, g_ref, sh_ref, sc_ref, o_ref):
    x = x_ref[...]
    y = x * _rms_scale(x) * g_ref[...]
    o_ref[...] = (y * (1.0 + sc_ref[...]) + sh_ref[...]).astype(o_ref.dtype)


def prenorm_modulate(x, g, shift, scale, group_of_block, tm):
    m = x.shape[0]
    gmap = lambda i: (group_of_block(i), 0, 0)
    return pl.pallas_call(
        _prenorm_body,
        grid=(m // tm,),
        in_specs=[
            pl.BlockSpec((tm, D_MODEL), lambda i: (i, 0)),
            pl.BlockSpec((1, D_MODEL), lambda i: (0, 0)),
            pl.BlockSpec((None, 1, D_MODEL), gmap),
            pl.BlockSpec((None, 1, D_MODEL), gmap),
        ],
        out_specs=pl.BlockSpec((tm, D_MODEL), lambda i: (i, 0)),
        out_shape=jax.ShapeDtypeStruct((m, D_MODEL), BF16),
        compiler_params=_params(("arbitrary",)),
        name="prenorm_modulate",
    )(x, g.reshape(1, D_MODEL), shift, scale)


def _inproj_body(h_ref, w_ref, o_ref, wbf_ref):
    @pl.when(pl.program_id(1) == 0)
    def _():
        wbf_ref[...] = w_ref[...].astype(BF16)

    o_ref[...] = jnp.dot(h_ref[...], wbf_ref[...], preferred_element_type=F32)


def in_projection(h, w_in):
    m = h.shape[0]
    tm, tn = 1024, 1024
    return pl.pallas_call(
        _inproj_body,
        grid=(IN_COLS // tn, m // tm),
        in_specs=[
            pl.BlockSpec((tm, D_MODEL), lambda j, i: (i, 0)),
            pl.BlockSpec((D_MODEL, tn), lambda j, i: (0, j)),
        ],
        out_specs=pl.BlockSpec((tm, tn), lambda j, i: (i, j)),
        out_shape=jax.ShapeDtypeStruct((m, IN_COLS), F32),
        scratch_shapes=[pltpu.VMEM((D_MODEL, tn), BF16)],
        compiler_params=_params(("arbitrary", "arbitrary")),
        name="in_projection",
    )(h, w_in)


def _rope(x, cos, sin_lo, sin_hi):
    return x * cos + pltpu.roll(x, 96, 1) * sin_lo + pltpu.roll(x, 32, 1) * sin_hi


def _head_norm(x, g):
    return x * _rms_scale(x) * g


def _softmax_pv(score_blocks, value_blocks):
    m = None
    for s in score_blocks:
        mi = jnp.max(s, axis=-1, keepdims=True)
        m = mi if m is None else jnp.maximum(m, mi)
    ps = [jnp.exp(s - m) for s in score_blocks]
    denom = None
    for p in ps:
        li = jnp.sum(p, axis=-1, keepdims=True)
        denom = li if denom is None else denom + li
    inv = 1.0 / denom
    out = None
    for p, v in zip(ps, value_blocks):
        o = jnp.dot((p * inv).astype(BF16), v, preferred_element_type=F32)
        out = o if out is None else out + o
    return out


def _attn_ctx_body(q_ref, kv_ref, gq_ref, gk_ref, o_ref, ko_ref, vo_ref):
    tq = q_ref.shape[0]
    scale = HEAD_DIM ** -0.5
    gq = gq_ref[...]
    gk = gk_ref[...]
    for g in range(N_KV_HEADS):
        kcols = slice(g * HEAD_DIM, (g + 1) * HEAD_DIM)
        kn = _head_norm(kv_ref[:, kcols], gk)
        v = kv_ref[:, KV_COLS + g * HEAD_DIM:KV_COLS + (g + 1) * HEAD_DIM]
        ko_ref[pl.ds(g, tq, stride=N_KV_HEADS), :] = kn
        vo_ref[pl.ds(g, tq, stride=N_KV_HEADS), :] = v
        qs = []
        for hh in range(KV_GROUP):
            h = g * KV_GROUP + hh
            qs.append(_head_norm(q_ref[:, h * HEAD_DIM:(h + 1) * HEAD_DIM], gq).astype(BF16))
        q4 = jnp.concatenate(qs, axis=0)
        s = lax.dot_general(q4, kn.astype(BF16), (((1,), (1,)), ((), ())),
                            preferred_element_type=F32) * scale
        o = _softmax_pv([s], [v.astype(BF16)])
        for hh in range(KV_GROUP):
            h = g * KV_GROUP + hh
            o_ref[:, h * HEAD_DIM:(h + 1) * HEAD_DIM] = o[hh * tq:(hh + 1) * tq].astype(o_ref.dtype)


def attention_ctx(z, g_q, g_k):
    nb = N_CTX_SEQ
    t = CTX_LEN
    return pl.pallas_call(
        _attn_ctx_body,
        grid=(nb,),
        in_specs=[
            pl.BlockSpec((t, Q_COLS), lambda b: (b, 0)),
            pl.BlockSpec((t, 2 * KV_COLS), lambda b: (b, COL_K // (2 * KV_COLS))),
            pl.BlockSpec((1, HEAD_DIM), lambda b: (0, 0)),
            pl.BlockSpec((1, HEAD_DIM), lambda b: (0, 0)),
        ],
        out_specs=[
            pl.BlockSpec((t, Q_COLS), lambda b: (b, 0)),
            pl.BlockSpec((t * N_KV_HEADS, HEAD_DIM), lambda b: (b, 0)),
            pl.BlockSpec((t * N_KV_HEADS, HEAD_DIM), lambda b: (b, 0)),
        ],
        out_shape=[
            jax.ShapeDtypeStruct((N_CTX, Q_COLS), BF16),
            jax.ShapeDtypeStruct((N_CTX * N_KV_HEADS, HEAD_DIM), F32),
            jax.ShapeDtypeStruct((N_CTX * N_KV_HEADS, HEAD_DIM), F32),
        ],
        compiler_params=_params(("arbitrary",)),
        name="attention_ctx",
    )(z, z, g_q.reshape(1, HEAD_DIM), g_k.reshape(1, HEAD_DIM))


def _attn_lat_body(q_ref, kv_ref, ck_ref, cv_ref, cos_ref, slo_ref, shi_ref, gq_ref, gk_ref,
                   o_ref, kr_ref):
    tq = q_ref.shape[0]
    qb = pl.program_id(1)
    scale = HEAD_DIM ** -0.5
    gq = gq_ref[...]

    @pl.when(qb == 0)
    def _():
        gk = gk_ref[...]
        for g in range(N_KV_HEADS):
            kcols = slice(g * HEAD_DIM, (g + 1) * HEAD_DIM)
            kn = _head_norm(kv_ref[:, kcols], gk)
            kr_ref[:, kcols] = _rope(kn, cos_ref[...], slo_ref[...], shi_ref[...]).astype(BF16)

    row0 = pl.multiple_of(qb * tq, tq)
    cos = cos_ref[pl.ds(row0, tq), :]
    slo = slo_ref[pl.ds(row0, tq), :]
    shi = shi_ref[pl.ds(row0, tq), :]
    for g in range(N_KV_HEADS):
        kcols = slice(g * HEAD_DIM, (g + 1) * HEAD_DIM)
        qs = []
        for hh in range(KV_GROUP):
            h = g * KV_GROUP + hh
            qn = _head_norm(q_ref[:, h * HEAD_DIM:(h + 1) * HEAD_DIM], gq)
            qs.append(_rope(qn, cos, slo, shi).astype(BF16))
        q4 = jnp.concatenate(qs, axis=0)
        dn = (((1,), (1,)), ((), ()))
        s_past = lax.dot_general(q4, ck_ref[:, kcols].astype(BF16), dn,
                                 preferred_element_type=F32) * scale
        s_new = lax.dot_general(q4, kr_ref[:, kcols], dn, preferred_element_type=F32) * scale
        v_past = cv_ref[:, kcols].astype(BF16)
        v_new = kv_ref[:, KV_COLS + g * HEAD_DIM:KV_COLS + (g + 1) * HEAD_DIM].astype(BF16)
        o = _softmax_pv([s_past, s_new], [v_past, v_new])
        for hh in range(KV_GROUP):
            h = g * KV_GROUP + hh
            o_ref[:, h * HEAD_DIM:(h + 1) * HEAD_DIM] = o[hh * tq:(hh + 1) * tq].astype(o_ref.dtype)


def attention_lat(z, cache_k, cache_v, rope_tabs, g_q, g_k):
    tq = 256
    nq = LAT_LEN // tq
    cos, slo, shi = rope_tabs
    tab = pl.BlockSpec((LAT_LEN, HEAD_DIM), lambda b, q: (0, 0))
    return pl.pallas_call(
        _attn_lat_body,
        grid=(N_LAT_SEQ, nq),
        in_specs=[
            pl.BlockSpec((tq, Q_COLS), lambda b, q: (b * nq + q, 0)),
            pl.BlockSpec((LAT_LEN, 2 * KV_COLS), lambda b, q: (b, COL_K // (2 * KV_COLS))),
            pl.BlockSpec((None, PAST_LEN, KV_COLS), lambda b, q: (b, 0, 0)),
            pl.BlockSpec((None, PAST_LEN, KV_COLS), lambda b, q: (b, 0, 0)),
            tab, tab, tab,
            pl.BlockSpec((1, HEAD_DIM), lambda b, q: (0, 0)),
            pl.BlockSpec((1, HEAD_DIM), lambda b, q: (0, 0)),
        ],
        out_specs=pl.BlockSpec((tq, Q_COLS), lambda b, q: (b * nq + q, 0)),
        out_shape=jax.ShapeDtypeStruct((N_LAT, Q_COLS), BF16),
        scratch_shapes=[pltpu.VMEM((LAT_LEN, KV_COLS), BF16)],
        compiler_params=_params(("arbitrary", "arbitrary")),
        name="attention_lat",
    )(z, z, cache_k, cache_v, cos, slo, shi, g_q.reshape(1, HEAD_DIM), g_k.reshape(1, HEAD_DIM))


def _rope_tables():
    t = np.arange(LAT_LEN)
    row = jnp.asarray(t // GRID_W, F32)
    col = jnp.asarray(t % GRID_W, F32)
    nf = HEAD_DIM // 4
    inv_freq = ROPE_THETA ** (-jnp.arange(nf, dtype=F32) / nf)
    ang_row = row[:, None] * inv_freq[None, :]
    ang_col = col[:, None] * inv_freq[None, :]
    ang = jnp.concatenate([ang_row, ang_row, ang_col, ang_col], axis=1)
    cos = jnp.cos(ang)
    sin = jnp.sin(ang)
    first = jnp.asarray((np.arange(HEAD_DIM) % (2 * nf)) < nf)[None, :]
    return cos, jnp.where(first, -sin, 0.0), jnp.where(first, 0.0, sin)


RNN_ROWS = 2048
RNN_COLS = 512
RNN_SUB = RNN_COLS // RNN_BLOCK_DIM


def _gelu_tanh(y):
    return 0.5 * y * (1.0 + jnp.tanh(0.7978845608028654 * (y + 0.044715 * (y * y * y))))


def _rglru_body(seq_len, xr_ref, yr_ref, cw_ref, cb_ref, wg_ref, bg_ref, lam_ref, h0f_ref, h0b_ref,
                o_ref, hf_ref, hb_ref, xs_ref, af_ref, bf_ref, ab_ref, bb_ref):
    n_seq = RNN_ROWS // seq_len
    for n in range(RNN_SUB):
        cols = slice(n * RNN_BLOCK_DIM, (n + 1) * RNN_BLOCK_DIM)
        for s in range(n_seq):
            xs_ref[n, pl.ds(s, seq_len, stride=n_seq), :] = xr_ref[s * seq_len:(s + 1) * seq_len, cols]

    row = lax.broadcasted_iota(jnp.int32, (RNN_ROWS, 1), 0)
    lam = lam_ref[...]
    softplus_neg = jnp.maximum(-lam, 0.0) + jnp.log(1.0 + jnp.exp(-jnp.abs(lam)))
    rate = softplus_neg * (-RG_C * 1.4426950408889634)
    for n in range(RNN_SUB):
        cols = slice(n * RNN_BLOCK_DIM, (n + 1) * RNN_BLOCK_DIM)
        x = xs_ref[n]
        x_m1 = jnp.where(row >= n_seq, pltpu.roll(x, n_seq, 0), 0.0)
        x_p1 = jnp.where(row < RNN_ROWS - n_seq, pltpu.roll(x, RNN_ROWS - n_seq, 0), 0.0)
        x_p2 = jnp.where(row < RNN_ROWS - 2 * n_seq, pltpu.roll(x, RNN_ROWS - 2 * n_seq, 0), 0.0)
        xn = (cb_ref[:, cols] + x_m1 * cw_ref[0:1, cols] + x * cw_ref[1:2, cols]
              + x_p1 * cw_ref[2:3, cols] + x_p2 * cw_ref[3:4, cols])
        pre = jnp.dot(xn.astype(BF16), wg_ref[n], preferred_element_type=F32) + bg_ref[n]
        for d, (a_ref, b_ref) in enumerate(((af_ref, bf_ref), (ab_ref, bb_ref))):
            r = 0.5 * jnp.tanh(0.5 * pre[:, (2 * d) * RNN_BLOCK_DIM:(2 * d + 1) * RNN_BLOCK_DIM]) + 0.5
            gate_in = 0.5 * jnp.tanh(
                0.5 * pre[:, (2 * d + 1) * RNN_BLOCK_DIM:(2 * d + 2) * RNN_BLOCK_DIM]) + 0.5
            a = jnp.exp2(r * rate[d:d + 1, cols])
            v = 1.0 - a * a
            a_ref[n] = a
            b_ref[n] = (v * lax.rsqrt(jnp.maximum(v, 1e-30))) * (gate_in * xn)

    def step(t, carry):
        rows_f = pl.ds(pl.multiple_of(t * n_seq, n_seq), n_seq)
        rows_b = pl.ds(pl.multiple_of((seq_len - 1 - t) * n_seq, n_seq), n_seq)
        out = []
        for n in range(RNN_SUB):
            hf = af_ref[n, rows_f, :] * carry[2 * n] + bf_ref[n, rows_f, :]
            hb = ab_ref[n, rows_b, :] * carry[2 * n + 1] + bb_ref[n, rows_b, :]
            bf_ref[n, rows_f, :] = hf
            bb_ref[n, rows_b, :] = hb
            out += [hf, hb]
        return tuple(out)

    init = []
    for n in range(RNN_SUB):
        cols = slice(n * RNN_BLOCK_DIM, (n + 1) * RNN_BLOCK_DIM)
        init += [h0f_ref[:, 0, cols], h0b_ref[:, 0, cols]]
    last = lax.fori_loop(0, seq_len, step, tuple(init), unroll=8)
    for n in range(RNN_SUB):
        cols = slice(n * RNN_BLOCK_DIM, (n + 1) * RNN_BLOCK_DIM)
        hf_ref[:, 0, cols] = last[2 * n]
        hb_ref[:, 0, cols] = last[2 * n + 1]
        bf_ref[n] = bf_ref[n] + bb_ref[n]
        for s in range(n_seq):
            rows = slice(s * seq_len, (s + 1) * seq_len)
            h_sum = bf_ref[n, pl.ds(s, seq_len, stride=n_seq), :]
            o_ref[rows, cols] = (h_sum * _gelu_tanh(yr_ref[rows, cols])).astype(o_ref.dtype)


def rglru_mixer(z, seq_len, conv_w, conv_b, w_gates, b_gates, lam, h0_f, h0_b):
    m = z.shape[0]
    n_seq_total = m // seq_len
    n_seq = RNN_ROWS // seq_len
    cblk = lambda base: (lambda r, c: (r, base // RNN_COLS + c))
    state_spec = pl.BlockSpec((n_seq, 1, RNN_COLS), lambda r, c: (r, 0, c))
    return pl.pallas_call(
        functools.partial(_rglru_body, seq_len),
        grid=(m // RNN_ROWS, D_MODEL // RNN_COLS),
        in_specs=[
            pl.BlockSpec((RNN_ROWS, RNN_COLS), cblk(COL_XR)),
            pl.BlockSpec((RNN_ROWS, RNN_COLS), cblk(COL_YR)),
            pl.BlockSpec((4, RNN_COLS), lambda r, c: (0, c)),
            pl.BlockSpec((1, RNN_COLS), lambda r, c: (0, c)),
            pl.BlockSpec((RNN_SUB, RNN_BLOCK_DIM, 4 * RNN_BLOCK_DIM), lambda r, c: (c, 0, 0)),
            pl.BlockSpec((RNN_SUB, 1, 4 * RNN_BLOCK_DIM), lambda r, c: (c, 0, 0)),
            pl.BlockSpec((2, RNN_COLS), lambda r, c: (0, c)),
            state_spec, state_spec,
        ],
        out_specs=[
            pl.BlockSpec((RNN_ROWS, RNN_COLS), lambda r, c: (r, c)),
            state_spec, state_spec,
        ],
        out_shape=[
            jax.ShapeDtypeStruct((m, D_MODEL), BF16),
            jax.ShapeDtypeStruct((n_seq_total, 1, D_MODEL), F32),
            jax.ShapeDtypeStruct((n_seq_total, 1, D_MODEL), F32),
        ],
        scratch_shapes=[pltpu.VMEM((RNN_SUB, RNN_ROWS, RNN_BLOCK_DIM), F32) for _ in range(5)],
        compiler_params=_params(("arbitrary", "arbitrary")),
        name="rglru_mixer_t%d" % seq_len,
    )(z, z, conv_w, conv_b.reshape(1, D_MODEL), w_gates, b_gates, lam, h0_f, h0_b)


def _merge_body(a_ref, r_ref, wa_ref, wr_ref, ga_ref, gr_ref, o_ref, wa_bf, wr_bf):
    @pl.when(pl.program_id(1) == 0)
    def _():
        wa_bf[...] = wa_ref[...].astype(BF16)
        wr_bf[...] = wr_ref[...].astype(BF16)

    pa = jnp.dot(a_ref[...], wa_bf[...], preferred_element_type=F32)
    pr = jnp.dot(r_ref[...], wr_bf[...], preferred_element_type=F32)
    o_ref[...] = (_sigmoid(ga_ref[...]) * pa + _sigmoid(gr_ref[...]) * pr).astype(o_ref.dtype)


def gated_merge(attn, rnn, z, w_o_attn, w_o_rnn):
    m = attn.shape[0]
    tm, tn = 1024, 512
    return pl.pallas_call(
        _merge_body,
        grid=(D_MODEL // tn, m // tm),
        in_specs=[
            pl.BlockSpec((tm, Q_COLS), lambda j, i: (i, 0)),
            pl.BlockSpec((tm, D_MODEL), lambda j, i: (i, 0)),
            pl.BlockSpec((Q_COLS, tn), lambda j, i: (0, j)),
            pl.BlockSpec((D_MODEL, tn), lambda j, i: (0, j)),
            pl.BlockSpec((tm, tn), lambda j, i: (i, COL_GA // tn + j)),
            pl.BlockSpec((tm, tn), lambda j, i: (i, COL_GR // tn + j)),
        ],
        out_specs=pl.BlockSpec((tm, tn), lambda j, i: (i, j)),
        out_shape=jax.ShapeDtypeStruct((m, D_MODEL), BF16),
        scratch_shapes=[pltpu.VMEM((Q_COLS, tn), BF16), pltpu.VMEM((D_MODEL, tn), BF16)],
        compiler_params=_params(("arbitrary", "arbitrary")),
        name="gated_merge",
    )(attn, rnn, w_o_attn, w_o_rnn, z, z)


POST_TM = 512
POST_SPLIT = 1
HALF_D = D_MODEL // 2
WORD_ROWS = HALF_D // 128
SUBLANES = 8
POST_CTX_BLOCKS = N_CTX // POST_TM
LAT_BLOCKS_PER_SEQ = LAT_LEN // POST_TM


def _post_group(i):
    return jnp.where(i < POST_CTX_BLOCKS, 0, 1 + (i - POST_CTX_BLOCKS) // LAT_BLOCKS_PER_SEQ)


def _postmix_body(mc_ref, ml_ref, xc_ref, xl_ref, wo_ref, gpm_ref, gt1_ref, gpf_ref, sh2_ref, sc2_ref,
                  wr_ref, br_ref, x1_ref, h2_ref, e_ref, gate_ref, rank_ref, cnt_ref, carry_ref):
    i = pl.program_id(0)
    tm = POST_TM

    @pl.when(i == 0)
    def _():
        carry_ref[...] = jnp.zeros_like(carry_ref)

    is_ctx = i < POST_CTX_BLOCKS
    th = tm // POST_SPLIT
    r_io = lax.broadcasted_iota(jnp.int32, (th, th), 0)
    c_io = lax.broadcasted_iota(jnp.int32, (th, th), 1)
    lower = jnp.where(c_io < r_io, 1.0, 0.0).astype(BF16)
    lane = lax.broadcasted_iota(jnp.int32, (th, N_EXPERTS), 1)
    lane_k = lax.broadcasted_iota(jnp.int32, (th, TOP_K), 1)
    wr = wr_ref[...].astype(BF16)
    carry = carry_ref[...]
    for part in range(POST_SPLIT):
        rows = slice(part * th, (part + 1) * th)
        merged = jnp.where(is_ctx, mc_ref[rows, :], ml_ref[rows, :])
        x = jnp.where(is_ctx, xc_ref[rows, :], xl_ref[rows, :])
        o = jnp.dot(merged, wo_ref[...], preferred_element_type=F32)
        x1 = x + gt1_ref[...] * (o * _rms_scale(o) * gpm_ref[...])
        x1_ref[rows, :] = x1
        h2 = (x1 * _rms_scale(x1) * gpf_ref[...]) * (1.0 + sc2_ref[...]) + sh2_ref[...]
        h2_bf = h2.astype(BF16)
        bits = lax.bitcast_convert_type(h2_bf.astype(F32), jnp.uint32)
        words = (lax.shift_right_logical(bits[:, :HALF_D], jnp.uint32(16))
                 | (bits[:, HALF_D:] & jnp.uint32(0xFFFF0000)))
        for c in range(WORD_ROWS):
            h2_ref[pl.ds(part * th * WORD_ROWS + c, th, stride=WORD_ROWS), :] = words[:, c * 128:(c + 1) * 128]

        logits = jnp.dot(h2_bf, wr, preferred_element_type=F32) + br_ref[...]
        work = logits
        chosen = jnp.zeros((th, N_EXPERTS), F32)
        sels, vals, idxs = [], [], []
        for _ in range(TOP_K):
            mx = jnp.max(work, axis=-1, keepdims=True)
            idx = jnp.min(jnp.where(work == mx, lane, N_EXPERTS), axis=-1, keepdims=True)
            sel = lane == idx
            work = jnp.where(sel, -jnp.inf, work)
            chosen = jnp.where(sel, 1.0, chosen)
            sels.append(sel)
            vals.append(mx)
            idxs.append(idx)
        exps = [jnp.exp(v - vals[0]) for v in vals]
        inv = 1.0 / (exps[0] + exps[1] + exps[2] + exps[3])

        before = jnp.dot(lower, chosen.astype(BF16), preferred_element_type=F32) + carry
        carry = carry + jnp.sum(chosen, axis=0, keepdims=True)

        e_out = jnp.zeros((th, TOP_K), jnp.int32)
        g_out = jnp.zeros((th, TOP_K), F32)
        r_out = jnp.zeros((th, TOP_K), jnp.int32)
        for k in range(TOP_K):
            rk = jnp.sum(jnp.where(sels[k], before, 0.0), axis=-1, keepdims=True).astype(jnp.int32)
            e_out = jnp.where(lane_k == k, idxs[k], e_out)
            g_out = jnp.where(lane_k == k, exps[k] * inv, g_out)
            r_out = jnp.where(lane_k == k, rk, r_out)
        e_ref[rows, :] = e_out
        gate_ref[rows, :] = g_out
        rank_ref[rows, :] = r_out
    carry_ref[...] = carry
    cnt_ref[...] = carry


def post_mix_router(merged_ctx, merged_lat, x_ctx, x_lat, w_out_bf, g_post_mix, gt1, g_pre_ffn, sh2, sc2,
                    w_router, b_router):
    tm = POST_TM
    ctx_map = lambda i: (jnp.minimum(i, POST_CTX_BLOCKS - 1), 0)
    lat_map = lambda i: (jnp.maximum(i - POST_CTX_BLOCKS, 0), 0)
    gmap = lambda i: (_post_group(i), 0, 0)
    row = lambda i: (i, 0)
    const = lambda i: (0, 0)
    vec = pl.BlockSpec((1, D_MODEL), const)
    gvec = pl.BlockSpec((None, 1, D_MODEL), gmap)
    return pl.pallas_call(
        _postmix_body,
        grid=(N_TOK // tm,),
        in_specs=[
            pl.BlockSpec((tm, D_MODEL), ctx_map),
            pl.BlockSpec((tm, D_MODEL), lat_map),
            pl.BlockSpec((tm, D_MODEL), ctx_map),
            pl.BlockSpec((tm, D_MODEL), lat_map),
            pl.BlockSpec((D_MODEL, D_MODEL), const),
            vec, gvec, vec, gvec, gvec,
            pl.BlockSpec((D_MODEL, N_EXPERTS), const),
            pl.BlockSpec((1, N_EXPERTS), const),
        ],
        out_specs=[
            pl.BlockSpec((tm, D_MODEL), row),
            pl.BlockSpec((tm * WORD_ROWS, 128), row),
            pl.BlockSpec((tm, TOP_K), row),
            pl.BlockSpec((tm, TOP_K), row),
            pl.BlockSpec((tm, TOP_K), row),
            pl.BlockSpec((1, N_EXPERTS), const),
        ],
        out_shape=[
            jax.ShapeDtypeStruct((N_TOK, D_MODEL), F32),
            jax.ShapeDtypeStruct((N_TOK * WORD_ROWS, 128), jnp.uint32),
            jax.ShapeDtypeStruct((N_TOK, TOP_K), jnp.int32),
            jax.ShapeDtypeStruct((N_TOK, TOP_K), F32),
            jax.ShapeDtypeStruct((N_TOK, TOP_K), jnp.int32),
            jax.ShapeDtypeStruct((1, N_EXPERTS), F32),
        ],
        scratch_shapes=[pltpu.VMEM((1, N_EXPERTS), F32)],
        compiler_params=_params(("arbitrary",)),
        name="post_mix_router",
    )(merged_ctx, merged_lat, x_ctx, x_lat, w_out_bf, g_post_mix.reshape(1, D_MODEL), gt1,
      g_pre_ffn.reshape(1, D_MODEL), sh2, sc2, w_router, b_router.reshape(1, N_EXPERTS))


GATHER_ROWS = ROW_TILE // (2 * N_FF_CHUNKS)
N_DMA_QUEUES = 2


def _unpack_tile(xbuf_ref, slot, i):
    base = pl.multiple_of(i * (ROW_TILE * WORD_ROWS), ROW_TILE * WORD_ROWS)
    lo, hi = [], []
    for c in range(WORD_ROWS):
        words = xbuf_ref[slot, pl.ds(base + c, ROW_TILE, stride=WORD_ROWS), :]
        lo.append(lax.bitcast_convert_type(lax.shift_left(words, jnp.uint32(16)), F32).astype(BF16))
        hi.append(lax.bitcast_convert_type(words & jnp.uint32(0xFFFF0000), F32).astype(BF16))
    return jnp.concatenate(lo + hi, axis=1)


def _for_tiles(n_tiles, body):
    def one(i, _):
        body(i)
        return 0

    lax.fori_loop(0, n_tiles, one, 0)


def _moe_body(exp_ref, row_ref, nsub_ref, nzero_ref, tok_ref, h_ref, wg_ref, wl_ref, wd_ref, bg_ref, bl_ref,
              bd_ref, y_ref, xbuf_ref, act_ref, wg_bf, wl_bf, wd_bf, stage_ref, idx_ref, pend_ref,
              xsem, isem, ysem):
    s = pl.program_id(0)
    j = pl.program_id(1)
    n_sub = nsub_ref[s]
    row_start = row_ref[s]
    n_next = nsub_ref[jnp.minimum(s + 1, N_SUPER - 1)]

    def idx_copy(p):
        tile0 = pl.multiple_of(row_ref[p], ROW_TILE) // ROW_TILE
        return pltpu.make_async_copy(tok_ref.at[pl.ds(tile0, SUPER_TILES)], idx_ref.at[p % 2],
                                     isem.at[p % 2])

    def gather_row(slot, tile, col, r, priority=0):
        t = idx_ref[slot, tile, 0, col]
        src = h_ref.at[pl.ds(pl.multiple_of(t * WORD_ROWS, WORD_ROWS), WORD_ROWS), :]
        dst = xbuf_ref.at[slot, pl.ds(pl.multiple_of(r * WORD_ROWS, WORD_ROWS), WORD_ROWS), :]
        pltpu.make_async_copy(src, dst, xsem.at[slot]).start(priority=priority)

    def gather_chunk(step, i):
        first = (step * n_sub + i) * GATHER_ROWS
        tile = lax.shift_right_logical(first, 8)
        col = jnp.bitwise_and(first, ROW_TILE - 1)
        for g in range(GATHER_ROWS):
            gather_row((s + 1) % 2, tile, col + g, first + g, priority=g % N_DMA_QUEUES)

    def gather_range(p, first, last):
        def issue(r, _):
            gather_row(p % 2, lax.shift_right_logical(r, 8), jnp.bitwise_and(r, ROW_TILE - 1), r)
            return 0

        lax.fori_loop(first, last, issue, 0)

    def wait_rows(slot, count):
        @pl.when(count > 0)
        def _():
            n = pl.multiple_of(count * WORD_ROWS, ROW_TILE * WORD_ROWS)
            window = xbuf_ref.at[slot, pl.ds(0, n), :]
            pltpu.make_async_copy(window, window, xsem.at[slot]).wait()

    @pl.when(jnp.logical_and(s == 0, j == 0))
    def _():
        pend_ref[0] = 0
        pend_ref[1] = 0

    def drain_stage(slot):
        @pl.when(pend_ref[slot] == 1)
        def _():
            pltpu.make_async_copy(stage_ref.at[slot], stage_ref.at[slot], ysem.at[slot]).wait()
            pend_ref[slot] = 0

    @pl.when(jnp.logical_and(s == 0, j == 0))
    def _():
        idx_copy(0).start()
        idx_copy(0).wait()
        gather_range(0, 0, n_sub * ROW_TILE)
        wait_rows(0, n_sub * ROW_TILE)
        idx_copy(1).start()

    @pl.when(jnp.logical_and(s > 0, j == 0))
    def _():
        wait_rows(s % 2, jnp.maximum(nsub_ref[jnp.maximum(s - 1, 0)], n_sub) * ROW_TILE)

    @pl.when(jnp.logical_and(j == 0, s + 1 < N_SUPER))
    def _():
        idx_copy(s + 1).wait()

    @pl.when(jnp.logical_and(j == 0, s + 2 < N_SUPER))
    def _():
        idx_copy(s + 2).start()

    @pl.when(jnp.logical_and(j == 0, jnp.logical_and(n_sub == 0, s + 1 < N_SUPER)))
    def _():
        gather_range(s + 1, 0, n_next * ROW_TILE)

    @pl.when(jnp.logical_and(j < N_FF_CHUNKS, n_sub > 0))
    def _():
        wg_bf[...] = wg_ref[...].astype(BF16)
        wl_bf[...] = wl_ref[...].astype(BF16)
        bg = bg_ref[...]
        bl = bl_ref[...]

        def up_tile(i):
            rows = pl.ds(pl.multiple_of(i * ROW_TILE, ROW_TILE), ROW_TILE)
            xt = _unpack_tile(xbuf_ref, s % 2, i)
            glu = jnp.minimum(jnp.dot(xt, wg_bf[...], preferred_element_type=F32) + bg, SWIGLU_LIMIT)
            lin = jnp.clip(jnp.dot(xt, wl_bf[...], preferred_element_type=F32) + bl,
                           -SWIGLU_LIMIT, SWIGLU_LIMIT)
            gather_chunk(j, i)
            act = glu * _sigmoid(SWIGLU_ALPHA * glu) * (lin + 1.0)
            act_ref[j, rows, :] = act.astype(BF16)

        _for_tiles(n_sub, up_tile)

    for cc in range(N_FF_CHUNKS):
        @pl.when(jnp.logical_and(j == N_FF_CHUNKS + cc, n_sub > 0))
        def _(cc=cc):
            wd_bf[...] = wd_ref[...].astype(BF16)
            bd = bd_ref[...]

            def out_copy(i, slot):
                dst = y_ref.at[pl.ds(pl.multiple_of(row_start + i * ROW_TILE, ROW_TILE), ROW_TILE),
                               cc * FF_CHUNK:(cc + 1) * FF_CHUNK]
                return pltpu.make_async_copy(stage_ref.at[slot], dst, ysem.at[slot])

            def down_tile(i):
                rows = pl.ds(pl.multiple_of(i * ROW_TILE, ROW_TILE), ROW_TILE)
                slot = i % 2
                drain_stage(slot)
                acc = bd
                for c in range(N_FF_CHUNKS):
                    acc = acc + jnp.dot(act_ref[c, rows, :], wd_bf[c * FF_CHUNK:(c + 1) * FF_CHUNK, :],
                                        preferred_element_type=F32)
                gather_chunk(N_FF_CHUNKS + cc, i)
                stage_ref[slot] = acc
                out_copy(i, slot).start()
                pend_ref[slot] = 1

            _for_tiles(n_sub, down_tile)
            if cc == N_FF_CHUNKS - 1:
                gather_range(s + 1, n_sub * ROW_TILE, n_next * ROW_TILE)

    n_zero = nzero_ref[s]

    @pl.when(jnp.logical_and(j == 0, n_zero > 0))
    def _():
        drain_stage(0)
        stage_ref[0] = jnp.zeros((ROW_TILE, FF_CHUNK), F32)

        def zero_copy(i, cc):
            dst = y_ref.at[pl.ds(pl.multiple_of(row_start + i * ROW_TILE, ROW_TILE), ROW_TILE),
                           cc * FF_CHUNK:(cc + 1) * FF_CHUNK]
            return pltpu.make_async_copy(stage_ref.at[0], dst, ysem.at[0])

        def issue(i, _):
            for cc in range(N_FF_CHUNKS):
                zero_copy(i, cc).start()
            return 0

        def drain(i, _):
            for cc in range(N_FF_CHUNKS):
                zero_copy(i, cc).wait()
            return 0

        lax.fori_loop(0, n_zero, issue, 0)
        lax.fori_loop(0, n_zero, drain, 0)

    @pl.when(jnp.logical_and(s == N_SUPER - 1, j == 2 * N_FF_CHUNKS - 1))
    def _():
        drain_stage(0)
        drain_stage(1)


def expert_mlp(h_packed, tok_sorted, sched, w_gate_up, b_gate_up, w_down, b_down):
    exp_of, row_of, nsub_of, nzero_of = sched
    last = N_FF_CHUNKS - 1
    up_of = lambda s, j, n: jnp.where(n[s] > 0, jnp.minimum(j, last), last)
    down_of = lambda s, j, n: jnp.where(n[s] > 0, jnp.maximum(j - N_FF_CHUNKS, 0), last)
    up_chunk = lambda s, j, e, r, n, z: (e[s], 0, up_of(s, j, n))
    lin_chunk = lambda s, j, e, r, n, z: (e[s], 0, N_FF_CHUNKS + up_of(s, j, n))
    down_chunk = lambda s, j, e, r, n, z: (e[s], 0, down_of(s, j, n))
    grid_spec = pltpu.PrefetchScalarGridSpec(
        num_scalar_prefetch=4,
        grid=(N_SUPER, 2 * N_FF_CHUNKS),
        in_specs=[
            pl.BlockSpec(memory_space=pl.ANY),
            pl.BlockSpec(memory_space=pl.ANY),
            pl.BlockSpec((None, D_MODEL, FF_CHUNK), up_chunk),
            pl.BlockSpec((None, D_MODEL, FF_CHUNK), lin_chunk),
            pl.BlockSpec((None, D_FF, FF_CHUNK), down_chunk),
            pl.BlockSpec((None, 1, FF_CHUNK), up_chunk),
            pl.BlockSpec((None, 1, FF_CHUNK), lin_chunk),
            pl.BlockSpec((None, 1, FF_CHUNK), down_chunk),
        ],
        out_specs=pl.BlockSpec(memory_space=pl.ANY),
        scratch_shapes=[
            pltpu.VMEM((2, SUPER_ROWS * WORD_ROWS, 128), jnp.uint32),
            pltpu.VMEM((N_FF_CHUNKS, SUPER_ROWS, FF_CHUNK), BF16),
            pltpu.VMEM((D_MODEL, FF_CHUNK), BF16),
            pltpu.VMEM((D_MODEL, FF_CHUNK), BF16),
            pltpu.VMEM((D_FF, FF_CHUNK), BF16),
            pltpu.VMEM((2, ROW_TILE, FF_CHUNK), F32),
            pltpu.SMEM((2, SUPER_TILES, 1, ROW_TILE), jnp.int32),
            pltpu.SMEM((2,), jnp.int32),
            pltpu.SemaphoreType.DMA((2,)),
            pltpu.SemaphoreType.DMA((2,)),
            pltpu.SemaphoreType.DMA((2,)),
        ],
    )
    tok_tiles = jnp.concatenate([tok_sorted.reshape(N_ROW_TILES, 1, ROW_TILE),
                                 jnp.zeros((SUPER_TILES, 1, ROW_TILE), jnp.int32)], axis=0)
    return pl.pallas_call(
        _moe_body,
        grid_spec=grid_spec,
        out_shape=jax.ShapeDtypeStruct((N_ROWS, D_MODEL), F32),
        compiler_params=_params(("arbitrary", "arbitrary"), vmem=EXPERT_VMEM_LIMIT),
        name="expert_mlp",
    )(exp_of, row_of, nsub_of, nzero_of, tok_tiles, h_packed, w_gate_up, w_gate_up, w_down,
      b_gate_up.reshape(N_EXPERTS, 1, 2 * D_FF), b_gate_up.reshape(N_EXPERTS, 1, 2 * D_FF),
      b_down.reshape(N_EXPERTS, 1, D_MODEL))


COMB_TB = 256


def _combine_start(y_ref, ybuf_ref, pos_ref, sem):
    def issue(t, _):
        for k in range(TOP_K):
            p = pos_ref[0, 0, t * TOP_K + k]
            pltpu.make_async_copy(y_ref.at[pl.ds(p, 1), :], ybuf_ref.at[k, pl.ds(t, 1), :],
                                  sem).start(priority=k % N_DMA_QUEUES)
        return 0

    lax.fori_loop(0, COMB_TB, issue, 0, unroll=4)


def _combine_body(n, pos_ref, pos_next_ref, y_ref, gate_ref, x1_ref, gt2_ref, g_ref, o_ref, ybuf_ref, sem_ref):
    i = pl.program_id(0)
    slot = i % 2

    @pl.when(i == 0)
    def _():
        _combine_start(y_ref, ybuf_ref.at[0], pos_ref, sem_ref.at[0])

    @pl.when(i + 1 < n)
    def _():
        _combine_start(y_ref, ybuf_ref.at[1 - slot], pos_next_ref, sem_ref.at[1 - slot])

    for k in range(TOP_K):
        pltpu.make_async_copy(y_ref.at[pl.ds(0, COMB_TB), :], ybuf_ref.at[slot, k], sem_ref.at[slot]).wait()
    gates = gate_ref[...]
    ffn = gates[:, 0:1] * ybuf_ref[slot, 0]
    for k in range(1, TOP_K):
        ffn = ffn + gates[:, k:k + 1] * ybuf_ref[slot, k]
    o_ref[...] = x1_ref[...] + gt2_ref[...] * (ffn * _rms_scale(ffn) * g_ref[...])


def combine_residual(y_sorted, pos, gates, x1, gt2, g_post_ffn, row_offset, n_rows, group_of_block):
    tb = COMB_TB
    nblk = n_rows // tb
    off = row_offset // tb
    pos3 = pos.reshape(N_TOK // tb, 1, tb * TOP_K)
    smem_blk = lambda f: pl.BlockSpec((1, 1, tb * TOP_K), f, memory_space=pltpu.SMEM)
    return pl.pallas_call(
        functools.partial(_combine_body, nblk),
        grid=(nblk,),
        in_specs=[
            smem_blk(lambda i: (off + i, 0, 0)),
            smem_blk(lambda i: (off + jnp.minimum(i + 1, nblk - 1), 0, 0)),
            pl.BlockSpec(memory_space=pl.ANY),
            pl.BlockSpec((tb, TOP_K), lambda i: (off + i, 0)),
            pl.BlockSpec((tb, D_MODEL), lambda i: (off + i, 0)),
            pl.BlockSpec((None, 1, D_MODEL), lambda i: (group_of_block(i), 0, 0)),
            pl.BlockSpec((1, D_MODEL), lambda i: (0, 0)),
        ],
        out_specs=pl.BlockSpec((tb, D_MODEL), lambda i: (i, 0)),
        out_shape=jax.ShapeDtypeStruct((n_rows, D_MODEL), F32),
        scratch_shapes=[pltpu.VMEM((2, TOP_K, tb, D_MODEL), F32), pltpu.SemaphoreType.DMA((2,))],
        compiler_params=_params(("arbitrary",)),
        name="combine_residual",
    )(pos3, pos3, y_sorted, gates, x1, gt2, g_post_ffn.reshape(1, D_MODEL))


INV_CHUNK = 4096


def _row_tokens_body(pos_ref, zeros_ref, o_ref, sem):
    i = pl.program_id(0)

    @pl.when(i == 0)
    def _():
        cp = pltpu.make_async_copy(zeros_ref, o_ref, sem)
        cp.start()
        cp.wait()

    base = i * INV_CHUNK

    def put(r, _):
        o_ref[pos_ref[r]] = lax.shift_right_logical(base + r, 2)
        return 0

    lax.fori_loop(0, INV_CHUNK, put, 0, unroll=8)


def row_tokens(pos):
    return pl.pallas_call(
        _row_tokens_body,
        grid=(N_ASSIGN // INV_CHUNK,),
        in_specs=[
            pl.BlockSpec((INV_CHUNK,), lambda i: (i,), memory_space=pltpu.SMEM),
            pl.BlockSpec(memory_space=pl.ANY),
        ],
        out_specs=pl.BlockSpec(memory_space=pltpu.SMEM),
        out_shape=jax.ShapeDtypeStruct((N_ROWS,), jnp.int32),
        scratch_shapes=[pltpu.SemaphoreType.DMA(())],
        compiler_params=_params(("arbitrary",)),
        name="row_tokens",
    )(pos.reshape(N_ASSIGN), jnp.zeros((N_ROWS,), jnp.int32))


def _routing_tables(e_idx, rank, counts_f):
    counts = counts_f.reshape(N_EXPERTS).astype(jnp.int32)
    n_tiles = (counts + ROW_TILE - 1) // ROW_TILE
    padded = n_tiles * ROW_TILE
    pad_end = jnp.cumsum(padded)
    pad_start = pad_end - padded
    pos = (pad_start[e_idx] + rank).astype(jnp.int32)
    tok_sorted = row_tokens(pos)
    n_pass = (n_tiles + SUPER_TILES - 1) // SUPER_TILES
    pass_end = jnp.cumsum(n_pass)
    total = pass_end[-1]
    s = jnp.arange(N_SUPER, dtype=jnp.int32)
    s_eff = jnp.minimum(s, total - 1)
    e_of = jnp.minimum(jnp.searchsorted(pass_end, s_eff, side="right"), N_EXPERTS - 1).astype(jnp.int32)
    local = s_eff - (pass_end[e_of] - n_pass[e_of])
    row_of = pad_start[e_of] + local * SUPER_ROWS
    nsub = jnp.minimum(SUPER_TILES, n_tiles[e_of] - local * SUPER_TILES)
    nsub = jnp.where(s < total, nsub, 0).astype(jnp.int32)
    zero_row = pad_end[-1] + (s - total) * SUPER_ROWS
    nzero = jnp.clip((N_ROWS - zero_row) // ROW_TILE, 0, SUPER_TILES)
    nzero = jnp.where(s >= total, nzero, 0).astype(jnp.int32)
    row_of = jnp.where(s < total, row_of, jnp.minimum(zero_row, N_ROWS - ROW_TILE)).astype(jnp.int32)
    return pos.astype(jnp.int32), tok_sorted, (e_of, row_of, nsub, nzero)


def kernel(x_prompt, x_sample, cache_k, cache_v, state_rnn_fwd, state_rnn_bwd, c, c_ctx, w_mod, b_mod, g_pre_mix, w_in, g_q_norm, g_k_norm, conv_w, conv_b, rg_w_a, rg_b_a, rg_w_x, rg_b_x, rg_lambda, w_o_attn, w_o_rnn, w_out, g_post_mix, g_pre_ffn, w_router, b_router, w_gate_up, b_gate_up, w_down, b_down, g_post_ffn):
    l = 0
    x_ctx = x_prompt.reshape(N_CTX, D_MODEL)
    x_lat = x_sample.reshape(N_LAT, D_MODEL)

    cond8 = jnp.concatenate([c_ctx[None, :], c, jnp.zeros((8 - 1 - N_LAT_SEQ, D_MODEL), F32)], axis=0)
    mod = modulation(cond8, w_mod[l], b_mod[l])[:1 + N_LAT_SEQ].reshape(1 + N_LAT_SEQ, 6, 1, D_MODEL)
    sh1, sc1, gt1, sh2, sc2, gt2 = [mod[:, i] for i in range(6)]

    ctx_group = lambda i: 0
    lat_group_1024 = lambda i: 1 + i
    h_ctx = prenorm_modulate(x_ctx, g_pre_mix[l], sh1, sc1, ctx_group, 1024)
    h_lat = prenorm_modulate(x_lat, g_pre_mix[l], sh1, sc1, lat_group_1024, 1024)
    z_ctx = in_projection(h_ctx, w_in[l])
    z_lat = in_projection(h_lat, w_in[l])

    attn_ctx, k_new, v_new = attention_ctx(z_ctx, g_q_norm[l], g_k_norm[l])
    attn_lat = attention_lat(z_lat, cache_k[:, l].reshape(N_LAT_SEQ, PAST_LEN, KV_COLS),
                             cache_v[:, l].reshape(N_LAT_SEQ, PAST_LEN, KV_COLS),
                             _rope_tables(), g_q_norm[l], g_k_norm[l])

    def per_block(w):
        return w.reshape(2, RNN_BLOCKS, 1, RNN_BLOCK_DIM)

    w_gates = jnp.concatenate([rg_w_a[l, 0], rg_w_x[l, 0], rg_w_a[l, 1], rg_w_x[l, 1]], axis=-1).astype(BF16)
    ba, bx = per_block(rg_b_a[l]), per_block(rg_b_x[l])
    b_gates = jnp.concatenate([ba[0], bx[0], ba[1], bx[1]], axis=-1)
    zeros_state = jnp.zeros((N_CTX_SEQ, 1, D_MODEL), F32)
    rnn_ctx, hf_ctx, hb_ctx = rglru_mixer(z_ctx, CTX_LEN, conv_w[l], conv_b[l], w_gates, b_gates,
                                          rg_lambda[l], zeros_state, zeros_state)
    rnn_lat, _, _ = rglru_mixer(z_lat, LAT_LEN, conv_w[l], conv_b[l], w_gates, b_gates, rg_lambda[l],
                                state_rnn_fwd[:, l].reshape(N_LAT_SEQ, 1, D_MODEL),
                                state_rnn_bwd[:, l].reshape(N_LAT_SEQ, 1, D_MODEL))

    merged_ctx = gated_merge(attn_ctx, rnn_ctx, z_ctx, w_o_attn[l], w_o_rnn[l])
    merged_lat = gated_merge(attn_lat, rnn_lat, z_lat, w_o_attn[l], w_o_rnn[l])

    x1, h2, e_idx, gates, rank, counts = post_mix_router(
        merged_ctx, merged_lat, x_ctx, x_lat, w_out[l].astype(BF16), g_post_mix[l], gt1, g_pre_ffn[l],
        sh2, sc2, w_router[l], b_router[l])

    pos, tok_sorted, sched = _routing_tables(e_idx, rank, counts)
    y_sorted = expert_mlp(h2, tok_sorted, sched, w_gate_up[l], b_gate_up[l], w_down[l], b_down[l])

    y_ctx = combine_residual(y_sorted, pos, gates, x1, gt2, g_post_ffn[l], 0, N_CTX, ctx_group)
    y_lat = combine_residual(y_sorted, pos, gates, x1, gt2, g_post_ffn[l], N_CTX, N_LAT,
                             lambda i: 1 + i // (LAT_LEN // COMB_TB))

    return (y_ctx.reshape(N_CTX_SEQ, CTX_LEN, D_MODEL),
            y_lat.reshape(N_LAT_SEQ, LAT_LEN, D_MODEL),
            k_new.reshape(N_CTX_SEQ, 1, CTX_LEN, N_KV_HEADS, HEAD_DIM),
            v_new.reshape(N_CTX_SEQ, 1, CTX_LEN, N_KV_HEADS, HEAD_DIM),
            hf_ctx,
            hb_ctx)
```

```python
import functools

import jax
import jax.numpy as jnp
import numpy as np
from jax import lax
from jax.experimental import pallas as pl
from jax.experimental.pallas import tpu as pltpu

D_MODEL = 2048
N_CTX_SEQ = 32
CTX_LEN = 256
N_LAT_SEQ = 2
LAT_LEN = 1024
PAST_LEN = 512
N_CTX = N_CTX_SEQ * CTX_LEN
N_LAT = N_LAT_SEQ * LAT_LEN
N_TOK = N_CTX + N_LAT
GRID_W = 64
N_HEADS = 16
N_KV_HEADS = 4
HEAD_DIM = 128
KV_GROUP = N_HEADS // N_KV_HEADS
ROPE_THETA = 10000.0
RNN_BLOCKS = 16
RNN_BLOCK_DIM = 128
RG_C = 8.0
N_EXPERTS = 32
TOP_K = 4
D_FF = 2048
SWIGLU_LIMIT = 7.0
SWIGLU_ALPHA = 1.702
EPS = 1e-6
Q_COLS = N_HEADS * HEAD_DIM
KV_COLS = N_KV_HEADS * HEAD_DIM
IN_COLS = Q_COLS + 2 * KV_COLS + 4 * D_MODEL
COL_K = Q_COLS
COL_XR = Q_COLS + 2 * KV_COLS
COL_YR = COL_XR + D_MODEL
COL_GA = COL_YR + D_MODEL
COL_GR = COL_GA + D_MODEL

V7X_VMEM_BYTES = 64 * 1024 * 1024
VMEM_LIMIT = 56 * 1024 * 1024
EXPERT_VMEM_LIMIT = 60 * 1024 * 1024

ROW_TILE = 256
SUPER_TILES = 8
SUPER_ROWS = ROW_TILE * SUPER_TILES
N_ASSIGN = N_TOK * TOP_K
N_ROWS = N_ASSIGN + N_EXPERTS * ROW_TILE
N_ROW_TILES = N_ROWS // ROW_TILE
N_SUPER = N_ROW_TILES // SUPER_TILES + N_EXPERTS
FF_CHUNK = 512
N_FF_CHUNKS = D_FF // FF_CHUNK

BF16 = jnp.bfloat16
F32 = jnp.float32


def _params(semantics, vmem=VMEM_LIMIT):
    return pltpu.CompilerParams(dimension_semantics=semantics, vmem_limit_bytes=vmem)


def _rms_scale(x):
    return lax.rsqrt(jnp.mean(x * x, axis=-1, keepdims=True) + EPS)


def _sigmoid(x):
    return 1.0 / (1.0 + jnp.exp(-x))


def _mod_body(c_ref, w_ref, b_ref, o_ref):
    c = c_ref[...]
    a = (c * _sigmoid(c)).astype(BF16)
    o_ref[...] = jnp.dot(a, w_ref[...].astype(BF16), preferred_element_type=F32) + b_ref[...]


def modulation(cond8, w_mod, b_mod):
    tn = 1024
    n = w_mod.shape[1]
    return pl.pallas_call(
        _mod_body,
        grid=(n // tn,),
        in_specs=[
            pl.BlockSpec((8, D_MODEL), lambda j: (0, 0)),
            pl.BlockSpec((D_MODEL, tn), lambda j: (0, j)),
            pl.BlockSpec((1, tn), lambda j: (0, j)),
        ],
        out_specs=pl.BlockSpec((8, tn), lambda j: (0, j)),
        out_shape=jax.ShapeDtypeStruct((8, n), F32),
        compiler_params=_params(("arbitrary",)),
        name="modulation",
    )(cond8, w_mod, b_mod.reshape(1, n))


def _prenorm_body(x_ref, g_ref, sh_ref, sc_ref, o_ref):
    x = x_ref[...]
    y = x * _rms_scale(x) * g_ref[...]
    o_ref[...] = (y * (1.0 + sc_ref[...]) + sh_ref[...]).astype(o_ref.dtype)


def prenorm_modulate(x, g, shift, scale, group_of_block, tm):
    m = x.shape[0]
    gmap = lambda i: (group_of_block(i), 0, 0)
    return pl.pallas_call(
        _prenorm_body,
        grid=(m // tm,),
        in_specs=[
            pl.BlockSpec((tm, D_MODEL), lambda i: (i, 0)),
            pl.BlockSpec((1, D_MODEL), lambda i: (0, 0)),
            pl.BlockSpec((None, 1, D_MODEL), gmap),
            pl.BlockSpec((None, 1, D_MODEL), gmap),
        ],
        out_specs=pl.BlockSpec((tm, D_MODEL), lambda i: (i, 0)),
        out_shape=jax.ShapeDtypeStruct((m, D_MODEL), BF16),
        compiler_params=_params(("arbitrary",)),
        name="prenorm_modulate",
    )(x, g.reshape(1, D_MODEL), shift, scale)


def _inproj_body(h_ref, w_ref, o_ref, wbf_ref):
    @pl.when(pl.program_id(1) == 0)
    def _():
        wbf_ref[...] = w_ref[...].astype(BF16)

    o_ref[...] = jnp.dot(h_ref[...], wbf_ref[...], preferred_element_type=F32)


def in_projection(h, w_in):
    m = h.shape[0]
    tm, tn = 1024, 1024
    return pl.pallas_call(
        _inproj_body,
        grid=(IN_COLS // tn, m // tm),
        in_specs=[
            pl.BlockSpec((tm, D_MODEL), lambda j, i: (i, 0)),
            pl.BlockSpec((D_MODEL, tn), lambda j, i: (0, j)),
        ],
        out_specs=pl.BlockSpec((tm, tn), lambda j, i: (i, j)),
        out_shape=jax.ShapeDtypeStruct((m, IN_COLS), F32),
        scratch_shapes=[pltpu.VMEM((D_MODEL, tn), BF16)],
        compiler_params=_params(("arbitrary", "arbitrary")),
        name="in_projection",
    )(h, w_in)


def _rope(x, cos, sin_lo, sin_hi):
    return x * cos + pltpu.roll(x, 96, 1) * sin_lo + pltpu.roll(x, 32, 1) * sin_hi


def _head_norm(x, g):
    return x * _rms_scale(x) * g


def _softmax_pv(score_blocks, value_blocks):
    m = None
    for s in score_blocks:
        mi = jnp.max(s, axis=-1, keepdims=True)
        m = mi if m is None else jnp.maximum(m, mi)
    ps = [jnp.exp(s - m) for s in score_blocks]
    denom = None
    for p in ps:
        li = jnp.sum(p, axis=-1, keepdims=True)
        denom = li if denom is None else denom + li
    inv = 1.0 / denom
    out = None
    for p, v in zip(ps, value_blocks):
        o = jnp.dot((p * inv).astype(BF16), v, preferred_element_type=F32)
        out = o if out is None else out + o
    return out


def _attn_ctx_body(q_ref, kv_ref, gq_ref, gk_ref, o_ref, ko_ref, vo_ref):
    tq = q_ref.shape[0]
    scale = HEAD_DIM ** -0.5
    gq = gq_ref[...]
    gk = gk_ref[...]
    for g in range(N_KV_HEADS):
        kcols = slice(g * HEAD_DIM, (g + 1) * HEAD_DIM)
        kn = _head_norm(kv_ref[:, kcols], gk)
        v = kv_ref[:, KV_COLS + g * HEAD_DIM:KV_COLS + (g + 1) * HEAD_DIM]
        ko_ref[pl.ds(g, tq, stride=N_KV_HEADS), :] = kn
        vo_ref[pl.ds(g, tq, stride=N_KV_HEADS), :] = v
        qs = []
        for hh in range(KV_GROUP):
            h = g * KV_GROUP + hh
            qs.append(_head_norm(q_ref[:, h * HEAD_DIM:(h + 1) * HEAD_DIM], gq).astype(BF16))
        q4 = jnp.concatenate(qs, axis=0)
        s = lax.dot_general(q4, kn.astype(BF16), (((1,), (1,)), ((), ())),
                            preferred_element_type=F32) * scale
        o = _softmax_pv([s], [v.astype(BF16)])
        for hh in range(KV_GROUP):
            h = g * KV_GROUP + hh
            o_ref[:, h * HEAD_DIM:(h + 1) * HEAD_DIM] = o[hh * tq:(hh + 1) * tq].astype(o_ref.dtype)


def attention_ctx(z, g_q, g_k):
    nb = N_CTX_SEQ
    t = CTX_LEN
    return pl.pallas_call(
        _attn_ctx_body,
        grid=(nb,),
        in_specs=[
            pl.BlockSpec((t, Q_COLS), lambda b: (b, 0)),
            pl.BlockSpec((t, 2 * KV_COLS), lambda b: (b, COL_K // (2 * KV_COLS))),
            pl.BlockSpec((1, HEAD_DIM), lambda b: (0, 0)),
            pl.BlockSpec((1, HEAD_DIM), lambda b: (0, 0)),
        ],
        out_specs=[
            pl.BlockSpec((t, Q_COLS), lambda b: (b, 0)),
            pl.BlockSpec((t * N_KV_HEADS, HEAD_DIM), lambda b: (b, 0)),
            pl.BlockSpec((t * N_KV_HEADS, HEAD_DIM), lambda b: (b, 0)),
        ],
        out_shape=[
            jax.ShapeDtypeStruct((N_CTX, Q_COLS), BF16),
            jax.ShapeDtypeStruct((N_CTX * N_KV_HEADS, HEAD_DIM), F32),
            jax.ShapeDtypeStruct((N_CTX * N_KV_HEADS, HEAD_DIM), F32),
        ],
        compiler_params=_params(("arbitrary",)),
        name="attention_ctx",
    )(z, z, g_q.reshape(1, HEAD_DIM), g_k.reshape(1, HEAD_DIM))


def _attn_lat_body(q_ref, kv_ref, ck_ref, cv_ref, cos_ref, slo_ref, shi_ref, gq_ref, gk_ref,
                   o_ref, kr_ref):
    tq = q_ref.shape[0]
    qb = pl.program_id(1)
    scale = HEAD_DIM ** -0.5
    gq = gq_ref[...]

    @pl.when(qb == 0)
    def _():
        gk = gk_ref[...]
        for g in range(N_KV_HEADS):
            kcols = slice(g * HEAD_DIM, (g + 1) * HEAD_DIM)
            kn = _head_norm(kv_ref[:, kcols], gk)
            kr_ref[:, kcols] = _rope(kn, cos_ref[...], slo_ref[...], shi_ref[...]).astype(BF16)

    row0 = pl.multiple_of(qb * tq, tq)
    cos = cos_ref[pl.ds(row0, tq), :]
    slo = slo_ref[pl.ds(row0, tq), :]
    shi = shi_ref[pl.ds(row0, tq), :]
    for g in range(N_KV_HEADS):
        kcols = slice(g * HEAD_DIM, (g + 1) * HEAD_DIM)
        qs = []
        for hh in range(KV_GROUP):
            h = g * KV_GROUP + hh
            qn = _head_norm(q_ref[:, h * HEAD_DIM:(h + 1) * HEAD_DIM], gq)
            qs.append(_rope(qn, cos, slo, shi).astype(BF16))
        q4 = jnp.concatenate(qs, axis=0)
        dn = (((1,), (1,)), ((), ()))
        s_past = lax.dot_general(q4, ck_ref[:, kcols].astype(BF16), dn,
                                 preferred_element_type=F32) * scale
        s_new = lax.dot_general(q4, kr_ref[:, kcols], dn, preferred_element_type=F32) * scale
        v_past = cv_ref[:, kcols].astype(BF16)
        v_new = kv_ref[:, KV_COLS + g * HEAD_DIM:KV_COLS + (g + 1) * HEAD_DIM].astype(BF16)
        o = _softmax_pv([s_past, s_new], [v_past, v_new])
        for hh in range(KV_GROUP):
            h = g * KV_GROUP + hh
            o_ref[:, h * HEAD_DIM:(h + 1) * HEAD_DIM] = o[hh * tq:(hh + 1) * tq].astype(o_ref.dtype)


def attention_lat(z, cache_k, cache_v, rope_tabs, g_q, g_k):
    tq = 256
    nq = LAT_LEN // tq
    cos, slo, shi = rope_tabs
    tab = pl.BlockSpec((LAT_LEN, HEAD_DIM), lambda b, q: (0, 0))
    return pl.pallas_call(
        _attn_lat_body,
        grid=(N_LAT_SEQ, nq),
        in_specs=[
            pl.BlockSpec((tq, Q_COLS), lambda b, q: (b * nq + q, 0)),
            pl.BlockSpec((LAT_LEN, 2 * KV_COLS), lambda b, q: (b, COL_K // (2 * KV_COLS))),
            pl.BlockSpec((None, PAST_LEN, KV_COLS), lambda b, q: (b, 0, 0)),
            pl.BlockSpec((None, PAST_LEN, KV_COLS), lambda b, q: (b, 0, 0)),
            tab, tab, tab,
            pl.BlockSpec((1, HEAD_DIM), lambda b, q: (0, 0)),
            pl.BlockSpec((1, HEAD_DIM), lambda b, q: (0, 0)),
        ],
        out_specs=pl.BlockSpec((tq, Q_COLS), lambda b, q: (b * nq + q, 0)),
        out_shape=jax.ShapeDtypeStruct((N_LAT, Q_COLS), BF16),
        scratch_shapes=[pltpu.VMEM((LAT_LEN, KV_COLS), BF16)],
        compiler_params=_params(("arbitrary", "arbitrary")),
        name="attention_lat",
    )(z, z, cache_k, cache_v, cos, slo, shi, g_q.reshape(1, HEAD_DIM), g_k.reshape(1, HEAD_DIM))


def _rope_tables():
    t = np.arange(LAT_LEN)
    row = jnp.asarray(t // GRID_W, F32)
    col = jnp.asarray(t % GRID_W, F32)
    nf = HEAD_DIM // 4
    inv_freq = ROPE_THETA ** (-jnp.arange(nf, dtype=F32) / nf)
    ang_row = row[:, None] * inv_freq[None, :]
    ang_col = col[:, None] * inv_freq[None, :]
    ang = jnp.concatenate([ang_row, ang_row, ang_col, ang_col], axis=1)
    cos = jnp.cos(ang)
    sin = jnp.sin(ang)
    first = jnp.asarray((np.arange(HEAD_DIM) % (2 * nf)) < nf)[None, :]
    return cos, jnp.where(first, -sin, 0.0), jnp.where(first, 0.0, sin)


RNN_ROWS = 2048
RNN_COLS = 512
RNN_SUB = RNN_COLS // RNN_BLOCK_DIM


def _gelu_tanh(y):
    return 0.5 * y * (1.0 + jnp.tanh(0.7978845608028654 * (y + 0.044715 * (y * y * y))))


def _rglru_body(seq_len, xr_ref, yr_ref, cw_ref, cb_ref, wg_ref, bg_ref, lam_ref, h0f_ref, h0b_ref,
                o_ref, hf_ref, hb_ref, xs_ref, af_ref, bf_ref, ab_ref, bb_ref):
    n_seq = RNN_ROWS // seq_len
    for n in range(RNN_SUB):
        cols = slice(n * RNN_BLOCK_DIM, (n + 1) * RNN_BLOCK_DIM)
        for s in range(n_seq):
            xs_ref[n, pl.ds(s, seq_len, stride=n_seq), :] = xr_ref[s * seq_len:(s + 1) * seq_len, cols]

    row = lax.broadcasted_iota(jnp.int32, (RNN_ROWS, 1), 0)
    lam = lam_ref[...]
    softplus_neg = jnp.maximum(-lam, 0.0) + jnp.log(1.0 + jnp.exp(-jnp.abs(lam)))
    rate = softplus_neg * (-RG_C * 1.4426950408889634)
    for n in range(RNN_SUB):
        cols = slice(n * RNN_BLOCK_DIM, (n + 1) * RNN_BLOCK_DIM)
        x = xs_ref[n]
        x_m1 = jnp.where(row >= n_seq, pltpu.roll(x, n_seq, 0), 0.0)
        x_p1 = jnp.where(row < RNN_ROWS - n_seq, pltpu.roll(x, RNN_ROWS - n_seq, 0), 0.0)
        x_p2 = jnp.where(row < RNN_ROWS - 2 * n_seq, pltpu.roll(x, RNN_ROWS - 2 * n_seq, 0), 0.0)
        xn = (cb_ref[:, cols] + x_m1 * cw_ref[0:1, cols] + x * cw_ref[1:2, cols]
              + x_p1 * cw_ref[2:3, cols] + x_p2 * cw_ref[3:4, cols])
        pre = jnp.dot(xn.astype(BF16), wg_ref[n], preferred_element_type=F32) + bg_ref[n]
        for d, (a_ref, b_ref) in enumerate(((af_ref, bf_ref), (ab_ref, bb_ref))):
            r = 0.5 * jnp.tanh(0.5 * pre[:, (2 * d) * RNN_BLOCK_DIM:(2 * d + 1) * RNN_BLOCK_DIM]) + 0.5
            gate_in = 0.5 * jnp.tanh(
                0.5 * pre[:, (2 * d + 1) * RNN_BLOCK_DIM:(2 * d + 2) * RNN_BLOCK_DIM]) + 0.5
            a = jnp.exp2(r * rate[d:d + 1, cols])
            v = 1.0 - a * a
            a_ref[n] = a
            b_ref[n] = (v * lax.rsqrt(jnp.maximum(v, 1e-30))) * (gate_in * xn)

    def step(t, carry):
        rows_f = pl.ds(pl.multiple_of(t * n_seq, n_seq), n_seq)
        rows_b = pl.ds(pl.multiple_of((seq_len - 1 - t) * n_seq, n_seq), n_seq)
        out = []
        for n in range(RNN_SUB):
            hf = af_ref[n, rows_f, :] * carry[2 * n] + bf_ref[n, rows_f, :]
            hb = ab_ref[n, rows_b, :] * carry[2 * n + 1] + bb_ref[n, rows_b, :]
            bf_ref[n, rows_f, :] = hf
            bb_ref[n, rows_b, :] = hb
            out += [hf, hb]
        return tuple(out)

    init = []
    for n in range(RNN_SUB):
        cols = slice(n * RNN_BLOCK_DIM, (n + 1) * RNN_BLOCK_DIM)
        init += [h0f_ref[:, 0, cols], h0b_ref[:, 0, cols]]
    last = lax.fori_loop(0, seq_len, step, tuple(init), unroll=8)
    for n in range(RNN_SUB):
        cols = slice(n * RNN_BLOCK_DIM, (n + 1) * RNN_BLOCK_DIM)
        hf_ref[:, 0, cols] = last[2 * n]
        hb_ref[:, 0, cols] = last[2 * n + 1]
        bf_ref[n] = bf_ref[n] + bb_ref[n]
        for s in range(n_seq):
            rows = slice(s * seq_len, (s + 1) * seq_len)
            h_sum = bf_ref[n, pl.ds(s, seq_len, stride=n_seq), :]
            o_ref[rows, cols] = (h_sum * _gelu_tanh(yr_ref[rows, cols])).astype(o_ref.dtype)


def rglru_mixer(z, seq_len, conv_w, conv_b, w_gates, b_gates, lam, h0_f, h0_b):
    m = z.shape[0]
    n_seq_total = m // seq_len
    n_seq = RNN_ROWS // seq_len
    cblk = lambda base: (lambda r, c: (r, base // RNN_COLS + c))
    state_spec = pl.BlockSpec((n_seq, 1, RNN_COLS), lambda r, c: (r, 0, c))
    return pl.pallas_call(
        functools.partial(_rglru_body, seq_len),
        grid=(m // RNN_ROWS, D_MODEL // RNN_COLS),
        in_specs=[
            pl.BlockSpec((RNN_ROWS, RNN_COLS), cblk(COL_XR)),
            pl.BlockSpec((RNN_ROWS, RNN_COLS), cblk(COL_YR)),
            pl.BlockSpec((4, RNN_COLS), lambda r, c: (0, c)),
            pl.BlockSpec((1, RNN_COLS), lambda r, c: (0, c)),
            pl.BlockSpec((RNN_SUB, RNN_BLOCK_DIM, 4 * RNN_BLOCK_DIM), lambda r, c: (c, 0, 0)),
            pl.BlockSpec((RNN_SUB, 1, 4 * RNN_BLOCK_DIM), lambda r, c: (c, 0, 0)),
            pl.BlockSpec((2, RNN_COLS), lambda r, c: (0, c)),
            state_spec, state_spec,
        ],
        out_specs=[
            pl.BlockSpec((RNN_ROWS, RNN_COLS), lambda r, c: (r, c)),
            state_spec, state_spec,
        ],
        out_shape=[
            jax.ShapeDtypeStruct((m, D_MODEL), BF16),
            jax.ShapeDtypeStruct((n_seq_total, 1, D_MODEL), F32),
            jax.ShapeDtypeStruct((n_seq_total, 1, D_MODEL), F32),
        ],
        scratch_shapes=[pltpu.VMEM((RNN_SUB, RNN_ROWS, RNN_BLOCK_DIM), F32) for _ in range(5)],
        compiler_params=_params(("arbitrary", "arbitrary")),
        name="rglru_mixer_t%d" % seq_len,
    )(z, z, conv_w, conv_b.reshape(1, D_MODEL), w_gates, b_gates, lam, h0_f, h0_b)


def _merge_body(a_ref, r_ref, wa_ref, wr_ref, ga_ref, gr_ref, o_ref, wa_bf, wr_bf):
    @pl.when(pl.program_id(1) == 0)
    def _():
        wa_bf[...] = wa_ref[...].astype(BF16)
        wr_bf[...] = wr_ref[...].astype(BF16)

    pa = jnp.dot(a_ref[...], wa_bf[...], preferred_element_type=F32)
    pr = jnp.dot(r_ref[...], wr_bf[...], preferred_element_type=F32)
    o_ref[...] = (_sigmoid(ga_ref[...]) * pa + _sigmoid(gr_ref[...]) * pr).astype(o_ref.dtype)


def gated_merge(attn, rnn, z, w_o_attn, w_o_rnn):
    m = attn.shape[0]
    tm, tn = 1024, 512
    return pl.pallas_call(
        _merge_body,
        grid=(D_MODEL // tn, m // tm),
        in_specs=[
            pl.BlockSpec((tm, Q_COLS), lambda j, i: (i, 0)),
            pl.BlockSpec((tm, D_MODEL), lambda j, i: (i, 0)),
            pl.BlockSpec((Q_COLS, tn), lambda j, i: (0, j)),
            pl.BlockSpec((D_MODEL, tn), lambda j, i: (0, j)),
            pl.BlockSpec((tm, tn), lambda j, i: (i, COL_GA // tn + j)),
            pl.BlockSpec((tm, tn), lambda j, i: (i, COL_GR // tn + j)),
        ],
        out_specs=pl.BlockSpec((tm, tn), lambda j, i: (i, j)),
        out_shape=jax.ShapeDtypeStruct((m, D_MODEL), BF16),
        scratch_shapes=[pltpu.VMEM((Q_COLS, tn), BF16), pltpu.VMEM((D_MODEL, tn), BF16)],
        compiler_params=_params(("arbitrary", "arbitrary")),
        name="gated_merge",
    )(attn, rnn, w_o_attn, w_o_rnn, z, z)


POST_TM = 512
POST_SPLIT = 1
HALF_D = D_MODEL // 2
WORD_ROWS = HALF_D // 128
SUBLANES = 8
POST_CTX_BLOCKS = N_CTX // POST_TM
LAT_BLOCKS_PER_SEQ = LAT_LEN // POST_TM


def _post_group(i):
    return jnp.where(i < POST_CTX_BLOCKS, 0, 1 + (i - POST_CTX_BLOCKS) // LAT_BLOCKS_PER_SEQ)


def _postmix_body(mc_ref, ml_ref, xc_ref, xl_ref, wo_ref, gpm_ref, gt1_ref, gpf_ref, sh2_ref, sc2_ref,
                  wr_ref, br_ref, x1_ref, h2_ref, e_ref, gate_ref, rank_ref, cnt_ref, carry_ref):
    i = pl.program_id(0)
    tm = POST_TM

    @pl.when(i == 0)
    def _():
        carry_ref[...] = jnp.zeros_like(carry_ref)

    is_ctx = i < POST_CTX_BLOCKS
    th = tm // POST_SPLIT
    r_io = lax.broadcasted_iota(jnp.int32, (th, th), 0)
    c_io = lax.broadcasted_iota(jnp.int32, (th, th), 1)
    lower = jnp.where(c_io < r_io, 1.0, 0.0).astype(BF16)
    lane = lax.broadcasted_iota(jnp.int32, (th, N_EXPERTS), 1)
    lane_k = lax.broadcasted_iota(jnp.int32, (th, TOP_K), 1)
    wr = wr_ref[...].astype(BF16)
    carry = carry_ref[...]
    for part in range(POST_SPLIT):
        rows = slice(part * th, (part + 1) * th)
        merged = jnp.where(is_ctx, mc_ref[rows, :], ml_ref[rows, :])
        x = jnp.where(is_ctx, xc_ref[rows, :], xl_ref[rows, :])
        o = jnp.dot(merged, wo_ref[...], preferred_element_type=F32)
        x1 = x + gt1_ref[...] * (o * _rms_scale(o) * gpm_ref[...])
        x1_ref[rows, :] = x1
        h2 = (x1 * _rms_scale(x1) * gpf_ref[...]) * (1.0 + sc2_ref[...]) + sh2_ref[...]
        h2_bf = h2.astype(BF16)
        bits = lax.bitcast_convert_type(h2_bf.astype(F32), jnp.uint32)
        words = (lax.shift_right_logical(bits[:, :HALF_D], jnp.uint32(16))
                 | (bits[:, HALF_D:] & jnp.uint32(0xFFFF0000)))
        for c in range(WORD_ROWS):
            h2_ref[pl.ds(part * th * WORD_ROWS + c, th, stride=WORD_ROWS), :] = words[:, c * 128:(c + 1) * 128]

        logits = jnp.dot(h2_bf, wr, preferred_element_type=F32) + br_ref[...]
        work = logits
        chosen = jnp.zeros((th, N_EXPERTS), F32)
        sels, vals, idxs = [], [], []
        for _ in range(TOP_K):
            mx = jnp.max(work, axis=-1, keepdims=True)
            idx = jnp.min(jnp.where(work == mx, lane, N_EXPERTS), axis=-1, keepdims=True)
            sel = lane == idx
            work = jnp.where(sel, -jnp.inf, work)
            chosen = jnp.where(sel, 1.0, chosen)
            sels.append(sel)
            vals.append(mx)
            idxs.append(idx)
        exps = [jnp.exp(v - vals[0]) for v in vals]
        inv = 1.0 / (exps[0] + exps[1] + exps[2] + exps[3])

        before = jnp.dot(lower, chosen.astype(BF16), preferred_element_type=F32) + carry
        carry = carry + jnp.sum(chosen, axis=0, keepdims=True)

        e_out = jnp.zeros((th, TOP_K), jnp.int32)
        g_out = jnp.zeros((th, TOP_K), F32)
        r_out = jnp.zeros((th, TOP_K), jnp.int32)
        for k in range(TOP_K):
            rk = jnp.sum(jnp.where(sels[k], before, 0.0), axis=-1, keepdims=True).astype(jnp.int32)
            e_out = jnp.where(lane_k == k, idxs[k], e_out)
            g_out = jnp.where(lane_k == k, exps[k] * inv, g_out)
            r_out = jnp.where(lane_k == k, rk, r_out)
        e_ref[rows, :] = e_out
        gate_ref[rows, :] = g_out
        rank_ref[rows, :] = r_out
    carry_ref[...] = carry
    cnt_ref[...] = carry


def post_mix_router(merged_ctx, merged_lat, x_ctx, x_lat, w_out_bf, g_post_mix, gt1, g_pre_ffn, sh2, sc2,
                    w_router, b_router):
    tm = POST_TM
    ctx_map = lambda i: (jnp.minimum(i, POST_CTX_BLOCKS - 1), 0)
    lat_map = lambda i: (jnp.maximum(i - POST_CTX_BLOCKS, 0), 0)
    gmap = lambda i: (_post_group(i), 0, 0)
    row = lambda i: (i, 0)
    const = lambda i: (0, 0)
    vec = pl.BlockSpec((1, D_MODEL), const)
    gvec = pl.BlockSpec((None, 1, D_MODEL), gmap)
    return pl.pallas_call(
        _postmix_body,
        grid=(N_TOK // tm,),
        in_specs=[
            pl.BlockSpec((tm, D_MODEL), ctx_map),
            pl.BlockSpec((tm, D_MODEL), lat_map),
            pl.BlockSpec((tm, D_MODEL), ctx_map),
            pl.BlockSpec((tm, D_MODEL), lat_map),
            pl.BlockSpec((D_MODEL, D_MODEL), const),
            vec, gvec, vec, gvec, gvec,
            pl.BlockSpec((D_MODEL, N_EXPERTS), const),
            pl.BlockSpec((1, N_EXPERTS), const),
        ],
        out_specs=[
            pl.BlockSpec((tm, D_MODEL), row),
            pl.BlockSpec((tm * WORD_ROWS, 128), row),
            pl.BlockSpec((tm, TOP_K), row),
            pl.BlockSpec((tm, TOP_K), row),
            pl.BlockSpec((tm, TOP_K), row),
            pl.BlockSpec((1, N_EXPERTS), const),
        ],
        out_shape=[
            jax.ShapeDtypeStruct((N_TOK, D_MODEL), F32),
            jax.ShapeDtypeStruct((N_TOK * WORD_ROWS, 128), jnp.uint32),
            jax.ShapeDtypeStruct((N_TOK, TOP_K), jnp.int32),
            jax.ShapeDtypeStruct((N_TOK, TOP_K), F32),
            jax.ShapeDtypeStruct((N_TOK, TOP_K), jnp.int32),
            jax.ShapeDtypeStruct((1, N_EXPERTS), F32),
        ],
        scratch_shapes=[pltpu.VMEM((1, N_EXPERTS), F32)],
        compiler_params=_params(("arbitrary",)),
        name="post_mix_router",
    )(merged_ctx, merged_lat, x_ctx, x_lat, w_out_bf, g_post_mix.reshape(1, D_MODEL), gt1,
      g_pre_ffn.reshape(1, D_MODEL), sh2, sc2, w_router, b_router.reshape(1, N_EXPERTS))


GATHER_ROWS = ROW_TILE // (2 * N_FF_CHUNKS)
GATHER_PRIORITY = 1


def _unpack_tile(xbuf_ref, slot, i):
    base = pl.multiple_of(i * (ROW_TILE * WORD_ROWS), ROW_TILE * WORD_ROWS)
    lo, hi = [], []
    for c in range(WORD_ROWS):
        words = xbuf_ref[slot, pl.ds(base + c, ROW_TILE, stride=WORD_ROWS), :]
        lo.append(lax.bitcast_convert_type(lax.shift_left(words, jnp.uint32(16)), F32).astype(BF16))
        hi.append(lax.bitcast_convert_type(words & jnp.uint32(0xFFFF0000), F32).astype(BF16))
    return jnp.concatenate(lo + hi, axis=1)


def _for_tiles(n_tiles, body):
    def one(i, _):
        body(i)
        return 0

    lax.fori_loop(0, n_tiles, one, 0)


def _moe_body(exp_ref, row_ref, nsub_ref, nzero_ref, npass_ref, tok_ref, h_ref, wg_ref, wl_ref, wd_ref, bg_ref,
              bl_ref, bd_ref, y_ref, xbuf_ref, act_ref, wg_bf, wl_bf, wd_bf, stage_ref, idx_ref, pend_ref,
              xsem, isem, ysem):
    s = pl.program_id(0)
    j = pl.program_id(1)
    n_pass = npass_ref[0]
    n_sub = nsub_ref[s]
    row_start = row_ref[s]
    n_next = nsub_ref[jnp.minimum(s + 1, N_SUPER - 1)]

    def idx_copy(p):
        tile0 = pl.multiple_of(row_ref[p], ROW_TILE) // ROW_TILE
        return pltpu.make_async_copy(tok_ref.at[pl.ds(tile0, SUPER_TILES)], idx_ref.at[p % 2],
                                     isem.at[p % 2])

    def gather_row(slot, tile, col, r, priority=0):
        t = idx_ref[slot, tile, 0, col]
        src = h_ref.at[pl.ds(pl.multiple_of(t * WORD_ROWS, WORD_ROWS), WORD_ROWS), :]
        dst = xbuf_ref.at[slot, pl.ds(pl.multiple_of(r * WORD_ROWS, WORD_ROWS), WORD_ROWS), :]
        pltpu.make_async_copy(src, dst, xsem.at[slot]).start(priority=priority)

    def gather_chunk(step, i):
        first = (step * n_sub + i) * GATHER_ROWS
        tile = lax.shift_right_logical(first, 8)
        col = jnp.bitwise_and(first, ROW_TILE - 1)
        for g in range(GATHER_ROWS):
            gather_row((s + 1) % 2, tile, col + g, first + g, priority=GATHER_PRIORITY)

    def gather_range(p, first, last):
        def issue(r, _):
            gather_row(p % 2, lax.shift_right_logical(r, 8), jnp.bitwise_and(r, ROW_TILE - 1), r)
            return 0

        lax.fori_loop(first, last, issue, 0)

    def wait_rows(slot, count):
        @pl.when(count > 0)
        def _():
            n = pl.multiple_of(count * WORD_ROWS, ROW_TILE * WORD_ROWS)
            window = xbuf_ref.at[slot, pl.ds(0, n), :]
            pltpu.make_async_copy(window, window, xsem.at[slot]).wait()

    @pl.when(jnp.logical_and(s == 0, j == 0))
    def _():
        pend_ref[0] = 0
        pend_ref[1] = 0

    def drain_stage(slot):
        @pl.when(pend_ref[slot] == 1)
        def _():
            pltpu.make_async_copy(stage_ref.at[slot], stage_ref.at[slot], ysem.at[slot]).wait()
            pend_ref[slot] = 0

    @pl.when(jnp.logical_and(s == 0, j == 0))
    def _():
        idx_copy(0).start()
        idx_copy(0).wait()
        gather_range(0, 0, n_sub * ROW_TILE)
        wait_rows(0, n_sub * ROW_TILE)
        idx_copy(1).start()

    @pl.when(jnp.logical_and(s > 0, j == 0))
    def _():
        wait_rows(s % 2, jnp.maximum(nsub_ref[jnp.maximum(s - 1, 0)], n_sub) * ROW_TILE)

    @pl.when(jnp.logical_and(j == 0, s + 1 < n_pass))
    def _():
        idx_copy(s + 1).wait()

    @pl.when(jnp.logical_and(j == 0, s + 2 < n_pass))
    def _():
        idx_copy(s + 2).start()

    @pl.when(jnp.logical_and(j == 0, jnp.logical_and(n_sub == 0, s + 1 < n_pass)))
    def _():
        gather_range(s + 1, 0, n_next * ROW_TILE)

    @pl.when(jnp.logical_and(j < N_FF_CHUNKS, n_sub > 0))
    def _():
        wg_bf[...] = wg_ref[...].astype(BF16)
        wl_bf[...] = wl_ref[...].astype(BF16)
        bg = bg_ref[...]
        bl = bl_ref[...]

        def up_tile(i):
            rows = pl.ds(pl.multiple_of(i * ROW_TILE, ROW_TILE), ROW_TILE)
            xt = _unpack_tile(xbuf_ref, s % 2, i)
            glu = jnp.minimum(jnp.dot(xt, wg_bf[...], preferred_element_type=F32) + bg, SWIGLU_LIMIT)
            lin = jnp.clip(jnp.dot(xt, wl_bf[...], preferred_element_type=F32) + bl,
                           -SWIGLU_LIMIT, SWIGLU_LIMIT)
            gather_chunk(j, i)
            act = glu * _sigmoid(SWIGLU_ALPHA * glu) * (lin + 1.0)
            act_ref[j, rows, :] = act.astype(BF16)

        _for_tiles(n_sub, up_tile)

    for cc in range(N_FF_CHUNKS):
        @pl.when(jnp.logical_and(j == N_FF_CHUNKS + cc, n_sub > 0))
        def _(cc=cc):
            wd_bf[...] = wd_ref[...].astype(BF16)
            bd = bd_ref[...]

            def out_copy(i, slot):
                dst = y_ref.at[pl.ds(pl.multiple_of(row_start + i * ROW_TILE, ROW_TILE), ROW_TILE),
                               cc * FF_CHUNK:(cc + 1) * FF_CHUNK]
                return pltpu.make_async_copy(stage_ref.at[slot], dst, ysem.at[slot])

            def down_tile(i):
                rows = pl.ds(pl.multiple_of(i * ROW_TILE, ROW_TILE), ROW_TILE)
                slot = i % 2
                drain_stage(slot)
                acc = bd
                for c in range(N_FF_CHUNKS):
                    acc = acc + jnp.dot(act_ref[c, rows, :], wd_bf[c * FF_CHUNK:(c + 1) * FF_CHUNK, :],
                                        preferred_element_type=F32)
                gather_chunk(N_FF_CHUNKS + cc, i)
                stage_ref[slot] = acc
                out_copy(i, slot).start()
                pend_ref[slot] = 1

            _for_tiles(n_sub, down_tile)
            if cc == N_FF_CHUNKS - 1:
                gather_range(s + 1, n_sub * ROW_TILE, n_next * ROW_TILE)

    n_zero = nzero_ref[s]

    @pl.when(jnp.logical_and(j == 0, n_zero > 0))
    def _():
        drain_stage(0)
        stage_ref[0] = jnp.zeros((ROW_TILE, FF_CHUNK), F32)

        def zero_copy(i, cc):
            dst = y_ref.at[pl.ds(pl.multiple_of(row_start + i * ROW_TILE, ROW_TILE), ROW_TILE),
                           cc * FF_CHUNK:(cc + 1) * FF_CHUNK]
            return pltpu.make_async_copy(stage_ref.at[0], dst, ysem.at[0])

        def issue(i, _):
            for cc in range(N_FF_CHUNKS):
                zero_copy(i, cc).start()
            return 0

        def drain(i, _):
            for cc in range(N_FF_CHUNKS):
                zero_copy(i, cc).wait()
            return 0

        lax.fori_loop(0, n_zero, issue, 0)
        lax.fori_loop(0, n_zero, drain, 0)

    @pl.when(jnp.logical_and(s == n_pass - 1, j == 2 * N_FF_CHUNKS - 1))
    def _():
        drain_stage(0)
        drain_stage(1)


def expert_mlp(h_packed, tok_sorted, sched, w_gate_up, b_gate_up, w_down, b_down):
    exp_of, row_of, nsub_of, nzero_of, n_pass = sched
    last = N_FF_CHUNKS - 1
    up_of = lambda s, j, n: jnp.where(n[s] > 0, jnp.minimum(j, last), last)
    down_of = lambda s, j, n: jnp.where(n[s] > 0, jnp.maximum(j - N_FF_CHUNKS, 0), last)
    up_chunk = lambda s, j, e, r, n, z, p: (e[s], 0, up_of(s, j, n))
    lin_chunk = lambda s, j, e, r, n, z, p: (e[s], 0, N_FF_CHUNKS + up_of(s, j, n))
    down_chunk = lambda s, j, e, r, n, z, p: (e[s], 0, down_of(s, j, n))
    grid_spec = pltpu.PrefetchScalarGridSpec(
        num_scalar_prefetch=5,
        grid=(n_pass[0], 2 * N_FF_CHUNKS),
        in_specs=[
            pl.BlockSpec(memory_space=pl.ANY),
            pl.BlockSpec(memory_space=pl.ANY),
            pl.BlockSpec((None, D_MODEL, FF_CHUNK), up_chunk),
            pl.BlockSpec((None, D_MODEL, FF_CHUNK), lin_chunk),
            pl.BlockSpec((None, D_FF, FF_CHUNK), down_chunk),
            pl.BlockSpec((None, 1, FF_CHUNK), up_chunk),
            pl.BlockSpec((None, 1, FF_CHUNK), lin_chunk),
            pl.BlockSpec((None, 1, FF_CHUNK), down_chunk),
        ],
        out_specs=pl.BlockSpec(memory_space=pl.ANY),
        scratch_shapes=[
            pltpu.VMEM((2, SUPER_ROWS * WORD_ROWS, 128), jnp.uint32),
            pltpu.VMEM((N_FF_CHUNKS, SUPER_ROWS, FF_CHUNK), BF16),
            pltpu.VMEM((D_MODEL, FF_CHUNK), BF16),
            pltpu.VMEM((D_MODEL, FF_CHUNK), BF16),
            pltpu.VMEM((D_FF, FF_CHUNK), BF16),
            pltpu.VMEM((2, ROW_TILE, FF_CHUNK), F32),
            pltpu.SMEM((2, SUPER_TILES, 1, ROW_TILE), jnp.int32),
            pltpu.SMEM((2,), jnp.int32),
            pltpu.SemaphoreType.DMA((2,)),
            pltpu.SemaphoreType.DMA((2,)),
            pltpu.SemaphoreType.DMA((2,)),
        ],
    )
    tok_tiles = jnp.concatenate([tok_sorted.reshape(N_ROW_TILES, 1, ROW_TILE),
                                 jnp.zeros((SUPER_TILES, 1, ROW_TILE), jnp.int32)], axis=0)
    return pl.pallas_call(
        _moe_body,
        grid_spec=grid_spec,
        out_shape=jax.ShapeDtypeStruct((N_ROWS, D_MODEL), F32),
        compiler_params=_params(("arbitrary", "arbitrary"), vmem=EXPERT_VMEM_LIMIT),
        name="expert_mlp",
    )(exp_of, row_of, nsub_of, nzero_of, n_pass, tok_tiles, h_packed, w_gate_up, w_gate_up, w_down,
      b_gate_up.reshape(N_EXPERTS, 1, 2 * D_FF), b_gate_up.reshape(N_EXPERTS, 1, 2 * D_FF),
      b_down.reshape(N_EXPERTS, 1, D_MODEL))


COMB_TB = 256


def _combine_start(y_ref, ybuf_ref, pos_ref, sem):
    def issue(t, _):
        for k in range(TOP_K):
            p = pos_ref[0, 0, t * TOP_K + k]
            pltpu.make_async_copy(y_ref.at[pl.ds(p, 1), :], ybuf_ref.at[k, pl.ds(t, 1), :], sem).start()
        return 0

    lax.fori_loop(0, COMB_TB, issue, 0, unroll=4)


def _combine_body(n, pos_ref, pos_next_ref, y_ref, gate_ref, x1_ref, gt2_ref, g_ref, o_ref, ybuf_ref, sem_ref):
    i = pl.program_id(0)
    slot = i % 2

    @pl.when(i == 0)
    def _():
        _combine_start(y_ref, ybuf_ref.at[0], pos_ref, sem_ref.at[0])

    @pl.when(i + 1 < n)
    def _():
        _combine_start(y_ref, ybuf_ref.at[1 - slot], pos_next_ref, sem_ref.at[1 - slot])

    for k in range(TOP_K):
        pltpu.make_async_copy(y_ref.at[pl.ds(0, COMB_TB), :], ybuf_ref.at[slot, k], sem_ref.at[slot]).wait()
    gates = gate_ref[...]
    ffn = gates[:, 0:1] * ybuf_ref[slot, 0]
    for k in range(1, TOP_K):
        ffn = ffn + gates[:, k:k + 1] * ybuf_ref[slot, k]
    o_ref[...] = x1_ref[...] + gt2_ref[...] * (ffn * _rms_scale(ffn) * g_ref[...])


def combine_residual(y_sorted, pos, gates, x1, gt2, g_post_ffn, row_offset, n_rows, group_of_block):
    tb = COMB_TB
    nblk = n_rows // tb
    off = row_offset // tb
    pos3 = pos.reshape(N_TOK // tb, 1, tb * TOP_K)
    smem_blk = lambda f: pl.BlockSpec((1, 1, tb * TOP_K), f, memory_space=pltpu.SMEM)
    return pl.pallas_call(
        functools.partial(_combine_body, nblk),
        grid=(nblk,),
        in_specs=[
            smem_blk(lambda i: (off + i, 0, 0)),
            smem_blk(lambda i: (off + jnp.minimum(i + 1, nblk - 1), 0, 0)),
            pl.BlockSpec(memory_space=pl.ANY),
            pl.BlockSpec((tb, TOP_K), lambda i: (off + i, 0)),
            pl.BlockSpec((tb, D_MODEL), lambda i: (off + i, 0)),
            pl.BlockSpec((None, 1, D_MODEL), lambda i: (group_of_block(i), 0, 0)),
            pl.BlockSpec((1, D_MODEL), lambda i: (0, 0)),
        ],
        out_specs=pl.BlockSpec((tb, D_MODEL), lambda i: (i, 0)),
        out_shape=jax.ShapeDtypeStruct((n_rows, D_MODEL), F32),
        scratch_shapes=[pltpu.VMEM((2, TOP_K, tb, D_MODEL), F32), pltpu.SemaphoreType.DMA((2,))],
        compiler_params=_params(("arbitrary",)),
        name="combine_residual",
    )(pos3, pos3, y_sorted, gates, x1, gt2, g_post_ffn.reshape(1, D_MODEL))


INV_CHUNK = 4096


def _row_tokens_body(pos_ref, zeros_ref, o_ref, sem):
    i = pl.program_id(0)

    @pl.when(i == 0)
    def _():
        cp = pltpu.make_async_copy(zeros_ref, o_ref, sem)
        cp.start()
        cp.wait()

    base = i * INV_CHUNK

    def put(r, _):
        o_ref[pos_ref[r]] = lax.shift_right_logical(base + r, 2)
        return 0

    lax.fori_loop(0, INV_CHUNK, put, 0, unroll=8)


def row_tokens(pos):
    return pl.pallas_call(
        _row_tokens_body,
        grid=(N_ASSIGN // INV_CHUNK,),
        in_specs=[
            pl.BlockSpec((INV_CHUNK,), lambda i: (i,), memory_space=pltpu.SMEM),
            pl.BlockSpec(memory_space=pl.ANY),
        ],
        out_specs=pl.BlockSpec(memory_space=pltpu.SMEM),
        out_shape=jax.ShapeDtypeStruct((N_ROWS,), jnp.int32),
        scratch_shapes=[pltpu.SemaphoreType.DMA(())],
        compiler_params=_params(("arbitrary",)),
        name="row_tokens",
    )(pos.reshape(N_ASSIGN), jnp.zeros((N_ROWS,), jnp.int32))


def _routing_tables(e_idx, rank, counts_f):
    counts = counts_f.reshape(N_EXPERTS).astype(jnp.int32)
    n_tiles = (counts + ROW_TILE - 1) // ROW_TILE
    padded = n_tiles * ROW_TILE
    pad_end = jnp.cumsum(padded)
    pad_start = pad_end - padded
    pos = (pad_start[e_idx] + rank).astype(jnp.int32)
    tok_sorted = row_tokens(pos)
    n_pass = (n_tiles + SUPER_TILES - 1) // SUPER_TILES
    pass_end = jnp.cumsum(n_pass)
    total = pass_end[-1]
    s = jnp.arange(N_SUPER, dtype=jnp.int32)
    s_eff = jnp.minimum(s, total - 1)
    e_of = jnp.minimum(jnp.searchsorted(pass_end, s_eff, side="right"), N_EXPERTS - 1).astype(jnp.int32)
    local = s_eff - (pass_end[e_of] - n_pass[e_of])
    row_of = pad_start[e_of] + local * SUPER_ROWS
    nsub = jnp.minimum(SUPER_TILES, n_tiles[e_of] - local * SUPER_TILES)
    nsub = jnp.where(s < total, nsub, 0).astype(jnp.int32)
    zero_row = pad_end[-1] + (s - total) * SUPER_ROWS
    nzero = jnp.clip((N_ROWS - zero_row) // ROW_TILE, 0, SUPER_TILES)
    nzero = jnp.where(s >= total, nzero, 0).astype(jnp.int32)
    row_of = jnp.where(s < total, row_of, jnp.minimum(zero_row, N_ROWS - ROW_TILE)).astype(jnp.int32)
    tail_tiles = (N_ROWS - pad_end[-1]) // ROW_TILE
    n_pass = jnp.minimum(total + jnp.maximum((tail_tiles + SUPER_TILES - 1) // SUPER_TILES, 1), N_SUPER)
    n_pass = n_pass.astype(jnp.int32).reshape(1)
    return pos.astype(jnp.int32), tok_sorted, (e_of, row_of, nsub, nzero, n_pass)


def kernel(x_prompt, x_sample, cache_k, cache_v, state_rnn_fwd, state_rnn_bwd, c, c_ctx, w_mod, b_mod, g_pre_mix, w_in, g_q_norm, g_k_norm, conv_w, conv_b, rg_w_a, rg_b_a, rg_w_x, rg_b_x, rg_lambda, w_o_attn, w_o_rnn, w_out, g_post_mix, g_pre_ffn, w_router, b_router, w_gate_up, b_gate_up, w_down, b_down, g_post_ffn):
    l = 0
    x_ctx = x_prompt.reshape(N_CTX, D_MODEL)
    x_lat = x_sample.reshape(N_LAT, D_MODEL)

    cond8 = jnp.concatenate([c_ctx[None, :], c, jnp.zeros((8 - 1 - N_LAT_SEQ, D_MODEL), F32)], axis=0)
    mod = modulation(cond8, w_mod[l], b_mod[l])[:1 + N_LAT_SEQ].reshape(1 + N_LAT_SEQ, 6, 1, D_MODEL)
    sh1, sc1, gt1, sh2, sc2, gt2 = [mod[:, i] for i in range(6)]

    ctx_group = lambda i: 0
    lat_group_1024 = lambda i: 1 + i
    h_ctx = prenorm_modulate(x_ctx, g_pre_mix[l], sh1, sc1, ctx_group, 1024)
    h_lat = prenorm_modulate(x_lat, g_pre_mix[l], sh1, sc1, lat_group_1024, 1024)
    z_ctx = in_projection(h_ctx, w_in[l])
    z_lat = in_projection(h_lat, w_in[l])

    attn_ctx, k_new, v_new = attention_ctx(z_ctx, g_q_norm[l], g_k_norm[l])
    attn_lat = attention_lat(z_lat, cache_k[:, l].reshape(N_LAT_SEQ, PAST_LEN, KV_COLS),
                             cache_v[:, l].reshape(N_LAT_SEQ, PAST_LEN, KV_COLS),
                             _rope_tables(), g_q_norm[l], g_k_norm[l])

    def per_block(w):
        return w.reshape(2, RNN_BLOCKS, 1, RNN_BLOCK_DIM)

    w_gates = jnp.concatenate([rg_w_a[l, 0], rg_w_x[l, 0], rg_w_a[l, 1], rg_w_x[l, 1]], axis=-1).astype(BF16)
    ba, bx = per_block(rg_b_a[l]), per_block(rg_b_x[l])
    b_gates = jnp.concatenate([ba[0], bx[0], ba[1], bx[1]], axis=-1)
    zeros_state = jnp.zeros((N_CTX_SEQ, 1, D_MODEL), F32)
    rnn_ctx, hf_ctx, hb_ctx = rglru_mixer(z_ctx, CTX_LEN, conv_w[l], conv_b[l], w_gates, b_gates,
                                          rg_lambda[l], zeros_state, zeros_state)
    rnn_lat, _, _ = rglru_mixer(z_lat, LAT_LEN, conv_w[l], conv_b[l], w_gates, b_gates, rg_lambda[l],
                                state_rnn_fwd[:, l].reshape(N_LAT_SEQ, 1, D_MODEL),
                                state_rnn_bwd[:, l].reshape(N_LAT_SEQ, 1, D_MODEL))

    merged_ctx = gated_merge(attn_ctx, rnn_ctx, z_ctx, w_o_attn[l], w_o_rnn[l])
    merged_lat = gated_merge(attn_lat, rnn_lat, z_lat, w_o_attn[l], w_o_rnn[l])

    x1, h2, e_idx, gates, rank, counts = post_mix_router(
        merged_ctx, merged_lat, x_ctx, x_lat, w_out[l].astype(BF16), g_post_mix[l], gt1, g_pre_ffn[l],
        sh2, sc2, w_router[l], b_router[l])

    pos, tok_sorted, sched = _routing_tables(e_idx, rank, counts)
    y_sorted = expert_mlp(h2, tok_sorted, sched, w_gate_up[l], b_gate_up[l], w_down[l], b_down[l])

    y_ctx = combine_residual(y_sorted, pos, gates, x1, gt2, g_post_ffn[l], 0, N_CTX, ctx_group)
    y_lat = combine_residual(y_sorted, pos, gates, x1, gt2, g_post_ffn[l], N_CTX, N_LAT,
                             lambda i: 1 + i // (LAT_LEN // COMB_TB))

    return (y_ctx.reshape(N_CTX_SEQ, CTX_LEN, D_MODEL),
            y_lat.reshape(N_LAT_SEQ, LAT_LEN, D_MODEL),
            k_new.reshape(N_CTX_SEQ, 1, CTX_LEN, N_KV_HEADS, HEAD_DIM),
            v_new.reshape(N_CTX_SEQ, 1, CTX_LEN, N_KV_HEADS, HEAD_DIM),
            hf_ctx,
            hb_ctx)
```

```python
import functools

import jax
import jax.numpy as jnp
import numpy as np
from jax import lax
from jax.experimental import pallas as pl
from jax.experimental.pallas import tpu as pltpu

D_MODEL = 2048
N_CTX_SEQ = 32
CTX_LEN = 256
N_LAT_SEQ = 2
LAT_LEN = 1024
PAST_LEN = 512
N_CTX = N_CTX_SEQ * CTX_LEN
N_LAT = N_LAT_SEQ * LAT_LEN
N_TOK = N_CTX + N_LAT
GRID_W = 64
N_HEADS = 16
N_KV_HEADS = 4
HEAD_DIM = 128
KV_GROUP = N_HEADS // N_KV_HEADS
ROPE_THETA = 10000.0
RNN_BLOCKS = 16
RNN_BLOCK_DIM = 128
RG_C = 8.0
N_EXPERTS = 32
TOP_K = 4
D_FF = 2048
SWIGLU_LIMIT = 7.0
SWIGLU_ALPHA = 1.702
EPS = 1e-6
Q_COLS = N_HEADS * HEAD_DIM
KV_COLS = N_KV_HEADS * HEAD_DIM
IN_COLS = Q_COLS + 2 * KV_COLS + 4 * D_MODEL
COL_K = Q_COLS
COL_XR = Q_COLS + 2 * KV_COLS
COL_YR = COL_XR + D_MODEL
COL_GA = COL_YR + D_MODEL
COL_GR = COL_GA + D_MODEL

V7X_VMEM_BYTES = 64 * 1024 * 1024
VMEM_LIMIT = 56 * 1024 * 1024
EXPERT_VMEM_LIMIT = 60 * 1024 * 1024

ROW_TILE = 256
SUPER_TILES = 8
SUPER_ROWS = ROW_TILE * SUPER_TILES
N_ASSIGN = N_TOK * TOP_K
N_ROWS = N_ASSIGN + N_EXPERTS * ROW_TILE
N_ROW_TILES = N_ROWS // ROW_TILE
N_SUPER = N_ROW_TILES // SUPER_TILES + N_EXPERTS
FF_CHUNK = 512
N_FF_CHUNKS = D_FF // FF_CHUNK

BF16 = jnp.bfloat16
F32 = jnp.float32


def _params(semantics, vmem=VMEM_LIMIT):
    return pltpu.CompilerParams(dimension_semantics=semantics, vmem_limit_bytes=vmem)


def _rms_scale(x):
    return lax.rsqrt(jnp.mean(x * x, axis=-1, keepdims=True) + EPS)


def _sigmoid(x):
    return 1.0 / (1.0 + jnp.exp(-x))


def _mod_body(c_ref, w_ref, b_ref, o_ref):
    c = c_ref[...]
    a = (c * _sigmoid(c)).astype(BF16)
    o_ref[...] = jnp.dot(a, w_ref[...].astype(BF16), preferred_element_type=F32) + b_ref[...]


def modulation(cond8, w_mod, b_mod):
    tn = 1024
    n = w_mod.shape[1]
    return pl.pallas_call(
        _mod_body,
        grid=(n // tn,),
        in_specs=[
            pl.BlockSpec((8, D_MODEL), lambda j: (0, 0)),
            pl.BlockSpec((D_MODEL, tn), lambda j: (0, j)),
            pl.BlockSpec((1, tn), lambda j: (0, j)),
        ],
        out_specs=pl.BlockSpec((8, tn), lambda j: (0, j)),
        out_shape=jax.ShapeDtypeStruct((8, n), F32),
        compiler_params=_params(("arbitrary",)),
        name="modulation",
    )(cond8, w_mod, b_mod.reshape(1, n))


def _prenorm_body(x_ref, g_ref, sh_ref, sc_ref, o_ref):
    x = x_ref[...]
    y = x * _rms_scale(x) * g_ref[...]
    o_ref[...] = (y * (1.0 + sc_ref[...]) + sh_ref[...]).astype(o_ref.dtype)


def prenorm_modulate(x, g, shift, scale, group_of_block, tm):
    m = x.shape[0]
    gmap = lambda i: (group_of_block(i), 0, 0)
    return pl.pallas_call(
        _prenorm_body,
        grid=(m // tm,),
        in_specs=[
            pl.BlockSpec((tm, D_MODEL), lambda i: (i, 0)),
            pl.BlockSpec((1, D_MODEL), lambda i: (0, 0)),
            pl.BlockSpec((None, 1, D_MODEL), gmap),
            pl.BlockSpec((None, 1, D_MODEL), gmap),
        ],
        out_specs=pl.BlockSpec((tm, D_MODEL), lambda i: (i, 0)),
        out_shape=jax.ShapeDtypeStruct((m, D_MODEL), BF16),
        compiler_params=_params(("arbitrary",)),
        name="prenorm_modulate",
    )(x, g.reshape(1, D_MODEL), shift, scale)


def _inproj_body(h_ref, w_ref, o_ref, wbf_ref):
    @pl.when(pl.program_id(1) == 0)
    def _():
        wbf_ref[...] = w_ref[...].astype(BF16)

    o_ref[...] = jnp.dot(h_ref[...], wbf_ref[...], preferred_element_type=F32)


def in_projection(h, w_in):
    m = h.shape[0]
    tm, tn = 1024, 1024
    return pl.pallas_call(
        _inproj_body,
        grid=(IN_COLS // tn, m // tm),
        in_specs=[
            pl.BlockSpec((tm, D_MODEL), lambda j, i: (i, 0)),
            pl.BlockSpec((D_MODEL, tn), lambda j, i: (0, j)),
        ],
        out_specs=pl.BlockSpec((tm, tn), lambda j, i: (i, j)),
        out_shape=jax.ShapeDtypeStruct((m, IN_COLS), F32),
        scratch_shapes=[pltpu.VMEM((D_MODEL, tn), BF16)],
        compiler_params=_params(("arbitrary", "arbitrary")),
        name="in_projection",
    )(h, w_in)


def _rope(x, cos, sin_lo, sin_hi):
    return x * cos + pltpu.roll(x, 96, 1) * sin_lo + pltpu.roll(x, 32, 1) * sin_hi


def _head_norm(x, g):
    return x * _rms_scale(x) * g


def _softmax_pv(score_blocks, value_blocks):
    m = None
    for s in score_blocks:
        mi = jnp.max(s, axis=-1, keepdims=True)
        m = mi if m is None else jnp.maximum(m, mi)
    ps = [jnp.exp(s - m) for s in score_blocks]
    denom = None
    for p in ps:
        li = jnp.sum(p, axis=-1, keepdims=True)
        denom = li if denom is None else denom + li
    inv = 1.0 / denom
    out = None
    for p, v in zip(ps, value_blocks):
        o = jnp.dot((p * inv).astype(BF16), v, preferred_element_type=F32)
        out = o if out is None else out + o
    return out


def _attn_ctx_body(q_ref, kv_ref, gq_ref, gk_ref, o_ref, ko_ref, vo_ref):
    tq = q_ref.shape[0]
    scale = HEAD_DIM ** -0.5
    gq = gq_ref[...]
    gk = gk_ref[...]
    for g in range(N_KV_HEADS):
        kcols = slice(g * HEAD_DIM, (g + 1) * HEAD_DIM)
        kn = _head_norm(kv_ref[:, kcols], gk)
        v = kv_ref[:, KV_COLS + g * HEAD_DIM:KV_COLS + (g + 1) * HEAD_DIM]
        ko_ref[pl.ds(g, tq, stride=N_KV_HEADS), :] = kn
        vo_ref[pl.ds(g, tq, stride=N_KV_HEADS), :] = v
        qs = []
        for hh in range(KV_GROUP):
            h = g * KV_GROUP + hh
            qs.append(_head_norm(q_ref[:, h * HEAD_DIM:(h + 1) * HEAD_DIM], gq).astype(BF16))
        q4 = jnp.concatenate(qs, axis=0)
        s = lax.dot_general(q4, kn.astype(BF16), (((1,), (1,)), ((), ())),
                            preferred_element_type=F32) * scale
        o = _softmax_pv([s], [v.astype(BF16)])
        for hh in range(KV_GROUP):
            h = g * KV_GROUP + hh
            o_ref[:, h * HEAD_DIM:(h + 1) * HEAD_DIM] = o[hh * tq:(hh + 1) * tq].astype(o_ref.dtype)


def attention_ctx(z, g_q, g_k):
    nb = N_CTX_SEQ
    t = CTX_LEN
    return pl.pallas_call(
        _attn_ctx_body,
        grid=(nb,),
        in_specs=[
            pl.BlockSpec((t, Q_COLS), lambda b: (b, 0)),
            pl.BlockSpec((t, 2 * KV_COLS), lambda b: (b, COL_K // (2 * KV_COLS))),
            pl.BlockSpec((1, HEAD_DIM), lambda b: (0, 0)),
            pl.BlockSpec((1, HEAD_DIM), lambda b: (0, 0)),
        ],
        out_specs=[
            pl.BlockSpec((t, Q_COLS), lambda b: (b, 0)),
            pl.BlockSpec((t * N_KV_HEADS, HEAD_DIM), lambda b: (b, 0)),
            pl.BlockSpec((t * N_KV_HEADS, HEAD_DIM), lambda b: (b, 0)),
        ],
        out_shape=[
            jax.ShapeDtypeStruct((N_CTX, Q_COLS), BF16),
            jax.ShapeDtypeStruct((N_CTX * N_KV_HEADS, HEAD_DIM), F32),
            jax.ShapeDtypeStruct((N_CTX * N_KV_HEADS, HEAD_DIM), F32),
        ],
        compiler_params=_params(("arbitrary",)),
        name="attention_ctx",
    )(z, z, g_q.reshape(1, HEAD_DIM), g_k.reshape(1, HEAD_DIM))


def _attn_lat_body(q_ref, kv_ref, ck_ref, cv_ref, cos_ref, slo_ref, shi_ref, gq_ref, gk_ref,
                   o_ref, kr_ref):
    tq = q_ref.shape[0]
    qb = pl.program_id(1)
    scale = HEAD_DIM ** -0.5
    gq = gq_ref[...]

    @pl.when(qb == 0)
    def _():
        gk = gk_ref[...]
        for g in range(N_KV_HEADS):
            kcols = slice(g * HEAD_DIM, (g + 1) * HEAD_DIM)
            kn = _head_norm(kv_ref[:, kcols], gk)
            kr_ref[:, kcols] = _rope(kn, cos_ref[...], slo_ref[...], shi_ref[...]).astype(BF16)

    row0 = pl.multiple_of(qb * tq, tq)
    cos = cos_ref[pl.ds(row0, tq), :]
    slo = slo_ref[pl.ds(row0, tq), :]
    shi = shi_ref[pl.ds(row0, tq), :]
    for g in range(N_KV_HEADS):
        kcols = slice(g * HEAD_DIM, (g + 1) * HEAD_DIM)
        qs = []
        for hh in range(KV_GROUP):
            h = g * KV_GROUP + hh
            qn = _head_norm(q_ref[:, h * HEAD_DIM:(h + 1) * HEAD_DIM], gq)
            qs.append(_rope(qn, cos, slo, shi).astype(BF16))
        q4 = jnp.concatenate(qs, axis=0)
        dn = (((1,), (1,)), ((), ()))
        s_past = lax.dot_general(q4, ck_ref[:, kcols].astype(BF16), dn,
                                 preferred_element_type=F32) * scale
        s_new = lax.dot_general(q4, kr_ref[:, kcols], dn, preferred_element_type=F32) * scale
        v_past = cv_ref[:, kcols].astype(BF16)
        v_new = kv_ref[:, KV_COLS + g * HEAD_DIM:KV_COLS + (g + 1) * HEAD_DIM].astype(BF16)
        o = _softmax_pv([s_past, s_new], [v_past, v_new])
        for hh in range(KV_GROUP):
            h = g * KV_GROUP + hh
            o_ref[:, h * HEAD_DIM:(h + 1) * HEAD_DIM] = o[hh * tq:(hh + 1) * tq].astype(o_ref.dtype)


def attention_lat(z, cache_k, cache_v, rope_tabs, g_q, g_k):
    tq = 256
    nq = LAT_LEN // tq
    cos, slo, shi = rope_tabs
    tab = pl.BlockSpec((LAT_LEN, HEAD_DIM), lambda b, q: (0, 0))
    return pl.pallas_call(
        _attn_lat_body,
        grid=(N_LAT_SEQ, nq),
        in_specs=[
            pl.BlockSpec((tq, Q_COLS), lambda b, q: (b * nq + q, 0)),
            pl.BlockSpec((LAT_LEN, 2 * KV_COLS), lambda b, q: (b, COL_K // (2 * KV_COLS))),
            pl.BlockSpec((None, PAST_LEN, KV_COLS), lambda b, q: (b, 0, 0)),
            pl.BlockSpec((None, PAST_LEN, KV_COLS), lambda b, q: (b, 0, 0)),
            tab, tab, tab,
            pl.BlockSpec((1, HEAD_DIM), lambda b, q: (0, 0)),
            pl.BlockSpec((1, HEAD_DIM), lambda b, q: (0, 0)),
        ],
        out_specs=pl.BlockSpec((tq, Q_COLS), lambda b, q: (b * nq + q, 0)),
        out_shape=jax.ShapeDtypeStruct((N_LAT, Q_COLS), BF16),
        scratch_shapes=[pltpu.VMEM((LAT_LEN, KV_COLS), BF16)],
        compiler_params=_params(("arbitrary", "arbitrary")),
        name="attention_lat",
    )(z, z, cache_k, cache_v, cos, slo, shi, g_q.reshape(1, HEAD_DIM), g_k.reshape(1, HEAD_DIM))


def _rope_tables():
    t = np.arange(LAT_LEN)
    row = jnp.asarray(t // GRID_W, F32)
    col = jnp.asarray(t % GRID_W, F32)
    nf = HEAD_DIM // 4
    inv_freq = ROPE_THETA ** (-jnp.arange(nf, dtype=F32) / nf)
    ang_row = row[:, None] * inv_freq[None, :]
    ang_col = col[:, None] * inv_freq[None, :]
    ang = jnp.concatenate([ang_row, ang_row, ang_col, ang_col], axis=1)
    cos = jnp.cos(ang)
    sin = jnp.sin(ang)
    first = jnp.asarray((np.arange(HEAD_DIM) % (2 * nf)) < nf)[None, :]
    return cos, jnp.where(first, -sin, 0.0), jnp.where(first, 0.0, sin)


RNN_ROWS = 2048
RNN_COLS = 512
RNN_SUB = RNN_COLS // RNN_BLOCK_DIM


def _gelu_tanh(y):
    return 0.5 * y * (1.0 + jnp.tanh(0.7978845608028654 * (y + 0.044715 * (y * y * y))))


def _rglru_body(seq_len, xr_ref, yr_ref, cw_ref, cb_ref, wg_ref, bg_ref, lam_ref, h0f_ref, h0b_ref,
                o_ref, hf_ref, hb_ref, xs_ref, af_ref, bf_ref, ab_ref, bb_ref):
    n_seq = RNN_ROWS // seq_len
    for n in range(RNN_SUB):
        cols = slice(n * RNN_BLOCK_DIM, (n + 1) * RNN_BLOCK_DIM)
        for s in range(n_seq):
            xs_ref[n, pl.ds(s, seq_len, stride=n_seq), :] = xr_ref[s * seq_len:(s + 1) * seq_len, cols]

    row = lax.broadcasted_iota(jnp.int32, (RNN_ROWS, 1), 0)
    lam = lam_ref[...]
    softplus_neg = jnp.maximum(-lam, 0.0) + jnp.log(1.0 + jnp.exp(-jnp.abs(lam)))
    rate = softplus_neg * (-RG_C * 1.4426950408889634)
    for n in range(RNN_SUB):
        cols = slice(n * RNN_BLOCK_DIM, (n + 1) * RNN_BLOCK_DIM)
        x = xs_ref[n]
        x_m1 = jnp.where(row >= n_seq, pltpu.roll(x, n_seq, 0), 0.0)
        x_p1 = jnp.where(row < RNN_ROWS - n_seq, pltpu.roll(x, RNN_ROWS - n_seq, 0), 0.0)
        x_p2 = jnp.where(row < RNN_ROWS - 2 * n_seq, pltpu.roll(x, RNN_ROWS - 2 * n_seq, 0), 0.0)
        xn = (cb_ref[:, cols] + x_m1 * cw_ref[0:1, cols] + x * cw_ref[1:2, cols]
              + x_p1 * cw_ref[2:3, cols] + x_p2 * cw_ref[3:4, cols])
        pre = jnp.dot(xn.astype(BF16), wg_ref[n], preferred_element_type=F32) + bg_ref[n]
        for d, (a_ref, b_ref) in enumerate(((af_ref, bf_ref), (ab_ref, bb_ref))):
            r = 0.5 * jnp.tanh(0.5 * pre[:, (2 * d) * RNN_BLOCK_DIM:(2 * d + 1) * RNN_BLOCK_DIM]) + 0.5
            gate_in = 0.5 * jnp.tanh(
                0.5 * pre[:, (2 * d + 1) * RNN_BLOCK_DIM:(2 * d + 2) * RNN_BLOCK_DIM]) + 0.5
            a = jnp.exp2(r * rate[d:d + 1, cols])
            v = 1.0 - a * a
            a_ref[n] = a
            b_ref[n] = (v * lax.rsqrt(jnp.maximum(v, 1e-30))) * (gate_in * xn)

    def step(t, carry):
        rows_f = pl.ds(pl.multiple_of(t * n_seq, n_seq), n_seq)
        rows_b = pl.ds(pl.multiple_of((seq_len - 1 - t) * n_seq, n_seq), n_seq)
        out = []
        for n in range(RNN_SUB):
            hf = af_ref[n, rows_f, :] * carry[2 * n] + bf_ref[n, rows_f, :]
            hb = ab_ref[n, rows_b, :] * carry[2 * n + 1] + bb_ref[n, rows_b, :]
            bf_ref[n, rows_f, :] = hf
            bb_ref[n, rows_b, :] = hb
            out += [hf, hb]
        return tuple(out)

    init = []
    for n in range(RNN_SUB):
        cols = slice(n * RNN_BLOCK_DIM, (n + 1) * RNN_BLOCK_DIM)
        init += [h0f_ref[:, 0, cols], h0b_ref[:, 0, cols]]
    last = lax.fori_loop(0, seq_len, step, tuple(init), unroll=8)
    for n in range(RNN_SUB):
        cols = slice(n * RNN_BLOCK_DIM, (n + 1) * RNN_BLOCK_DIM)
        hf_ref[:, 0, cols] = last[2 * n]
        hb_ref[:, 0, cols] = last[2 * n + 1]
        bf_ref[n] = bf_ref[n] + bb_ref[n]
        for s in range(n_seq):
            rows = slice(s * seq_len, (s + 1) * seq_len)
            h_sum = bf_ref[n, pl.ds(s, seq_len, stride=n_seq), :]
            o_ref[rows, cols] = (h_sum * _gelu_tanh(yr_ref[rows, cols])).astype(o_ref.dtype)


def rglru_mixer(z, seq_len, conv_w, conv_b, w_gates, b_gates, lam, h0_f, h0_b):
    m = z.shape[0]
    n_seq_total = m // seq_len
    n_seq = RNN_ROWS // seq_len
    cblk = lambda base: (lambda r, c: (r, base // RNN_COLS + c))
    state_spec = pl.BlockSpec((n_seq, 1, RNN_COLS), lambda r, c: (r, 0, c))
    return pl.pallas_call(
        functools.partial(_rglru_body, seq_len),
        grid=(m // RNN_ROWS, D_MODEL // RNN_COLS),
        in_specs=[
            pl.BlockSpec((RNN_ROWS, RNN_COLS), cblk(COL_XR)),
            pl.BlockSpec((RNN_ROWS, RNN_COLS), cblk(COL_YR)),
            pl.BlockSpec((4, RNN_COLS), lambda r, c: (0, c)),
            pl.BlockSpec((1, RNN_COLS), lambda r, c: (0, c)),
            pl.BlockSpec((RNN_SUB, RNN_BLOCK_DIM, 4 * RNN_BLOCK_DIM), lambda r, c: (c, 0, 0)),
            pl.BlockSpec((RNN_SUB, 1, 4 * RNN_BLOCK_DIM), lambda r, c: (c, 0, 0)),
            pl.BlockSpec((2, RNN_COLS), lambda r, c: (0, c)),
            state_spec, state_spec,
        ],
        out_specs=[
            pl.BlockSpec((RNN_ROWS, RNN_COLS), lambda r, c: (r, c)),
            state_spec, state_spec,
        ],
        out_shape=[
            jax.ShapeDtypeStruct((m, D_MODEL), BF16),
            jax.ShapeDtypeStruct((n_seq_total, 1, D_MODEL), F32),
            jax.ShapeDtypeStruct((n_seq_total, 1, D_MODEL), F32),
        ],
        scratch_shapes=[pltpu.VMEM((RNN_SUB, RNN_ROWS, RNN_BLOCK_DIM), F32) for _ in range(5)],
        compiler_params=_params(("arbitrary", "arbitrary")),
        name="rglru_mixer_t%d" % seq_len,
    )(z, z, conv_w, conv_b.reshape(1, D_MODEL), w_gates, b_gates, lam, h0_f, h0_b)


def _merge_body(a_ref, r_ref, wa_ref, wr_ref, ga_ref, gr_ref, o_ref, wa_bf, wr_bf):
    @pl.when(pl.program_id(1) == 0)
    def _():
        wa_bf[...] = wa_ref[...].astype(BF16)
        wr_bf[...] = wr_ref[...].astype(BF16)

    pa = jnp.dot(a_ref[...], wa_bf[...], preferred_element_type=F32)
    pr = jnp.dot(r_ref[...], wr_bf[...], preferred_element_type=F32)
    o_ref[...] = (_sigmoid(ga_ref[...]) * pa + _sigmoid(gr_ref[...]) * pr).astype(o_ref.dtype)


def gated_merge(attn, rnn, z, w_o_attn, w_o_rnn):
    m = attn.shape[0]
    tm, tn = 1024, 512
    return pl.pallas_call(
        _merge_body,
        grid=(D_MODEL // tn, m // tm),
        in_specs=[
            pl.BlockSpec((tm, Q_COLS), lambda j, i: (i, 0)),
            pl.BlockSpec((tm, D_MODEL), lambda j, i: (i, 0)),
            pl.BlockSpec((Q_COLS, tn), lambda j, i: (0, j)),
            pl.BlockSpec((D_MODEL, tn), lambda j, i: (0, j)),
            pl.BlockSpec((tm, tn), lambda j, i: (i, COL_GA // tn + j)),
            pl.BlockSpec((tm, tn), lambda j, i: (i, COL_GR // tn + j)),
        ],
        out_specs=pl.BlockSpec((tm, tn), lambda j, i: (i, j)),
        out_shape=jax.ShapeDtypeStruct((m, D_MODEL), BF16),
        scratch_shapes=[pltpu.VMEM((Q_COLS, tn), BF16), pltpu.VMEM((D_MODEL, tn), BF16)],
        compiler_params=_params(("arbitrary", "arbitrary")),
        name="gated_merge",
    )(attn, rnn, w_o_attn, w_o_rnn, z, z)


POST_TM = 512
POST_SPLIT = 1
HALF_D = D_MODEL // 2
WORD_ROWS = HALF_D // 128
SUBLANES = 8
POST_CTX_BLOCKS = N_CTX // POST_TM
LAT_BLOCKS_PER_SEQ = LAT_LEN // POST_TM


def _post_group(i):
    return jnp.where(i < POST_CTX_BLOCKS, 0, 1 + (i - POST_CTX_BLOCKS) // LAT_BLOCKS_PER_SEQ)


def _postmix_body(mc_ref, ml_ref, xc_ref, xl_ref, wo_ref, gpm_ref, gt1_ref, gpf_ref, sh2_ref, sc2_ref,
                  wr_ref, br_ref, x1_ref, h2_ref, e_ref, gate_ref, rank_ref, cnt_ref, carry_ref):
    i = pl.program_id(0)
    tm = POST_TM

    @pl.when(i == 0)
    def _():
        carry_ref[...] = jnp.zeros_like(carry_ref)

    is_ctx = i < POST_CTX_BLOCKS
    th = tm // POST_SPLIT
    r_io = lax.broadcasted_iota(jnp.int32, (th, th), 0)
    c_io = lax.broadcasted_iota(jnp.int32, (th, th), 1)
    lower = jnp.where(c_io < r_io, 1.0, 0.0).astype(BF16)
    lane = lax.broadcasted_iota(jnp.int32, (th, N_EXPERTS), 1)
    lane_k = lax.broadcasted_iota(jnp.int32, (th, TOP_K), 1)
    wr = wr_ref[...].astype(BF16)
    carry = carry_ref[...]
    for part in range(POST_SPLIT):
        rows = slice(part * th, (part + 1) * th)
        merged = jnp.where(is_ctx, mc_ref[rows, :], ml_ref[rows, :])
        x = jnp.where(is_ctx, xc_ref[rows, :], xl_ref[rows, :])
        o = jnp.dot(merged, wo_ref[...], preferred_element_type=F32)
        x1 = x + gt1_ref[...] * (o * _rms_scale(o) * gpm_ref[...])
        x1_ref[rows, :] = x1
        h2 = (x1 * _rms_scale(x1) * gpf_ref[...]) * (1.0 + sc2_ref[...]) + sh2_ref[...]
        h2_bf = h2.astype(BF16)
        bits = lax.bitcast_convert_type(h2_bf.astype(F32), jnp.uint32)
        words = (lax.shift_right_logical(bits[:, :HALF_D], jnp.uint32(16))
                 | (bits[:, HALF_D:] & jnp.uint32(0xFFFF0000)))
        for c in range(WORD_ROWS):
            h2_ref[pl.ds(part * th * WORD_ROWS + c, th, stride=WORD_ROWS), :] = words[:, c * 128:(c + 1) * 128]

        logits = jnp.dot(h2_bf, wr, preferred_element_type=F32) + br_ref[...]
        work = logits
        chosen = jnp.zeros((th, N_EXPERTS), F32)
        sels, vals, idxs = [], [], []
        for _ in range(TOP_K):
            mx = jnp.max(work, axis=-1, keepdims=True)
            idx = jnp.min(jnp.where(work == mx, lane, N_EXPERTS), axis=-1, keepdims=True)
            sel = lane == idx
            work = jnp.where(sel, -jnp.inf, work)
            chosen = jnp.where(sel, 1.0, chosen)
            sels.append(sel)
            vals.append(mx)
            idxs.append(idx)
        exps = [jnp.exp(v - vals[0]) for v in vals]
        inv = 1.0 / (exps[0] + exps[1] + exps[2] + exps[3])

        before = jnp.dot(lower, chosen.astype(BF16), preferred_element_type=F32) + carry
        carry = carry + jnp.sum(chosen, axis=0, keepdims=True)

        e_out = jnp.zeros((th, TOP_K), jnp.int32)
        g_out = jnp.zeros((th, TOP_K), F32)
        r_out = jnp.zeros((th, TOP_K), jnp.int32)
        for k in range(TOP_K):
            rk = jnp.sum(jnp.where(sels[k], before, 0.0), axis=-1, keepdims=True).astype(jnp.int32)
            e_out = jnp.where(lane_k == k, idxs[k], e_out)
            g_out = jnp.where(lane_k == k, exps[k] * inv, g_out)
            r_out = jnp.where(lane_k == k, rk, r_out)
        e_ref[rows, :] = e_out
        gate_ref[rows, :] = g_out
        rank_ref[rows, :] = r_out
    carry_ref[...] = carry
    cnt_ref[...] = carry


def post_mix_router(merged_ctx, merged_lat, x_ctx, x_lat, w_out_bf, g_post_mix, gt1, g_pre_ffn, sh2, sc2,
                    w_router, b_router):
    tm = POST_TM
    ctx_map = lambda i: (jnp.minimum(i, POST_CTX_BLOCKS - 1), 0)
    lat_map = lambda i: (jnp.maximum(i - POST_CTX_BLOCKS, 0), 0)
    gmap = lambda i: (_post_group(i), 0, 0)
    row = lambda i: (i, 0)
    const = lambda i: (0, 0)
    vec = pl.BlockSpec((1, D_MODEL), const)
    gvec = pl.BlockSpec((None, 1, D_MODEL), gmap)
    return pl.pallas_call(
        _postmix_body,
        grid=(N_TOK // tm,),
        in_specs=[
            pl.BlockSpec((tm, D_MODEL), ctx_map),
            pl.BlockSpec((tm, D_MODEL), lat_map),
            pl.BlockSpec((tm, D_MODEL), ctx_map),
            pl.BlockSpec((tm, D_MODEL), lat_map),
            pl.BlockSpec((D_MODEL, D_MODEL), const),
            vec, gvec, vec, gvec, gvec,
            pl.BlockSpec((D_MODEL, N_EXPERTS), const),
            pl.BlockSpec((1, N_EXPERTS), const),
        ],
        out_specs=[
            pl.BlockSpec((tm, D_MODEL), row),
            pl.BlockSpec((tm * WORD_ROWS, 128), row),
            pl.BlockSpec((tm, TOP_K), row),
            pl.BlockSpec((tm, TOP_K), row),
            pl.BlockSpec((tm, TOP_K), row),
            pl.BlockSpec((1, N_EXPERTS), const),
        ],
        out_shape=[
            jax.ShapeDtypeStruct((N_TOK, D_MODEL), F32),
            jax.ShapeDtypeStruct((N_TOK * WORD_ROWS, 128), jnp.uint32),
            jax.ShapeDtypeStruct((N_TOK, TOP_K), jnp.int32),
            jax.ShapeDtypeStruct((N_TOK, TOP_K), F32),
            jax.ShapeDtypeStruct((N_TOK, TOP_K), jnp.int32),
            jax.ShapeDtypeStruct((1, N_EXPERTS), F32),
        ],
        scratch_shapes=[pltpu.VMEM((1, N_EXPERTS), F32)],
        compiler_params=_params(("arbitrary",)),
        name="post_mix_router",
    )(merged_ctx, merged_lat, x_ctx, x_lat, w_out_bf, g_post_mix.reshape(1, D_MODEL), gt1,
      g_pre_ffn.reshape(1, D_MODEL), sh2, sc2, w_router, b_router.reshape(1, N_EXPERTS))


GATHER_SHIFT = 4
GATHER_UNROLL = 1 << GATHER_SHIFT
GATHER_PRIORITY = 1


def _unpack_tile(xbuf_ref, slot, i):
    base = pl.multiple_of(i * (ROW_TILE * WORD_ROWS), ROW_TILE * WORD_ROWS)
    lo, hi = [], []
    for c in range(WORD_ROWS):
        words = xbuf_ref[slot, pl.ds(base + c, ROW_TILE, stride=WORD_ROWS), :]
        lo.append(lax.bitcast_convert_type(lax.shift_left(words, jnp.uint32(16)), F32).astype(BF16))
        hi.append(lax.bitcast_convert_type(words & jnp.uint32(0xFFFF0000), F32).astype(BF16))
    return jnp.concatenate(lo + hi, axis=1)


def _for_tiles(n_tiles, body):
    def one(i, _):
        body(i)
        return 0

    lax.fori_loop(0, n_tiles, one, 0)


def _moe_body(exp_ref, row_ref, nsub_ref, nzero_ref, npass_ref, rows_ref, tok_ref, h_ref, wg_ref, wl_ref, wd_ref, bg_ref,
              bl_ref, bd_ref, y_ref, xbuf_ref, act_ref, wg_bf, wl_bf, wd_bf, stage_ref, idx_ref, pend_ref,
              xsem, isem, ysem):
    s = pl.program_id(0)
    j = pl.program_id(1)
    n_pass = npass_ref[0]
    n_sub = nsub_ref[s]
    row_start = row_ref[s]

    def idx_copy(p):
        tile0 = pl.multiple_of(row_ref[p], ROW_TILE) // ROW_TILE
        return pltpu.make_async_copy(tok_ref.at[pl.ds(tile0, SUPER_TILES)], idx_ref.at[p % 2],
                                     isem.at[p % 2])

    def row_groups(p):
        return lax.shift_right_logical(rows_ref[p] + (GATHER_UNROLL - 1), GATHER_SHIFT)

    def gather_rows(p):
        slot = p % 2

        def issue(grp, _):
            first = grp * GATHER_UNROLL
            tile = lax.shift_right_logical(first, 8)
            col = jnp.bitwise_and(first, ROW_TILE - 1)
            for g in range(GATHER_UNROLL):
                t = idx_ref[slot, tile, 0, col + g]
                src = h_ref.at[pl.ds(pl.multiple_of(t * WORD_ROWS, WORD_ROWS), WORD_ROWS), :]
                dst = xbuf_ref.at[slot, pl.ds(pl.multiple_of((first + g) * WORD_ROWS, WORD_ROWS), WORD_ROWS), :]
                pltpu.make_async_copy(src, dst, xsem.at[slot]).start(priority=GATHER_PRIORITY)
            return 0

        lax.fori_loop(0, row_groups(p), issue, 0)

    def wait_rows(p):
        n_grp = row_groups(p)

        @pl.when(n_grp > 0)
        def _():
            n = pl.multiple_of(n_grp * (GATHER_UNROLL * WORD_ROWS), GATHER_UNROLL * WORD_ROWS)
            window = xbuf_ref.at[p % 2, pl.ds(0, n), :]
            pltpu.make_async_copy(window, window, xsem.at[p % 2]).wait()

    @pl.when(jnp.logical_and(s == 0, j == 0))
    def _():
        pend_ref[0] = 0
        pend_ref[1] = 0

    def drain_stage(slot):
        @pl.when(pend_ref[slot] == 1)
        def _():
            pltpu.make_async_copy(stage_ref.at[slot], stage_ref.at[slot], ysem.at[slot]).wait()
            pend_ref[slot] = 0

    @pl.when(jnp.logical_and(s == 0, j == 0))
    def _():
        xbuf_ref[...] = jnp.zeros(xbuf_ref.shape, xbuf_ref.dtype)
        idx_copy(0).start()
        idx_copy(0).wait()
        gather_rows(0)
        idx_copy(1).start()

    @pl.when(j == 0)
    def _():
        wait_rows(s)

    @pl.when(jnp.logical_and(j == 0, s + 1 < n_pass))
    def _():
        idx_copy(s + 1).wait()
        gather_rows(s + 1)

    @pl.when(jnp.logical_and(j == 0, s + 2 < n_pass))
    def _():
        idx_copy(s + 2).start()

    @pl.when(jnp.logical_and(j < N_FF_CHUNKS, n_sub > 0))
    def _():
        wg_bf[...] = wg_ref[...].astype(BF16)
        wl_bf[...] = wl_ref[...].astype(BF16)
        bg = bg_ref[...]
        bl = bl_ref[...]

        def up_tile(i):
            rows = pl.ds(pl.multiple_of(i * ROW_TILE, ROW_TILE), ROW_TILE)
            xt = _unpack_tile(xbuf_ref, s % 2, i)
            glu = jnp.minimum(jnp.dot(xt, wg_bf[...], preferred_element_type=F32) + bg, SWIGLU_LIMIT)
            lin = jnp.clip(jnp.dot(xt, wl_bf[...], preferred_element_type=F32) + bl,
                           -SWIGLU_LIMIT, SWIGLU_LIMIT)
            act = glu * _sigmoid(SWIGLU_ALPHA * glu) * (lin + 1.0)
            act_ref[j, rows, :] = act.astype(BF16)

        _for_tiles(n_sub, up_tile)

    for cc in range(N_FF_CHUNKS):
        @pl.when(jnp.logical_and(j == N_FF_CHUNKS + cc, n_sub > 0))
        def _(cc=cc):
            wd_bf[...] = wd_ref[...].astype(BF16)
            bd = bd_ref[...]

            def out_copy(i, slot):
                dst = y_ref.at[pl.ds(pl.multiple_of(row_start + i * ROW_TILE, ROW_TILE), ROW_TILE),
                               cc * FF_CHUNK:(cc + 1) * FF_CHUNK]
                return pltpu.make_async_copy(stage_ref.at[slot], dst, ysem.at[slot])

            def down_tile(i):
                rows = pl.ds(pl.multiple_of(i * ROW_TILE, ROW_TILE), ROW_TILE)
                slot = i % 2
                drain_stage(slot)
                acc = bd
                for c in range(N_FF_CHUNKS):
                    acc = acc + jnp.dot(act_ref[c, rows, :], wd_bf[c * FF_CHUNK:(c + 1) * FF_CHUNK, :],
                                        preferred_element_type=F32)
                stage_ref[slot] = acc
                out_copy(i, slot).start()
                pend_ref[slot] = 1

            _for_tiles(n_sub, down_tile)

    n_zero = nzero_ref[s]

    @pl.when(jnp.logical_and(j == 0, n_zero > 0))
    def _():
        drain_stage(0)
        stage_ref[0] = jnp.zeros((ROW_TILE, FF_CHUNK), F32)

        def zero_copy(i, cc):
            dst = y_ref.at[pl.ds(pl.multiple_of(row_start + i * ROW_TILE, ROW_TILE), ROW_TILE),
                           cc * FF_CHUNK:(cc + 1) * FF_CHUNK]
            return pltpu.make_async_copy(stage_ref.at[0], dst, ysem.at[0])

        def issue(i, _):
            for cc in range(N_FF_CHUNKS):
                zero_copy(i, cc).start()
            return 0

        def drain(i, _):
            for cc in range(N_FF_CHUNKS):
                zero_copy(i, cc).wait()
            return 0

        lax.fori_loop(0, n_zero, issue, 0)
        lax.fori_loop(0, n_zero, drain, 0)

    @pl.when(jnp.logical_and(s == n_pass - 1, j == 2 * N_FF_CHUNKS - 1))
    def _():
        drain_stage(0)
        drain_stage(1)


def expert_mlp(h_packed, tok_sorted, sched, w_gate_up, b_gate_up, w_down, b_down):
    exp_of, row_of, nsub_of, nzero_of, n_pass, rows_of = sched
    last = N_FF_CHUNKS - 1
    up_of = lambda s, j, n: jnp.where(n[s] > 0, jnp.minimum(j, last), last)
    down_of = lambda s, j, n: jnp.where(n[s] > 0, jnp.maximum(j - N_FF_CHUNKS, 0), last)
    up_chunk = lambda s, j, e, r, n, z, p, c: (e[s], 0, up_of(s, j, n))
    lin_chunk = lambda s, j, e, r, n, z, p, c: (e[s], 0, N_FF_CHUNKS + up_of(s, j, n))
    down_chunk = lambda s, j, e, r, n, z, p, c: (e[s], 0, down_of(s, j, n))
    grid_spec = pltpu.PrefetchScalarGridSpec(
        num_scalar_prefetch=6,
        grid=(n_pass[0], 2 * N_FF_CHUNKS),
        in_specs=[
            pl.BlockSpec(memory_space=pl.ANY),
            pl.BlockSpec(memory_space=pl.ANY),
            pl.BlockSpec((None, D_MODEL, FF_CHUNK), up_chunk),
            pl.BlockSpec((None, D_MODEL, FF_CHUNK), lin_chunk),
            pl.BlockSpec((None, D_FF, FF_CHUNK), down_chunk),
            pl.BlockSpec((None, 1, FF_CHUNK), up_chunk),
            pl.BlockSpec((None, 1, FF_CHUNK), lin_chunk),
            pl.BlockSpec((None, 1, FF_CHUNK), down_chunk),
        ],
        out_specs=pl.BlockSpec(memory_space=pl.ANY),
        scratch_shapes=[
            pltpu.VMEM((2, SUPER_ROWS * WORD_ROWS, 128), jnp.uint32),
            pltpu.VMEM((N_FF_CHUNKS, SUPER_ROWS, FF_CHUNK), BF16),
            pltpu.VMEM((D_MODEL, FF_CHUNK), BF16),
            pltpu.VMEM((D_MODEL, FF_CHUNK), BF16),
            pltpu.VMEM((D_FF, FF_CHUNK), BF16),
            pltpu.VMEM((2, ROW_TILE, FF_CHUNK), F32),
            pltpu.SMEM((2, SUPER_TILES, 1, ROW_TILE), jnp.int32),
            pltpu.SMEM((2,), jnp.int32),
            pltpu.SemaphoreType.DMA((2,)),
            pltpu.SemaphoreType.DMA((2,)),
            pltpu.SemaphoreType.DMA((2,)),
        ],
    )
    tok_tiles = jnp.concatenate([tok_sorted.reshape(N_ROW_TILES, 1, ROW_TILE),
                                 jnp.zeros((SUPER_TILES, 1, ROW_TILE), jnp.int32)], axis=0)
    return pl.pallas_call(
        _moe_body,
        grid_spec=grid_spec,
        out_shape=jax.ShapeDtypeStruct((N_ROWS, D_MODEL), F32),
        compiler_params=_params(("arbitrary", "arbitrary"), vmem=EXPERT_VMEM_LIMIT),
        name="expert_mlp",
    )(exp_of, row_of, nsub_of, nzero_of, n_pass, rows_of, tok_tiles, h_packed, w_gate_up, w_gate_up, w_down,
      b_gate_up.reshape(N_EXPERTS, 1, 2 * D_FF), b_gate_up.reshape(N_EXPERTS, 1, 2 * D_FF),
      b_down.reshape(N_EXPERTS, 1, D_MODEL))


COMB_TB = 256


def _combine_start(y_ref, ybuf_ref, pos_ref, sem):
    def issue(t, _):
        for k in range(TOP_K):
            p = pos_ref[0, 0, t * TOP_K + k]
            pltpu.make_async_copy(y_ref.at[pl.ds(p, 1), :], ybuf_ref.at[k, pl.ds(t, 1), :], sem).start()
        return 0

    lax.fori_loop(0, COMB_TB, issue, 0, unroll=4)


def _combine_body(n, pos_ref, pos_next_ref, y_ref, gate_ref, x1_ref, gt2_ref, g_ref, o_ref, ybuf_ref, sem_ref):
    i = pl.program_id(0)
    slot = i % 2

    @pl.when(i == 0)
    def _():
        _combine_start(y_ref, ybuf_ref.at[0], pos_ref, sem_ref.at[0])

    @pl.when(i + 1 < n)
    def _():
        _combine_start(y_ref, ybuf_ref.at[1 - slot], pos_next_ref, sem_ref.at[1 - slot])

    for k in range(TOP_K):
        pltpu.make_async_copy(y_ref.at[pl.ds(0, COMB_TB), :], ybuf_ref.at[slot, k], sem_ref.at[slot]).wait()
    gates = gate_ref[...]
    ffn = gates[:, 0:1] * ybuf_ref[slot, 0]
    for k in range(1, TOP_K):
        ffn = ffn + gates[:, k:k + 1] * ybuf_ref[slot, k]
    o_ref[...] = x1_ref[...] + gt2_ref[...] * (ffn * _rms_scale(ffn) * g_ref[...])


def combine_residual(y_sorted, pos, gates, x1, gt2, g_post_ffn, row_offset, n_rows, group_of_block):
    tb = COMB_TB
    nblk = n_rows // tb
    off = row_offset // tb
    pos3 = pos.reshape(N_TOK // tb, 1, tb * TOP_K)
    smem_blk = lambda f: pl.BlockSpec((1, 1, tb * TOP_K), f, memory_space=pltpu.SMEM)
    return pl.pallas_call(
        functools.partial(_combine_body, nblk),
        grid=(nblk,),
        in_specs=[
            smem_blk(lambda i: (off + i, 0, 0)),
            smem_blk(lambda i: (off + jnp.minimum(i + 1, nblk - 1), 0, 0)),
            pl.BlockSpec(memory_space=pl.ANY),
            pl.BlockSpec((tb, TOP_K), lambda i: (off + i, 0)),
            pl.BlockSpec((tb, D_MODEL), lambda i: (off + i, 0)),
            pl.BlockSpec((None, 1, D_MODEL), lambda i: (group_of_block(i), 0, 0)),
            pl.BlockSpec((1, D_MODEL), lambda i: (0, 0)),
        ],
        out_specs=pl.BlockSpec((tb, D_MODEL), lambda i: (i, 0)),
        out_shape=jax.ShapeDtypeStruct((n_rows, D_MODEL), F32),
        scratch_shapes=[pltpu.VMEM((2, TOP_K, tb, D_MODEL), F32), pltpu.SemaphoreType.DMA((2,))],
        compiler_params=_params(("arbitrary",)),
        name="combine_residual",
    )(pos3, pos3, y_sorted, gates, x1, gt2, g_post_ffn.reshape(1, D_MODEL))


INV_CHUNK = 4096


def _row_tokens_body(pos_ref, zeros_ref, o_ref, sem):
    i = pl.program_id(0)

    @pl.when(i == 0)
    def _():
        cp = pltpu.make_async_copy(zeros_ref, o_ref, sem)
        cp.start()
        cp.wait()

    base = i * INV_CHUNK

    def put(r, _):
        o_ref[pos_ref[r]] = lax.shift_right_logical(base + r, 2)
        return 0

    lax.fori_loop(0, INV_CHUNK, put, 0, unroll=8)


def row_tokens(pos):
    return pl.pallas_call(
        _row_tokens_body,
        grid=(N_ASSIGN // INV_CHUNK,),
        in_specs=[
            pl.BlockSpec((INV_CHUNK,), lambda i: (i,), memory_space=pltpu.SMEM),
            pl.BlockSpec(memory_space=pl.ANY),
        ],
        out_specs=pl.BlockSpec(memory_space=pltpu.SMEM),
        out_shape=jax.ShapeDtypeStruct((N_ROWS,), jnp.int32),
        scratch_shapes=[pltpu.SemaphoreType.DMA(())],
        compiler_params=_params(("arbitrary",)),
        name="row_tokens",
    )(pos.reshape(N_ASSIGN), jnp.zeros((N_ROWS,), jnp.int32))


def _routing_tables(e_idx, rank, counts_f):
    counts = counts_f.reshape(N_EXPERTS).astype(jnp.int32)
    n_tiles = (counts + ROW_TILE - 1) // ROW_TILE
    padded = n_tiles * ROW_TILE
    pad_end = jnp.cumsum(padded)
    pad_start = pad_end - padded
    pos = (pad_start[e_idx] + rank).astype(jnp.int32)
    tok_sorted = row_tokens(pos)
    n_pass = (n_tiles + SUPER_TILES - 1) // SUPER_TILES
    pass_end = jnp.cumsum(n_pass)
    total = pass_end[-1]
    s = jnp.arange(N_SUPER, dtype=jnp.int32)
    s_eff = jnp.minimum(s, total - 1)
    e_of = jnp.minimum(jnp.searchsorted(pass_end, s_eff, side="right"), N_EXPERTS - 1).astype(jnp.int32)
    local = s_eff - (pass_end[e_of] - n_pass[e_of])
    row_of = pad_start[e_of] + local * SUPER_ROWS
    nsub = jnp.minimum(SUPER_TILES, n_tiles[e_of] - local * SUPER_TILES)
    nsub = jnp.where(s < total, nsub, 0).astype(jnp.int32)
    zero_row = pad_end[-1] + (s - total) * SUPER_ROWS
    nzero = jnp.clip((N_ROWS - zero_row) // ROW_TILE, 0, SUPER_TILES)
    nzero = jnp.where(s >= total, nzero, 0).astype(jnp.int32)
    row_of = jnp.where(s < total, row_of, jnp.minimum(zero_row, N_ROWS - ROW_TILE)).astype(jnp.int32)
    tail_tiles = (N_ROWS - pad_end[-1]) // ROW_TILE
    n_pass = jnp.minimum(total + jnp.maximum((tail_tiles + SUPER_TILES - 1) // SUPER_TILES, 1), N_SUPER)
    n_pass = n_pass.astype(jnp.int32).reshape(1)
    rows_of = jnp.clip(counts[e_of] - local * SUPER_ROWS, 0, SUPER_ROWS)
    rows_of = jnp.where(s < total, rows_of, 0).astype(jnp.int32)
    return pos.astype(jnp.int32), tok_sorted, (e_of, row_of, nsub, nzero, n_pass, rows_of)


def kernel(x_prompt, x_sample, cache_k, cache_v, state_rnn_fwd, state_rnn_bwd, c, c_ctx, w_mod, b_mod, g_pre_mix, w_in, g_q_norm, g_k_norm, conv_w, conv_b, rg_w_a, rg_b_a, rg_w_x, rg_b_x, rg_lambda, w_o_attn, w_o_rnn, w_out, g_post_mix, g_pre_ffn, w_router, b_router, w_gate_up, b_gate_up, w_down, b_down, g_post_ffn):
    l = 0
    x_ctx = x_prompt.reshape(N_CTX, D_MODEL)
    x_lat = x_sample.reshape(N_LAT, D_MODEL)

    cond8 = jnp.concatenate([c_ctx[None, :], c, jnp.zeros((8 - 1 - N_LAT_SEQ, D_MODEL), F32)], axis=0)
    mod = modulation(cond8, w_mod[l], b_mod[l])[:1 + N_LAT_SEQ].reshape(1 + N_LAT_SEQ, 6, 1, D_MODEL)
    sh1, sc1, gt1, sh2, sc2, gt2 = [mod[:, i] for i in range(6)]

    ctx_group = lambda i: 0
    lat_group_1024 = lambda i: 1 + i
    h_ctx = prenorm_modulate(x_ctx, g_pre_mix[l], sh1, sc1, ctx_group, 1024)
    h_lat = prenorm_modulate(x_lat, g_pre_mix[l], sh1, sc1, lat_group_1024, 1024)
    z_ctx = in_projection(h_ctx, w_in[l])
    z_lat = in_projection(h_lat, w_in[l])

    attn_ctx, k_new, v_new = attention_ctx(z_ctx, g_q_norm[l], g_k_norm[l])
    attn_lat = attention_lat(z_lat, cache_k[:, l].reshape(N_LAT_SEQ, PAST_LEN, KV_COLS),
                             cache_v[:, l].reshape(N_LAT_SEQ, PAST_LEN, KV_COLS),
                             _rope_tables(), g_q_norm[l], g_k_norm[l])

    def per_block(w):
        return w.reshape(2, RNN_BLOCKS, 1, RNN_BLOCK_DIM)

    w_gates = jnp.concatenate([rg_w_a[l, 0], rg_w_x[l, 0], rg_w_a[l, 1], rg_w_x[l, 1]], axis=-1).astype(BF16)
    ba, bx = per_block(rg_b_a[l]), per_block(rg_b_x[l])
    b_gates = jnp.concatenate([ba[0], bx[0], ba[1], bx[1]], axis=-1)
    zeros_state = jnp.zeros((N_CTX_SEQ, 1, D_MODEL), F32)
    rnn_ctx, hf_ctx, hb_ctx = rglru_mixer(z_ctx, CTX_LEN, conv_w[l], conv_b[l], w_gates, b_gates,
                                          rg_lambda[l], zeros_state, zeros_state)
    rnn_lat, _, _ = rglru_mixer(z_lat, LAT_LEN, conv_w[l], conv_b[l], w_gates, b_gates, rg_lambda[l],
                                state_rnn_fwd[:, l].reshape(N_LAT_SEQ, 1, D_MODEL),
                                state_rnn_bwd[:, l].reshape(N_LAT_SEQ, 1, D_MODEL))

    merged_ctx = gated_merge(attn_ctx, rnn_ctx, z_ctx, w_o_attn[l], w_o_rnn[l])
    merged_lat = gated_merge(attn_lat, rnn_lat, z_lat, w_o_attn[l], w_o_rnn[l])

    x1, h2, e_idx, gates, rank, counts = post_mix_router(
        merged_ctx, merged_lat, x_ctx, x_lat, w_out[l].astype(BF16), g_post_mix[l], gt1, g_pre_ffn[l],
        sh2, sc2, w_router[l], b_router[l])

    pos, tok_sorted, sched = _routing_tables(e_idx, rank, counts)
    y_sorted = expert_mlp(h2, tok_sorted, sched, w_gate_up[l], b_gate_up[l], w_down[l], b_down[l])

    y_ctx = combine_residual(y_sorted, pos, gates, x1, gt2, g_post_ffn[l], 0, N_CTX, ctx_group)
    y_lat = combine_residual(y_sorted, pos, gates, x1, gt2, g_post_ffn[l], N_CTX, N_LAT,
                             lambda i: 1 + i // (LAT_LEN // COMB_TB))

    return (y_ctx.reshape(N_CTX_SEQ, CTX_LEN, D_MODEL),
            y_lat.reshape(N_LAT_SEQ, LAT_LEN, D_MODEL),
            k_new.reshape(N_CTX_SEQ, 1, CTX_LEN, N_KV_HEADS, HEAD_DIM),
            v_new.reshape(N_CTX_SEQ, 1, CTX_LEN, N_KV_HEADS, HEAD_DIM),
            hf_ctx,
            hb_ctx)
```

```python
import functools

import jax
import jax.numpy as jnp
import numpy as np
from jax import lax
from jax.experimental import pallas as pl
from jax.experimental.pallas import tpu as pltpu

D_MODEL = 2048
N_CTX_SEQ = 32
CTX_LEN = 256
N_LAT_SEQ = 2
LAT_LEN = 1024
PAST_LEN = 512
N_CTX = N_CTX_SEQ * CTX_LEN
N_LAT = N_LAT_SEQ * LAT_LEN
N_TOK = N_CTX + N_LAT
GRID_W = 64
N_HEADS = 16
N_KV_HEADS = 4
HEAD_DIM = 128
KV_GROUP = N_HEADS // N_KV_HEADS
ROPE_THETA = 10000.0
RNN_BLOCKS = 16
RNN_BLOCK_DIM = 128
RG_C = 8.0
N_EXPERTS = 32
TOP_K = 4
D_FF = 2048
SWIGLU_LIMIT = 7.0
SWIGLU_ALPHA = 1.702
EPS = 1e-6
Q_COLS = N_HEADS * HEAD_DIM
KV_COLS = N_KV_HEADS * HEAD_DIM
IN_COLS = Q_COLS + 2 * KV_COLS + 4 * D_MODEL
COL_K = Q_COLS
COL_XR = Q_COLS + 2 * KV_COLS
COL_YR = COL_XR + D_MODEL
COL_GA = COL_YR + D_MODEL
COL_GR = COL_GA + D_MODEL

V7X_VMEM_BYTES = 64 * 1024 * 1024
VMEM_LIMIT = 56 * 1024 * 1024
EXPERT_VMEM_LIMIT = 60 * 1024 * 1024

ROW_TILE = 256
SUPER_TILES = 8
SUPER_ROWS = ROW_TILE * SUPER_TILES
N_ASSIGN = N_TOK * TOP_K
N_ROWS = N_ASSIGN + N_EXPERTS * ROW_TILE
N_ROW_TILES = N_ROWS // ROW_TILE
N_SUPER = N_ROW_TILES // SUPER_TILES + N_EXPERTS
FF_CHUNK = 512
N_FF_CHUNKS = D_FF // FF_CHUNK

BF16 = jnp.bfloat16
F32 = jnp.float32


def _params(semantics, vmem=VMEM_LIMIT):
    return pltpu.CompilerParams(dimension_semantics=semantics, vmem_limit_bytes=vmem)


def _rms_scale(x):
    return lax.rsqrt(jnp.mean(x * x, axis=-1, keepdims=True) + EPS)


def _sigmoid(x):
    return 1.0 / (1.0 + jnp.exp(-x))


def _mod_body(c_ref, w_ref, b_ref, o_ref):
    c = c_ref[...]
    a = (c * _sigmoid(c)).astype(BF16)
    o_ref[...] = jnp.dot(a, w_ref[...].astype(BF16), preferred_element_type=F32) + b_ref[...]


def modulation(cond8, w_mod, b_mod):
    tn = 1024
    n = w_mod.shape[1]
    return pl.pallas_call(
        _mod_body,
        grid=(n // tn,),
        in_specs=[
            pl.BlockSpec((8, D_MODEL), lambda j: (0, 0)),
            pl.BlockSpec((D_MODEL, tn), lambda j: (0, j)),
            pl.BlockSpec((1, tn), lambda j: (0, j)),
        ],
        out_specs=pl.BlockSpec((8, tn), lambda j: (0, j)),
        out_shape=jax.ShapeDtypeStruct((8, n), F32),
        compiler_params=_params(("arbitrary",)),
        name="modulation",
    )(cond8, w_mod, b_mod.reshape(1, n))


def _prenorm_body(x_ref, g_ref, sh_ref, sc_ref, o_ref):
    x = x_ref[...]
    y = x * _rms_scale(x) * g_ref[...]
    o_ref[...] = (y * (1.0 + sc_ref[...]) + sh_ref[...]).astype(o_ref.dtype)


def prenorm_modulate(x, g, shift, scale, group_of_block, tm):
    m = x.shape[0]
    gmap = lambda i: (group_of_block(i), 0, 0)
    return pl.pallas_call(
        _prenorm_body,
        grid=(m // tm,),
        in_specs=[
            pl.BlockSpec((tm, D_MODEL), lambda i: (i, 0)),
            pl.BlockSpec((1, D_MODEL), lambda i: (0, 0)),
            pl.BlockSpec((None, 1, D_MODEL), gmap),
            pl.BlockSpec((None, 1, D_MODEL), gmap),
        ],
        out_specs=pl.BlockSpec((tm, D_MODEL), lambda i: (i, 0)),
        out_shape=jax.ShapeDtypeStruct((m, D_MODEL), BF16),
        compiler_params=_params(("arbitrary",)),
        name="prenorm_modulate",
    )(x, g.reshape(1, D_MODEL), shift, scale)


def _inproj_body(h_ref, w_ref, o_ref, wbf_ref):
    @pl.when(pl.program_id(1) == 0)
    def _():
        wbf_ref[...] = w_ref[...].astype(BF16)

    o_ref[...] = jnp.dot(h_ref[...], wbf_ref[...], preferred_element_type=F32)


def in_projection(h, w_in):
    m = h.shape[0]
    tm, tn = 1024, 1024
    return pl.pallas_call(
        _inproj_body,
        grid=(IN_COLS // tn, m // tm),
        in_specs=[
            pl.BlockSpec((tm, D_MODEL), lambda j, i: (i, 0)),
            pl.BlockSpec((D_MODEL, tn), lambda j, i: (0, j)),
        ],
        out_specs=pl.BlockSpec((tm, tn), lambda j, i: (i, j)),
        out_shape=jax.ShapeDtypeStruct((m, IN_COLS), F32),
        scratch_shapes=[pltpu.VMEM((D_MODEL, tn), BF16)],
        compiler_params=_params(("arbitrary", "arbitrary")),
        name="in_projection",
    )(h, w_in)


def _rope(x, cos, sin_lo, sin_hi):
    return x * cos + pltpu.roll(x, 96, 1) * sin_lo + pltpu.roll(x, 32, 1) * sin_hi


def _head_norm(x, g):
    return x * _rms_scale(x) * g


def _softmax_pv(score_blocks, value_blocks):
    m = None
    for s in score_blocks:
        mi = jnp.max(s, axis=-1, keepdims=True)
        m = mi if m is None else jnp.maximum(m, mi)
    ps = [jnp.exp(s - m) for s in score_blocks]
    denom = None
    for p in ps:
        li = jnp.sum(p, axis=-1, keepdims=True)
        denom = li if denom is None else denom + li
    out = None
    for p, v in zip(ps, value_blocks):
        o = jnp.dot(p.astype(BF16), v, preferred_element_type=F32)
        out = o if out is None else out + o
    return out * (1.0 / denom)


def _attn_ctx_body(q_ref, kv_ref, gq_ref, gk_ref, o_ref, ko_ref, vo_ref):
    tq = q_ref.shape[0]
    scale = HEAD_DIM ** -0.5
    gq = gq_ref[...]
    gk = gk_ref[...]
    for g in range(N_KV_HEADS):
        kcols = slice(g * HEAD_DIM, (g + 1) * HEAD_DIM)
        kn = _head_norm(kv_ref[:, kcols], gk)
        v = kv_ref[:, KV_COLS + g * HEAD_DIM:KV_COLS + (g + 1) * HEAD_DIM]
        ko_ref[pl.ds(g, tq, stride=N_KV_HEADS), :] = kn
        vo_ref[pl.ds(g, tq, stride=N_KV_HEADS), :] = v
        qs = []
        for hh in range(KV_GROUP):
            h = g * KV_GROUP + hh
            qs.append((_head_norm(q_ref[:, h * HEAD_DIM:(h + 1) * HEAD_DIM], gq) * scale).astype(BF16))
        q4 = jnp.concatenate(qs, axis=0)
        s = lax.dot_general(q4, kn.astype(BF16), (((1,), (1,)), ((), ())),
                            preferred_element_type=F32)
        o = _softmax_pv([s], [v.astype(BF16)])
        for hh in range(KV_GROUP):
            h = g * KV_GROUP + hh
            o_ref[:, h * HEAD_DIM:(h + 1) * HEAD_DIM] = o[hh * tq:(hh + 1) * tq].astype(o_ref.dtype)


def attention_ctx(z, g_q, g_k):
    nb = N_CTX_SEQ
    t = CTX_LEN
    return pl.pallas_call(
        _attn_ctx_body,
        grid=(nb,),
        in_specs=[
            pl.BlockSpec((t, Q_COLS), lambda b: (b, 0)),
            pl.BlockSpec((t, 2 * KV_COLS), lambda b: (b, COL_K // (2 * KV_COLS))),
            pl.BlockSpec((1, HEAD_DIM), lambda b: (0, 0)),
            pl.BlockSpec((1, HEAD_DIM), lambda b: (0, 0)),
        ],
        out_specs=[
            pl.BlockSpec((t, Q_COLS), lambda b: (b, 0)),
            pl.BlockSpec((t * N_KV_HEADS, HEAD_DIM), lambda b: (b, 0)),
            pl.BlockSpec((t * N_KV_HEADS, HEAD_DIM), lambda b: (b, 0)),
        ],
        out_shape=[
            jax.ShapeDtypeStruct((N_CTX, Q_COLS), BF16),
            jax.ShapeDtypeStruct((N_CTX * N_KV_HEADS, HEAD_DIM), F32),
            jax.ShapeDtypeStruct((N_CTX * N_KV_HEADS, HEAD_DIM), F32),
        ],
        compiler_params=_params(("arbitrary",)),
        name="attention_ctx",
    )(z, z, g_q.reshape(1, HEAD_DIM), g_k.reshape(1, HEAD_DIM))


def _attn_lat_body(q_ref, kv_ref, ck_ref, cv_ref, cos_ref, slo_ref, shi_ref, gq_ref, gk_ref,
                   o_ref, kr_ref):
    tq = q_ref.shape[0]
    qb = pl.program_id(1)
    scale = HEAD_DIM ** -0.5
    gq = gq_ref[...]

    @pl.when(qb == 0)
    def _():
        gk = gk_ref[...]
        for g in range(N_KV_HEADS):
            kcols = slice(g * HEAD_DIM, (g + 1) * HEAD_DIM)
            kn = _head_norm(kv_ref[:, kcols], gk)
            kr_ref[:, kcols] = _rope(kn, cos_ref[...], slo_ref[...], shi_ref[...]).astype(BF16)

    row0 = pl.multiple_of(qb * tq, tq)
    cos = cos_ref[pl.ds(row0, tq), :]
    slo = slo_ref[pl.ds(row0, tq), :]
    shi = shi_ref[pl.ds(row0, tq), :]
    for g in range(N_KV_HEADS):
        kcols = slice(g * HEAD_DIM, (g + 1) * HEAD_DIM)
        qs = []
        for hh in range(KV_GROUP):
            h = g * KV_GROUP + hh
            qn = _head_norm(q_ref[:, h * HEAD_DIM:(h + 1) * HEAD_DIM], gq)
            qs.append((_rope(qn, cos, slo, shi) * scale).astype(BF16))
        q4 = jnp.concatenate(qs, axis=0)
        dn = (((1,), (1,)), ((), ()))
        s_past = lax.dot_general(q4, ck_ref[:, kcols].astype(BF16), dn, preferred_element_type=F32)
        s_new = lax.dot_general(q4, kr_ref[:, kcols], dn, preferred_element_type=F32)
        v_past = cv_ref[:, kcols].astype(BF16)
        v_new = kv_ref[:, KV_COLS + g * HEAD_DIM:KV_COLS + (g + 1) * HEAD_DIM].astype(BF16)
        o = _softmax_pv([s_past, s_new], [v_past, v_new])
        for hh in range(KV_GROUP):
            h = g * KV_GROUP + hh
            o_ref[:, h * HEAD_DIM:(h + 1) * HEAD_DIM] = o[hh * tq:(hh + 1) * tq].astype(o_ref.dtype)


def attention_lat(z, cache_k, cache_v, rope_tabs, g_q, g_k):
    tq = 256
    nq = LAT_LEN // tq
    cos, slo, shi = rope_tabs
    tab = pl.BlockSpec((LAT_LEN, HEAD_DIM), lambda b, q: (0, 0))
    return pl.pallas_call(
        _attn_lat_body,
        grid=(N_LAT_SEQ, nq),
        in_specs=[
            pl.BlockSpec((tq, Q_COLS), lambda b, q: (b * nq + q, 0)),
            pl.BlockSpec((LAT_LEN, 2 * KV_COLS), lambda b, q: (b, COL_K // (2 * KV_COLS))),
            pl.BlockSpec((None, PAST_LEN, KV_COLS), lambda b, q: (b, 0, 0)),
            pl.BlockSpec((None, PAST_LEN, KV_COLS), lambda b, q: (b, 0, 0)),
            tab, tab, tab,
            pl.BlockSpec((1, HEAD_DIM), lambda b, q: (0, 0)),
            pl.BlockSpec((1, HEAD_DIM), lambda b, q: (0, 0)),
        ],
        out_specs=pl.BlockSpec((tq, Q_COLS), lambda b, q: (b * nq + q, 0)),
        out_shape=jax.ShapeDtypeStruct((N_LAT, Q_COLS), BF16),
        scratch_shapes=[pltpu.VMEM((LAT_LEN, KV_COLS), BF16)],
        compiler_params=_params(("arbitrary", "arbitrary")),
        name="attention_lat",
    )(z, z, cache_k, cache_v, cos, slo, shi, g_q.reshape(1, HEAD_DIM), g_k.reshape(1, HEAD_DIM))


def _rope_tables():
    t = np.arange(LAT_LEN)
    row = jnp.asarray(t // GRID_W, F32)
    col = jnp.asarray(t % GRID_W, F32)
    nf = HEAD_DIM // 4
    inv_freq = ROPE_THETA ** (-jnp.arange(nf, dtype=F32) / nf)
    ang_row = row[:, None] * inv_freq[None, :]
    ang_col = col[:, None] * inv_freq[None, :]
    ang = jnp.concatenate([ang_row, ang_row, ang_col, ang_col], axis=1)
    cos = jnp.cos(ang)
    sin = jnp.sin(ang)
    first = jnp.asarray((np.arange(HEAD_DIM) % (2 * nf)) < nf)[None, :]
    return cos, jnp.where(first, -sin, 0.0), jnp.where(first, 0.0, sin)


RNN_ROWS = 2048
RNN_COLS = 512
RNN_SUB = RNN_COLS // RNN_BLOCK_DIM


def _gelu_tanh(y):
    return 0.5 * y * (1.0 + jnp.tanh(0.7978845608028654 * (y + 0.044715 * (y * y * y))))


def _rglru_body(seq_len, xr_ref, yr_ref, cw_ref, cb_ref, wg_ref, bg_ref, lam_ref, h0f_ref, h0b_ref,
                o_ref, hf_ref, hb_ref, xs_ref, af_ref, bf_ref, ab_ref, bb_ref):
    n_seq = RNN_ROWS // seq_len
    for n in range(RNN_SUB):
        cols = slice(n * RNN_BLOCK_DIM, (n + 1) * RNN_BLOCK_DIM)
        for s in range(n_seq):
            xs_ref[n, pl.ds(s, seq_len, stride=n_seq), :] = xr_ref[s * seq_len:(s + 1) * seq_len, cols]

    row = lax.broadcasted_iota(jnp.int32, (RNN_ROWS, 1), 0)
    lam = lam_ref[...]
    softplus_neg = jnp.maximum(-lam, 0.0) + jnp.log(1.0 + jnp.exp(-jnp.abs(lam)))
    rate = softplus_neg * (-RG_C * 1.4426950408889634)
    for n in range(RNN_SUB):
        cols = slice(n * RNN_BLOCK_DIM, (n + 1) * RNN_BLOCK_DIM)
        x = xs_ref[n]
        x_m1 = jnp.where(row >= n_seq, pltpu.roll(x, n_seq, 0), 0.0)
        x_p1 = jnp.where(row < RNN_ROWS - n_seq, pltpu.roll(x, RNN_ROWS - n_seq, 0), 0.0)
        x_p2 = jnp.where(row < RNN_ROWS - 2 * n_seq, pltpu.roll(x, RNN_ROWS - 2 * n_seq, 0), 0.0)
        xn = (cb_ref[:, cols] + x_m1 * cw_ref[0:1, cols] + x * cw_ref[1:2, cols]
              + x_p1 * cw_ref[2:3, cols] + x_p2 * cw_ref[3:4, cols])
        pre = jnp.dot(xn.astype(BF16), wg_ref[n], preferred_element_type=F32) + bg_ref[n]
        for d, (a_ref, b_ref) in enumerate(((af_ref, bf_ref), (ab_ref, bb_ref))):
            r = 0.5 * jnp.tanh(0.5 * pre[:, (2 * d) * RNN_BLOCK_DIM:(2 * d + 1) * RNN_BLOCK_DIM]) + 0.5
            gate_in = 0.5 * jnp.tanh(
                0.5 * pre[:, (2 * d + 1) * RNN_BLOCK_DIM:(2 * d + 2) * RNN_BLOCK_DIM]) + 0.5
            a = jnp.exp2(r * rate[d:d + 1, cols])
            v = 1.0 - a * a
            a_ref[n] = a
            b_ref[n] = (v * lax.rsqrt(jnp.maximum(v, 1e-30))) * (gate_in * xn)

    def step(t, carry):
        rows_f = pl.ds(pl.multiple_of(t * n_seq, n_seq), n_seq)
        rows_b = pl.ds(pl.multiple_of((seq_len - 1 - t) * n_seq, n_seq), n_seq)
        out = []
        for n in range(RNN_SUB):
            hf = af_ref[n, rows_f, :] * carry[2 * n] + bf_ref[n, rows_f, :]
            hb = ab_ref[n, rows_b, :] * carry[2 * n + 1] + bb_ref[n, rows_b, :]
            bf_ref[n, rows_f, :] = hf
            bb_ref[n, rows_b, :] = hb
            out += [hf, hb]
        return tuple(out)

    init = []
    for n in range(RNN_SUB):
        cols = slice(n * RNN_BLOCK_DIM, (n + 1) * RNN_BLOCK_DIM)
        init += [h0f_ref[:, 0, cols], h0b_ref[:, 0, cols]]
    last = lax.fori_loop(0, seq_len, step, tuple(init), unroll=8)
    for n in range(RNN_SUB):
        cols = slice(n * RNN_BLOCK_DIM, (n + 1) * RNN_BLOCK_DIM)
        hf_ref[:, 0, cols] = last[2 * n]
        hb_ref[:, 0, cols] = last[2 * n + 1]
        bf_ref[n] = bf_ref[n] + bb_ref[n]
        for s in range(n_seq):
            rows = slice(s * seq_len, (s + 1) * seq_len)
            h_sum = bf_ref[n, pl.ds(s, seq_len, stride=n_seq), :]
            o_ref[rows, cols] = (h_sum * _gelu_tanh(yr_ref[rows, cols])).astype(o_ref.dtype)


def rglru_mixer(z, seq_len, conv_w, conv_b, w_gates, b_gates, lam, h0_f, h0_b):
    m = z.shape[0]
    n_seq_total = m // seq_len
    n_seq = RNN_ROWS // seq_len
    cblk = lambda base: (lambda r, c: (r, base // RNN_COLS + c))
    state_spec = pl.BlockSpec((n_seq, 1, RNN_COLS), lambda r, c: (r, 0, c))
    return pl.pallas_call(
        functools.partial(_rglru_body, seq_len),
        grid=(m // RNN_ROWS, D_MODEL // RNN_COLS),
        in_specs=[
            pl.BlockSpec((RNN_ROWS, RNN_COLS), cblk(COL_XR)),
            pl.BlockSpec((RNN_ROWS, RNN_COLS), cblk(COL_YR)),
            pl.BlockSpec((4, RNN_COLS), lambda r, c: (0, c)),
            pl.BlockSpec((1, RNN_COLS), lambda r, c: (0, c)),
            pl.BlockSpec((RNN_SUB, RNN_BLOCK_DIM, 4 * RNN_BLOCK_DIM), lambda r, c: (c, 0, 0)),
            pl.BlockSpec((RNN_SUB, 1, 4 * RNN_BLOCK_DIM), lambda r, c: (c, 0, 0)),
            pl.BlockSpec((2, RNN_COLS), lambda r, c: (0, c)),
            state_spec, state_spec,
        ],
        out_specs=[
            pl.BlockSpec((RNN_ROWS, RNN_COLS), lambda r, c: (r, c)),
            state_spec, state_spec,
        ],
        out_shape=[
            jax.ShapeDtypeStruct((m, D_MODEL), BF16),
            jax.ShapeDtypeStruct((n_seq_total, 1, D_MODEL), F32),
            jax.ShapeDtypeStruct((n_seq_total, 1, D_MODEL), F32),
        ],
        scratch_shapes=[pltpu.VMEM((RNN_SUB, RNN_ROWS, RNN_BLOCK_DIM), F32) for _ in range(5)],
        compiler_params=_params(("arbitrary", "arbitrary")),
        name="rglru_mixer_t%d" % seq_len,
    )(z, z, conv_w, conv_b.reshape(1, D_MODEL), w_gates, b_gates, lam, h0_f, h0_b)


def _merge_body(a_ref, r_ref, wa_ref, wr_ref, ga_ref, gr_ref, o_ref, wa_bf, wr_bf):
    @pl.when(pl.program_id(1) == 0)
    def _():
        wa_bf[...] = wa_ref[...].astype(BF16)
        wr_bf[...] = wr_ref[...].astype(BF16)

    pa = jnp.dot(a_ref[...], wa_bf[...], preferred_element_type=F32)
    pr = jnp.dot(r_ref[...], wr_bf[...], preferred_element_type=F32)
    o_ref[...] = (_sigmoid(ga_ref[...]) * pa + _sigmoid(gr_ref[...]) * pr).astype(o_ref.dtype)


def gated_merge(attn, rnn, z, w_o_attn, w_o_rnn):
    m = attn.shape[0]
    tm, tn = 1024, 512
    return pl.pallas_call(
        _merge_body,
        grid=(D_MODEL // tn, m // tm),
        in_specs=[
            pl.BlockSpec((tm, Q_COLS), lambda j, i: (i, 0)),
            pl.BlockSpec((tm, D_MODEL), lambda j, i: (i, 0)),
            pl.BlockSpec((Q_COLS, tn), lambda j, i: (0, j)),
            pl.BlockSpec((D_MODEL, tn), lambda j, i: (0, j)),
            pl.BlockSpec((tm, tn), lambda j, i: (i, COL_GA // tn + j)),
            pl.BlockSpec((tm, tn), lambda j, i: (i, COL_GR // tn + j)),
        ],
        out_specs=pl.BlockSpec((tm, tn), lambda j, i: (i, j)),
        out_shape=jax.ShapeDtypeStruct((m, D_MODEL), BF16),
        scratch_shapes=[pltpu.VMEM((Q_COLS, tn), BF16), pltpu.VMEM((D_MODEL, tn), BF16)],
        compiler_params=_params(("arbitrary", "arbitrary")),
        name="gated_merge",
    )(attn, rnn, w_o_attn, w_o_rnn, z, z)


POST_TM = 512
POST_SPLIT = 1
HALF_D = D_MODEL // 2
WORD_ROWS = HALF_D // 128
SUBLANES = 8
POST_CTX_BLOCKS = N_CTX // POST_TM
LAT_BLOCKS_PER_SEQ = LAT_LEN // POST_TM


def _post_group(i):
    return jnp.where(i < POST_CTX_BLOCKS, 0, 1 + (i - POST_CTX_BLOCKS) // LAT_BLOCKS_PER_SEQ)


def _postmix_body(mc_ref, ml_ref, xc_ref, xl_ref, wo_ref, gpm_ref, gt1_ref, gpf_ref, sh2_ref, sc2_ref,
                  wr_ref, br_ref, x1_ref, h2_ref, e_ref, gate_ref, rank_ref, cnt_ref, carry_ref):
    i = pl.program_id(0)
    tm = POST_TM

    @pl.when(i == 0)
    def _():
        carry_ref[...] = jnp.zeros_like(carry_ref)

    is_ctx = i < POST_CTX_BLOCKS
    th = tm // POST_SPLIT
    r_io = lax.broadcasted_iota(jnp.int32, (th, th), 0)
    c_io = lax.broadcasted_iota(jnp.int32, (th, th), 1)
    lower = jnp.where(c_io < r_io, 1.0, 0.0).astype(BF16)
    lane = lax.broadcasted_iota(jnp.int32, (th, N_EXPERTS), 1)
    lane_k = lax.broadcasted_iota(jnp.int32, (th, TOP_K), 1)
    wr = wr_ref[...].astype(BF16)
    carry = carry_ref[...]
    for part in range(POST_SPLIT):
        rows = slice(part * th, (part + 1) * th)
        merged = jnp.where(is_ctx, mc_ref[rows, :], ml_ref[rows, :])
        x = jnp.where(is_ctx, xc_ref[rows, :], xl_ref[rows, :])
        o = jnp.dot(merged, wo_ref[...], preferred_element_type=F32)
        x1 = x + gt1_ref[...] * (o * _rms_scale(o) * gpm_ref[...])
        x1_ref[rows, :] = x1
        h2 = (x1 * _rms_scale(x1) * gpf_ref[...]) * (1.0 + sc2_ref[...]) + sh2_ref[...]
        h2_bf = h2.astype(BF16)
        bits = lax.bitcast_convert_type(h2_bf.astype(F32), jnp.uint32)
        words = (lax.shift_right_logical(bits[:, :HALF_D], jnp.uint32(16))
                 | (bits[:, HALF_D:] & jnp.uint32(0xFFFF0000)))
        for c in range(WORD_ROWS):
            h2_ref[pl.ds(part * th * WORD_ROWS + c, th, stride=WORD_ROWS), :] = words[:, c * 128:(c + 1) * 128]

        logits = jnp.dot(h2_bf, wr, preferred_element_type=F32) + br_ref[...]
        work = logits
        chosen = jnp.zeros((th, N_EXPERTS), F32)
        sels, vals, idxs = [], [], []
        for _ in range(TOP_K):
            mx = jnp.max(work, axis=-1, keepdims=True)
            idx = jnp.min(jnp.where(work == mx, lane, N_EXPERTS), axis=-1, keepdims=True)
            sel = lane == idx
            work = jnp.where(sel, -jnp.inf, work)
            chosen = jnp.where(sel, 1.0, chosen)
            sels.append(sel)
            vals.append(mx)
            idxs.append(idx)
        exps = [jnp.exp(v - vals[0]) for v in vals]
        inv = 1.0 / (exps[0] + exps[1] + exps[2] + exps[3])

        before = jnp.dot(lower, chosen.astype(BF16), preferred_element_type=F32) + carry
        carry = carry + jnp.sum(chosen, axis=0, keepdims=True)

        e_out = jnp.zeros((th, TOP_K), jnp.int32)
        g_out = jnp.zeros((th, TOP_K), F32)
        r_out = jnp.zeros((th, TOP_K), jnp.int32)
        for k in range(TOP_K):
            rk = jnp.sum(jnp.where(sels[k], before, 0.0), axis=-1, keepdims=True).astype(jnp.int32)
            e_out = jnp.where(lane_k == k, idxs[k], e_out)
            g_out = jnp.where(lane_k == k, exps[k] * inv, g_out)
            r_out = jnp.where(lane_k == k, rk, r_out)
        e_ref[rows, :] = e_out
        gate_ref[rows, :] = g_out
        rank_ref[rows, :] = r_out
    carry_ref[...] = carry
    cnt_ref[...] = carry


def post_mix_router(merged_ctx, merged_lat, x_ctx, x_lat, w_out_bf, g_post_mix, gt1, g_pre_ffn, sh2, sc2,
                    w_router, b_router):
    tm = POST_TM
    ctx_map = lambda i: (jnp.minimum(i, POST_CTX_BLOCKS - 1), 0)
    lat_map = lambda i: (jnp.maximum(i - POST_CTX_BLOCKS, 0), 0)
    gmap = lambda i: (_post_group(i), 0, 0)
    row = lambda i: (i, 0)
    const = lambda i: (0, 0)
    vec = pl.BlockSpec((1, D_MODEL), const)
    gvec = pl.BlockSpec((None, 1, D_MODEL), gmap)
    return pl.pallas_call(
        _postmix_body,
        grid=(N_TOK // tm,),
        in_specs=[
            pl.BlockSpec((tm, D_MODEL), ctx_map),
            pl.BlockSpec((tm, D_MODEL), lat_map),
            pl.BlockSpec((tm, D_MODEL), ctx_map),
            pl.BlockSpec((tm, D_MODEL), lat_map),
            pl.BlockSpec((D_MODEL, D_MODEL), const),
            vec, gvec, vec, gvec, gvec,
            pl.BlockSpec((D_MODEL, N_EXPERTS), const),
            pl.BlockSpec((1, N_EXPERTS), const),
        ],
        out_specs=[
            pl.BlockSpec((tm, D_MODEL), row),
            pl.BlockSpec((tm * WORD_ROWS, 128), row),
            pl.BlockSpec((tm, TOP_K), row),
            pl.BlockSpec((tm, TOP_K), row),
            pl.BlockSpec((tm, TOP_K), row),
            pl.BlockSpec((1, N_EXPERTS), const),
        ],
        out_shape=[
            jax.ShapeDtypeStruct((N_TOK, D_MODEL), F32),
            jax.ShapeDtypeStruct((N_TOK * WORD_ROWS, 128), jnp.uint32),
            jax.ShapeDtypeStruct((N_TOK, TOP_K), jnp.int32),
            jax.ShapeDtypeStruct((N_TOK, TOP_K), F32),
            jax.ShapeDtypeStruct((N_TOK, TOP_K), jnp.int32),
            jax.ShapeDtypeStruct((1, N_EXPERTS), F32),
        ],
        scratch_shapes=[pltpu.VMEM((1, N_EXPERTS), F32)],
        compiler_params=_params(("arbitrary",)),
        name="post_mix_router",
    )(merged_ctx, merged_lat, x_ctx, x_lat, w_out_bf, g_post_mix.reshape(1, D_MODEL), gt1,
      g_pre_ffn.reshape(1, D_MODEL), sh2, sc2, w_router, b_router.reshape(1, N_EXPERTS))


GATHER_SHIFT = 4
GATHER_UNROLL = 1 << GATHER_SHIFT
GATHER_PRIORITY = 1


def _unpack_tile(xbuf_ref, slot, i):
    base = pl.multiple_of(i * (ROW_TILE * WORD_ROWS), ROW_TILE * WORD_ROWS)
    lo, hi = [], []
    for c in range(WORD_ROWS):
        words = xbuf_ref[slot, pl.ds(base + c, ROW_TILE, stride=WORD_ROWS), :]
        lo.append(lax.bitcast_convert_type(lax.shift_left(words, jnp.uint32(16)), F32).astype(BF16))
        hi.append(lax.bitcast_convert_type(words & jnp.uint32(0xFFFF0000), F32).astype(BF16))
    return jnp.concatenate(lo + hi, axis=1)


def _for_tiles(n_tiles, body):
    def one(i, _):
        body(i)
        return 0

    lax.fori_loop(0, n_tiles, one, 0)


def _moe_body(exp_ref, row_ref, nsub_ref, nzero_ref, npass_ref, rows_ref, tok_ref, h_ref, wg_ref, wl_ref, wd_ref, bg_ref,
              bl_ref, bd_ref, y_ref, xbuf_ref, act_ref, wg_bf, wl_bf, wd_bf, stage_ref, idx_ref, pend_ref,
              xsem, isem, ysem):
    s = pl.program_id(0)
    j = pl.program_id(1)
    n_pass = npass_ref[0]
    n_sub = nsub_ref[s]
    row_start = row_ref[s]

    def idx_copy(p):
        tile0 = pl.multiple_of(row_ref[p], ROW_TILE) // ROW_TILE
        return pltpu.make_async_copy(tok_ref.at[pl.ds(tile0, SUPER_TILES)], idx_ref.at[p % 2],
                                     isem.at[p % 2])

    def row_groups(p):
        return lax.shift_right_logical(rows_ref[p] + (GATHER_UNROLL - 1), GATHER_SHIFT)

    def gather_rows(p):
        slot = p % 2

        def issue(grp, _):
            first = grp * GATHER_UNROLL
            tile = lax.shift_right_logical(first, 8)
            col = jnp.bitwise_and(first, ROW_TILE - 1)
            for g in range(GATHER_UNROLL):
                t = idx_ref[slot, tile, 0, col + g]
                src = h_ref.at[pl.ds(pl.multiple_of(t * WORD_ROWS, WORD_ROWS), WORD_ROWS), :]
                dst = xbuf_ref.at[slot, pl.ds(pl.multiple_of((first + g) * WORD_ROWS, WORD_ROWS), WORD_ROWS), :]
                pltpu.make_async_copy(src, dst, xsem.at[slot]).start(priority=GATHER_PRIORITY)
            return 0

        lax.fori_loop(0, row_groups(p), issue, 0)

    def wait_rows(p):
        n_grp = row_groups(p)

        @pl.when(n_grp > 0)
        def _():
            n = pl.multiple_of(n_grp * (GATHER_UNROLL * WORD_ROWS), GATHER_UNROLL * WORD_ROWS)
            window = xbuf_ref.at[p % 2, pl.ds(0, n), :]
            pltpu.make_async_copy(window, window, xsem.at[p % 2]).wait()

    @pl.when(jnp.logical_and(s == 0, j == 0))
    def _():
        pend_ref[0] = 0
        pend_ref[1] = 0

    def drain_stage(slot):
        @pl.when(pend_ref[slot] == 1)
        def _():
            pltpu.make_async_copy(stage_ref.at[slot], stage_ref.at[slot], ysem.at[slot]).wait()
            pend_ref[slot] = 0

    @pl.when(jnp.logical_and(s == 0, j == 0))
    def _():
        xbuf_ref[...] = jnp.zeros(xbuf_ref.shape, xbuf_ref.dtype)
        idx_copy(0).start()
        idx_copy(0).wait()
        gather_rows(0)
        idx_copy(1).start()

    @pl.when(j == 0)
    def _():
        wait_rows(s)

    @pl.when(jnp.logical_and(j == 0, s + 1 < n_pass))
    def _():
        idx_copy(s + 1).wait()
        gather_rows(s + 1)

    @pl.when(jnp.logical_and(j == 0, s + 2 < n_pass))
    def _():
        idx_copy(s + 2).start()

    @pl.when(jnp.logical_and(j < N_FF_CHUNKS, n_sub > 0))
    def _():
        wg_bf[...] = wg_ref[...].astype(BF16)
        wl_bf[...] = wl_ref[...].astype(BF16)
        bg = bg_ref[...]
        bl = bl_ref[...]

        def up_tile(i):
            rows = pl.ds(pl.multiple_of(i * ROW_TILE, ROW_TILE), ROW_TILE)
            xt = _unpack_tile(xbuf_ref, s % 2, i)
            glu = jnp.minimum(jnp.dot(xt, wg_bf[...], preferred_element_type=F32) + bg, SWIGLU_LIMIT)
            lin = jnp.clip(jnp.dot(xt, wl_bf[...], preferred_element_type=F32) + bl,
                           -SWIGLU_LIMIT, SWIGLU_LIMIT)
            act = glu * _sigmoid(SWIGLU_ALPHA * glu) * (lin + 1.0)
            act_ref[j, rows, :] = act.astype(BF16)

        _for_tiles(n_sub, up_tile)

    for cc in range(N_FF_CHUNKS):
        @pl.when(jnp.logical_and(j == N_FF_CHUNKS + cc, n_sub > 0))
        def _(cc=cc):
            wd_bf[...] = wd_ref[...].astype(BF16)
            bd = bd_ref[...]

            def out_copy(i, slot):
                dst = y_ref.at[pl.ds(pl.multiple_of(row_start + i * ROW_TILE, ROW_TILE), ROW_TILE),
                               cc * FF_CHUNK:(cc + 1) * FF_CHUNK]
                return pltpu.make_async_copy(stage_ref.at[slot], dst, ysem.at[slot])

            def down_tile(i):
                rows = pl.ds(pl.multiple_of(i * ROW_TILE, ROW_TILE), ROW_TILE)
                slot = i % 2
                drain_stage(slot)
                acc = bd
                for c in range(N_FF_CHUNKS):
                    acc = acc + jnp.dot(act_ref[c, rows, :], wd_bf[c * FF_CHUNK:(c + 1) * FF_CHUNK, :],
                                        preferred_element_type=F32)
                stage_ref[slot] = acc
                out_copy(i, slot).start()
                pend_ref[slot] = 1

            _for_tiles(n_sub, down_tile)

    n_zero = nzero_ref[s]

    @pl.when(jnp.logical_and(j == 0, n_zero > 0))
    def _():
        drain_stage(0)
        stage_ref[0] = jnp.zeros((ROW_TILE, FF_CHUNK), F32)

        def zero_copy(i, cc):
            dst = y_ref.at[pl.ds(pl.multiple_of(row_start + i * ROW_TILE, ROW_TILE), ROW_TILE),
                           cc * FF_CHUNK:(cc + 1) * FF_CHUNK]
            return pltpu.make_async_copy(stage_ref.at[0], dst, ysem.at[0])

        def issue(i, _):
            for cc in range(N_FF_CHUNKS):
                zero_copy(i, cc).start()
            return 0

        def drain(i, _):
            for cc in range(N_FF_CHUNKS):
                zero_copy(i, cc).wait()
            return 0

        lax.fori_loop(0, n_zero, issue, 0)
        lax.fori_loop(0, n_zero, drain, 0)

    @pl.when(jnp.logical_and(s == n_pass - 1, j == 2 * N_FF_CHUNKS - 1))
    def _():
        drain_stage(0)
        drain_stage(1)


def expert_mlp(h_packed, tok_sorted, sched, w_gate_up, b_gate_up, w_down, b_down):
    exp_of, row_of, nsub_of, nzero_of, n_pass, rows_of = sched
    last = N_FF_CHUNKS - 1
    up_of = lambda s, j, n: jnp.where(n[s] > 0, jnp.minimum(j, last), last)
    up_chunk = lambda s, j, e, r, n, z, p, c: (e[s], 0, up_of(s, j, n))
    lin_chunk = lambda s, j, e, r, n, z, p, c: (e[s], 0, N_FF_CHUNKS + up_of(s, j, n))

    def down_chunk(s, j, e, r, n, z, p, c):
        in_down = jnp.logical_and(n[s] > 0, j >= N_FF_CHUNKS)
        expert = jnp.where(in_down, e[s], e[jnp.maximum(s - 1, 0)])
        return expert, 0, jnp.where(in_down, j - N_FF_CHUNKS, last)
    grid_spec = pltpu.PrefetchScalarGridSpec(
        num_scalar_prefetch=6,
        grid=(n_pass[0], 2 * N_FF_CHUNKS),
        in_specs=[
            pl.BlockSpec(memory_space=pl.ANY),
            pl.BlockSpec(memory_space=pl.ANY),
            pl.BlockSpec((None, D_MODEL, FF_CHUNK), up_chunk),
            pl.BlockSpec((None, D_MODEL, FF_CHUNK), lin_chunk),
            pl.BlockSpec((None, D_FF, FF_CHUNK), down_chunk),
            pl.BlockSpec((None, 1, FF_CHUNK), up_chunk),
            pl.BlockSpec((None, 1, FF_CHUNK), lin_chunk),
            pl.BlockSpec((None, 1, FF_CHUNK), down_chunk),
        ],
        out_specs=pl.BlockSpec(memory_space=pl.ANY),
        scratch_shapes=[
            pltpu.VMEM((2, SUPER_ROWS * WORD_ROWS, 128), jnp.uint32),
            pltpu.VMEM((N_FF_CHUNKS, SUPER_ROWS, FF_CHUNK), BF16),
            pltpu.VMEM((D_MODEL, FF_CHUNK), BF16),
            pltpu.VMEM((D_MODEL, FF_CHUNK), BF16),
            pltpu.VMEM((D_FF, FF_CHUNK), BF16),
            pltpu.VMEM((2, ROW_TILE, FF_CHUNK), F32),
            pltpu.SMEM((2, SUPER_TILES, 1, ROW_TILE), jnp.int32),
            pltpu.SMEM((2,), jnp.int32),
            pltpu.SemaphoreType.DMA((2,)),
            pltpu.SemaphoreType.DMA((2,)),
            pltpu.SemaphoreType.DMA((2,)),
        ],
    )
    tok_tiles = jnp.concatenate([tok_sorted.reshape(N_ROW_TILES, 1, ROW_TILE),
                                 jnp.zeros((SUPER_TILES, 1, ROW_TILE), jnp.int32)], axis=0)
    return pl.pallas_call(
        _moe_body,
        grid_spec=grid_spec,
        out_shape=jax.ShapeDtypeStruct((N_ROWS, D_MODEL), F32),
        compiler_params=_params(("arbitrary", "arbitrary"), vmem=EXPERT_VMEM_LIMIT),
        name="expert_mlp",
    )(exp_of, row_of, nsub_of, nzero_of, n_pass, rows_of, tok_tiles, h_packed, w_gate_up, w_gate_up, w_down,
      b_gate_up.reshape(N_EXPERTS, 1, 2 * D_FF), b_gate_up.reshape(N_EXPERTS, 1, 2 * D_FF),
      b_down.reshape(N_EXPERTS, 1, D_MODEL))


COMB_TB = 256


def _combine_start(y_ref, ybuf_ref, pos_ref, sem):
    def issue(t, _):
        for k in range(TOP_K):
            p = pos_ref[0, 0, t * TOP_K + k]
            pltpu.make_async_copy(y_ref.at[pl.ds(p, 1), :], ybuf_ref.at[k, pl.ds(t, 1), :], sem).start()
        return 0

    lax.fori_loop(0, COMB_TB, issue, 0, unroll=4)


def _combine_body(n, pos_ref, pos_next_ref, y_ref, gate_ref, x1_ref, gt2_ref, g_ref, o_ref, ybuf_ref, sem_ref):
    i = pl.program_id(0)
    slot = i % 2

    @pl.when(i == 0)
    def _():
        _combine_start(y_ref, ybuf_ref.at[0], pos_ref, sem_ref.at[0])

    @pl.when(i + 1 < n)
    def _():
        _combine_start(y_ref, ybuf_ref.at[1 - slot], pos_next_ref, sem_ref.at[1 - slot])

    for k in range(TOP_K):
        pltpu.make_async_copy(y_ref.at[pl.ds(0, COMB_TB), :], ybuf_ref.at[slot, k], sem_ref.at[slot]).wait()
    gates = gate_ref[...]
    ffn = gates[:, 0:1] * ybuf_ref[slot, 0]
    for k in range(1, TOP_K):
        ffn = ffn + gates[:, k:k + 1] * ybuf_ref[slot, k]
    o_ref[...] = x1_ref[...] + gt2_ref[...] * (ffn * _rms_scale(ffn) * g_ref[...])


def combine_residual(y_sorted, pos, gates, x1, gt2, g_post_ffn, row_offset, n_rows, group_of_block):
    tb = COMB_TB
    nblk = n_rows // tb
    off = row_offset // tb
    pos3 = pos.reshape(N_TOK // tb, 1, tb * TOP_K)
    smem_blk = lambda f: pl.BlockSpec((1, 1, tb * TOP_K), f, memory_space=pltpu.SMEM)
    return pl.pallas_call(
        functools.partial(_combine_body, nblk),
        grid=(nblk,),
        in_specs=[
            smem_blk(lambda i: (off + i, 0, 0)),
            smem_blk(lambda i: (off + jnp.minimum(i + 1, nblk - 1), 0, 0)),
            pl.BlockSpec(memory_space=pl.ANY),
            pl.BlockSpec((tb, TOP_K), lambda i: (off + i, 0)),
            pl.BlockSpec((tb, D_MODEL), lambda i: (off + i, 0)),
            pl.BlockSpec((None, 1, D_MODEL), lambda i: (group_of_block(i), 0, 0)),
            pl.BlockSpec((1, D_MODEL), lambda i: (0, 0)),
        ],
        out_specs=pl.BlockSpec((tb, D_MODEL), lambda i: (i, 0)),
        out_shape=jax.ShapeDtypeStruct((n_rows, D_MODEL), F32),
        scratch_shapes=[pltpu.VMEM((2, TOP_K, tb, D_MODEL), F32), pltpu.SemaphoreType.DMA((2,))],
        compiler_params=_params(("arbitrary",)),
        name="combine_residual",
    )(pos3, pos3, y_sorted, gates, x1, gt2, g_post_ffn.reshape(1, D_MODEL))


INV_CHUNK = 4096


def _row_tokens_body(pos_ref, zeros_ref, o_ref, sem):
    i = pl.program_id(0)

    @pl.when(i == 0)
    def _():
        cp = pltpu.make_async_copy(zeros_ref, o_ref, sem)
        cp.start()
        cp.wait()

    base = i * INV_CHUNK

    def put(r, _):
        o_ref[pos_ref[r]] = lax.shift_right_logical(base + r, 2)
        return 0

    lax.fori_loop(0, INV_CHUNK, put, 0, unroll=8)


def row_tokens(pos):
    return pl.pallas_call(
        _row_tokens_body,
        grid=(N_ASSIGN // INV_CHUNK,),
        in_specs=[
            pl.BlockSpec((INV_CHUNK,), lambda i: (i,), memory_space=pltpu.SMEM),
            pl.BlockSpec(memory_space=pl.ANY),
        ],
        out_specs=pl.BlockSpec(memory_space=pltpu.SMEM),
        out_shape=jax.ShapeDtypeStruct((N_ROWS,), jnp.int32),
        scratch_shapes=[pltpu.SemaphoreType.DMA(())],
        compiler_params=_params(("arbitrary",)),
        name="row_tokens",
    )(pos.reshape(N_ASSIGN), jnp.zeros((N_ROWS,), jnp.int32))


def _routing_tables(e_idx, rank, counts_f):
    counts = counts_f.reshape(N_EXPERTS).astype(jnp.int32)
    n_tiles = (counts + ROW_TILE - 1) // ROW_TILE
    padded = n_tiles * ROW_TILE
    pad_end = jnp.cumsum(padded)
    pad_start = pad_end - padded
    pos = (pad_start[e_idx] + rank).astype(jnp.int32)
    tok_sorted = row_tokens(pos)
    n_pass = (n_tiles + SUPER_TILES - 1) // SUPER_TILES
    pass_end = jnp.cumsum(n_pass)
    total = pass_end[-1]
    s = jnp.arange(N_SUPER, dtype=jnp.int32)
    s_eff = jnp.minimum(s, total - 1)
    e_of = jnp.minimum(jnp.searchsorted(pass_end, s_eff, side="right"), N_EXPERTS - 1).astype(jnp.int32)
    local = s_eff - (pass_end[e_of] - n_pass[e_of])
    row_of = pad_start[e_of] + local * SUPER_ROWS
    nsub = jnp.minimum(SUPER_TILES, n_tiles[e_of] - local * SUPER_TILES)
    nsub = jnp.where(s < total, nsub, 0).astype(jnp.int32)
    zero_row = pad_end[-1] + (s - total) * SUPER_ROWS
    nzero = jnp.clip((N_ROWS - zero_row) // ROW_TILE, 0, SUPER_TILES)
    nzero = jnp.where(s >= total, nzero, 0).astype(jnp.int32)
    row_of = jnp.where(s < total, row_of, jnp.minimum(zero_row, N_ROWS - ROW_TILE)).astype(jnp.int32)
    tail_tiles = (N_ROWS - pad_end[-1]) // ROW_TILE
    n_pass = jnp.minimum(total + jnp.maximum((tail_tiles + SUPER_TILES - 1) // SUPER_TILES, 1), N_SUPER)
    n_pass = n_pass.astype(jnp.int32).reshape(1)
    rows_of = jnp.clip(counts[e_of] - local * SUPER_ROWS, 0, SUPER_ROWS)
    rows_of = jnp.where(s < total, rows_of, 0).astype(jnp.int32)
    return pos.astype(jnp.int32), tok_sorted, (e_of, row_of, nsub, nzero, n_pass, rows_of)


def kernel(x_prompt, x_sample, cache_k, cache_v, state_rnn_fwd, state_rnn_bwd, c, c_ctx, w_mod, b_mod, g_pre_mix, w_in, g_q_norm, g_k_norm, conv_w, conv_b, rg_w_a, rg_b_a, rg_w_x, rg_b_x, rg_lambda, w_o_attn, w_o_rnn, w_out, g_post_mix, g_pre_ffn, w_router, b_router, w_gate_up, b_gate_up, w_down, b_down, g_post_ffn):
    l = 0
    x_ctx = x_prompt.reshape(N_CTX, D_MODEL)
    x_lat = x_sample.reshape(N_LAT, D_MODEL)

    cond8 = jnp.concatenate([c_ctx[None, :], c, jnp.zeros((8 - 1 - N_LAT_SEQ, D_MODEL), F32)], axis=0)
    mod = modulation(cond8, w_mod[l], b_mod[l])[:1 + N_LAT_SEQ].reshape(1 + N_LAT_SEQ, 6, 1, D_MODEL)
    sh1, sc1, gt1, sh2, sc2, gt2 = [mod[:, i] for i in range(6)]

    ctx_group = lambda i: 0
    lat_group_1024 = lambda i: 1 + i
    h_ctx = prenorm_modulate(x_ctx, g_pre_mix[l], sh1, sc1, ctx_group, 1024)
    h_lat = prenorm_modulate(x_lat, g_pre_mix[l], sh1, sc1, lat_group_1024, 1024)
    z_ctx = in_projection(h_ctx, w_in[l])
    z_lat = in_projection(h_lat, w_in[l])

    attn_ctx, k_new, v_new = attention_ctx(z_ctx, g_q_norm[l], g_k_norm[l])
    attn_lat = attention_lat(z_lat, cache_k[:, l].reshape(N_LAT_SEQ, PAST_LEN, KV_COLS),
                             cache_v[:, l].reshape(N_LAT_SEQ, PAST_LEN, KV_COLS),
                             _rope_tables(), g_q_norm[l], g_k_norm[l])

    def per_block(w):
        return w.reshape(2, RNN_BLOCKS, 1, RNN_BLOCK_DIM)

    w_gates = jnp.concatenate([rg_w_a[l, 0], rg_w_x[l, 0], rg_w_a[l, 1], rg_w_x[l, 1]], axis=-1).astype(BF16)
    ba, bx = per_block(rg_b_a[l]), per_block(rg_b_x[l])
    b_gates = jnp.concatenate([ba[0], bx[0], ba[1], bx[1]], axis=-1)
    zeros_state = jnp.zeros((N_CTX_SEQ, 1, D_MODEL), F32)
    rnn_ctx, hf_ctx, hb_ctx = rglru_mixer(z_ctx, CTX_LEN, conv_w[l], conv_b[l], w_gates, b_gates,
                                          rg_lambda[l], zeros_state, zeros_state)
    rnn_lat, _, _ = rglru_mixer(z_lat, LAT_LEN, conv_w[l], conv_b[l], w_gates, b_gates, rg_lambda[l],
                                state_rnn_fwd[:, l].reshape(N_LAT_SEQ, 1, D_MODEL),
                                state_rnn_bwd[:, l].reshape(N_LAT_SEQ, 1, D_MODEL))

    merged_ctx = gated_merge(attn_ctx, rnn_ctx, z_ctx, w_o_attn[l], w_o_rnn[l])
    merged_lat = gated_merge(attn_lat, rnn_lat, z_lat, w_o_attn[l], w_o_rnn[l])

    x1, h2, e_idx, gates, rank, counts = post_mix_router(
        merged_ctx, merged_lat, x_ctx, x_lat, w_out[l].astype(BF16), g_post_mix[l], gt1, g_pre_ffn[l],
        sh2, sc2, w_router[l], b_router[l])

    pos, tok_sorted, sched = _routing_tables(e_idx, rank, counts)
    y_sorted = expert_mlp(h2, tok_sorted, sched, w_gate_up[l], b_gate_up[l], w_down[l], b_down[l])

    y_ctx = combine_residual(y_sorted, pos, gates, x1, gt2, g_post_ffn[l], 0, N_CTX, ctx_group)
    y_lat = combine_residual(y_sorted, pos, gates, x1, gt2, g_post_ffn[l], N_CTX, N_LAT,
                             lambda i: 1 + i // (LAT_LEN // COMB_TB))

    return (y_ctx.reshape(N_CTX_SEQ, CTX_LEN, D_MODEL),
            y_lat.reshape(N_LAT_SEQ, LAT_LEN, D_MODEL),
            k_new.reshape(N_CTX_SEQ, 1, CTX_LEN, N_KV_HEADS, HEAD_DIM),
            v_new.reshape(N_CTX_SEQ, 1, CTX_LEN, N_KV_HEADS, HEAD_DIM),
            hf_ctx,
            hb_ctx)
```

```python
import functools

import jax
import jax.numpy as jnp
import numpy as np
from jax import lax
from jax.experimental import pallas as pl
from jax.experimental.pallas import tpu as pltpu

D_MODEL = 2048
N_CTX_SEQ = 32
CTX_LEN = 256
N_LAT_SEQ = 2
LAT_LEN = 1024
PAST_LEN = 512
N_CTX = N_CTX_SEQ * CTX_LEN
N_LAT = N_LAT_SEQ * LAT_LEN
N_TOK = N_CTX + N_LAT
GRID_W = 64
N_HEADS = 16
N_KV_HEADS = 4
HEAD_DIM = 128
KV_GROUP = N_HEADS // N_KV_HEADS
ROPE_THETA = 10000.0
RNN_BLOCKS = 16
RNN_BLOCK_DIM = 128
RG_C = 8.0
N_EXPERTS = 32
TOP_K = 4
D_FF = 2048
SWIGLU_LIMIT = 7.0
SWIGLU_ALPHA = 1.702
EPS = 1e-6
Q_COLS = N_HEADS * HEAD_DIM
KV_COLS = N_KV_HEADS * HEAD_DIM
IN_COLS = Q_COLS + 2 * KV_COLS + 4 * D_MODEL
COL_K = Q_COLS
COL_XR = Q_COLS + 2 * KV_COLS
COL_YR = COL_XR + D_MODEL
COL_GA = COL_YR + D_MODEL
COL_GR = COL_GA + D_MODEL

V7X_VMEM_BYTES = 64 * 1024 * 1024
VMEM_LIMIT = 56 * 1024 * 1024
EXPERT_VMEM_LIMIT = 60 * 1024 * 1024

ROW_TILE = 256
SUPER_TILES = 8
SUPER_ROWS = ROW_TILE * SUPER_TILES
N_ASSIGN = N_TOK * TOP_K
N_ROWS = N_ASSIGN + N_EXPERTS * ROW_TILE
N_ROW_TILES = N_ROWS // ROW_TILE
N_SUPER = N_ROW_TILES // SUPER_TILES + N_EXPERTS
FF_CHUNK = 512
N_FF_CHUNKS = D_FF // FF_CHUNK

BF16 = jnp.bfloat16
F32 = jnp.float32


def _params(semantics, vmem=VMEM_LIMIT):
    return pltpu.CompilerParams(dimension_semantics=semantics, vmem_limit_bytes=vmem)


def _rms_scale(x):
    return lax.rsqrt(jnp.mean(x * x, axis=-1, keepdims=True) + EPS)


def _sigmoid(x):
    return 1.0 / (1.0 + jnp.exp(-x))


def _mod_body(c_ref, w_ref, b_ref, o_ref):
    c = c_ref[...]
    a = (c * _sigmoid(c)).astype(BF16)
    o_ref[...] = jnp.dot(a, w_ref[...].astype(BF16), preferred_element_type=F32) + b_ref[...]


def modulation(cond8, w_mod, b_mod):
    tn = 1024
    n = w_mod.shape[1]
    return pl.pallas_call(
        _mod_body,
        grid=(n // tn,),
        in_specs=[
            pl.BlockSpec((8, D_MODEL), lambda j: (0, 0)),
            pl.BlockSpec((D_MODEL, tn), lambda j: (0, j)),
            pl.BlockSpec((1, tn), lambda j: (0, j)),
        ],
        out_specs=pl.BlockSpec((8, tn), lambda j: (0, j)),
        out_shape=jax.ShapeDtypeStruct((8, n), F32),
        compiler_params=_params(("arbitrary",)),
        name="modulation",
    )(cond8, w_mod, b_mod.reshape(1, n))


def _prenorm_body(x_ref, g_ref, sh_ref, sc_ref, o_ref):
    x = x_ref[...]
    y = x * _rms_scale(x) * g_ref[...]
    o_ref[...] = (y * (1.0 + sc_ref[...]) + sh_ref[...]).astype(o_ref.dtype)


def prenorm_modulate(x, g, shift, scale, group_of_block, tm):
    m = x.shape[0]
    gmap = lambda i: (group_of_block(i), 0, 0)
    return pl.pallas_call(
        _prenorm_body,
        grid=(m // tm,),
        in_specs=[
            pl.BlockSpec((tm, D_MODEL), lambda i: (i, 0)),
            pl.BlockSpec((1, D_MODEL), lambda i: (0, 0)),
            pl.BlockSpec((None, 1, D_MODEL), gmap),
            pl.BlockSpec((None, 1, D_MODEL), gmap),
        ],
        out_specs=pl.BlockSpec((tm, D_MODEL), lambda i: (i, 0)),
        out_shape=jax.ShapeDtypeStruct((m, D_MODEL), BF16),
        compiler_params=_params(("arbitrary",)),
        name="prenorm_modulate",
    )(x, g.reshape(1, D_MODEL), shift, scale)


def _inproj_body(h_ref, w_ref, o_ref, wbf_ref):
    @pl.when(pl.program_id(1) == 0)
    def _():
        wbf_ref[...] = w_ref[...].astype(BF16)

    o_ref[...] = jnp.dot(h_ref[...], wbf_ref[...], preferred_element_type=F32)


def in_projection(h, w_in):
    m = h.shape[0]
    tm, tn = 1024, 1024
    return pl.pallas_call(
        _inproj_body,
        grid=(IN_COLS // tn, m // tm),
        in_specs=[
            pl.BlockSpec((tm, D_MODEL), lambda j, i: (i, 0)),
            pl.BlockSpec((D_MODEL, tn), lambda j, i: (0, j)),
        ],
        out_specs=pl.BlockSpec((tm, tn), lambda j, i: (i, j)),
        out_shape=jax.ShapeDtypeStruct((m, IN_COLS), F32),
        scratch_shapes=[pltpu.VMEM((D_MODEL, tn), BF16)],
        compiler_params=_params(("arbitrary", "arbitrary")),
        name="in_projection",
    )(h, w_in)


def _rope(x, cos, sin_lo, sin_hi):
    return x * cos + pltpu.roll(x, 96, 1) * sin_lo + pltpu.roll(x, 32, 1) * sin_hi


def _head_norm(x, g):
    return x * _rms_scale(x) * g


def _softmax_pv(score_blocks, value_blocks):
    m = None
    for s in score_blocks:
        mi = jnp.max(s, axis=-1, keepdims=True)
        m = mi if m is None else jnp.maximum(m, mi)
    ps = [jnp.exp(s - m) for s in score_blocks]
    denom = None
    for p in ps:
        li = jnp.sum(p, axis=-1, keepdims=True)
        denom = li if denom is None else denom + li
    out = None
    for p, v in zip(ps, value_blocks):
        o = jnp.dot(p.astype(BF16), v, preferred_element_type=F32)
        out = o if out is None else out + o
    return out * (1.0 / denom)


def _attn_ctx_body(q_ref, kv_ref, gq_ref, gk_ref, o_ref, ko_ref, vo_ref):
    tq = q_ref.shape[0]
    scale = HEAD_DIM ** -0.5
    gq = gq_ref[...]
    gk = gk_ref[...]
    for g in range(N_KV_HEADS):
        kcols = slice(g * HEAD_DIM, (g + 1) * HEAD_DIM)
        kn = _head_norm(kv_ref[:, kcols], gk)
        v = kv_ref[:, KV_COLS + g * HEAD_DIM:KV_COLS + (g + 1) * HEAD_DIM]
        ko_ref[pl.ds(g, tq, stride=N_KV_HEADS), :] = kn
        vo_ref[pl.ds(g, tq, stride=N_KV_HEADS), :] = v
        qs = []
        for hh in range(KV_GROUP):
            h = g * KV_GROUP + hh
            qs.append((_head_norm(q_ref[:, h * HEAD_DIM:(h + 1) * HEAD_DIM], gq) * scale).astype(BF16))
        q4 = jnp.concatenate(qs, axis=0)
        s = lax.dot_general(q4, kn.astype(BF16), (((1,), (1,)), ((), ())),
                            preferred_element_type=F32)
        o = _softmax_pv([s], [v.astype(BF16)])
        for hh in range(KV_GROUP):
            h = g * KV_GROUP + hh
            o_ref[:, h * HEAD_DIM:(h + 1) * HEAD_DIM] = o[hh * tq:(hh + 1) * tq].astype(o_ref.dtype)


def attention_ctx(z, g_q, g_k):
    nb = N_CTX_SEQ
    t = CTX_LEN
    return pl.pallas_call(
        _attn_ctx_body,
        grid=(nb,),
        in_specs=[
            pl.BlockSpec((t, Q_COLS), lambda b: (b, 0)),
            pl.BlockSpec((t, 2 * KV_COLS), lambda b: (b, COL_K // (2 * KV_COLS))),
            pl.BlockSpec((1, HEAD_DIM), lambda b: (0, 0)),
            pl.BlockSpec((1, HEAD_DIM), lambda b: (0, 0)),
        ],
        out_specs=[
            pl.BlockSpec((t, Q_COLS), lambda b: (b, 0)),
            pl.BlockSpec((t * N_KV_HEADS, HEAD_DIM), lambda b: (b, 0)),
            pl.BlockSpec((t * N_KV_HEADS, HEAD_DIM), lambda b: (b, 0)),
        ],
        out_shape=[
            jax.ShapeDtypeStruct((N_CTX, Q_COLS), BF16),
            jax.ShapeDtypeStruct((N_CTX * N_KV_HEADS, HEAD_DIM), F32),
            jax.ShapeDtypeStruct((N_CTX * N_KV_HEADS, HEAD_DIM), F32),
        ],
        compiler_params=_params(("arbitrary",)),
        name="attention_ctx",
    )(z, z, g_q.reshape(1, HEAD_DIM), g_k.reshape(1, HEAD_DIM))


def _attn_lat_body(q_ref, kv_ref, ck_ref, cv_ref, cos_ref, slo_ref, shi_ref, gq_ref, gk_ref,
                   o_ref, kr_ref):
    tq = q_ref.shape[0]
    qb = pl.program_id(1)
    scale = HEAD_DIM ** -0.5
    gq = gq_ref[...]

    @pl.when(qb == 0)
    def _():
        gk = gk_ref[...]
        for g in range(N_KV_HEADS):
            kcols = slice(g * HEAD_DIM, (g + 1) * HEAD_DIM)
            kn = _head_norm(kv_ref[:, kcols], gk)
            kr_ref[:, kcols] = _rope(kn, cos_ref[...], slo_ref[...], shi_ref[...]).astype(BF16)

    row0 = pl.multiple_of(qb * tq, tq)
    cos = cos_ref[pl.ds(row0, tq), :]
    slo = slo_ref[pl.ds(row0, tq), :]
    shi = shi_ref[pl.ds(row0, tq), :]
    for g in range(N_KV_HEADS):
        kcols = slice(g * HEAD_DIM, (g + 1) * HEAD_DIM)
        qs = []
        for hh in range(KV_GROUP):
            h = g * KV_GROUP + hh
            qn = _head_norm(q_ref[:, h * HEAD_DIM:(h + 1) * HEAD_DIM], gq)
            qs.append((_rope(qn, cos, slo, shi) * scale).astype(BF16))
        q4 = jnp.concatenate(qs, axis=0)
        dn = (((1,), (1,)), ((), ()))
        s_past = lax.dot_general(q4, ck_ref[:, kcols].astype(BF16), dn, preferred_element_type=F32)
        s_new = lax.dot_general(q4, kr_ref[:, kcols], dn, preferred_element_type=F32)
        v_past = cv_ref[:, kcols].astype(BF16)
        v_new = kv_ref[:, KV_COLS + g * HEAD_DIM:KV_COLS + (g + 1) * HEAD_DIM].astype(BF16)
        o = _softmax_pv([s_past, s_new], [v_past, v_new])
        for hh in range(KV_GROUP):
            h = g * KV_GROUP + hh
            o_ref[:, h * HEAD_DIM:(h + 1) * HEAD_DIM] = o[hh * tq:(hh + 1) * tq].astype(o_ref.dtype)


def attention_lat(z, cache_k, cache_v, rope_tabs, g_q, g_k):
    tq = 256
    nq = LAT_LEN // tq
    cos, slo, shi = rope_tabs
    tab = pl.BlockSpec((LAT_LEN, HEAD_DIM), lambda b, q: (0, 0))
    return pl.pallas_call(
        _attn_lat_body,
        grid=(N_LAT_SEQ, nq),
        in_specs=[
            pl.BlockSpec((tq, Q_COLS), lambda b, q: (b * nq + q, 0)),
            pl.BlockSpec((LAT_LEN, 2 * KV_COLS), lambda b, q: (b, COL_K // (2 * KV_COLS))),
            pl.BlockSpec((None, PAST_LEN, KV_COLS), lambda b, q: (b, 0, 0)),
            pl.BlockSpec((None, PAST_LEN, KV_COLS), lambda b, q: (b, 0, 0)),
            tab, tab, tab,
            pl.BlockSpec((1, HEAD_DIM), lambda b, q: (0, 0)),
            pl.BlockSpec((1, HEAD_DIM), lambda b, q: (0, 0)),
        ],
        out_specs=pl.BlockSpec((tq, Q_COLS), lambda b, q: (b * nq + q, 0)),
        out_shape=jax.ShapeDtypeStruct((N_LAT, Q_COLS), BF16),
        scratch_shapes=[pltpu.VMEM((LAT_LEN, KV_COLS), BF16)],
        compiler_params=_params(("arbitrary", "arbitrary")),
        name="attention_lat",
    )(z, z, cache_k, cache_v, cos, slo, shi, g_q.reshape(1, HEAD_DIM), g_k.reshape(1, HEAD_DIM))


def _rope_tables():
    t = np.arange(LAT_LEN)
    row = jnp.asarray(t // GRID_W, F32)
    col = jnp.asarray(t % GRID_W, F32)
    nf = HEAD_DIM // 4
    inv_freq = ROPE_THETA ** (-jnp.arange(nf, dtype=F32) / nf)
    ang_row = row[:, None] * inv_freq[None, :]
    ang_col = col[:, None] * inv_freq[None, :]
    ang = jnp.concatenate([ang_row, ang_row, ang_col, ang_col], axis=1)
    cos = jnp.cos(ang)
    sin = jnp.sin(ang)
    first = jnp.asarray((np.arange(HEAD_DIM) % (2 * nf)) < nf)[None, :]
    return cos, jnp.where(first, -sin, 0.0), jnp.where(first, 0.0, sin)


RNN_ROWS = 2048
RNN_COLS = 512
RNN_SUB = RNN_COLS // RNN_BLOCK_DIM


def _gelu_tanh(y):
    return 0.5 * y * (1.0 + jnp.tanh(0.7978845608028654 * (y + 0.044715 * (y * y * y))))


def _rglru_body(seq_len, xr_ref, yr_ref, cw_ref, cb_ref, wg_ref, bg_ref, lam_ref, h0f_ref, h0b_ref,
                o_ref, hf_ref, hb_ref, xs_ref, af_ref, bf_ref, ab_ref, bb_ref):
    n_seq = RNN_ROWS // seq_len
    for n in range(RNN_SUB):
        cols = slice(n * RNN_BLOCK_DIM, (n + 1) * RNN_BLOCK_DIM)
        for s in range(n_seq):
            xs_ref[n, pl.ds(s, seq_len, stride=n_seq), :] = xr_ref[s * seq_len:(s + 1) * seq_len, cols]

    row = lax.broadcasted_iota(jnp.int32, (RNN_ROWS, 1), 0)
    lam = lam_ref[...]
    softplus_neg = jnp.maximum(-lam, 0.0) + jnp.log(1.0 + jnp.exp(-jnp.abs(lam)))
    rate = softplus_neg * (-RG_C * 1.4426950408889634)
    for n in range(RNN_SUB):
        cols = slice(n * RNN_BLOCK_DIM, (n + 1) * RNN_BLOCK_DIM)
        x = xs_ref[n]
        x_m1 = jnp.where(row >= n_seq, pltpu.roll(x, n_seq, 0), 0.0)
        x_p1 = jnp.where(row < RNN_ROWS - n_seq, pltpu.roll(x, RNN_ROWS - n_seq, 0), 0.0)
        x_p2 = jnp.where(row < RNN_ROWS - 2 * n_seq, pltpu.roll(x, RNN_ROWS - 2 * n_seq, 0), 0.0)
        xn = (cb_ref[:, cols] + x_m1 * cw_ref[0:1, cols] + x * cw_ref[1:2, cols]
              + x_p1 * cw_ref[2:3, cols] + x_p2 * cw_ref[3:4, cols])
        pre = jnp.dot(xn.astype(BF16), wg_ref[n], preferred_element_type=F32) + bg_ref[n]
        for d, (a_ref, b_ref) in enumerate(((af_ref, bf_ref), (ab_ref, bb_ref))):
            r = 0.5 * jnp.tanh(0.5 * pre[:, (2 * d) * RNN_BLOCK_DIM:(2 * d + 1) * RNN_BLOCK_DIM]) + 0.5
            gate_in = 0.5 * jnp.tanh(
                0.5 * pre[:, (2 * d + 1) * RNN_BLOCK_DIM:(2 * d + 2) * RNN_BLOCK_DIM]) + 0.5
            a = jnp.exp2(r * rate[d:d + 1, cols])
            v = 1.0 - a * a
            a_ref[n] = a
            b_ref[n] = (v * lax.rsqrt(jnp.maximum(v, 1e-30))) * (gate_in * xn)

    def step(t, carry):
        rows_f = pl.ds(pl.multiple_of(t * n_seq, n_seq), n_seq)
        rows_b = pl.ds(pl.multiple_of((seq_len - 1 - t) * n_seq, n_seq), n_seq)
        out = []
        for n in range(RNN_SUB):
            hf = af_ref[n, rows_f, :] * carry[2 * n] + bf_ref[n, rows_f, :]
            hb = ab_ref[n, rows_b, :] * carry[2 * n + 1] + bb_ref[n, rows_b, :]
            bf_ref[n, rows_f, :] = hf
            bb_ref[n, rows_b, :] = hb
            out += [hf, hb]
        return tuple(out)

    init = []
    for n in range(RNN_SUB):
        cols = slice(n * RNN_BLOCK_DIM, (n + 1) * RNN_BLOCK_DIM)
        init += [h0f_ref[:, 0, cols], h0b_ref[:, 0, cols]]
    last = lax.fori_loop(0, seq_len, step, tuple(init), unroll=8)
    for n in range(RNN_SUB):
        cols = slice(n * RNN_BLOCK_DIM, (n + 1) * RNN_BLOCK_DIM)
        hf_ref[:, 0, cols] = last[2 * n]
        hb_ref[:, 0, cols] = last[2 * n + 1]
        bf_ref[n] = bf_ref[n] + bb_ref[n]
        for s in range(n_seq):
            rows = slice(s * seq_len, (s + 1) * seq_len)
            h_sum = bf_ref[n, pl.ds(s, seq_len, stride=n_seq), :]
            o_ref[rows, cols] = (h_sum * _gelu_tanh(yr_ref[rows, cols])).astype(o_ref.dtype)


def rglru_mixer(z, seq_len, conv_w, conv_b, w_gates, b_gates, lam, h0_f, h0_b):
    m = z.shape[0]
    n_seq_total = m // seq_len
    n_seq = RNN_ROWS // seq_len
    cblk = lambda base: (lambda r, c: (r, base // RNN_COLS + c))
    state_spec = pl.BlockSpec((n_seq, 1, RNN_COLS), lambda r, c: (r, 0, c))
    return pl.pallas_call(
        functools.partial(_rglru_body, seq_len),
        grid=(m // RNN_ROWS, D_MODEL // RNN_COLS),
        in_specs=[
            pl.BlockSpec((RNN_ROWS, RNN_COLS), cblk(COL_XR)),
            pl.BlockSpec((RNN_ROWS, RNN_COLS), cblk(COL_YR)),
            pl.BlockSpec((4, RNN_COLS), lambda r, c: (0, c)),
            pl.BlockSpec((1, RNN_COLS), lambda r, c: (0, c)),
            pl.BlockSpec((RNN_SUB, RNN_BLOCK_DIM, 4 * RNN_BLOCK_DIM), lambda r, c: (c, 0, 0)),
            pl.BlockSpec((RNN_SUB, 1, 4 * RNN_BLOCK_DIM), lambda r, c: (c, 0, 0)),
            pl.BlockSpec((2, RNN_COLS), lambda r, c: (0, c)),
            state_spec, state_spec,
        ],
        out_specs=[
            pl.BlockSpec((RNN_ROWS, RNN_COLS), lambda r, c: (r, c)),
            state_spec, state_spec,
        ],
        out_shape=[
            jax.ShapeDtypeStruct((m, D_MODEL), BF16),
            jax.ShapeDtypeStruct((n_seq_total, 1, D_MODEL), F32),
            jax.ShapeDtypeStruct((n_seq_total, 1, D_MODEL), F32),
        ],
        scratch_shapes=[pltpu.VMEM((RNN_SUB, RNN_ROWS, RNN_BLOCK_DIM), F32) for _ in range(5)],
        compiler_params=_params(("arbitrary", "arbitrary")),
        name="rglru_mixer_t%d" % seq_len,
    )(z, z, conv_w, conv_b.reshape(1, D_MODEL), w_gates, b_gates, lam, h0_f, h0_b)


def _merge_body(a_ref, r_ref, wa_ref, wr_ref, ga_ref, gr_ref, o_ref, wa_bf, wr_bf):
    @pl.when(pl.program_id(1) == 0)
    def _():
        wa_bf[...] = wa_ref[...].astype(BF16)
        wr_bf[...] = wr_ref[...].astype(BF16)

    pa = jnp.dot(a_ref[...], wa_bf[...], preferred_element_type=F32)
    pr = jnp.dot(r_ref[...], wr_bf[...], preferred_element_type=F32)
    o_ref[...] = (_sigmoid(ga_ref[...]) * pa + _sigmoid(gr_ref[...]) * pr).astype(o_ref.dtype)


def gated_merge(attn, rnn, z, w_o_attn, w_o_rnn):
    m = attn.shape[0]
    tm, tn = 1024, 512
    return pl.pallas_call(
        _merge_body,
        grid=(D_MODEL // tn, m // tm),
        in_specs=[
            pl.BlockSpec((tm, Q_COLS), lambda j, i: (i, 0)),
            pl.BlockSpec((tm, D_MODEL), lambda j, i: (i, 0)),
            pl.BlockSpec((Q_COLS, tn), lambda j, i: (0, j)),
            pl.BlockSpec((D_MODEL, tn), lambda j, i: (0, j)),
            pl.BlockSpec((tm, tn), lambda j, i: (i, COL_GA // tn + j)),
            pl.BlockSpec((tm, tn), lambda j, i: (i, COL_GR // tn + j)),
        ],
        out_specs=pl.BlockSpec((tm, tn), lambda j, i: (i, j)),
        out_shape=jax.ShapeDtypeStruct((m, D_MODEL), BF16),
        scratch_shapes=[pltpu.VMEM((Q_COLS, tn), BF16), pltpu.VMEM((D_MODEL, tn), BF16)],
        compiler_params=_params(("arbitrary", "arbitrary")),
        name="gated_merge",
    )(attn, rnn, w_o_attn, w_o_rnn, z, z)


POST_TM = 512
POST_SPLIT = 1
HALF_D = D_MODEL // 2
WORD_ROWS = HALF_D // 128
SUBLANES = 8
POST_CTX_BLOCKS = N_CTX // POST_TM
LAT_BLOCKS_PER_SEQ = LAT_LEN // POST_TM


def _post_group(i):
    return jnp.where(i < POST_CTX_BLOCKS, 0, 1 + (i - POST_CTX_BLOCKS) // LAT_BLOCKS_PER_SEQ)


def _postmix_body(mc_ref, ml_ref, xc_ref, xl_ref, wo_ref, gpm_ref, gt1_ref, gpf_ref, sh2_ref, sc2_ref,
                  wr_ref, br_ref, x1_ref, h2_ref, e_ref, gate_ref, rank_ref, cnt_ref, carry_ref):
    i = pl.program_id(0)
    tm = POST_TM

    @pl.when(i == 0)
    def _():
        carry_ref[...] = jnp.zeros_like(carry_ref)

    is_ctx = i < POST_CTX_BLOCKS
    th = tm // POST_SPLIT
    r_io = lax.broadcasted_iota(jnp.int32, (th, th), 0)
    c_io = lax.broadcasted_iota(jnp.int32, (th, th), 1)
    lower = jnp.where(c_io < r_io, 1.0, 0.0).astype(BF16)
    lane = lax.broadcasted_iota(jnp.int32, (th, N_EXPERTS), 1)
    lane_k = lax.broadcasted_iota(jnp.int32, (th, TOP_K), 1)
    wr = wr_ref[...].astype(BF16)
    carry = carry_ref[...]
    for part in range(POST_SPLIT):
        rows = slice(part * th, (part + 1) * th)
        merged = jnp.where(is_ctx, mc_ref[rows, :], ml_ref[rows, :])
        x = jnp.where(is_ctx, xc_ref[rows, :], xl_ref[rows, :])
        o = jnp.dot(merged, wo_ref[...], preferred_element_type=F32)
        x1 = x + gt1_ref[...] * (o * _rms_scale(o) * gpm_ref[...])
        x1_ref[rows, :] = x1
        h2 = (x1 * _rms_scale(x1) * gpf_ref[...]) * (1.0 + sc2_ref[...]) + sh2_ref[...]
        h2_bf = h2.astype(BF16)
        bits = lax.bitcast_convert_type(h2_bf.astype(F32), jnp.uint32)
        words = (lax.shift_right_logical(bits[:, :HALF_D], jnp.uint32(16))
                 | (bits[:, HALF_D:] & jnp.uint32(0xFFFF0000)))
        for c in range(WORD_ROWS):
            h2_ref[pl.ds(part * th * WORD_ROWS + c, th, stride=WORD_ROWS), :] = words[:, c * 128:(c + 1) * 128]

        logits = jnp.dot(h2_bf, wr, preferred_element_type=F32) + br_ref[...]
        work = logits
        chosen = jnp.zeros((th, N_EXPERTS), F32)
        sels, vals, idxs = [], [], []
        for _ in range(TOP_K):
            mx = jnp.max(work, axis=-1, keepdims=True)
            idx = jnp.min(jnp.where(work == mx, lane, N_EXPERTS), axis=-1, keepdims=True)
            sel = lane == idx
            work = jnp.where(sel, -jnp.inf, work)
            chosen = jnp.where(sel, 1.0, chosen)
            sels.append(sel)
            vals.append(mx)
            idxs.append(idx)
        exps = [jnp.exp(v - vals[0]) for v in vals]
        inv = 1.0 / (exps[0] + exps[1] + exps[2] + exps[3])

        before = jnp.dot(lower, chosen.astype(BF16), preferred_element_type=F32) + carry
        carry = carry + jnp.sum(chosen, axis=0, keepdims=True)

        e_out = jnp.zeros((th, TOP_K), jnp.int32)
        g_out = jnp.zeros((th, TOP_K), F32)
        r_out = jnp.zeros((th, TOP_K), jnp.int32)
        for k in range(TOP_K):
            rk = jnp.sum(jnp.where(sels[k], before, 0.0), axis=-1, keepdims=True).astype(jnp.int32)
            e_out = jnp.where(lane_k == k, idxs[k], e_out)
            g_out = jnp.where(lane_k == k, exps[k] * inv, g_out)
            r_out = jnp.where(lane_k == k, rk, r_out)
        e_ref[rows, :] = e_out
        gate_ref[rows, :] = g_out
        rank_ref[rows, :] = r_out
    carry_ref[...] = carry
    cnt_ref[...] = carry


def post_mix_router(merged_ctx, merged_lat, x_ctx, x_lat, w_out_bf, g_post_mix, gt1, g_pre_ffn, sh2, sc2,
                    w_router, b_router):
    tm = POST_TM
    ctx_map = lambda i: (jnp.minimum(i, POST_CTX_BLOCKS - 1), 0)
    lat_map = lambda i: (jnp.maximum(i - POST_CTX_BLOCKS, 0), 0)
    gmap = lambda i: (_post_group(i), 0, 0)
    row = lambda i: (i, 0)
    const = lambda i: (0, 0)
    vec = pl.BlockSpec((1, D_MODEL), const)
    gvec = pl.BlockSpec((None, 1, D_MODEL), gmap)
    return pl.pallas_call(
        _postmix_body,
        grid=(N_TOK // tm,),
        in_specs=[
            pl.BlockSpec((tm, D_MODEL), ctx_map),
            pl.BlockSpec((tm, D_MODEL), lat_map),
            pl.BlockSpec((tm, D_MODEL), ctx_map),
            pl.BlockSpec((tm, D_MODEL), lat_map),
            pl.BlockSpec((D_MODEL, D_MODEL), const),
            vec, gvec, vec, gvec, gvec,
            pl.BlockSpec((D_MODEL, N_EXPERTS), const),
            pl.BlockSpec((1, N_EXPERTS), const),
        ],
        out_specs=[
            pl.BlockSpec((tm, D_MODEL), row),
            pl.BlockSpec((tm * WORD_ROWS, 128), row),
            pl.BlockSpec((tm, TOP_K), row),
            pl.BlockSpec((tm, TOP_K), row),
            pl.BlockSpec((tm, TOP_K), row),
            pl.BlockSpec((1, N_EXPERTS), const),
        ],
        out_shape=[
            jax.ShapeDtypeStruct((N_TOK, D_MODEL), F32),
            jax.ShapeDtypeStruct((N_TOK * WORD_ROWS, 128), jnp.uint32),
            jax.ShapeDtypeStruct((N_TOK, TOP_K), jnp.int32),
            jax.ShapeDtypeStruct((N_TOK, TOP_K), F32),
            jax.ShapeDtypeStruct((N_TOK, TOP_K), jnp.int32),
            jax.ShapeDtypeStruct((1, N_EXPERTS), F32),
        ],
        scratch_shapes=[pltpu.VMEM((1, N_EXPERTS), F32)],
        compiler_params=_params(("arbitrary",)),
        name="post_mix_router",
    )(merged_ctx, merged_lat, x_ctx, x_lat, w_out_bf, g_post_mix.reshape(1, D_MODEL), gt1,
      g_pre_ffn.reshape(1, D_MODEL), sh2, sc2, w_router, b_router.reshape(1, N_EXPERTS))


GATHER_SHIFT = 4
GATHER_UNROLL = 1 << GATHER_SHIFT
GATHER_PRIORITY = 1


def _unpack_tile(xbuf_ref, slot, i):
    base = pl.multiple_of(i * (ROW_TILE * WORD_ROWS), ROW_TILE * WORD_ROWS)
    lo, hi = [], []
    for c in range(WORD_ROWS):
        words = xbuf_ref[slot, pl.ds(base + c, ROW_TILE, stride=WORD_ROWS), :]
        lo.append(lax.bitcast_convert_type(lax.shift_left(words, jnp.uint32(16)), F32).astype(BF16))
        hi.append(lax.bitcast_convert_type(words & jnp.uint32(0xFFFF0000), F32).astype(BF16))
    return jnp.concatenate(lo + hi, axis=1)


def _for_tiles(n_tiles, body):
    def one(i, _):
        body(i)
        return 0

    lax.fori_loop(0, n_tiles, one, 0)


def _moe_body(exp_ref, row_ref, nsub_ref, nzero_ref, npass_ref, rows_ref, tok_ref, h_ref, wg_ref, wl_ref, wd_ref, bg_ref,
              bl_ref, bd_ref, y_ref, xbuf_ref, act_ref, wg_bf, wl_bf, wd_bf, stage_ref, idx_ref, pend_ref,
              xsem, isem, ysem):
    s = pl.program_id(0)
    j = pl.program_id(1)
    n_pass = npass_ref[0]
    n_sub = nsub_ref[s]
    row_start = row_ref[s]

    def idx_copy(p):
        tile0 = pl.multiple_of(row_ref[p], ROW_TILE) // ROW_TILE
        return pltpu.make_async_copy(tok_ref.at[pl.ds(tile0, SUPER_TILES)], idx_ref.at[p % 2],
                                     isem.at[p % 2])

    def row_groups(p):
        return lax.shift_right_logical(rows_ref[p] + (GATHER_UNROLL - 1), GATHER_SHIFT)

    def gather_rows(p):
        slot = p % 2

        def issue(grp, _):
            first = grp * GATHER_UNROLL
            tile = lax.shift_right_logical(first, 8)
            col = jnp.bitwise_and(first, ROW_TILE - 1)
            for g in range(GATHER_UNROLL):
                t = idx_ref[slot, tile, 0, col + g]
                src = h_ref.at[t]
                dst = xbuf_ref.at[slot, pl.ds(pl.multiple_of((first + g) * WORD_ROWS, WORD_ROWS), WORD_ROWS), :]
                pltpu.make_async_copy(src, dst, xsem.at[slot]).start(priority=GATHER_PRIORITY)
            return 0

        lax.fori_loop(0, row_groups(p), issue, 0)

    def wait_rows(p):
        n_grp = row_groups(p)

        @pl.when(n_grp > 0)
        def _():
            n = pl.multiple_of(n_grp * (GATHER_UNROLL * WORD_ROWS), GATHER_UNROLL * WORD_ROWS)
            window = xbuf_ref.at[p % 2, pl.ds(0, n), :]
            pltpu.make_async_copy(window, window, xsem.at[p % 2]).wait()

    @pl.when(jnp.logical_and(s == 0, j == 0))
    def _():
        pend_ref[0] = 0
        pend_ref[1] = 0

    def drain_stage(slot):
        @pl.when(pend_ref[slot] == 1)
        def _():
            pltpu.make_async_copy(stage_ref.at[slot], stage_ref.at[slot], ysem.at[slot]).wait()
            pend_ref[slot] = 0

    @pl.when(jnp.logical_and(s == 0, j == 0))
    def _():
        xbuf_ref[...] = jnp.zeros(xbuf_ref.shape, xbuf_ref.dtype)
        idx_copy(0).start()
        idx_copy(0).wait()
        gather_rows(0)
        idx_copy(1).start()

    @pl.when(j == 0)
    def _():
        wait_rows(s)

    @pl.when(jnp.logical_and(j == 0, s + 1 < n_pass))
    def _():
        idx_copy(s + 1).wait()
        gather_rows(s + 1)

    @pl.when(jnp.logical_and(j == 0, s + 2 < n_pass))
    def _():
        idx_copy(s + 2).start()

    @pl.when(jnp.logical_and(j < N_FF_CHUNKS, n_sub > 0))
    def _():
        wg_bf[...] = wg_ref[...].astype(BF16)
        wl_bf[...] = wl_ref[...].astype(BF16)
        bg = bg_ref[...]
        bl = bl_ref[...]

        def up_tile(i):
            rows = pl.ds(pl.multiple_of(i * ROW_TILE, ROW_TILE), ROW_TILE)
            xt = _unpack_tile(xbuf_ref, s % 2, i)
            glu = jnp.minimum(jnp.dot(xt, wg_bf[...], preferred_element_type=F32) + bg, SWIGLU_LIMIT)
            lin = jnp.clip(jnp.dot(xt, wl_bf[...], preferred_element_type=F32) + bl,
                           -SWIGLU_LIMIT, SWIGLU_LIMIT)
            act = glu * _sigmoid(SWIGLU_ALPHA * glu) * (lin + 1.0)
            act_ref[j, rows, :] = act.astype(BF16)

        _for_tiles(n_sub, up_tile)

    for cc in range(N_FF_CHUNKS):
        @pl.when(jnp.logical_and(j == N_FF_CHUNKS + cc, n_sub > 0))
        def _(cc=cc):
            wd_bf[...] = wd_ref[...].astype(BF16)
            bd = bd_ref[...]

            def out_copy(i, slot):
                dst = y_ref.at[pl.ds(pl.multiple_of(row_start + i * ROW_TILE, ROW_TILE), ROW_TILE),
                               cc * FF_CHUNK:(cc + 1) * FF_CHUNK]
                return pltpu.make_async_copy(stage_ref.at[slot], dst, ysem.at[slot])

            def down_tile(i):
                rows = pl.ds(pl.multiple_of(i * ROW_TILE, ROW_TILE), ROW_TILE)
                slot = i % 2
                drain_stage(slot)
                acc = bd
                for c in range(N_FF_CHUNKS):
                    acc = acc + jnp.dot(act_ref[c, rows, :], wd_bf[c * FF_CHUNK:(c + 1) * FF_CHUNK, :],
                                        preferred_element_type=F32)
                stage_ref[slot] = acc
                out_copy(i, slot).start()
                pend_ref[slot] = 1

            _for_tiles(n_sub, down_tile)

    n_zero = nzero_ref[s]

    @pl.when(jnp.logical_and(j == 0, n_zero > 0))
    def _():
        drain_stage(0)
        stage_ref[0] = jnp.zeros((ROW_TILE, FF_CHUNK), F32)

        def zero_copy(i, cc):
            dst = y_ref.at[pl.ds(pl.multiple_of(row_start + i * ROW_TILE, ROW_TILE), ROW_TILE),
                           cc * FF_CHUNK:(cc + 1) * FF_CHUNK]
            return pltpu.make_async_copy(stage_ref.at[0], dst, ysem.at[0])

        def issue(i, _):
            for cc in range(N_FF_CHUNKS):
                zero_copy(i, cc).start()
            return 0

        def drain(i, _):
            for cc in range(N_FF_CHUNKS):
                zero_copy(i, cc).wait()
            return 0

        lax.fori_loop(0, n_zero, issue, 0)
        lax.fori_loop(0, n_zero, drain, 0)

    @pl.when(jnp.logical_and(s == n_pass - 1, j == 2 * N_FF_CHUNKS - 1))
    def _():
        drain_stage(0)
        drain_stage(1)


def expert_mlp(h_packed, tok_sorted, sched, w_gate_up, b_gate_up, w_down, b_down):
    exp_of, row_of, nsub_of, nzero_of, n_pass, rows_of = sched
    last = N_FF_CHUNKS - 1
    up_of = lambda s, j, n: jnp.where(n[s] > 0, jnp.minimum(j, last), last)
    up_chunk = lambda s, j, e, r, n, z, p, c: (e[s], 0, up_of(s, j, n))
    lin_chunk = lambda s, j, e, r, n, z, p, c: (e[s], 0, N_FF_CHUNKS + up_of(s, j, n))

    def down_chunk(s, j, e, r, n, z, p, c):
        in_down = jnp.logical_and(n[s] > 0, j >= N_FF_CHUNKS)
        expert = jnp.where(in_down, e[s], e[jnp.maximum(s - 1, 0)])
        return expert, 0, jnp.where(in_down, j - N_FF_CHUNKS, last)
    grid_spec = pltpu.PrefetchScalarGridSpec(
        num_scalar_prefetch=6,
        grid=(n_pass[0], 2 * N_FF_CHUNKS),
        in_specs=[
            pl.BlockSpec(memory_space=pl.ANY),
            pl.BlockSpec(memory_space=pl.ANY),
            pl.BlockSpec((None, D_MODEL, FF_CHUNK), up_chunk),
            pl.BlockSpec((None, D_MODEL, FF_CHUNK), lin_chunk),
            pl.BlockSpec((None, D_FF, FF_CHUNK), down_chunk),
            pl.BlockSpec((None, 1, FF_CHUNK), up_chunk),
            pl.BlockSpec((None, 1, FF_CHUNK), lin_chunk),
            pl.BlockSpec((None, 1, FF_CHUNK), down_chunk),
        ],
        out_specs=pl.BlockSpec(memory_space=pl.ANY),
        scratch_shapes=[
            pltpu.VMEM((2, SUPER_ROWS * WORD_ROWS, 128), jnp.uint32),
            pltpu.VMEM((N_FF_CHUNKS, SUPER_ROWS, FF_CHUNK), BF16),
            pltpu.VMEM((D_MODEL, FF_CHUNK), BF16),
            pltpu.VMEM((D_MODEL, FF_CHUNK), BF16),
            pltpu.VMEM((D_FF, FF_CHUNK), BF16),
            pltpu.VMEM((2, ROW_TILE, FF_CHUNK), F32),
            pltpu.SMEM((2, SUPER_TILES, 1, ROW_TILE), jnp.int32),
            pltpu.SMEM((2,), jnp.int32),
            pltpu.SemaphoreType.DMA((2,)),
            pltpu.SemaphoreType.DMA((2,)),
            pltpu.SemaphoreType.DMA((2,)),
        ],
    )
    tok_tiles = jnp.concatenate([tok_sorted.reshape(N_ROW_TILES, 1, ROW_TILE),
                                 jnp.zeros((SUPER_TILES, 1, ROW_TILE), jnp.int32)], axis=0)
    h_packed = h_packed.reshape(N_TOK, WORD_ROWS, 128)
    return pl.pallas_call(
        _moe_body,
        grid_spec=grid_spec,
        out_shape=jax.ShapeDtypeStruct((N_ROWS, D_MODEL), F32),
        compiler_params=_params(("arbitrary", "arbitrary"), vmem=EXPERT_VMEM_LIMIT),
        name="expert_mlp",
    )(exp_of, row_of, nsub_of, nzero_of, n_pass, rows_of, tok_tiles, h_packed, w_gate_up, w_gate_up, w_down,
      b_gate_up.reshape(N_EXPERTS, 1, 2 * D_FF), b_gate_up.reshape(N_EXPERTS, 1, 2 * D_FF),
      b_down.reshape(N_EXPERTS, 1, D_MODEL))


COMB_TB = 256


def _combine_start(y_ref, ybuf_ref, pos_ref, sem):
    def issue(t, _):
        for k in range(TOP_K):
            p = pos_ref[0, 0, t * TOP_K + k]
            pltpu.make_async_copy(y_ref.at[pl.ds(p, 1), :], ybuf_ref.at[k, pl.ds(t, 1), :], sem).start()
        return 0

    lax.fori_loop(0, COMB_TB, issue, 0, unroll=4)


def _combine_body(n, pos_ref, pos_next_ref, y_ref, gate_ref, x1_ref, gt2_ref, g_ref, o_ref, ybuf_ref, sem_ref):
    i = pl.program_id(0)
    slot = i % 2

    @pl.when(i == 0)
    def _():
        _combine_start(y_ref, ybuf_ref.at[0], pos_ref, sem_ref.at[0])

    @pl.when(i + 1 < n)
    def _():
        _combine_start(y_ref, ybuf_ref.at[1 - slot], pos_next_ref, sem_ref.at[1 - slot])

    for k in range(TOP_K):
        pltpu.make_async_copy(y_ref.at[pl.ds(0, COMB_TB), :], ybuf_ref.at[slot, k], sem_ref.at[slot]).wait()
    gates = gate_ref[...]
    ffn = gates[:, 0:1] * ybuf_ref[slot, 0]
    for k in range(1, TOP_K):
        ffn = ffn + gates[:, k:k + 1] * ybuf_ref[slot, k]
    o_ref[...] = x1_ref[...] + gt2_ref[...] * (ffn * _rms_scale(ffn) * g_ref[...])


def combine_residual(y_sorted, pos, gates, x1, gt2, g_post_ffn, row_offset, n_rows, group_of_block):
    tb = COMB_TB
    nblk = n_rows // tb
    off = row_offset // tb
    pos3 = pos.reshape(N_TOK // tb, 1, tb * TOP_K)
    smem_blk = lambda f: pl.BlockSpec((1, 1, tb * TOP_K), f, memory_space=pltpu.SMEM)
    return pl.pallas_call(
        functools.partial(_combine_body, nblk),
        grid=(nblk,),
        in_specs=[
            smem_blk(lambda i: (off + i, 0, 0)),
            smem_blk(lambda i: (off + jnp.minimum(i + 1, nblk - 1), 0, 0)),
            pl.BlockSpec(memory_space=pl.ANY),
            pl.BlockSpec((tb, TOP_K), lambda i: (off + i, 0)),
            pl.BlockSpec((tb, D_MODEL), lambda i: (off + i, 0)),
            pl.BlockSpec((None, 1, D_MODEL), lambda i: (group_of_block(i), 0, 0)),
            pl.BlockSpec((1, D_MODEL), lambda i: (0, 0)),
        ],
        out_specs=pl.BlockSpec((tb, D_MODEL), lambda i: (i, 0)),
        out_shape=jax.ShapeDtypeStruct((n_rows, D_MODEL), F32),
        scratch_shapes=[pltpu.VMEM((2, TOP_K, tb, D_MODEL), F32), pltpu.SemaphoreType.DMA((2,))],
        compiler_params=_params(("arbitrary",)),
        name="combine_residual",
    )(pos3, pos3, y_sorted, gates, x1, gt2, g_post_ffn.reshape(1, D_MODEL))


INV_CHUNK = 4096


def _row_tokens_body(pos_ref, zeros_ref, o_ref, sem):
    i = pl.program_id(0)

    @pl.when(i == 0)
    def _():
        cp = pltpu.make_async_copy(zeros_ref, o_ref, sem)
        cp.start()
        cp.wait()

    first_token = i * (INV_CHUNK // TOP_K)

    def put(t, _):
        for k in range(TOP_K):
            o_ref[pos_ref[t * TOP_K + k]] = first_token + t
        return 0

    lax.fori_loop(0, INV_CHUNK // TOP_K, put, 0, unroll=4)


def row_tokens(pos):
    return pl.pallas_call(
        _row_tokens_body,
        grid=(N_ASSIGN // INV_CHUNK,),
        in_specs=[
            pl.BlockSpec((INV_CHUNK,), lambda i: (i,), memory_space=pltpu.SMEM),
            pl.BlockSpec(memory_space=pl.ANY),
        ],
        out_specs=pl.BlockSpec(memory_space=pltpu.SMEM),
        out_shape=jax.ShapeDtypeStruct((N_ROWS,), jnp.int32),
        scratch_shapes=[pltpu.SemaphoreType.DMA(())],
        compiler_params=_params(("arbitrary",)),
        name="row_tokens",
    )(pos.reshape(N_ASSIGN), jnp.zeros((N_ROWS,), jnp.int32))


def _routing_tables(e_idx, rank, counts_f):
    counts = counts_f.reshape(N_EXPERTS).astype(jnp.int32)
    n_tiles = (counts + ROW_TILE - 1) // ROW_TILE
    padded = n_tiles * ROW_TILE
    pad_end = jnp.cumsum(padded)
    pad_start = pad_end - padded
    pos = (pad_start[e_idx] + rank).astype(jnp.int32)
    tok_sorted = row_tokens(pos)
    n_pass = (n_tiles + SUPER_TILES - 1) // SUPER_TILES
    pass_end = jnp.cumsum(n_pass)
    total = pass_end[-1]
    s = jnp.arange(N_SUPER, dtype=jnp.int32)
    s_eff = jnp.minimum(s, total - 1)
    e_of = jnp.minimum(jnp.searchsorted(pass_end, s_eff, side="right"), N_EXPERTS - 1).astype(jnp.int32)
    local = s_eff - (pass_end[e_of] - n_pass[e_of])
    row_of = pad_start[e_of] + local * SUPER_ROWS
    nsub = jnp.minimum(SUPER_TILES, n_tiles[e_of] - local * SUPER_TILES)
    nsub = jnp.where(s < total, nsub, 0).astype(jnp.int32)
    zero_row = pad_end[-1] + (s - total) * SUPER_ROWS
    nzero = jnp.clip((N_ROWS - zero_row) // ROW_TILE, 0, SUPER_TILES)
    nzero = jnp.where(s >= total, nzero, 0).astype(jnp.int32)
    row_of = jnp.where(s < total, row_of, jnp.minimum(zero_row, N_ROWS - ROW_TILE)).astype(jnp.int32)
    tail_tiles = (N_ROWS - pad_end[-1]) // ROW_TILE
    n_pass = jnp.minimum(total + jnp.maximum((tail_tiles + SUPER_TILES - 1) // SUPER_TILES, 1), N_SUPER)
    n_pass = n_pass.astype(jnp.int32).reshape(1)
    rows_of = jnp.clip(counts[e_of] - local * SUPER_ROWS, 0, SUPER_ROWS)
    rows_of = jnp.where(s < total, rows_of, 0).astype(jnp.int32)
    return pos.astype(jnp.int32), tok_sorted, (e_of, row_of, nsub, nzero, n_pass, rows_of)


def kernel(x_prompt, x_sample, cache_k, cache_v, state_rnn_fwd, state_rnn_bwd, c, c_ctx, w_mod, b_mod, g_pre_mix, w_in, g_q_norm, g_k_norm, conv_w, conv_b, rg_w_a, rg_b_a, rg_w_x, rg_b_x, rg_lambda, w_o_attn, w_o_rnn, w_out, g_post_mix, g_pre_ffn, w_router, b_router, w_gate_up, b_gate_up, w_down, b_down, g_post_ffn):
    l = 0
    x_ctx = x_prompt.reshape(N_CTX, D_MODEL)
    x_lat = x_sample.reshape(N_LAT, D_MODEL)

    cond8 = jnp.concatenate([c_ctx[None, :], c, jnp.zeros((8 - 1 - N_LAT_SEQ, D_MODEL), F32)], axis=0)
    mod = modulation(cond8, w_mod[l], b_mod[l])[:1 + N_LAT_SEQ].reshape(1 + N_LAT_SEQ, 6, 1, D_MODEL)
    sh1, sc1, gt1, sh2, sc2, gt2 = [mod[:, i] for i in range(6)]

    ctx_group = lambda i: 0
    lat_group_1024 = lambda i: 1 + i
    h_ctx = prenorm_modulate(x_ctx, g_pre_mix[l], sh1, sc1, ctx_group, 1024)
    h_lat = prenorm_modulate(x_lat, g_pre_mix[l], sh1, sc1, lat_group_1024, 1024)
    z_ctx = in_projection(h_ctx, w_in[l])
    z_lat = in_projection(h_lat, w_in[l])

    attn_ctx, k_new, v_new = attention_ctx(z_ctx, g_q_norm[l], g_k_norm[l])
    attn_lat = attention_lat(z_lat, cache_k[:, l].reshape(N_LAT_SEQ, PAST_LEN, KV_COLS),
                             cache_v[:, l].reshape(N_LAT_SEQ, PAST_LEN, KV_COLS),
                             _rope_tables(), g_q_norm[l], g_k_norm[l])

    def per_block(w):
        return w.reshape(2, RNN_BLOCKS, 1, RNN_BLOCK_DIM)

    w_gates = jnp.concatenate([rg_w_a[l, 0], rg_w_x[l, 0], rg_w_a[l, 1], rg_w_x[l, 1]], axis=-1).astype(BF16)
    ba, bx = per_block(rg_b_a[l]), per_block(rg_b_x[l])
    b_gates = jnp.concatenate([ba[0], bx[0], ba[1], bx[1]], axis=-1)
    zeros_state = jnp.zeros((N_CTX_SEQ, 1, D_MODEL), F32)
    rnn_ctx, hf_ctx, hb_ctx = rglru_mixer(z_ctx, CTX_LEN, conv_w[l], conv_b[l], w_gates, b_gates,
                                          rg_lambda[l], zeros_state, zeros_state)
    rnn_lat, _, _ = rglru_mixer(z_lat, LAT_LEN, conv_w[l], conv_b[l], w_gates, b_gates, rg_lambda[l],
                                state_rnn_fwd[:, l].reshape(N_LAT_SEQ, 1, D_MODEL),
                                state_rnn_bwd[:, l].reshape(N_LAT_SEQ, 1, D_MODEL))

    merged_ctx = gated_merge(attn_ctx, rnn_ctx, z_ctx, w_o_attn[l], w_o_rnn[l])
    merged_lat = gated_merge(attn_lat, rnn_lat, z_lat, w_o_attn[l], w_o_rnn[l])

    x1, h2, e_idx, gates, rank, counts = post_mix_router(
        merged_ctx, merged_lat, x_ctx, x_lat, w_out[l].astype(BF16), g_post_mix[l], gt1, g_pre_ffn[l],
        sh2, sc2, w_router[l], b_router[l])

    pos, tok_sorted, sched = _routing_tables(e_idx, rank, counts)
    y_sorted = expert_mlp(h2, tok_sorted, sched, w_gate_up[l], b_gate_up[l], w_down[l], b_down[l])

    y_ctx = combine_residual(y_sorted, pos, gates, x1, gt2, g_post_ffn[l], 0, N_CTX, ctx_group)
    y_lat = combine_residual(y_sorted, pos, gates, x1, gt2, g_post_ffn[l], N_CTX, N_LAT,
                             lambda i: 1 + i // (LAT_LEN // COMB_TB))

    return (y_ctx.reshape(N_CTX_SEQ, CTX_LEN, D_MODEL),
            y_lat.reshape(N_LAT_SEQ, LAT_LEN, D_MODEL),
            k_new.reshape(N_CTX_SEQ, 1, CTX_LEN, N_KV_HEADS, HEAD_DIM),
            v_new.reshape(N_CTX_SEQ, 1, CTX_LEN, N_KV_HEADS, HEAD_DIM),
            hf_ctx,
            hb_ctx)
```

```python
import functools

import jax
import jax.numpy as jnp
import numpy as np
from jax import lax
from jax.experimental import pallas as pl
from jax.experimental.pallas import tpu as pltpu

D_MODEL = 2048
N_CTX_SEQ = 32
CTX_LEN = 256
N_LAT_SEQ = 2
LAT_LEN = 1024
PAST_LEN = 512
N_CTX = N_CTX_SEQ * CTX_LEN
N_LAT = N_LAT_SEQ * LAT_LEN
N_TOK = N_CTX + N_LAT
GRID_W = 64
N_HEADS = 16
N_KV_HEADS = 4
HEAD_DIM = 128
KV_GROUP = N_HEADS // N_KV_HEADS
ROPE_THETA = 10000.0
RNN_BLOCKS = 16
RNN_BLOCK_DIM = 128
RG_C = 8.0
N_EXPERTS = 32
TOP_K = 4
D_FF = 2048
SWIGLU_LIMIT = 7.0
SWIGLU_ALPHA = 1.702
EPS = 1e-6
Q_COLS = N_HEADS * HEAD_DIM
KV_COLS = N_KV_HEADS * HEAD_DIM
IN_COLS = Q_COLS + 2 * KV_COLS + 4 * D_MODEL
COL_K = Q_COLS
COL_XR = Q_COLS + 2 * KV_COLS
COL_YR = COL_XR + D_MODEL
COL_GA = COL_YR + D_MODEL
COL_GR = COL_GA + D_MODEL

V7X_VMEM_BYTES = 64 * 1024 * 1024
VMEM_LIMIT = 56 * 1024 * 1024
EXPERT_VMEM_LIMIT = 60 * 1024 * 1024

ROW_TILE = 256
SUPER_TILES = 8
SUPER_ROWS = ROW_TILE * SUPER_TILES
N_ASSIGN = N_TOK * TOP_K
N_ROWS = N_ASSIGN + N_EXPERTS * ROW_TILE
N_ROW_TILES = N_ROWS // ROW_TILE
N_SUPER = N_ROW_TILES // SUPER_TILES + N_EXPERTS
FF_CHUNK = 512
N_FF_CHUNKS = D_FF // FF_CHUNK

BF16 = jnp.bfloat16
F32 = jnp.float32


def _params(semantics, vmem=VMEM_LIMIT):
    return pltpu.CompilerParams(dimension_semantics=semantics, vmem_limit_bytes=vmem)


def _rms_scale(x):
    return lax.rsqrt(jnp.mean(x * x, axis=-1, keepdims=True) + EPS)


def _sigmoid(x):
    return 1.0 / (1.0 + jnp.exp(-x))


def _mod_body(c_ref, w_ref, b_ref, o_ref):
    c = c_ref[...]
    a = (c * _sigmoid(c)).astype(BF16)
    o_ref[...] = jnp.dot(a, w_ref[...].astype(BF16), preferred_element_type=F32) + b_ref[...]


def modulation(cond8, w_mod, b_mod):
    tn = 1024
    n = w_mod.shape[1]
    return pl.pallas_call(
        _mod_body,
        grid=(n // tn,),
        in_specs=[
            pl.BlockSpec((8, D_MODEL), lambda j: (0, 0)),
            pl.BlockSpec((D_MODEL, tn), lambda j: (0, j)),
            pl.BlockSpec((1, tn), lambda j: (0, j)),
        ],
        out_specs=pl.BlockSpec((8, tn), lambda j: (0, j)),
        out_shape=jax.ShapeDtypeStruct((8, n), F32),
        compiler_params=_params(("arbitrary",)),
        name="modulation",
    )(cond8, w_mod, b_mod.reshape(1, n))


def _prenorm_body(x_ref, g_ref, sh_ref, sc_ref, o_ref):
    x = x_ref[...]
    y = x * _rms_scale(x) * g_ref[...]
    o_ref[...] = (y * (1.0 + sc_ref[...]) + sh_ref[...]).astype(o_ref.dtype)


def prenorm_modulate(x, g, shift, scale, group_of_block, tm):
    m = x.shape[0]
    gmap = lambda i: (group_of_block(i), 0, 0)
    return pl.pallas_call(
        _prenorm_body,
        grid=(m // tm,),
        in_specs=[
            pl.BlockSpec((tm, D_MODEL), lambda i: (i, 0)),
            pl.BlockSpec((1, D_MODEL), lambda i: (0, 0)),
            pl.BlockSpec((None, 1, D_MODEL), gmap),
            pl.BlockSpec((None, 1, D_MODEL), gmap),
        ],
        out_specs=pl.BlockSpec((tm, D_MODEL), lambda i: (i, 0)),
        out_shape=jax.ShapeDtypeStruct((m, D_MODEL), BF16),
        compiler_params=_params(("arbitrary",)),
        name="prenorm_modulate",
    )(x, g.reshape(1, D_MODEL), shift, scale)


def _inproj_body(h_ref, w_ref, o_ref, wbf_ref):
    @pl.when(pl.program_id(1) == 0)
    def _():
        wbf_ref[...] = w_ref[...].astype(BF16)

    o_ref[...] = jnp.dot(h_ref[...], wbf_ref[...], preferred_element_type=F32)


def in_projection(h, w_in):
    m = h.shape[0]
    tm, tn = 1024, 1024
    return pl.pallas_call(
        _inproj_body,
        grid=(IN_COLS // tn, m // tm),
        in_specs=[
            pl.BlockSpec((tm, D_MODEL), lambda j, i: (i, 0)),
            pl.BlockSpec((D_MODEL, tn), lambda j, i: (0, j)),
        ],
        out_specs=pl.BlockSpec((tm, tn), lambda j, i: (i, j)),
        out_shape=jax.ShapeDtypeStruct((m, IN_COLS), F32),
        scratch_shapes=[pltpu.VMEM((D_MODEL, tn), BF16)],
        compiler_params=_params(("arbitrary", "arbitrary")),
        name="in_projection",
    )(h, w_in)


def _rope(x, cos, sin_lo, sin_hi):
    return x * cos + pltpu.roll(x, 96, 1) * sin_lo + pltpu.roll(x, 32, 1) * sin_hi


def _head_norm(x, g):
    return x * _rms_scale(x) * g


def _softmax_pv(score_blocks, value_blocks):
    m = None
    for s in score_blocks:
        mi = jnp.max(s, axis=-1, keepdims=True)
        m = mi if m is None else jnp.maximum(m, mi)
    ps = [jnp.exp(s - m) for s in score_blocks]
    denom = None
    for p in ps:
        li = jnp.sum(p, axis=-1, keepdims=True)
        denom = li if denom is None else denom + li
    out = None
    for p, v in zip(ps, value_blocks):
        o = jnp.dot(p.astype(BF16), v, preferred_element_type=F32)
        out = o if out is None else out + o
    return out * (1.0 / denom)


def _attn_ctx_body(q_ref, kv_ref, gq_ref, gk_ref, o_ref, ko_ref, vo_ref):
    tq = q_ref.shape[0]
    scale = HEAD_DIM ** -0.5
    gq = gq_ref[...]
    gk = gk_ref[...]
    for g in range(N_KV_HEADS):
        kcols = slice(g * HEAD_DIM, (g + 1) * HEAD_DIM)
        kn = _head_norm(kv_ref[:, kcols], gk)
        v = kv_ref[:, KV_COLS + g * HEAD_DIM:KV_COLS + (g + 1) * HEAD_DIM]
        ko_ref[pl.ds(g, tq, stride=N_KV_HEADS), :] = kn
        vo_ref[pl.ds(g, tq, stride=N_KV_HEADS), :] = v
        qs = []
        for hh in range(KV_GROUP):
            h = g * KV_GROUP + hh
            qs.append((_head_norm(q_ref[:, h * HEAD_DIM:(h + 1) * HEAD_DIM], gq) * scale).astype(BF16))
        q4 = jnp.concatenate(qs, axis=0)
        s = lax.dot_general(q4, kn.astype(BF16), (((1,), (1,)), ((), ())),
                            preferred_element_type=F32)
        o = _softmax_pv([s], [v.astype(BF16)])
        for hh in range(KV_GROUP):
            h = g * KV_GROUP + hh
            o_ref[:, h * HEAD_DIM:(h + 1) * HEAD_DIM] = o[hh * tq:(hh + 1) * tq].astype(o_ref.dtype)


def attention_ctx(z, g_q, g_k):
    nb = N_CTX_SEQ
    t = CTX_LEN
    return pl.pallas_call(
        _attn_ctx_body,
        grid=(nb,),
        in_specs=[
            pl.BlockSpec((t, Q_COLS), lambda b: (b, 0)),
            pl.BlockSpec((t, 2 * KV_COLS), lambda b: (b, COL_K // (2 * KV_COLS))),
            pl.BlockSpec((1, HEAD_DIM), lambda b: (0, 0)),
            pl.BlockSpec((1, HEAD_DIM), lambda b: (0, 0)),
        ],
        out_specs=[
            pl.BlockSpec((t, Q_COLS), lambda b: (b, 0)),
            pl.BlockSpec((t * N_KV_HEADS, HEAD_DIM), lambda b: (b, 0)),
            pl.BlockSpec((t * N_KV_HEADS, HEAD_DIM), lambda b: (b, 0)),
        ],
        out_shape=[
            jax.ShapeDtypeStruct((N_CTX, Q_COLS), BF16),
            jax.ShapeDtypeStruct((N_CTX * N_KV_HEADS, HEAD_DIM), F32),
            jax.ShapeDtypeStruct((N_CTX * N_KV_HEADS, HEAD_DIM), F32),
        ],
        compiler_params=_params(("arbitrary",)),
        name="attention_ctx",
    )(z, z, g_q.reshape(1, HEAD_DIM), g_k.reshape(1, HEAD_DIM))


def _attn_lat_body(q_ref, kv_ref, ck_ref, cv_ref, cos_ref, slo_ref, shi_ref, gq_ref, gk_ref,
                   o_ref, kr_ref):
    tq = q_ref.shape[0]
    qb = pl.program_id(1)
    scale = HEAD_DIM ** -0.5
    gq = gq_ref[...]

    @pl.when(qb == 0)
    def _():
        gk = gk_ref[...]
        for g in range(N_KV_HEADS):
            kcols = slice(g * HEAD_DIM, (g + 1) * HEAD_DIM)
            kn = _head_norm(kv_ref[:, kcols], gk)
            kr_ref[:, kcols] = _rope(kn, cos_ref[...], slo_ref[...], shi_ref[...]).astype(BF16)

    row0 = pl.multiple_of(qb * tq, tq)
    cos = cos_ref[pl.ds(row0, tq), :]
    slo = slo_ref[pl.ds(row0, tq), :]
    shi = shi_ref[pl.ds(row0, tq), :]
    for g in range(N_KV_HEADS):
        kcols = slice(g * HEAD_DIM, (g + 1) * HEAD_DIM)
        qs = []
        for hh in range(KV_GROUP):
            h = g * KV_GROUP + hh
            qn = _head_norm(q_ref[:, h * HEAD_DIM:(h + 1) * HEAD_DIM], gq)
            qs.append((_rope(qn, cos, slo, shi) * scale).astype(BF16))
        q4 = jnp.concatenate(qs, axis=0)
        dn = (((1,), (1,)), ((), ()))
        s_past = lax.dot_general(q4, ck_ref[:, kcols].astype(BF16), dn, preferred_element_type=F32)
        s_new = lax.dot_general(q4, kr_ref[:, kcols], dn, preferred_element_type=F32)
        v_past = cv_ref[:, kcols].astype(BF16)
        v_new = kv_ref[:, KV_COLS + g * HEAD_DIM:KV_COLS + (g + 1) * HEAD_DIM].astype(BF16)
        o = _softmax_pv([s_past, s_new], [v_past, v_new])
        for hh in range(KV_GROUP):
            h = g * KV_GROUP + hh
            o_ref[:, h * HEAD_DIM:(h + 1) * HEAD_DIM] = o[hh * tq:(hh + 1) * tq].astype(o_ref.dtype)


def attention_lat(z, cache_k, cache_v, rope_tabs, g_q, g_k):
    tq = 256
    nq = LAT_LEN // tq
    cos, slo, shi = rope_tabs
    tab = pl.BlockSpec((LAT_LEN, HEAD_DIM), lambda b, q: (0, 0))
    return pl.pallas_call(
        _attn_lat_body,
        grid=(N_LAT_SEQ, nq),
        in_specs=[
            pl.BlockSpec((tq, Q_COLS), lambda b, q: (b * nq + q, 0)),
            pl.BlockSpec((LAT_LEN, 2 * KV_COLS), lambda b, q: (b, COL_K // (2 * KV_COLS))),
            pl.BlockSpec((None, PAST_LEN, KV_COLS), lambda b, q: (b, 0, 0)),
            pl.BlockSpec((None, PAST_LEN, KV_COLS), lambda b, q: (b, 0, 0)),
            tab, tab, tab,
            pl.BlockSpec((1, HEAD_DIM), lambda b, q: (0, 0)),
            pl.BlockSpec((1, HEAD_DIM), lambda b, q: (0, 0)),
        ],
        out_specs=pl.BlockSpec((tq, Q_COLS), lambda b, q: (b * nq + q, 0)),
        out_shape=jax.ShapeDtypeStruct((N_LAT, Q_COLS), BF16),
        scratch_shapes=[pltpu.VMEM((LAT_LEN, KV_COLS), BF16)],
        compiler_params=_params(("arbitrary", "arbitrary")),
        name="attention_lat",
    )(z, z, cache_k, cache_v, cos, slo, shi, g_q.reshape(1, HEAD_DIM), g_k.reshape(1, HEAD_DIM))


def _rope_tables():
    t = np.arange(LAT_LEN)
    row = jnp.asarray(t // GRID_W, F32)
    col = jnp.asarray(t % GRID_W, F32)
    nf = HEAD_DIM // 4
    inv_freq = ROPE_THETA ** (-jnp.arange(nf, dtype=F32) / nf)
    ang_row = row[:, None] * inv_freq[None, :]
    ang_col = col[:, None] * inv_freq[None, :]
    ang = jnp.concatenate([ang_row, ang_row, ang_col, ang_col], axis=1)
    cos = jnp.cos(ang)
    sin = jnp.sin(ang)
    first = jnp.asarray((np.arange(HEAD_DIM) % (2 * nf)) < nf)[None, :]
    return cos, jnp.where(first, -sin, 0.0), jnp.where(first, 0.0, sin)


RNN_ROWS = 2048
RNN_COLS = 512
RNN_SUB = RNN_COLS // RNN_BLOCK_DIM


def _gelu_tanh(y):
    return 0.5 * y * (1.0 + jnp.tanh(0.7978845608028654 * (y + 0.044715 * (y * y * y))))


def _rglru_body(seq_len, xr_ref, yr_ref, cw_ref, cb_ref, wg_ref, bg_ref, lam_ref, h0f_ref, h0b_ref,
                o_ref, hf_ref, hb_ref, xs_ref, af_ref, bf_ref, ab_ref, bb_ref):
    n_seq = RNN_ROWS // seq_len
    for n in range(RNN_SUB):
        cols = slice(n * RNN_BLOCK_DIM, (n + 1) * RNN_BLOCK_DIM)
        for s in range(n_seq):
            xs_ref[n, pl.ds(s, seq_len, stride=n_seq), :] = xr_ref[s * seq_len:(s + 1) * seq_len, cols]

    row = lax.broadcasted_iota(jnp.int32, (RNN_ROWS, 1), 0)
    lam = lam_ref[...]
    softplus_neg = jnp.maximum(-lam, 0.0) + jnp.log(1.0 + jnp.exp(-jnp.abs(lam)))
    rate = softplus_neg * (-RG_C * 1.4426950408889634)
    for n in range(RNN_SUB):
        cols = slice(n * RNN_BLOCK_DIM, (n + 1) * RNN_BLOCK_DIM)
        x = xs_ref[n]
        x_m1 = jnp.where(row >= n_seq, pltpu.roll(x, n_seq, 0), 0.0)
        x_p1 = jnp.where(row < RNN_ROWS - n_seq, pltpu.roll(x, RNN_ROWS - n_seq, 0), 0.0)
        x_p2 = jnp.where(row < RNN_ROWS - 2 * n_seq, pltpu.roll(x, RNN_ROWS - 2 * n_seq, 0), 0.0)
        xn = (cb_ref[:, cols] + x_m1 * cw_ref[0:1, cols] + x * cw_ref[1:2, cols]
              + x_p1 * cw_ref[2:3, cols] + x_p2 * cw_ref[3:4, cols])
        pre = jnp.dot(xn.astype(BF16), wg_ref[n], preferred_element_type=F32) + bg_ref[n]
        for d, (a_ref, b_ref) in enumerate(((af_ref, bf_ref), (ab_ref, bb_ref))):
            r = 0.5 * jnp.tanh(0.5 * pre[:, (2 * d) * RNN_BLOCK_DIM:(2 * d + 1) * RNN_BLOCK_DIM]) + 0.5
            gate_in = 0.5 * jnp.tanh(
                0.5 * pre[:, (2 * d + 1) * RNN_BLOCK_DIM:(2 * d + 2) * RNN_BLOCK_DIM]) + 0.5
            a = jnp.exp2(r * rate[d:d + 1, cols])
            v = 1.0 - a * a
            a_ref[n] = a
            b_ref[n] = (v * lax.rsqrt(jnp.maximum(v, 1e-30))) * (gate_in * xn)

    def step(t, carry):
        rows_f = pl.ds(pl.multiple_of(t * n_seq, n_seq), n_seq)
        rows_b = pl.ds(pl.multiple_of((seq_len - 1 - t) * n_seq, n_seq), n_seq)
        out = []
        for n in range(RNN_SUB):
            hf = af_ref[n, rows_f, :] * carry[2 * n] + bf_ref[n, rows_f, :]
            hb = ab_ref[n, rows_b, :] * carry[2 * n + 1] + bb_ref[n, rows_b, :]
            bf_ref[n, rows_f, :] = hf
            bb_ref[n, rows_b, :] = hb
            out += [hf, hb]
        return tuple(out)

    init = []
    for n in range(RNN_SUB):
        cols = slice(n * RNN_BLOCK_DIM, (n + 1) * RNN_BLOCK_DIM)
        init += [h0f_ref[:, 0, cols], h0b_ref[:, 0, cols]]
    last = lax.fori_loop(0, seq_len, step, tuple(init), unroll=8)
    for n in range(RNN_SUB):
        cols = slice(n * RNN_BLOCK_DIM, (n + 1) * RNN_BLOCK_DIM)
        hf_ref[:, 0, cols] = last[2 * n]
        hb_ref[:, 0, cols] = last[2 * n + 1]
        bf_ref[n] = bf_ref[n] + bb_ref[n]
        for s in range(n_seq):
            rows = slice(s * seq_len, (s + 1) * seq_len)
            h_sum = bf_ref[n, pl.ds(s, seq_len, stride=n_seq), :]
            o_ref[rows, cols] = (h_sum * _gelu_tanh(yr_ref[rows, cols])).astype(o_ref.dtype)


def rglru_mixer(z, seq_len, conv_w, conv_b, w_gates, b_gates, lam, h0_f, h0_b):
    m = z.shape[0]
    n_seq_total = m // seq_len
    n_seq = RNN_ROWS // seq_len
    cblk = lambda base: (lambda r, c: (r, base // RNN_COLS + c))
    state_spec = pl.BlockSpec((n_seq, 1, RNN_COLS), lambda r, c: (r, 0, c))
    return pl.pallas_call(
        functools.partial(_rglru_body, seq_len),
        grid=(m // RNN_ROWS, D_MODEL // RNN_COLS),
        in_specs=[
            pl.BlockSpec((RNN_ROWS, RNN_COLS), cblk(COL_XR)),
            pl.BlockSpec((RNN_ROWS, RNN_COLS), cblk(COL_YR)),
            pl.BlockSpec((4, RNN_COLS), lambda r, c: (0, c)),
            pl.BlockSpec((1, RNN_COLS), lambda r, c: (0, c)),
            pl.BlockSpec((RNN_SUB, RNN_BLOCK_DIM, 4 * RNN_BLOCK_DIM), lambda r, c: (c, 0, 0)),
            pl.BlockSpec((RNN_SUB, 1, 4 * RNN_BLOCK_DIM), lambda r, c: (c, 0, 0)),
            pl.BlockSpec((2, RNN_COLS), lambda r, c: (0, c)),
            state_spec, state_spec,
        ],
        out_specs=[
            pl.BlockSpec((RNN_ROWS, RNN_COLS), lambda r, c: (r, c)),
            state_spec, state_spec,
        ],
        out_shape=[
            jax.ShapeDtypeStruct((m, D_MODEL), BF16),
            jax.ShapeDtypeStruct((n_seq_total, 1, D_MODEL), F32),
            jax.ShapeDtypeStruct((n_seq_total, 1, D_MODEL), F32),
        ],
        scratch_shapes=[pltpu.VMEM((RNN_SUB, RNN_ROWS, RNN_BLOCK_DIM), F32) for _ in range(5)],
        compiler_params=_params(("arbitrary", "arbitrary")),
        name="rglru_mixer_t%d" % seq_len,
    )(z, z, conv_w, conv_b.reshape(1, D_MODEL), w_gates, b_gates, lam, h0_f, h0_b)


def _merge_body(a_ref, r_ref, wa_ref, wr_ref, ga_ref, gr_ref, o_ref, wa_bf, wr_bf):
    @pl.when(pl.program_id(1) == 0)
    def _():
        wa_bf[...] = wa_ref[...].astype(BF16)
        wr_bf[...] = wr_ref[...].astype(BF16)

    pa = jnp.dot(a_ref[...], wa_bf[...], preferred_element_type=F32)
    pr = jnp.dot(r_ref[...], wr_bf[...], preferred_element_type=F32)
    o_ref[...] = (_sigmoid(ga_ref[...]) * pa + _sigmoid(gr_ref[...]) * pr).astype(o_ref.dtype)


def gated_merge(attn, rnn, z, w_o_attn, w_o_rnn):
    m = attn.shape[0]
    tm, tn = 1024, 512
    return pl.pallas_call(
        _merge_body,
        grid=(D_MODEL // tn, m // tm),
        in_specs=[
            pl.BlockSpec((tm, Q_COLS), lambda j, i: (i, 0)),
            pl.BlockSpec((tm, D_MODEL), lambda j, i: (i, 0)),
            pl.BlockSpec((Q_COLS, tn), lambda j, i: (0, j)),
            pl.BlockSpec((D_MODEL, tn), lambda j, i: (0, j)),
            pl.BlockSpec((tm, tn), lambda j, i: (i, COL_GA // tn + j)),
            pl.BlockSpec((tm, tn), lambda j, i: (i, COL_GR // tn + j)),
        ],
        out_specs=pl.BlockSpec((tm, tn), lambda j, i: (i, j)),
        out_shape=jax.ShapeDtypeStruct((m, D_MODEL), BF16),
        scratch_shapes=[pltpu.VMEM((Q_COLS, tn), BF16), pltpu.VMEM((D_MODEL, tn), BF16)],
        compiler_params=_params(("arbitrary", "arbitrary")),
        name="gated_merge",
    )(attn, rnn, w_o_attn, w_o_rnn, z, z)


POST_TM = 512
POST_SPLIT = 1
HALF_D = D_MODEL // 2
WORD_ROWS = HALF_D // 128
SUBLANES = 8
POST_CTX_BLOCKS = N_CTX // POST_TM
LAT_BLOCKS_PER_SEQ = LAT_LEN // POST_TM


def _post_group(i):
    return jnp.where(i < POST_CTX_BLOCKS, 0, 1 + (i - POST_CTX_BLOCKS) // LAT_BLOCKS_PER_SEQ)


def _postmix_body(mc_ref, ml_ref, xc_ref, xl_ref, wo_ref, gpm_ref, gt1_ref, gpf_ref, sh2_ref, sc2_ref,
                  wr_ref, br_ref, x1_ref, h2_ref, e_ref, gate_ref, rank_ref, cnt_ref, carry_ref):
    i = pl.program_id(0)
    tm = POST_TM

    @pl.when(i == 0)
    def _():
        carry_ref[...] = jnp.zeros_like(carry_ref)

    is_ctx = i < POST_CTX_BLOCKS
    th = tm // POST_SPLIT
    r_io = lax.broadcasted_iota(jnp.int32, (th, th), 0)
    c_io = lax.broadcasted_iota(jnp.int32, (th, th), 1)
    lower = jnp.where(c_io < r_io, 1.0, 0.0).astype(BF16)
    lane = lax.broadcasted_iota(jnp.int32, (th, N_EXPERTS), 1)
    lane_k = lax.broadcasted_iota(jnp.int32, (th, TOP_K), 1)
    wr = wr_ref[...].astype(BF16)
    carry = carry_ref[...]
    for part in range(POST_SPLIT):
        rows = slice(part * th, (part + 1) * th)
        merged = jnp.where(is_ctx, mc_ref[rows, :], ml_ref[rows, :])
        x = jnp.where(is_ctx, xc_ref[rows, :], xl_ref[rows, :])
        o = jnp.dot(merged, wo_ref[...], preferred_element_type=F32)
        x1 = x + gt1_ref[...] * (o * _rms_scale(o) * gpm_ref[...])
        x1_ref[rows, :] = x1
        h2 = (x1 * _rms_scale(x1) * gpf_ref[...]) * (1.0 + sc2_ref[...]) + sh2_ref[...]
        h2_bf = h2.astype(BF16)
        bits = lax.bitcast_convert_type(h2_bf.astype(F32), jnp.uint32)
        words = (lax.shift_right_logical(bits[:, :HALF_D], jnp.uint32(16))
                 | (bits[:, HALF_D:] & jnp.uint32(0xFFFF0000)))
        for c in range(WORD_ROWS):
            h2_ref[pl.ds(part * th * WORD_ROWS + c, th, stride=WORD_ROWS), :] = words[:, c * 128:(c + 1) * 128]

        logits = jnp.dot(h2_bf, wr, preferred_element_type=F32) + br_ref[...]
        work = logits
        chosen = jnp.zeros((th, N_EXPERTS), F32)
        sels, vals, idxs = [], [], []
        for _ in range(TOP_K):
            mx = jnp.max(work, axis=-1, keepdims=True)
            idx = jnp.min(jnp.where(work == mx, lane, N_EXPERTS), axis=-1, keepdims=True)
            sel = lane == idx
            work = jnp.where(sel, -jnp.inf, work)
            chosen = jnp.where(sel, 1.0, chosen)
            sels.append(sel)
            vals.append(mx)
            idxs.append(idx)
        exps = [jnp.exp(v - vals[0]) for v in vals]
        inv = 1.0 / (exps[0] + exps[1] + exps[2] + exps[3])

        before = jnp.dot(lower, chosen.astype(BF16), preferred_element_type=F32) + carry
        carry = carry + jnp.sum(chosen, axis=0, keepdims=True)

        e_out = jnp.zeros((th, TOP_K), jnp.int32)
        g_out = jnp.zeros((th, TOP_K), F32)
        r_out = jnp.zeros((th, TOP_K), jnp.int32)
        for k in range(TOP_K):
            rk = jnp.sum(jnp.where(sels[k], before, 0.0), axis=-1, keepdims=True).astype(jnp.int32)
            e_out = jnp.where(lane_k == k, idxs[k], e_out)
            g_out = jnp.where(lane_k == k, exps[k] * inv, g_out)
            r_out = jnp.where(lane_k == k, rk, r_out)
        e_ref[rows, :] = e_out
        gate_ref[rows, :] = g_out
        rank_ref[rows, :] = r_out
    carry_ref[...] = carry
    cnt_ref[...] = carry


def post_mix_router(merged_ctx, merged_lat, x_ctx, x_lat, w_out_bf, g_post_mix, gt1, g_pre_ffn, sh2, sc2,
                    w_router, b_router):
    tm = POST_TM
    ctx_map = lambda i: (jnp.minimum(i, POST_CTX_BLOCKS - 1), 0)
    lat_map = lambda i: (jnp.maximum(i - POST_CTX_BLOCKS, 0), 0)
    gmap = lambda i: (_post_group(i), 0, 0)
    row = lambda i: (i, 0)
    const = lambda i: (0, 0)
    vec = pl.BlockSpec((1, D_MODEL), const)
    gvec = pl.BlockSpec((None, 1, D_MODEL), gmap)
    return pl.pallas_call(
        _postmix_body,
        grid=(N_TOK // tm,),
        in_specs=[
            pl.BlockSpec((tm, D_MODEL), ctx_map),
            pl.BlockSpec((tm, D_MODEL), lat_map),
            pl.BlockSpec((tm, D_MODEL), ctx_map),
            pl.BlockSpec((tm, D_MODEL), lat_map),
            pl.BlockSpec((D_MODEL, D_MODEL), const),
            vec, gvec, vec, gvec, gvec,
            pl.BlockSpec((D_MODEL, N_EXPERTS), const),
            pl.BlockSpec((1, N_EXPERTS), const),
        ],
        out_specs=[
            pl.BlockSpec((tm, D_MODEL), row),
            pl.BlockSpec((tm * WORD_ROWS, 128), row),
            pl.BlockSpec((tm, TOP_K), row),
            pl.BlockSpec((tm, TOP_K), row),
            pl.BlockSpec((tm, TOP_K), row),
            pl.BlockSpec((1, N_EXPERTS), const),
        ],
        out_shape=[
            jax.ShapeDtypeStruct((N_TOK, D_MODEL), F32),
            jax.ShapeDtypeStruct((N_TOK * WORD_ROWS, 128), jnp.uint32),
            jax.ShapeDtypeStruct((N_TOK, TOP_K), jnp.int32),
            jax.ShapeDtypeStruct((N_TOK, TOP_K), F32),
            jax.ShapeDtypeStruct((N_TOK, TOP_K), jnp.int32),
            jax.ShapeDtypeStruct((1, N_EXPERTS), F32),
        ],
        scratch_shapes=[pltpu.VMEM((1, N_EXPERTS), F32)],
        compiler_params=_params(("arbitrary",)),
        name="post_mix_router",
    )(merged_ctx, merged_lat, x_ctx, x_lat, w_out_bf, g_post_mix.reshape(1, D_MODEL), gt1,
      g_pre_ffn.reshape(1, D_MODEL), sh2, sc2, w_router, b_router.reshape(1, N_EXPERTS))


GATHER_SHIFT = 4
GATHER_UNROLL = 1 << GATHER_SHIFT
GATHER_PRIORITY = 1


def _unpack_tile(xbuf_ref, slot, i):
    base = pl.multiple_of(i * (ROW_TILE * WORD_ROWS), ROW_TILE * WORD_ROWS)
    lo, hi = [], []
    for c in range(WORD_ROWS):
        words = xbuf_ref[slot, pl.ds(base + c, ROW_TILE, stride=WORD_ROWS), :]
        lo.append(lax.bitcast_convert_type(lax.shift_left(words, jnp.uint32(16)), F32).astype(BF16))
        hi.append(lax.bitcast_convert_type(words & jnp.uint32(0xFFFF0000), F32).astype(BF16))
    return jnp.concatenate(lo + hi, axis=1)


def _for_tiles(n_tiles, body):
    def one(i, _):
        body(i)
        return 0

    lax.fori_loop(0, n_tiles, one, 0)


def _moe_body(exp_ref, row_ref, nsub_ref, nzero_ref, npass_ref, rows_ref, tok_ref, h_ref, wg_ref, wl_ref, wd_ref, bg_ref,
              bl_ref, bd_ref, y_ref, xbuf_ref, act_ref, stage_ref, idx_ref, pend_ref,
              xsem, isem, ysem):
    s = pl.program_id(0)
    j = pl.program_id(1)
    n_pass = npass_ref[0]
    n_sub = nsub_ref[s]
    row_start = row_ref[s]

    def idx_copy(p):
        tile0 = pl.multiple_of(row_ref[p], ROW_TILE) // ROW_TILE
        return pltpu.make_async_copy(tok_ref.at[pl.ds(tile0, SUPER_TILES)], idx_ref.at[p % 2],
                                     isem.at[p % 2])

    def row_groups(p):
        return lax.shift_right_logical(rows_ref[p] + (GATHER_UNROLL - 1), GATHER_SHIFT)

    def gather_rows(p):
        slot = p % 2

        def issue(grp, _):
            first = grp * GATHER_UNROLL
            tile = lax.shift_right_logical(first, 8)
            col = jnp.bitwise_and(first, ROW_TILE - 1)
            for g in range(GATHER_UNROLL):
                t = idx_ref[slot, tile, 0, col + g]
                src = h_ref.at[t]
                dst = xbuf_ref.at[slot, pl.ds(pl.multiple_of((first + g) * WORD_ROWS, WORD_ROWS), WORD_ROWS), :]
                pltpu.make_async_copy(src, dst, xsem.at[slot]).start(priority=GATHER_PRIORITY)
            return 0

        lax.fori_loop(0, row_groups(p), issue, 0)

    def wait_rows(p):
        n_grp = row_groups(p)

        @pl.when(n_grp > 0)
        def _():
            n = pl.multiple_of(n_grp * (GATHER_UNROLL * WORD_ROWS), GATHER_UNROLL * WORD_ROWS)
            window = xbuf_ref.at[p % 2, pl.ds(0, n), :]
            pltpu.make_async_copy(window, window, xsem.at[p % 2]).wait()

    @pl.when(jnp.logical_and(s == 0, j == 0))
    def _():
        pend_ref[0] = 0
        pend_ref[1] = 0

    def drain_stage(slot):
        @pl.when(pend_ref[slot] == 1)
        def _():
            pltpu.make_async_copy(stage_ref.at[slot], stage_ref.at[slot], ysem.at[slot]).wait()
            pend_ref[slot] = 0

    @pl.when(jnp.logical_and(s == 0, j == 0))
    def _():
        xbuf_ref[...] = jnp.zeros(xbuf_ref.shape, xbuf_ref.dtype)
        idx_copy(0).start()
        idx_copy(0).wait()
        gather_rows(0)
        idx_copy(1).start()

    @pl.when(j == 0)
    def _():
        wait_rows(s)

    @pl.when(jnp.logical_and(j == 0, s + 1 < n_pass))
    def _():
        idx_copy(s + 1).wait()
        gather_rows(s + 1)

    @pl.when(jnp.logical_and(j == 0, s + 2 < n_pass))
    def _():
        idx_copy(s + 2).start()

    @pl.when(jnp.logical_and(j < N_FF_CHUNKS, n_sub > 0))
    def _():
        bg = bg_ref[...]
        bl = bl_ref[...]

        def up_tile(i):
            rows = pl.ds(pl.multiple_of(i * ROW_TILE, ROW_TILE), ROW_TILE)
            xt = _unpack_tile(xbuf_ref, s % 2, i)
            glu = jnp.minimum(jnp.dot(xt, wg_ref[...].astype(BF16), preferred_element_type=F32) + bg,
                              SWIGLU_LIMIT)
            lin = jnp.clip(jnp.dot(xt, wl_ref[...].astype(BF16), preferred_element_type=F32) + bl,
                           -SWIGLU_LIMIT, SWIGLU_LIMIT)
            act = glu * _sigmoid(SWIGLU_ALPHA * glu) * (lin + 1.0)
            act_ref[j, rows, :] = act.astype(BF16)

        _for_tiles(n_sub, up_tile)

    for cc in range(N_FF_CHUNKS):
        @pl.when(jnp.logical_and(j == N_FF_CHUNKS + cc, n_sub > 0))
        def _(cc=cc):
            bd = bd_ref[...]

            def out_copy(i, slot):
                dst = y_ref.at[pl.ds(pl.multiple_of(row_start + i * ROW_TILE, ROW_TILE), ROW_TILE),
                               cc * FF_CHUNK:(cc + 1) * FF_CHUNK]
                return pltpu.make_async_copy(stage_ref.at[slot], dst, ysem.at[slot])

            def down_tile(i):
                rows = pl.ds(pl.multiple_of(i * ROW_TILE, ROW_TILE), ROW_TILE)
                slot = i % 2
                drain_stage(slot)
                acc = bd
                for c in range(N_FF_CHUNKS):
                    acc = acc + jnp.dot(act_ref[c, rows, :],
                                        wd_ref[c * FF_CHUNK:(c + 1) * FF_CHUNK, :].astype(BF16),
                                        preferred_element_type=F32)
                stage_ref[slot] = acc
                out_copy(i, slot).start()
                pend_ref[slot] = 1

            _for_tiles(n_sub, down_tile)

    n_zero = nzero_ref[s]

    @pl.when(jnp.logical_and(j == 0, n_zero > 0))
    def _():
        drain_stage(0)
        stage_ref[0] = jnp.zeros((ROW_TILE, FF_CHUNK), F32)

        def zero_copy(i, cc):
            dst = y_ref.at[pl.ds(pl.multiple_of(row_start + i * ROW_TILE, ROW_TILE), ROW_TILE),
                           cc * FF_CHUNK:(cc + 1) * FF_CHUNK]
            return pltpu.make_async_copy(stage_ref.at[0], dst, ysem.at[0])

        def issue(i, _):
            for cc in range(N_FF_CHUNKS):
                zero_copy(i, cc).start()
            return 0

        def drain(i, _):
            for cc in range(N_FF_CHUNKS):
                zero_copy(i, cc).wait()
            return 0

        lax.fori_loop(0, n_zero, issue, 0)
        lax.fori_loop(0, n_zero, drain, 0)

    @pl.when(jnp.logical_and(s == n_pass - 1, j == 2 * N_FF_CHUNKS - 1))
    def _():
        drain_stage(0)
        drain_stage(1)


def expert_mlp(h_packed, tok_sorted, sched, w_gate_up, b_gate_up, w_down, b_down):
    exp_of, row_of, nsub_of, nzero_of, n_pass, rows_of = sched
    last = N_FF_CHUNKS - 1
    up_of = lambda s, j, n: jnp.where(n[s] > 0, jnp.minimum(j, last), last)
    up_chunk = lambda s, j, e, r, n, z, p, c: (e[s], 0, up_of(s, j, n))
    lin_chunk = lambda s, j, e, r, n, z, p, c: (e[s], 0, N_FF_CHUNKS + up_of(s, j, n))

    def down_chunk(s, j, e, r, n, z, p, c):
        in_down = jnp.logical_and(n[s] > 0, j >= N_FF_CHUNKS)
        expert = jnp.where(in_down, e[s], e[jnp.maximum(s - 1, 0)])
        return expert, 0, jnp.where(in_down, j - N_FF_CHUNKS, last)
    grid_spec = pltpu.PrefetchScalarGridSpec(
        num_scalar_prefetch=6,
        grid=(n_pass[0], 2 * N_FF_CHUNKS),
        in_specs=[
            pl.BlockSpec(memory_space=pl.ANY),
            pl.BlockSpec(memory_space=pl.ANY),
            pl.BlockSpec((None, D_MODEL, FF_CHUNK), up_chunk),
            pl.BlockSpec((None, D_MODEL, FF_CHUNK), lin_chunk),
            pl.BlockSpec((None, D_FF, FF_CHUNK), down_chunk),
            pl.BlockSpec((None, 1, FF_CHUNK), up_chunk),
            pl.BlockSpec((None, 1, FF_CHUNK), lin_chunk),
            pl.BlockSpec((None, 1, FF_CHUNK), down_chunk),
        ],
        out_specs=pl.BlockSpec(memory_space=pl.ANY),
        scratch_shapes=[
            pltpu.VMEM((2, SUPER_ROWS * WORD_ROWS, 128), jnp.uint32),
            pltpu.VMEM((N_FF_CHUNKS, SUPER_ROWS, FF_CHUNK), BF16),
            pltpu.VMEM((2, ROW_TILE, FF_CHUNK), F32),
            pltpu.SMEM((2, SUPER_TILES, 1, ROW_TILE), jnp.int32),
            pltpu.SMEM((2,), jnp.int32),
            pltpu.SemaphoreType.DMA((2,)),
            pltpu.SemaphoreType.DMA((2,)),
            pltpu.SemaphoreType.DMA((2,)),
        ],
    )
    tok_tiles = jnp.concatenate([tok_sorted.reshape(N_ROW_TILES, 1, ROW_TILE),
                                 jnp.zeros((SUPER_TILES, 1, ROW_TILE), jnp.int32)], axis=0)
    h_packed = h_packed.reshape(N_TOK, WORD_ROWS, 128)
    return pl.pallas_call(
        _moe_body,
        grid_spec=grid_spec,
        out_shape=jax.ShapeDtypeStruct((N_ROWS, D_MODEL), F32),
        compiler_params=_params(("arbitrary", "arbitrary"), vmem=EXPERT_VMEM_LIMIT),
        name="expert_mlp",
    )(exp_of, row_of, nsub_of, nzero_of, n_pass, rows_of, tok_tiles, h_packed, w_gate_up, w_gate_up, w_down,
      b_gate_up.reshape(N_EXPERTS, 1, 2 * D_FF), b_gate_up.reshape(N_EXPERTS, 1, 2 * D_FF),
      b_down.reshape(N_EXPERTS, 1, D_MODEL))


COMB_TB = 256


def _combine_start(y_ref, ybuf_ref, pos_ref, sem):
    def issue(t, _):
        for k in range(TOP_K):
            p = pos_ref[0, 0, t * TOP_K + k]
            pltpu.make_async_copy(y_ref.at[pl.ds(p, 1), :], ybuf_ref.at[k, pl.ds(t, 1), :], sem).start()
        return 0

    lax.fori_loop(0, COMB_TB, issue, 0, unroll=4)


def _combine_body(n, pos_ref, pos_next_ref, y_ref, gate_ref, x1_ref, gt2_ref, g_ref, o_ref, ybuf_ref, sem_ref):
    i = pl.program_id(0)
    slot = i % 2

    @pl.when(i == 0)
    def _():
        _combine_start(y_ref, ybuf_ref.at[0], pos_ref, sem_ref.at[0])

    @pl.when(i + 1 < n)
    def _():
        _combine_start(y_ref, ybuf_ref.at[1 - slot], pos_next_ref, sem_ref.at[1 - slot])

    for k in range(TOP_K):
        pltpu.make_async_copy(y_ref.at[pl.ds(0, COMB_TB), :], ybuf_ref.at[slot, k], sem_ref.at[slot]).wait()
    gates = gate_ref[...]
    ffn = gates[:, 0:1] * ybuf_ref[slot, 0]
    for k in range(1, TOP_K):
        ffn = ffn + gates[:, k:k + 1] * ybuf_ref[slot, k]
    o_ref[...] = x1_ref[...] + gt2_ref[...] * (ffn * _rms_scale(ffn) * g_ref[...])


def combine_residual(y_sorted, pos, gates, x1, gt2, g_post_ffn, row_offset, n_rows, group_of_block):
    tb = COMB_TB
    nblk = n_rows // tb
    off = row_offset // tb
    pos3 = pos.reshape(N_TOK // tb, 1, tb * TOP_K)
    smem_blk = lambda f: pl.BlockSpec((1, 1, tb * TOP_K), f, memory_space=pltpu.SMEM)
    return pl.pallas_call(
        functools.partial(_combine_body, nblk),
        grid=(nblk,),
        in_specs=[
            smem_blk(lambda i: (off + i, 0, 0)),
            smem_blk(lambda i: (off + jnp.minimum(i + 1, nblk - 1), 0, 0)),
            pl.BlockSpec(memory_space=pl.ANY),
            pl.BlockSpec((tb, TOP_K), lambda i: (off + i, 0)),
            pl.BlockSpec((tb, D_MODEL), lambda i: (off + i, 0)),
            pl.BlockSpec((None, 1, D_MODEL), lambda i: (group_of_block(i), 0, 0)),
            pl.BlockSpec((1, D_MODEL), lambda i: (0, 0)),
        ],
        out_specs=pl.BlockSpec((tb, D_MODEL), lambda i: (i, 0)),
        out_shape=jax.ShapeDtypeStruct((n_rows, D_MODEL), F32),
        scratch_shapes=[pltpu.VMEM((2, TOP_K, tb, D_MODEL), F32), pltpu.SemaphoreType.DMA((2,))],
        compiler_params=_params(("arbitrary",)),
        name="combine_residual",
    )(pos3, pos3, y_sorted, gates, x1, gt2, g_post_ffn.reshape(1, D_MODEL))


INV_CHUNK = 4096


def _row_tokens_body(pos_ref, zeros_ref, o_ref, sem):
    i = pl.program_id(0)

    @pl.when(i == 0)
    def _():
        cp = pltpu.make_async_copy(zeros_ref, o_ref, sem)
        cp.start()
        cp.wait()

    first_token = i * (INV_CHUNK // TOP_K)

    def put(t, _):
        for k in range(TOP_K):
            o_ref[pos_ref[t * TOP_K + k]] = first_token + t
        return 0

    lax.fori_loop(0, INV_CHUNK // TOP_K, put, 0, unroll=4)


def row_tokens(pos):
    return pl.pallas_call(
        _row_tokens_body,
        grid=(N_ASSIGN // INV_CHUNK,),
        in_specs=[
            pl.BlockSpec((INV_CHUNK,), lambda i: (i,), memory_space=pltpu.SMEM),
            pl.BlockSpec(memory_space=pl.ANY),
        ],
        out_specs=pl.BlockSpec(memory_space=pltpu.SMEM),
        out_shape=jax.ShapeDtypeStruct((N_ROWS,), jnp.int32),
        scratch_shapes=[pltpu.SemaphoreType.DMA(())],
        compiler_params=_params(("arbitrary",)),
        name="row_tokens",
    )(pos.reshape(N_ASSIGN), jnp.zeros((N_ROWS,), jnp.int32))


def _routing_tables(e_idx, rank, counts_f):
    counts = counts_f.reshape(N_EXPERTS).astype(jnp.int32)
    n_tiles = (counts + ROW_TILE - 1) // ROW_TILE
    padded = n_tiles * ROW_TILE
    pad_end = jnp.cumsum(padded)
    pad_start = pad_end - padded
    pos = (pad_start[e_idx] + rank).astype(jnp.int32)
    tok_sorted = row_tokens(pos)
    n_pass = (n_tiles + SUPER_TILES - 1) // SUPER_TILES
    pass_end = jnp.cumsum(n_pass)
    total = pass_end[-1]
    s = jnp.arange(N_SUPER, dtype=jnp.int32)
    s_eff = jnp.minimum(s, total - 1)
    e_of = jnp.minimum(jnp.searchsorted(pass_end, s_eff, side="right"), N_EXPERTS - 1).astype(jnp.int32)
    local = s_eff - (pass_end[e_of] - n_pass[e_of])
    row_of = pad_start[e_of] + local * SUPER_ROWS
    nsub = jnp.minimum(SUPER_TILES, n_tiles[e_of] - local * SUPER_TILES)
    nsub = jnp.where(s < total, nsub, 0).astype(jnp.int32)
    zero_row = pad_end[-1] + (s - total) * SUPER_ROWS
    nzero = jnp.clip((N_ROWS - zero_row) // ROW_TILE, 0, SUPER_TILES)
    nzero = jnp.where(s >= total, nzero, 0).astype(jnp.int32)
    row_of = jnp.where(s < total, row_of, jnp.minimum(zero_row, N_ROWS - ROW_TILE)).astype(jnp.int32)
    tail_tiles = (N_ROWS - pad_end[-1]) // ROW_TILE
    n_pass = jnp.minimum(total + jnp.maximum((tail_tiles + SUPER_TILES - 1) // SUPER_TILES, 1), N_SUPER)
    n_pass = n_pass.astype(jnp.int32).reshape(1)
    rows_of = jnp.clip(counts[e_of] - local * SUPER_ROWS, 0, SUPER_ROWS)
    rows_of = jnp.where(s < total, rows_of, 0).astype(jnp.int32)
    return pos.astype(jnp.int32), tok_sorted, (e_of, row_of, nsub, nzero, n_pass, rows_of)


def kernel(x_prompt, x_sample, cache_k, cache_v, state_rnn_fwd, state_rnn_bwd, c, c_ctx, w_mod, b_mod, g_pre_mix, w_in, g_q_norm, g_k_norm, conv_w, conv_b, rg_w_a, rg_b_a, rg_w_x, rg_b_x, rg_lambda, w_o_attn, w_o_rnn, w_out, g_post_mix, g_pre_ffn, w_router, b_router, w_gate_up, b_gate_up, w_down, b_down, g_post_ffn):
    l = 0
    x_ctx = x_prompt.reshape(N_CTX, D_MODEL)
    x_lat = x_sample.reshape(N_LAT, D_MODEL)

    cond8 = jnp.concatenate([c_ctx[None, :], c, jnp.zeros((8 - 1 - N_LAT_SEQ, D_MODEL), F32)], axis=0)
    mod = modulation(cond8, w_mod[l], b_mod[l])[:1 + N_LAT_SEQ].reshape(1 + N_LAT_SEQ, 6, 1, D_MODEL)
    sh1, sc1, gt1, sh2, sc2, gt2 = [mod[:, i] for i in range(6)]

    ctx_group = lambda i: 0
    lat_group_1024 = lambda i: 1 + i
    h_ctx = prenorm_modulate(x_ctx, g_pre_mix[l], sh1, sc1, ctx_group, 1024)
    h_lat = prenorm_modulate(x_lat, g_pre_mix[l], sh1, sc1, lat_group_1024, 1024)
    z_ctx = in_projection(h_ctx, w_in[l])
    z_lat = in_projection(h_lat, w_in[l])

    attn_ctx, k_new, v_new = attention_ctx(z_ctx, g_q_norm[l], g_k_norm[l])
    attn_lat = attention_lat(z_lat, cache_k[:, l].reshape(N_LAT_SEQ, PAST_LEN, KV_COLS),
                             cache_v[:, l].reshape(N_LAT_SEQ, PAST_LEN, KV_COLS),
                             _rope_tables(), g_q_norm[l], g_k_norm[l])

    def per_block(w):
        return w.reshape(2, RNN_BLOCKS, 1, RNN_BLOCK_DIM)

    w_gates = jnp.concatenate([rg_w_a[l, 0], rg_w_x[l, 0], rg_w_a[l, 1], rg_w_x[l, 1]], axis=-1).astype(BF16)
    ba, bx = per_block(rg_b_a[l]), per_block(rg_b_x[l])
    b_gates = jnp.concatenate([ba[0], bx[0], ba[1], bx[1]], axis=-1)
    zeros_state = jnp.zeros((N_CTX_SEQ, 1, D_MODEL), F32)
    rnn_ctx, hf_ctx, hb_ctx = rglru_mixer(z_ctx, CTX_LEN, conv_w[l], conv_b[l], w_gates, b_gates,
                                          rg_lambda[l], zeros_state, zeros_state)
    rnn_lat, _, _ = rglru_mixer(z_lat, LAT_LEN, conv_w[l], conv_b[l], w_gates, b_gates, rg_lambda[l],
                                state_rnn_fwd[:, l].reshape(N_LAT_SEQ, 1, D_MODEL),
                                state_rnn_bwd[:, l].reshape(N_LAT_SEQ, 1, D_MODEL))

    merged_ctx = gated_merge(attn_ctx, rnn_ctx, z_ctx, w_o_attn[l], w_o_rnn[l])
    merged_lat = gated_merge(attn_lat, rnn_lat, z_lat, w_o_attn[l], w_o_rnn[l])

    x1, h2, e_idx, gates, rank, counts = post_mix_router(
        merged_ctx, merged_lat, x_ctx, x_lat, w_out[l].astype(BF16), g_post_mix[l], gt1, g_pre_ffn[l],
        sh2, sc2, w_router[l], b_router[l])

    pos, tok_sorted, sched = _routing_tables(e_idx, rank, counts)
    y_sorted = expert_mlp(h2, tok_sorted, sched, w_gate_up[l], b_gate_up[l], w_down[l], b_down[l])

    y_ctx = combine_residual(y_sorted, pos, gates, x1, gt2, g_post_ffn[l], 0, N_CTX, ctx_group)
    y_lat = combine_residual(y_sorted, pos, gates, x1, gt2, g_post_ffn[l], N_CTX, N_LAT,
                             lambda i: 1 + i // (LAT_LEN // COMB_TB))

    return (y_ctx.reshape(N_CTX_SEQ, CTX_LEN, D_MODEL),
            y_lat.reshape(N_LAT_SEQ, LAT_LEN, D_MODEL),
            k_new.reshape(N_CTX_SEQ, 1, CTX_LEN, N_KV_HEADS, HEAD_DIM),
            v_new.reshape(N_CTX_SEQ, 1, CTX_LEN, N_KV_HEADS, HEAD_DIM),
            hf_ctx,
            hb_ctx)
```

```python
import functools

import jax
import jax.numpy as jnp
import numpy as np
from jax import lax
from jax.experimental import pallas as pl
from jax.experimental.pallas import tpu as pltpu

D_MODEL = 2048
N_CTX_SEQ = 32
CTX_LEN = 256
N_LAT_SEQ = 2
LAT_LEN = 1024
PAST_LEN = 512
N_CTX = N_CTX_SEQ * CTX_LEN
N_LAT = N_LAT_SEQ * LAT_LEN
N_TOK = N_CTX + N_LAT
GRID_W = 64
N_HEADS = 16
N_KV_HEADS = 4
HEAD_DIM = 128
KV_GROUP = N_HEADS // N_KV_HEADS
ROPE_THETA = 10000.0
RNN_BLOCKS = 16
RNN_BLOCK_DIM = 128
RG_C = 8.0
N_EXPERTS = 32
TOP_K = 4
D_FF = 2048
SWIGLU_LIMIT = 7.0
SWIGLU_ALPHA = 1.702
EPS = 1e-6
Q_COLS = N_HEADS * HEAD_DIM
KV_COLS = N_KV_HEADS * HEAD_DIM
IN_COLS = Q_COLS + 2 * KV_COLS + 4 * D_MODEL
COL_K = Q_COLS
COL_XR = Q_COLS + 2 * KV_COLS
COL_YR = COL_XR + D_MODEL
COL_GA = COL_YR + D_MODEL
COL_GR = COL_GA + D_MODEL

V7X_VMEM_BYTES = 64 * 1024 * 1024
VMEM_LIMIT = 56 * 1024 * 1024
EXPERT_VMEM_LIMIT = 60 * 1024 * 1024

ROW_TILE = 256
SUPER_TILES = 8
SUPER_ROWS = ROW_TILE * SUPER_TILES
N_ASSIGN = N_TOK * TOP_K
N_ROWS = N_ASSIGN + N_EXPERTS * ROW_TILE
N_ROW_TILES = N_ROWS // ROW_TILE
N_SUPER = N_ROW_TILES // SUPER_TILES + N_EXPERTS
FF_CHUNK = 512
N_FF_CHUNKS = D_FF // FF_CHUNK

BF16 = jnp.bfloat16
F32 = jnp.float32


def _params(semantics, vmem=VMEM_LIMIT):
    return pltpu.CompilerParams(dimension_semantics=semantics, vmem_limit_bytes=vmem)


def _rms_scale(x):
    return lax.rsqrt(jnp.mean(x * x, axis=-1, keepdims=True) + EPS)


def _sigmoid(x):
    return 1.0 / (1.0 + jnp.exp(-x))


def _mod_body(c_ref, w_ref, b_ref, o_ref):
    c = c_ref[...]
    a = (c * _sigmoid(c)).astype(BF16)
    o_ref[...] = jnp.dot(a, w_ref[...].astype(BF16), preferred_element_type=F32) + b_ref[...]


def modulation(cond8, w_mod, b_mod):
    tn = 1024
    n = w_mod.shape[1]
    return pl.pallas_call(
        _mod_body,
        grid=(n // tn,),
        in_specs=[
            pl.BlockSpec((8, D_MODEL), lambda j: (0, 0)),
            pl.BlockSpec((D_MODEL, tn), lambda j: (0, j)),
            pl.BlockSpec((1, tn), lambda j: (0, j)),
        ],
        out_specs=pl.BlockSpec((8, tn), lambda j: (0, j)),
        out_shape=jax.ShapeDtypeStruct((8, n), F32),
        compiler_params=_params(("arbitrary",)),
        name="modulation",
    )(cond8, w_mod, b_mod.reshape(1, n))


def _prenorm_body(x_ref, g_ref, sh_ref, sc_ref, o_ref):
    x = x_ref[...]
    y = x * _rms_scale(x) * g_ref[...]
    o_ref[...] = (y * (1.0 + sc_ref[...]) + sh_ref[...]).astype(o_ref.dtype)


def prenorm_modulate(x, g, shift, scale, group_of_block, tm):
    m = x.shape[0]
    gmap = lambda i: (group_of_block(i), 0, 0)
    return pl.pallas_call(
        _prenorm_body,
        grid=(m // tm,),
        in_specs=[
            pl.BlockSpec((tm, D_MODEL), lambda i: (i, 0)),
            pl.BlockSpec((1, D_MODEL), lambda i: (0, 0)),
            pl.BlockSpec((None, 1, D_MODEL), gmap),
            pl.BlockSpec((None, 1, D_MODEL), gmap),
        ],
        out_specs=pl.BlockSpec((tm, D_MODEL), lambda i: (i, 0)),
        out_shape=jax.ShapeDtypeStruct((m, D_MODEL), BF16),
        compiler_params=_params(("arbitrary",)),
        name="prenorm_modulate",
    )(x, g.reshape(1, D_MODEL), shift, scale)


def _inproj_body(h_ref, w_ref, o_ref):
    o_ref[...] = jnp.dot(h_ref[...], w_ref[...].astype(BF16), preferred_element_type=F32)


def in_projection(h, w_in):
    m = h.shape[0]
    tm, tn = 2048, 1024
    return pl.pallas_call(
        _inproj_body,
        grid=(IN_COLS // tn, m // tm),
        in_specs=[
            pl.BlockSpec((tm, D_MODEL), lambda j, i: (i, 0)),
            pl.BlockSpec((D_MODEL, tn), lambda j, i: (0, j)),
        ],
        out_specs=pl.BlockSpec((tm, tn), lambda j, i: (i, j)),
        out_shape=jax.ShapeDtypeStruct((m, IN_COLS), F32),
        compiler_params=_params(("arbitrary", "arbitrary"), vmem=EXPERT_VMEM_LIMIT),
        name="in_projection",
    )(h, w_in)


def _rope(x, cos, sin_lo, sin_hi):
    return x * cos + pltpu.roll(x, 96, 1) * sin_lo + pltpu.roll(x, 32, 1) * sin_hi


def _head_norm(x, g):
    return x * _rms_scale(x) * g


def _softmax_pv(score_blocks, value_blocks):
    m = None
    for s in score_blocks:
        mi = jnp.max(s, axis=-1, keepdims=True)
        m = mi if m is None else jnp.maximum(m, mi)
    ps = [jnp.exp(s - m) for s in score_blocks]
    denom = None
    for p in ps:
        li = jnp.sum(p, axis=-1, keepdims=True)
        denom = li if denom is None else denom + li
    out = None
    for p, v in zip(ps, value_blocks):
        o = jnp.dot(p.astype(BF16), v, preferred_element_type=F32)
        out = o if out is None else out + o
    return out * (1.0 / denom)


def _attn_ctx_body(q_ref, kv_ref, gq_ref, gk_ref, o_ref, ko_ref, vo_ref):
    tq = q_ref.shape[0]
    scale = HEAD_DIM ** -0.5
    gq = gq_ref[...]
    gk = gk_ref[...]
    for g in range(N_KV_HEADS):
        kcols = slice(g * HEAD_DIM, (g + 1) * HEAD_DIM)
        kn = _head_norm(kv_ref[:, kcols], gk)
        v = kv_ref[:, KV_COLS + g * HEAD_DIM:KV_COLS + (g + 1) * HEAD_DIM]
        ko_ref[pl.ds(g, tq, stride=N_KV_HEADS), :] = kn
        vo_ref[pl.ds(g, tq, stride=N_KV_HEADS), :] = v
        qs = []
        for hh in range(KV_GROUP):
            h = g * KV_GROUP + hh
            qs.append((_head_norm(q_ref[:, h * HEAD_DIM:(h + 1) * HEAD_DIM], gq) * scale).astype(BF16))
        q4 = jnp.concatenate(qs, axis=0)
        s = lax.dot_general(q4, kn.astype(BF16), (((1,), (1,)), ((), ())),
                            preferred_element_type=F32)
        o = _softmax_pv([s], [v.astype(BF16)])
        for hh in range(KV_GROUP):
            h = g * KV_GROUP + hh
            o_ref[:, h * HEAD_DIM:(h + 1) * HEAD_DIM] = o[hh * tq:(hh + 1) * tq].astype(o_ref.dtype)


def attention_ctx(z, g_q, g_k):
    nb = N_CTX_SEQ
    t = CTX_LEN
    return pl.pallas_call(
        _attn_ctx_body,
        grid=(nb,),
        in_specs=[
            pl.BlockSpec((t, Q_COLS), lambda b: (b, 0)),
            pl.BlockSpec((t, 2 * KV_COLS), lambda b: (b, COL_K // (2 * KV_COLS))),
            pl.BlockSpec((1, HEAD_DIM), lambda b: (0, 0)),
            pl.BlockSpec((1, HEAD_DIM), lambda b: (0, 0)),
        ],
        out_specs=[
            pl.BlockSpec((t, Q_COLS), lambda b: (b, 0)),
            pl.BlockSpec((t * N_KV_HEADS, HEAD_DIM), lambda b: (b, 0)),
            pl.BlockSpec((t * N_KV_HEADS, HEAD_DIM), lambda b: (b, 0)),
        ],
        out_shape=[
            jax.ShapeDtypeStruct((N_CTX, Q_COLS), BF16),
            jax.ShapeDtypeStruct((N_CTX * N_KV_HEADS, HEAD_DIM), F32),
            jax.ShapeDtypeStruct((N_CTX * N_KV_HEADS, HEAD_DIM), F32),
        ],
        compiler_params=_params(("arbitrary",)),
        name="attention_ctx",
    )(z, z, g_q.reshape(1, HEAD_DIM), g_k.reshape(1, HEAD_DIM))


def _attn_lat_body(q_ref, kv_ref, ck_ref, cv_ref, cos_ref, slo_ref, shi_ref, gq_ref, gk_ref,
                   o_ref, kr_ref):
    tq = q_ref.shape[0]
    qb = pl.program_id(1)
    scale = HEAD_DIM ** -0.5
    gq = gq_ref[...]

    @pl.when(qb == 0)
    def _():
        gk = gk_ref[...]
        for g in range(N_KV_HEADS):
            kcols = slice(g * HEAD_DIM, (g + 1) * HEAD_DIM)
            kn = _head_norm(kv_ref[:, kcols], gk)
            kr_ref[:, kcols] = _rope(kn, cos_ref[...], slo_ref[...], shi_ref[...]).astype(BF16)

    row0 = pl.multiple_of(qb * tq, tq)
    cos = cos_ref[pl.ds(row0, tq), :]
    slo = slo_ref[pl.ds(row0, tq), :]
    shi = shi_ref[pl.ds(row0, tq), :]
    for g in range(N_KV_HEADS):
        kcols = slice(g * HEAD_DIM, (g + 1) * HEAD_DIM)
        qs = []
        for hh in range(KV_GROUP):
            h = g * KV_GROUP + hh
            qn = _head_norm(q_ref[:, h * HEAD_DIM:(h + 1) * HEAD_DIM], gq)
            qs.append((_rope(qn, cos, slo, shi) * scale).astype(BF16))
        q4 = jnp.concatenate(qs, axis=0)
        dn = (((1,), (1,)), ((), ()))
        s_past = lax.dot_general(q4, ck_ref[:, kcols].astype(BF16), dn, preferred_element_type=F32)
        s_new = lax.dot_general(q4, kr_ref[:, kcols], dn, preferred_element_type=F32)
        v_past = cv_ref[:, kcols].astype(BF16)
        v_new = kv_ref[:, KV_COLS + g * HEAD_DIM:KV_COLS + (g + 1) * HEAD_DIM].astype(BF16)
        o = _softmax_pv([s_past, s_new], [v_past, v_new])
        for hh in range(KV_GROUP):
            h = g * KV_GROUP + hh
            o_ref[:, h * HEAD_DIM:(h + 1) * HEAD_DIM] = o[hh * tq:(hh + 1) * tq].astype(o_ref.dtype)


def attention_lat(z, cache_k, cache_v, rope_tabs, g_q, g_k):
    tq = 256
    nq = LAT_LEN // tq
    cos, slo, shi = rope_tabs
    tab = pl.BlockSpec((LAT_LEN, HEAD_DIM), lambda b, q: (0, 0))
    return pl.pallas_call(
        _attn_lat_body,
        grid=(N_LAT_SEQ, nq),
        in_specs=[
            pl.BlockSpec((tq, Q_COLS), lambda b, q: (b * nq + q, 0)),
            pl.BlockSpec((LAT_LEN, 2 * KV_COLS), lambda b, q: (b, COL_K // (2 * KV_COLS))),
            pl.BlockSpec((None, PAST_LEN, KV_COLS), lambda b, q: (b, 0, 0)),
            pl.BlockSpec((None, PAST_LEN, KV_COLS), lambda b, q: (b, 0, 0)),
            tab, tab, tab,
            pl.BlockSpec((1, HEAD_DIM), lambda b, q: (0, 0)),
            pl.BlockSpec((1, HEAD_DIM), lambda b, q: (0, 0)),
        ],
        out_specs=pl.BlockSpec((tq, Q_COLS), lambda b, q: (b * nq + q, 0)),
        out_shape=jax.ShapeDtypeStruct((N_LAT, Q_COLS), BF16),
        scratch_shapes=[pltpu.VMEM((LAT_LEN, KV_COLS), BF16)],
        compiler_params=_params(("arbitrary", "arbitrary")),
        name="attention_lat",
    )(z, z, cache_k, cache_v, cos, slo, shi, g_q.reshape(1, HEAD_DIM), g_k.reshape(1, HEAD_DIM))


def _rope_tables():
    t = np.arange(LAT_LEN)
    row = jnp.asarray(t // GRID_W, F32)
    col = jnp.asarray(t % GRID_W, F32)
    nf = HEAD_DIM // 4
    inv_freq = ROPE_THETA ** (-jnp.arange(nf, dtype=F32) / nf)
    ang_row = row[:, None] * inv_freq[None, :]
    ang_col = col[:, None] * inv_freq[None, :]
    ang = jnp.concatenate([ang_row, ang_row, ang_col, ang_col], axis=1)
    cos = jnp.cos(ang)
    sin = jnp.sin(ang)
    first = jnp.asarray((np.arange(HEAD_DIM) % (2 * nf)) < nf)[None, :]
    return cos, jnp.where(first, -sin, 0.0), jnp.where(first, 0.0, sin)


RNN_ROWS = 2048
RNN_COLS = 512
RNN_SUB = RNN_COLS // RNN_BLOCK_DIM


def _gelu_tanh(y):
    return 0.5 * y * (1.0 + jnp.tanh(0.7978845608028654 * (y + 0.044715 * (y * y * y))))


def _rglru_body(seq_len, xr_ref, yr_ref, cw_ref, cb_ref, wg_ref, bg_ref, lam_ref, h0f_ref, h0b_ref,
                o_ref, hf_ref, hb_ref, xs_ref, af_ref, bf_ref, ab_ref, bb_ref):
    n_seq = RNN_ROWS // seq_len
    for n in range(RNN_SUB):
        cols = slice(n * RNN_BLOCK_DIM, (n + 1) * RNN_BLOCK_DIM)
        for s in range(n_seq):
            xs_ref[n, pl.ds(s, seq_len, stride=n_seq), :] = xr_ref[s * seq_len:(s + 1) * seq_len, cols]

    row = lax.broadcasted_iota(jnp.int32, (RNN_ROWS, 1), 0)
    lam = lam_ref[...]
    softplus_neg = jnp.maximum(-lam, 0.0) + jnp.log(1.0 + jnp.exp(-jnp.abs(lam)))
    rate = softplus_neg * (-RG_C * 1.4426950408889634)
    for n in range(RNN_SUB):
        cols = slice(n * RNN_BLOCK_DIM, (n + 1) * RNN_BLOCK_DIM)
        x = xs_ref[n]
        x_m1 = jnp.where(row >= n_seq, pltpu.roll(x, n_seq, 0), 0.0)
        x_p1 = jnp.where(row < RNN_ROWS - n_seq, pltpu.roll(x, RNN_ROWS - n_seq, 0), 0.0)
        x_p2 = jnp.where(row < RNN_ROWS - 2 * n_seq, pltpu.roll(x, RNN_ROWS - 2 * n_seq, 0), 0.0)
        xn = (cb_ref[:, cols] + x_m1 * cw_ref[0:1, cols] + x * cw_ref[1:2, cols]
              + x_p1 * cw_ref[2:3, cols] + x_p2 * cw_ref[3:4, cols])
        pre = jnp.dot(xn.astype(BF16), wg_ref[n], preferred_element_type=F32) + bg_ref[n]
        for d, (a_ref, b_ref) in enumerate(((af_ref, bf_ref), (ab_ref, bb_ref))):
            r = 0.5 * jnp.tanh(0.5 * pre[:, (2 * d) * RNN_BLOCK_DIM:(2 * d + 1) * RNN_BLOCK_DIM]) + 0.5
            gate_in = 0.5 * jnp.tanh(
                0.5 * pre[:, (2 * d + 1) * RNN_BLOCK_DIM:(2 * d + 2) * RNN_BLOCK_DIM]) + 0.5
            a = jnp.exp2(r * rate[d:d + 1, cols])
            v = 1.0 - a * a
            a_ref[n] = a
            b_ref[n] = (v * lax.rsqrt(jnp.maximum(v, 1e-30))) * (gate_in * xn)

    def step(t, carry):
        rows_f = pl.ds(pl.multiple_of(t * n_seq, n_seq), n_seq)
        rows_b = pl.ds(pl.multiple_of((seq_len - 1 - t) * n_seq, n_seq), n_seq)
        out = []
        for n in range(RNN_SUB):
            hf = af_ref[n, rows_f, :] * carry[2 * n] + bf_ref[n, rows_f, :]
            hb = ab_ref[n, rows_b, :] * carry[2 * n + 1] + bb_ref[n, rows_b, :]
            bf_ref[n, rows_f, :] = hf
            bb_ref[n, rows_b, :] = hb
            out += [hf, hb]
        return tuple(out)

    init = []
    for n in range(RNN_SUB):
        cols = slice(n * RNN_BLOCK_DIM, (n + 1) * RNN_BLOCK_DIM)
        init += [h0f_ref[:, 0, cols], h0b_ref[:, 0, cols]]
    last = lax.fori_loop(0, seq_len, step, tuple(init), unroll=8)
    for n in range(RNN_SUB):
        cols = slice(n * RNN_BLOCK_DIM, (n + 1) * RNN_BLOCK_DIM)
        hf_ref[:, 0, cols] = last[2 * n]
        hb_ref[:, 0, cols] = last[2 * n + 1]
        bf_ref[n] = bf_ref[n] + bb_ref[n]
        for s in range(n_seq):
            rows = slice(s * seq_len, (s + 1) * seq_len)
            h_sum = bf_ref[n, pl.ds(s, seq_len, stride=n_seq), :]
            o_ref[rows, cols] = (h_sum * _gelu_tanh(yr_ref[rows, cols])).astype(o_ref.dtype)


def rglru_mixer(z, seq_len, conv_w, conv_b, w_gates, b_gates, lam, h0_f, h0_b):
    m = z.shape[0]
    n_seq_total = m // seq_len
    n_seq = RNN_ROWS // seq_len
    cblk = lambda base: (lambda r, c: (r, base // RNN_COLS + c))
    state_spec = pl.BlockSpec((n_seq, 1, RNN_COLS), lambda r, c: (r, 0, c))
    return pl.pallas_call(
        functools.partial(_rglru_body, seq_len),
        grid=(m // RNN_ROWS, D_MODEL // RNN_COLS),
        in_specs=[
            pl.BlockSpec((RNN_ROWS, RNN_COLS), cblk(COL_XR)),
            pl.BlockSpec((RNN_ROWS, RNN_COLS), cblk(COL_YR)),
            pl.BlockSpec((4, RNN_COLS), lambda r, c: (0, c)),
            pl.BlockSpec((1, RNN_COLS), lambda r, c: (0, c)),
            pl.BlockSpec((RNN_SUB, RNN_BLOCK_DIM, 4 * RNN_BLOCK_DIM), lambda r, c: (c, 0, 0)),
            pl.BlockSpec((RNN_SUB, 1, 4 * RNN_BLOCK_DIM), lambda r, c: (c, 0, 0)),
            pl.BlockSpec((2, RNN_COLS), lambda r, c: (0, c)),
            state_spec, state_spec,
        ],
        out_specs=[
            pl.BlockSpec((RNN_ROWS, RNN_COLS), lambda r, c: (r, c)),
            state_spec, state_spec,
        ],
        out_shape=[
            jax.ShapeDtypeStruct((m, D_MODEL), BF16),
            jax.ShapeDtypeStruct((n_seq_total, 1, D_MODEL), F32),
            jax.ShapeDtypeStruct((n_seq_total, 1, D_MODEL), F32),
        ],
        scratch_shapes=[pltpu.VMEM((RNN_SUB, RNN_ROWS, RNN_BLOCK_DIM), F32) for _ in range(5)],
        compiler_params=_params(("arbitrary", "arbitrary")),
        name="rglru_mixer_t%d" % seq_len,
    )(z, z, conv_w, conv_b.reshape(1, D_MODEL), w_gates, b_gates, lam, h0_f, h0_b)


def _merge_body(a_ref, r_ref, wa_ref, wr_ref, ga_ref, gr_ref, o_ref, wa_bf, wr_bf):
    @pl.when(pl.program_id(1) == 0)
    def _():
        wa_bf[...] = wa_ref[...].astype(BF16)
        wr_bf[...] = wr_ref[...].astype(BF16)

    pa = jnp.dot(a_ref[...], wa_bf[...], preferred_element_type=F32)
    pr = jnp.dot(r_ref[...], wr_bf[...], preferred_element_type=F32)
    o_ref[...] = (_sigmoid(ga_ref[...]) * pa + _sigmoid(gr_ref[...]) * pr).astype(o_ref.dtype)


def gated_merge(attn, rnn, z, w_o_attn, w_o_rnn):
    m = attn.shape[0]
    tm, tn = 1024, 512
    return pl.pallas_call(
        _merge_body,
        grid=(D_MODEL // tn, m // tm),
        in_specs=[
            pl.BlockSpec((tm, Q_COLS), lambda j, i: (i, 0)),
            pl.BlockSpec((tm, D_MODEL), lambda j, i: (i, 0)),
            pl.BlockSpec((Q_COLS, tn), lambda j, i: (0, j)),
            pl.BlockSpec((D_MODEL, tn), lambda j, i: (0, j)),
            pl.BlockSpec((tm, tn), lambda j, i: (i, COL_GA // tn + j)),
            pl.BlockSpec((tm, tn), lambda j, i: (i, COL_GR // tn + j)),
        ],
        out_specs=pl.BlockSpec((tm, tn), lambda j, i: (i, j)),
        out_shape=jax.ShapeDtypeStruct((m, D_MODEL), BF16),
        scratch_shapes=[pltpu.VMEM((Q_COLS, tn), BF16), pltpu.VMEM((D_MODEL, tn), BF16)],
        compiler_params=_params(("arbitrary", "arbitrary")),
        name="gated_merge",
    )(attn, rnn, w_o_attn, w_o_rnn, z, z)


POST_TM = 512
POST_SPLIT = 1
HALF_D = D_MODEL // 2
WORD_ROWS = HALF_D // 128
SUBLANES = 8
POST_CTX_BLOCKS = N_CTX // POST_TM
LAT_BLOCKS_PER_SEQ = LAT_LEN // POST_TM


def _post_group(i):
    return jnp.where(i < POST_CTX_BLOCKS, 0, 1 + (i - POST_CTX_BLOCKS) // LAT_BLOCKS_PER_SEQ)


def _postmix_body(mc_ref, ml_ref, xc_ref, xl_ref, wo_ref, gpm_ref, gt1_ref, gpf_ref, sh2_ref, sc2_ref,
                  wr_ref, br_ref, x1_ref, h2_ref, e_ref, gate_ref, rank_ref, cnt_ref, carry_ref):
    i = pl.program_id(0)
    tm = POST_TM

    @pl.when(i == 0)
    def _():
        carry_ref[...] = jnp.zeros_like(carry_ref)

    is_ctx = i < POST_CTX_BLOCKS
    th = tm // POST_SPLIT
    r_io = lax.broadcasted_iota(jnp.int32, (th, th), 0)
    c_io = lax.broadcasted_iota(jnp.int32, (th, th), 1)
    lower = jnp.where(c_io < r_io, 1.0, 0.0).astype(BF16)
    lane = lax.broadcasted_iota(jnp.int32, (th, N_EXPERTS), 1)
    lane_k = lax.broadcasted_iota(jnp.int32, (th, TOP_K), 1)
    wr = wr_ref[...].astype(BF16)
    carry = carry_ref[...]
    for part in range(POST_SPLIT):
        rows = slice(part * th, (part + 1) * th)
        merged = jnp.where(is_ctx, mc_ref[rows, :], ml_ref[rows, :])
        x = jnp.where(is_ctx, xc_ref[rows, :], xl_ref[rows, :])
        o = jnp.dot(merged, wo_ref[...], preferred_element_type=F32)
        x1 = x + gt1_ref[...] * (o * _rms_scale(o) * gpm_ref[...])
        x1_ref[rows, :] = x1
        h2 = (x1 * _rms_scale(x1) * gpf_ref[...]) * (1.0 + sc2_ref[...]) + sh2_ref[...]
        h2_bf = h2.astype(BF16)
        bits = lax.bitcast_convert_type(h2_bf.astype(F32), jnp.uint32)
        words = (lax.shift_right_logical(bits[:, :HALF_D], jnp.uint32(16))
                 | (bits[:, HALF_D:] & jnp.uint32(0xFFFF0000)))
        for c in range(WORD_ROWS):
            h2_ref[pl.ds(part * th * WORD_ROWS + c, th, stride=WORD_ROWS), :] = words[:, c * 128:(c + 1) * 128]

        logits = jnp.dot(h2_bf, wr, preferred_element_type=F32) + br_ref[...]
        work = logits
        chosen = jnp.zeros((th, N_EXPERTS), F32)
        sels, vals, idxs = [], [], []
        for _ in range(TOP_K):
            mx = jnp.max(work, axis=-1, keepdims=True)
            idx = jnp.min(jnp.where(work == mx, lane, N_EXPERTS), axis=-1, keepdims=True)
            sel = lane == idx
            work = jnp.where(sel, -jnp.inf, work)
            chosen = jnp.where(sel, 1.0, chosen)
            sels.append(sel)
            vals.append(mx)
            idxs.append(idx)
        exps = [jnp.exp(v - vals[0]) for v in vals]
        inv = 1.0 / (exps[0] + exps[1] + exps[2] + exps[3])

        before = jnp.dot(lower, chosen.astype(BF16), preferred_element_type=F32) + carry
        carry = carry + jnp.sum(chosen, axis=0, keepdims=True)

        e_out = jnp.zeros((th, TOP_K), jnp.int32)
        g_out = jnp.zeros((th, TOP_K), F32)
        r_out = jnp.zeros((th, TOP_K), jnp.int32)
        for k in range(TOP_K):
            rk = jnp.sum(jnp.where(sels[k], before, 0.0), axis=-1, keepdims=True).astype(jnp.int32)
            e_out = jnp.where(lane_k == k, idxs[k], e_out)
            g_out = jnp.where(lane_k == k, exps[k] * inv, g_out)
            r_out = jnp.where(lane_k == k, rk, r_out)
        e_ref[rows, :] = e_out
        gate_ref[rows, :] = g_out
        rank_ref[rows, :] = r_out
    carry_ref[...] = carry
    cnt_ref[...] = carry


def post_mix_router(merged_ctx, merged_lat, x_ctx, x_lat, w_out_bf, g_post_mix, gt1, g_pre_ffn, sh2, sc2,
                    w_router, b_router):
    tm = POST_TM
    ctx_map = lambda i: (jnp.minimum(i, POST_CTX_BLOCKS - 1), 0)
    lat_map = lambda i: (jnp.maximum(i - POST_CTX_BLOCKS, 0), 0)
    gmap = lambda i: (_post_group(i), 0, 0)
    row = lambda i: (i, 0)
    const = lambda i: (0, 0)
    vec = pl.BlockSpec((1, D_MODEL), const)
    gvec = pl.BlockSpec((None, 1, D_MODEL), gmap)
    return pl.pallas_call(
        _postmix_body,
        grid=(N_TOK // tm,),
        in_specs=[
            pl.BlockSpec((tm, D_MODEL), ctx_map),
            pl.BlockSpec((tm, D_MODEL), lat_map),
            pl.BlockSpec((tm, D_MODEL), ctx_map),
            pl.BlockSpec((tm, D_MODEL), lat_map),
            pl.BlockSpec((D_MODEL, D_MODEL), const),
            vec, gvec, vec, gvec, gvec,
            pl.BlockSpec((D_MODEL, N_EXPERTS), const),
            pl.BlockSpec((1, N_EXPERTS), const),
        ],
        out_specs=[
            pl.BlockSpec((tm, D_MODEL), row),
            pl.BlockSpec((tm * WORD_ROWS, 128), row),
            pl.BlockSpec((tm, TOP_K), row),
            pl.BlockSpec((tm, TOP_K), row),
            pl.BlockSpec((tm, TOP_K), row),
            pl.BlockSpec((1, N_EXPERTS), const),
        ],
        out_shape=[
            jax.ShapeDtypeStruct((N_TOK, D_MODEL), F32),
            jax.ShapeDtypeStruct((N_TOK * WORD_ROWS, 128), jnp.uint32),
            jax.ShapeDtypeStruct((N_TOK, TOP_K), jnp.int32),
            jax.ShapeDtypeStruct((N_TOK, TOP_K), F32),
            jax.ShapeDtypeStruct((N_TOK, TOP_K), jnp.int32),
            jax.ShapeDtypeStruct((1, N_EXPERTS), F32),
        ],
        scratch_shapes=[pltpu.VMEM((1, N_EXPERTS), F32)],
        compiler_params=_params(("arbitrary",)),
        name="post_mix_router",
    )(merged_ctx, merged_lat, x_ctx, x_lat, w_out_bf, g_post_mix.reshape(1, D_MODEL), gt1,
      g_pre_ffn.reshape(1, D_MODEL), sh2, sc2, w_router, b_router.reshape(1, N_EXPERTS))


GATHER_SHIFT = 4
GATHER_UNROLL = 1 << GATHER_SHIFT
GATHER_PRIORITY = 1


def _unpack_tile(xbuf_ref, slot, i):
    base = pl.multiple_of(i * (ROW_TILE * WORD_ROWS), ROW_TILE * WORD_ROWS)
    lo, hi = [], []
    for c in range(WORD_ROWS):
        words = xbuf_ref[slot, pl.ds(base + c, ROW_TILE, stride=WORD_ROWS), :]
        lo.append(lax.bitcast_convert_type(lax.shift_left(words, jnp.uint32(16)), F32).astype(BF16))
        hi.append(lax.bitcast_convert_type(words & jnp.uint32(0xFFFF0000), F32).astype(BF16))
    return jnp.concatenate(lo + hi, axis=1)


def _for_tiles(n_tiles, body):
    def one(i, _):
        body(i)
        return 0

    lax.fori_loop(0, n_tiles, one, 0)


def _moe_body(exp_ref, row_ref, nsub_ref, nzero_ref, npass_ref, rows_ref, tok_ref, h_ref, wg_ref, wl_ref, wd_ref, bg_ref,
              bl_ref, bd_ref, y_ref, xbuf_ref, act_ref, stage_ref, idx_ref, pend_ref,
              xsem, isem, ysem):
    s = pl.program_id(0)
    j = pl.program_id(1)
    n_pass = npass_ref[0]
    n_sub = nsub_ref[s]
    row_start = row_ref[s]

    def idx_copy(p):
        tile0 = pl.multiple_of(row_ref[p], ROW_TILE) // ROW_TILE
        return pltpu.make_async_copy(tok_ref.at[pl.ds(tile0, SUPER_TILES)], idx_ref.at[p % 2],
                                     isem.at[p % 2])

    def row_groups(p):
        return lax.shift_right_logical(rows_ref[p] + (GATHER_UNROLL - 1), GATHER_SHIFT)

    def gather_rows(p):
        slot = p % 2

        def issue(grp, _):
            first = grp * GATHER_UNROLL
            tile = lax.shift_right_logical(first, 8)
            col = jnp.bitwise_and(first, ROW_TILE - 1)
            for g in range(GATHER_UNROLL):
                t = idx_ref[slot, tile, 0, col + g]
                src = h_ref.at[t]
                dst = xbuf_ref.at[slot, pl.ds(pl.multiple_of((first + g) * WORD_ROWS, WORD_ROWS), WORD_ROWS), :]
                pltpu.make_async_copy(src, dst, xsem.at[slot]).start(priority=GATHER_PRIORITY)
            return 0

        lax.fori_loop(0, row_groups(p), issue, 0)

    def wait_rows(p):
        n_grp = row_groups(p)

        @pl.when(n_grp > 0)
        def _():
            n = pl.multiple_of(n_grp * (GATHER_UNROLL * WORD_ROWS), GATHER_UNROLL * WORD_ROWS)
            window = xbuf_ref.at[p % 2, pl.ds(0, n), :]
            pltpu.make_async_copy(window, window, xsem.at[p % 2]).wait()

    @pl.when(jnp.logical_and(s == 0, j == 0))
    def _():
        pend_ref[0] = 0
        pend_ref[1] = 0

    def drain_stage(slot):
        @pl.when(pend_ref[slot] == 1)
        def _():
            pltpu.make_async_copy(stage_ref.at[slot], stage_ref.at[slot], ysem.at[slot]).wait()
            pend_ref[slot] = 0

    @pl.when(jnp.logical_and(s == 0, j == 0))
    def _():
        xbuf_ref[...] = jnp.zeros(xbuf_ref.shape, xbuf_ref.dtype)
        idx_copy(0).start()
        idx_copy(0).wait()
        gather_rows(0)
        idx_copy(1).start()

    @pl.when(j == 0)
    def _():
        wait_rows(s)

    @pl.when(jnp.logical_and(j == 0, s + 1 < n_pass))
    def _():
        idx_copy(s + 1).wait()
        gather_rows(s + 1)

    @pl.when(jnp.logical_and(j == 0, s + 2 < n_pass))
    def _():
        idx_copy(s + 2).start()

    @pl.when(jnp.logical_and(j < N_FF_CHUNKS, n_sub > 0))
    def _():
        bg = bg_ref[...]
        bl = bl_ref[...]

        def up_tile(i):
            rows = pl.ds(pl.multiple_of(i * ROW_TILE, ROW_TILE), ROW_TILE)
            xt = _unpack_tile(xbuf_ref, s % 2, i)
            glu = jnp.minimum(jnp.dot(xt, wg_ref[...].astype(BF16), preferred_element_type=F32) + bg,
                              SWIGLU_LIMIT)
            lin = jnp.clip(jnp.dot(xt, wl_ref[...].astype(BF16), preferred_element_type=F32) + bl,
                           -SWIGLU_LIMIT, SWIGLU_LIMIT)
            act = glu * _sigmoid(SWIGLU_ALPHA * glu) * (lin + 1.0)
            act_ref[j, rows, :] = act.astype(BF16)

        _for_tiles(n_sub, up_tile)

    for cc in range(N_FF_CHUNKS):
        @pl.when(jnp.logical_and(j == N_FF_CHUNKS + cc, n_sub > 0))
        def _(cc=cc):
            bd = bd_ref[...]

            def out_copy(i, slot):
                dst = y_ref.at[pl.ds(pl.multiple_of(row_start + i * ROW_TILE, ROW_TILE), ROW_TILE),
                               cc * FF_CHUNK:(cc + 1) * FF_CHUNK]
                return pltpu.make_async_copy(stage_ref.at[slot], dst, ysem.at[slot])

            def down_tile(i):
                rows = pl.ds(pl.multiple_of(i * ROW_TILE, ROW_TILE), ROW_TILE)
                slot = i % 2
                drain_stage(slot)
                acc = bd
                for c in range(N_FF_CHUNKS):
                    acc = acc + jnp.dot(act_ref[c, rows, :],
                                        wd_ref[c * FF_CHUNK:(c + 1) * FF_CHUNK, :].astype(BF16),
                                        preferred_element_type=F32)
                stage_ref[slot] = acc
                out_copy(i, slot).start()
                pend_ref[slot] = 1

            _for_tiles(n_sub, down_tile)

    n_zero = nzero_ref[s]

    @pl.when(jnp.logical_and(j == 0, n_zero > 0))
    def _():
        drain_stage(0)
        stage_ref[0] = jnp.zeros((ROW_TILE, FF_CHUNK), F32)

        def zero_copy(i, cc):
            dst = y_ref.at[pl.ds(pl.multiple_of(row_start + i * ROW_TILE, ROW_TILE), ROW_TILE),
                           cc * FF_CHUNK:(cc + 1) * FF_CHUNK]
            return pltpu.make_async_copy(stage_ref.at[0], dst, ysem.at[0])

        def issue(i, _):
            for cc in range(N_FF_CHUNKS):
                zero_copy(i, cc).start()
            return 0

        def drain(i, _):
            for cc in range(N_FF_CHUNKS):
                zero_copy(i, cc).wait()
            return 0

        lax.fori_loop(0, n_zero, issue, 0)
        lax.fori_loop(0, n_zero, drain, 0)

    @pl.when(jnp.logical_and(s == n_pass - 1, j == 2 * N_FF_CHUNKS - 1))
    def _():
        drain_stage(0)
        drain_stage(1)


def expert_mlp(h_packed, tok_sorted, sched, w_gate_up, b_gate_up, w_down, b_down):
    exp_of, row_of, nsub_of, nzero_of, n_pass, rows_of = sched
    last = N_FF_CHUNKS - 1
    up_of = lambda s, j, n: jnp.where(n[s] > 0, jnp.minimum(j, last), last)
    up_chunk = lambda s, j, e, r, n, z, p, c: (e[s], 0, up_of(s, j, n))
    lin_chunk = lambda s, j, e, r, n, z, p, c: (e[s], 0, N_FF_CHUNKS + up_of(s, j, n))

    def down_chunk(s, j, e, r, n, z, p, c):
        in_down = jnp.logical_and(n[s] > 0, j >= N_FF_CHUNKS)
        expert = jnp.where(in_down, e[s], e[jnp.maximum(s - 1, 0)])
        return expert, 0, jnp.where(in_down, j - N_FF_CHUNKS, last)
    grid_spec = pltpu.PrefetchScalarGridSpec(
        num_scalar_prefetch=6,
        grid=(n_pass[0], 2 * N_FF_CHUNKS),
        in_specs=[
            pl.BlockSpec(memory_space=pl.ANY),
            pl.BlockSpec(memory_space=pl.ANY),
            pl.BlockSpec((None, D_MODEL, FF_CHUNK), up_chunk),
            pl.BlockSpec((None, D_MODEL, FF_CHUNK), lin_chunk),
            pl.BlockSpec((None, D_FF, FF_CHUNK), down_chunk),
            pl.BlockSpec((None, 1, FF_CHUNK), up_chunk),
            pl.BlockSpec((None, 1, FF_CHUNK), lin_chunk),
            pl.BlockSpec((None, 1, FF_CHUNK), down_chunk),
        ],
        out_specs=pl.BlockSpec(memory_space=pl.ANY),
        scratch_shapes=[
            pltpu.VMEM((2, SUPER_ROWS * WORD_ROWS, 128), jnp.uint32),
            pltpu.VMEM((N_FF_CHUNKS, SUPER_ROWS, FF_CHUNK), BF16),
            pltpu.VMEM((2, ROW_TILE, FF_CHUNK), F32),
            pltpu.SMEM((2, SUPER_TILES, 1, ROW_TILE), jnp.int32),
            pltpu.SMEM((2,), jnp.int32),
            pltpu.SemaphoreType.DMA((2,)),
            pltpu.SemaphoreType.DMA((2,)),
            pltpu.SemaphoreType.DMA((2,)),
        ],
    )
    tok_tiles = jnp.concatenate([tok_sorted.reshape(N_ROW_TILES, 1, ROW_TILE),
                                 jnp.zeros((SUPER_TILES, 1, ROW_TILE), jnp.int32)], axis=0)
    h_packed = h_packed.reshape(N_TOK, WORD_ROWS, 128)
    return pl.pallas_call(
        _moe_body,
        grid_spec=grid_spec,
        out_shape=jax.ShapeDtypeStruct((N_ROWS, D_MODEL), F32),
        compiler_params=_params(("arbitrary", "arbitrary"), vmem=EXPERT_VMEM_LIMIT),
        name="expert_mlp",
    )(exp_of, row_of, nsub_of, nzero_of, n_pass, rows_of, tok_tiles, h_packed, w_gate_up, w_gate_up, w_down,
      b_gate_up.reshape(N_EXPERTS, 1, 2 * D_FF), b_gate_up.reshape(N_EXPERTS, 1, 2 * D_FF),
      b_down.reshape(N_EXPERTS, 1, D_MODEL))


COMB_TB = 256


def _combine_start(y_ref, ybuf_ref, pos_ref, sem):
    def issue(t, _):
        for k in range(TOP_K):
            p = pos_ref[0, 0, t * TOP_K + k]
            pltpu.make_async_copy(y_ref.at[pl.ds(p, 1), :], ybuf_ref.at[k, pl.ds(t, 1), :], sem).start()
        return 0

    lax.fori_loop(0, COMB_TB, issue, 0, unroll=4)


def _combine_body(n, pos_ref, pos_next_ref, y_ref, gate_ref, x1_ref, gt2_ref, g_ref, o_ref, ybuf_ref, sem_ref):
    i = pl.program_id(0)
    slot = i % 2

    @pl.when(i == 0)
    def _():
        _combine_start(y_ref, ybuf_ref.at[0], pos_ref, sem_ref.at[0])

    @pl.when(i + 1 < n)
    def _():
        _combine_start(y_ref, ybuf_ref.at[1 - slot], pos_next_ref, sem_ref.at[1 - slot])

    for k in range(TOP_K):
        pltpu.make_async_copy(y_ref.at[pl.ds(0, COMB_TB), :], ybuf_ref.at[slot, k], sem_ref.at[slot]).wait()
    gates = gate_ref[...]
    ffn = gates[:, 0:1] * ybuf_ref[slot, 0]
    for k in range(1, TOP_K):
        ffn = ffn + gates[:, k:k + 1] * ybuf_ref[slot, k]
    o_ref[...] = x1_ref[...] + gt2_ref[...] * (ffn * _rms_scale(ffn) * g_ref[...])


def combine_residual(y_sorted, pos, gates, x1, gt2, g_post_ffn, row_offset, n_rows, group_of_block):
    tb = COMB_TB
    nblk = n_rows // tb
    off = row_offset // tb
    pos3 = pos.reshape(N_TOK // tb, 1, tb * TOP_K)
    smem_blk = lambda f: pl.BlockSpec((1, 1, tb * TOP_K), f, memory_space=pltpu.SMEM)
    return pl.pallas_call(
        functools.partial(_combine_body, nblk),
        grid=(nblk,),
        in_specs=[
            smem_blk(lambda i: (off + i, 0, 0)),
            smem_blk(lambda i: (off + jnp.minimum(i + 1, nblk - 1), 0, 0)),
            pl.BlockSpec(memory_space=pl.ANY),
            pl.BlockSpec((tb, TOP_K), lambda i: (off + i, 0)),
            pl.BlockSpec((tb, D_MODEL), lambda i: (off + i, 0)),
            pl.BlockSpec((None, 1, D_MODEL), lambda i: (group_of_block(i), 0, 0)),
            pl.BlockSpec((1, D_MODEL), lambda i: (0, 0)),
        ],
        out_specs=pl.BlockSpec((tb, D_MODEL), lambda i: (i, 0)),
        out_shape=jax.ShapeDtypeStruct((n_rows, D_MODEL), F32),
        scratch_shapes=[pltpu.VMEM((2, TOP_K, tb, D_MODEL), F32), pltpu.SemaphoreType.DMA((2,))],
        compiler_params=_params(("arbitrary",)),
        name="combine_residual",
    )(pos3, pos3, y_sorted, gates, x1, gt2, g_post_ffn.reshape(1, D_MODEL))


INV_CHUNK = 4096


def _row_tokens_body(pos_ref, zeros_ref, o_ref, sem):
    i = pl.program_id(0)

    @pl.when(i == 0)
    def _():
        cp = pltpu.make_async_copy(zeros_ref, o_ref, sem)
        cp.start()
        cp.wait()

    first_token = i * (INV_CHUNK // TOP_K)

    def put(t, _):
        for k in range(TOP_K):
            o_ref[pos_ref[t * TOP_K + k]] = first_token + t
        return 0

    lax.fori_loop(0, INV_CHUNK // TOP_K, put, 0, unroll=4)


def row_tokens(pos):
    return pl.pallas_call(
        _row_tokens_body,
        grid=(N_ASSIGN // INV_CHUNK,),
        in_specs=[
            pl.BlockSpec((INV_CHUNK,), lambda i: (i,), memory_space=pltpu.SMEM),
            pl.BlockSpec(memory_space=pl.ANY),
        ],
        out_specs=pl.BlockSpec(memory_space=pltpu.SMEM),
        out_shape=jax.ShapeDtypeStruct((N_ROWS,), jnp.int32),
        scratch_shapes=[pltpu.SemaphoreType.DMA(())],
        compiler_params=_params(("arbitrary",)),
        name="row_tokens",
    )(pos.reshape(N_ASSIGN), jnp.zeros((N_ROWS,), jnp.int32))


def _routing_tables(e_idx, rank, counts_f):
    counts = counts_f.reshape(N_EXPERTS).astype(jnp.int32)
    n_tiles = (counts + ROW_TILE - 1) // ROW_TILE
    padded = n_tiles * ROW_TILE
    pad_end = jnp.cumsum(padded)
    pad_start = pad_end - padded
    pos = (pad_start[e_idx] + rank).astype(jnp.int32)
    tok_sorted = row_tokens(pos)
    n_pass = (n_tiles + SUPER_TILES - 1) // SUPER_TILES
    pass_end = jnp.cumsum(n_pass)
    total = pass_end[-1]
    s = jnp.arange(N_SUPER, dtype=jnp.int32)
    s_eff = jnp.minimum(s, total - 1)
    e_of = jnp.minimum(jnp.searchsorted(pass_end, s_eff, side="right"), N_EXPERTS - 1).astype(jnp.int32)
    local = s_eff - (pass_end[e_of] - n_pass[e_of])
    row_of = pad_start[e_of] + local * SUPER_ROWS
    nsub = jnp.minimum(SUPER_TILES, n_tiles[e_of] - local * SUPER_TILES)
    nsub = jnp.where(s < total, nsub, 0).astype(jnp.int32)
    zero_row = pad_end[-1] + (s - total) * SUPER_ROWS
    nzero = jnp.clip((N_ROWS - zero_row) // ROW_TILE, 0, SUPER_TILES)
    nzero = jnp.where(s >= total, nzero, 0).astype(jnp.int32)
    row_of = jnp.where(s < total, row_of, jnp.minimum(zero_row, N_ROWS - ROW_TILE)).astype(jnp.int32)
    tail_tiles = (N_ROWS - pad_end[-1]) // ROW_TILE
    n_pass = jnp.minimum(total + jnp.maximum((tail_tiles + SUPER_TILES - 1) // SUPER_TILES, 1), N_SUPER)
    n_pass = n_pass.astype(jnp.int32).reshape(1)
    rows_of = jnp.clip(counts[e_of] - local * SUPER_ROWS, 0, SUPER_ROWS)
    rows_of = jnp.where(s < total, rows_of, 0).astype(jnp.int32)
    return pos.astype(jnp.int32), tok_sorted, (e_of, row_of, nsub, nzero, n_pass, rows_of)


def kernel(x_prompt, x_sample, cache_k, cache_v, state_rnn_fwd, state_rnn_bwd, c, c_ctx, w_mod, b_mod, g_pre_mix, w_in, g_q_norm, g_k_norm, conv_w, conv_b, rg_w_a, rg_b_a, rg_w_x, rg_b_x, rg_lambda, w_o_attn, w_o_rnn, w_out, g_post_mix, g_pre_ffn, w_router, b_router, w_gate_up, b_gate_up, w_down, b_down, g_post_ffn):
    l = 0
    x_ctx = x_prompt.reshape(N_CTX, D_MODEL)
    x_lat = x_sample.reshape(N_LAT, D_MODEL)

    cond8 = jnp.concatenate([c_ctx[None, :], c, jnp.zeros((8 - 1 - N_LAT_SEQ, D_MODEL), F32)], axis=0)
    mod = modulation(cond8, w_mod[l], b_mod[l])[:1 + N_LAT_SEQ].reshape(1 + N_LAT_SEQ, 6, 1, D_MODEL)
    sh1, sc1, gt1, sh2, sc2, gt2 = [mod[:, i] for i in range(6)]

    ctx_group = lambda i: 0
    lat_group_1024 = lambda i: 1 + i
    h_ctx = prenorm_modulate(x_ctx, g_pre_mix[l], sh1, sc1, ctx_group, 1024)
    h_lat = prenorm_modulate(x_lat, g_pre_mix[l], sh1, sc1, lat_group_1024, 1024)
    z_ctx = in_projection(h_ctx, w_in[l])
    z_lat = in_projection(h_lat, w_in[l])

    attn_ctx, k_new, v_new = attention_ctx(z_ctx, g_q_norm[l], g_k_norm[l])
    attn_lat = attention_lat(z_lat, cache_k[:, l].reshape(N_LAT_SEQ, PAST_LEN, KV_COLS),
                             cache_v[:, l].reshape(N_LAT_SEQ, PAST_LEN, KV_COLS),
                             _rope_tables(), g_q_norm[l], g_k_norm[l])

    def per_block(w):
        return w.reshape(2, RNN_BLOCKS, 1, RNN_BLOCK_DIM)

    w_gates = jnp.concatenate([rg_w_a[l, 0], rg_w_x[l, 0], rg_w_a[l, 1], rg_w_x[l, 1]], axis=-1).astype(BF16)
    ba, bx = per_block(rg_b_a[l]), per_block(rg_b_x[l])
    b_gates = jnp.concatenate([ba[0], bx[0], ba[1], bx[1]], axis=-1)
    zeros_state = jnp.zeros((N_CTX_SEQ, 1, D_MODEL), F32)
    rnn_ctx, hf_ctx, hb_ctx = rglru_mixer(z_ctx, CTX_LEN, conv_w[l], conv_b[l], w_gates, b_gates,
                                          rg_lambda[l], zeros_state, zeros_state)
    rnn_lat, _, _ = rglru_mixer(z_lat, LAT_LEN, conv_w[l], conv_b[l], w_gates, b_gates, rg_lambda[l],
                                state_rnn_fwd[:, l].reshape(N_LAT_SEQ, 1, D_MODEL),
                                state_rnn_bwd[:, l].reshape(N_LAT_SEQ, 1, D_MODEL))

    merged_ctx = gated_merge(attn_ctx, rnn_ctx, z_ctx, w_o_attn[l], w_o_rnn[l])
    merged_lat = gated_merge(attn_lat, rnn_lat, z_lat, w_o_attn[l], w_o_rnn[l])

    x1, h2, e_idx, gates, rank, counts = post_mix_router(
        merged_ctx, merged_lat, x_ctx, x_lat, w_out[l].astype(BF16), g_post_mix[l], gt1, g_pre_ffn[l],
        sh2, sc2, w_router[l], b_router[l])

    pos, tok_sorted, sched = _routing_tables(e_idx, rank, counts)
    y_sorted = expert_mlp(h2, tok_sorted, sched, w_gate_up[l], b_gate_up[l], w_down[l], b_down[l])

    y_ctx = combine_residual(y_sorted, pos, gates, x1, gt2, g_post_ffn[l], 0, N_CTX, ctx_group)
    y_lat = combine_residual(y_sorted, pos, gates, x1, gt2, g_post_ffn[l], N_CTX, N_LAT,
                             lambda i: 1 + i // (LAT_LEN // COMB_TB))

    return (y_ctx.reshape(N_CTX_SEQ, CTX_LEN, D_MODEL),
            y_lat.reshape(N_LAT_SEQ, LAT_LEN, D_MODEL),
            k_new.reshape(N_CTX_SEQ, 1, CTX_LEN, N_KV_HEADS, HEAD_DIM),
            v_new.reshape(N_CTX_SEQ, 1, CTX_LEN, N_KV_HEADS, HEAD_DIM),
            hf_ctx,
            hb_ctx)
```

```python
import functools

import jax
import jax.numpy as jnp
import numpy as np
from jax import lax
from jax.experimental import pallas as pl
from jax.experimental.pallas import tpu as pltpu

D_MODEL = 2048
N_CTX_SEQ = 32
CTX_LEN = 256
N_LAT_SEQ = 2
LAT_LEN = 1024
PAST_LEN = 512
N_CTX = N_CTX_SEQ * CTX_LEN
N_LAT = N_LAT_SEQ * LAT_LEN
N_TOK = N_CTX + N_LAT
GRID_W = 64
N_HEADS = 16
N_KV_HEADS = 4
HEAD_DIM = 128
KV_GROUP = N_HEADS // N_KV_HEADS
ROPE_THETA = 10000.0
RNN_BLOCKS = 16
RNN_BLOCK_DIM = 128
RG_C = 8.0
N_EXPERTS = 32
TOP_K = 4
D_FF = 2048
SWIGLU_LIMIT = 7.0
SWIGLU_ALPHA = 1.702
EPS = 1e-6
Q_COLS = N_HEADS * HEAD_DIM
KV_COLS = N_KV_HEADS * HEAD_DIM
IN_COLS = Q_COLS + 2 * KV_COLS + 4 * D_MODEL
COL_K = Q_COLS
COL_XR = Q_COLS + 2 * KV_COLS
COL_YR = COL_XR + D_MODEL
COL_GA = COL_YR + D_MODEL
COL_GR = COL_GA + D_MODEL

V7X_VMEM_BYTES = 64 * 1024 * 1024
VMEM_LIMIT = 56 * 1024 * 1024
EXPERT_VMEM_LIMIT = 60 * 1024 * 1024

ROW_TILE = 256
SUPER_TILES = 8
SUPER_ROWS = ROW_TILE * SUPER_TILES
N_ASSIGN = N_TOK * TOP_K
N_ROWS = N_ASSIGN + N_EXPERTS * ROW_TILE
N_ROW_TILES = N_ROWS // ROW_TILE
N_SUPER = N_ROW_TILES // SUPER_TILES + N_EXPERTS
FF_CHUNK = 512
N_FF_CHUNKS = D_FF // FF_CHUNK

BF16 = jnp.bfloat16
F32 = jnp.float32


def _params(semantics, vmem=VMEM_LIMIT):
    return pltpu.CompilerParams(dimension_semantics=semantics, vmem_limit_bytes=vmem)


def _rms_scale(x):
    return lax.rsqrt(jnp.mean(x * x, axis=-1, keepdims=True) + EPS)


def _sigmoid(x):
    return 1.0 / (1.0 + jnp.exp(-x))


def _mod_body(c_ref, w_ref, b_ref, o_ref):
    c = c_ref[...]
    a = (c * _sigmoid(c)).astype(BF16)
    o_ref[...] = jnp.dot(a, w_ref[...].astype(BF16), preferred_element_type=F32) + b_ref[...]


def modulation(cond8, w_mod, b_mod):
    tn = 1024
    n = w_mod.shape[1]
    return pl.pallas_call(
        _mod_body,
        grid=(n // tn,),
        in_specs=[
            pl.BlockSpec((8, D_MODEL), lambda j: (0, 0)),
            pl.BlockSpec((D_MODEL, tn), lambda j: (0, j)),
            pl.BlockSpec((1, tn), lambda j: (0, j)),
        ],
        out_specs=pl.BlockSpec((8, tn), lambda j: (0, j)),
        out_shape=jax.ShapeDtypeStruct((8, n), F32),
        compiler_params=_params(("arbitrary",)),
        name="modulation",
    )(cond8, w_mod, b_mod.reshape(1, n))


def _prenorm_body(x_ref, g_ref, sh_ref, sc_ref, o_ref):
    x = x_ref[...]
    y = x * _rms_scale(x) * g_ref[...]
    o_ref[...] = (y * (1.0 + sc_ref[...]) + sh_ref[...]).astype(o_ref.dtype)


def prenorm_modulate(x, g, shift, scale, group_of_block, tm):
    m = x.shape[0]
    gmap = lambda i: (group_of_block(i), 0, 0)
    return pl.pallas_call(
        _prenorm_body,
        grid=(m // tm,),
        in_specs=[
            pl.BlockSpec((tm, D_MODEL), lambda i: (i, 0)),
            pl.BlockSpec((1, D_MODEL), lambda i: (0, 0)),
            pl.BlockSpec((None, 1, D_MODEL), gmap),
            pl.BlockSpec((None, 1, D_MODEL), gmap),
        ],
        out_specs=pl.BlockSpec((tm, D_MODEL), lambda i: (i, 0)),
        out_shape=jax.ShapeDtypeStruct((m, D_MODEL), BF16),
        compiler_params=_params(("arbitrary",)),
        name="prenorm_modulate",
    )(x, g.reshape(1, D_MODEL), shift, scale)


def _inproj_body(h_ref, w_ref, o_ref):
    o_ref[...] = jnp.dot(h_ref[...], w_ref[...].astype(BF16), preferred_element_type=F32)


def in_projection(h, w_in):
    m = h.shape[0]
    tm, tn = 2048, 1024
    return pl.pallas_call(
        _inproj_body,
        grid=(IN_COLS // tn, m // tm),
        in_specs=[
            pl.BlockSpec((tm, D_MODEL), lambda j, i: (i, 0)),
            pl.BlockSpec((D_MODEL, tn), lambda j, i: (0, j)),
        ],
        out_specs=pl.BlockSpec((tm, tn), lambda j, i: (i, j)),
        out_shape=jax.ShapeDtypeStruct((m, IN_COLS), F32),
        compiler_params=_params(("arbitrary", "arbitrary"), vmem=EXPERT_VMEM_LIMIT),
        name="in_projection",
    )(h, w_in)


def _rope(x, cos, sin_lo, sin_hi):
    return x * cos + pltpu.roll(x, 96, 1) * sin_lo + pltpu.roll(x, 32, 1) * sin_hi


def _head_norm(x, g):
    return x * _rms_scale(x) * g


def _softmax_pv(score_blocks, value_blocks):
    m = None
    for s in score_blocks:
        mi = jnp.max(s, axis=-1, keepdims=True)
        m = mi if m is None else jnp.maximum(m, mi)
    ps = [jnp.exp(s - m) for s in score_blocks]
    denom = None
    for p in ps:
        li = jnp.sum(p, axis=-1, keepdims=True)
        denom = li if denom is None else denom + li
    out = None
    for p, v in zip(ps, value_blocks):
        o = jnp.dot(p.astype(BF16), v, preferred_element_type=F32)
        out = o if out is None else out + o
    return out * (1.0 / denom)


def _attn_ctx_body(q_ref, kv_ref, gq_ref, gk_ref, o_ref, ko_ref, vo_ref):
    tq = q_ref.shape[0]
    scale = HEAD_DIM ** -0.5
    gq = gq_ref[...]
    gk = gk_ref[...]
    for g in range(N_KV_HEADS):
        kcols = slice(g * HEAD_DIM, (g + 1) * HEAD_DIM)
        kn = _head_norm(kv_ref[:, kcols], gk)
        v = kv_ref[:, KV_COLS + g * HEAD_DIM:KV_COLS + (g + 1) * HEAD_DIM]
        ko_ref[pl.ds(g, tq, stride=N_KV_HEADS), :] = kn
        vo_ref[pl.ds(g, tq, stride=N_KV_HEADS), :] = v
        qs = []
        for hh in range(KV_GROUP):
            h = g * KV_GROUP + hh
            qs.append((_head_norm(q_ref[:, h * HEAD_DIM:(h + 1) * HEAD_DIM], gq) * scale).astype(BF16))
        q4 = jnp.concatenate(qs, axis=0)
        s = lax.dot_general(q4, kn.astype(BF16), (((1,), (1,)), ((), ())),
                            preferred_element_type=F32)
        o = _softmax_pv([s], [v.astype(BF16)])
        for hh in range(KV_GROUP):
            h = g * KV_GROUP + hh
            o_ref[:, h * HEAD_DIM:(h + 1) * HEAD_DIM] = o[hh * tq:(hh + 1) * tq].astype(o_ref.dtype)


def attention_ctx(z, g_q, g_k):
    nb = N_CTX_SEQ
    t = CTX_LEN
    return pl.pallas_call(
        _attn_ctx_body,
        grid=(nb,),
        in_specs=[
            pl.BlockSpec((t, Q_COLS), lambda b: (b, 0)),
            pl.BlockSpec((t, 2 * KV_COLS), lambda b: (b, COL_K // (2 * KV_COLS))),
            pl.BlockSpec((1, HEAD_DIM), lambda b: (0, 0)),
            pl.BlockSpec((1, HEAD_DIM), lambda b: (0, 0)),
        ],
        out_specs=[
            pl.BlockSpec((t, Q_COLS), lambda b: (b, 0)),
            pl.BlockSpec((t * N_KV_HEADS, HEAD_DIM), lambda b: (b, 0)),
            pl.BlockSpec((t * N_KV_HEADS, HEAD_DIM), lambda b: (b, 0)),
        ],
        out_shape=[
            jax.ShapeDtypeStruct((N_CTX, Q_COLS), BF16),
            jax.ShapeDtypeStruct((N_CTX * N_KV_HEADS, HEAD_DIM), F32),
            jax.ShapeDtypeStruct((N_CTX * N_KV_HEADS, HEAD_DIM), F32),
        ],
        compiler_params=_params(("arbitrary",)),
        name="attention_ctx",
    )(z, z, g_q.reshape(1, HEAD_DIM), g_k.reshape(1, HEAD_DIM))


def _attn_lat_body(q_ref, kv_ref, ck_ref, cv_ref, cos_ref, slo_ref, shi_ref, gq_ref, gk_ref,
                   o_ref, kr_ref):
    tq = q_ref.shape[0]
    qb = pl.program_id(1)
    scale = HEAD_DIM ** -0.5
    gq = gq_ref[...]

    @pl.when(qb == 0)
    def _():
        gk = gk_ref[...]
        for g in range(N_KV_HEADS):
            kcols = slice(g * HEAD_DIM, (g + 1) * HEAD_DIM)
            kn = _head_norm(kv_ref[:, kcols], gk)
            kr_ref[:, kcols] = _rope(kn, cos_ref[...], slo_ref[...], shi_ref[...]).astype(BF16)

    row0 = pl.multiple_of(qb * tq, tq)
    cos = cos_ref[pl.ds(row0, tq), :]
    slo = slo_ref[pl.ds(row0, tq), :]
    shi = shi_ref[pl.ds(row0, tq), :]
    for g in range(N_KV_HEADS):
        kcols = slice(g * HEAD_DIM, (g + 1) * HEAD_DIM)
        dn = (((1,), (1,)), ((), ()))
        k_past = ck_ref[:, kcols].astype(BF16)
        v_past = cv_ref[:, kcols].astype(BF16)
        v_new = kv_ref[:, KV_COLS + g * HEAD_DIM:KV_COLS + (g + 1) * HEAD_DIM].astype(BF16)
        for hh in range(KV_GROUP):
            h = g * KV_GROUP + hh
            qn = _head_norm(q_ref[:, h * HEAD_DIM:(h + 1) * HEAD_DIM], gq)
            qh = (_rope(qn, cos, slo, shi) * scale).astype(BF16)
            s_past = lax.dot_general(qh, k_past, dn, preferred_element_type=F32)
            s_new = lax.dot_general(qh, kr_ref[:, kcols], dn, preferred_element_type=F32)
            o = _softmax_pv([s_past, s_new], [v_past, v_new])
            o_ref[:, h * HEAD_DIM:(h + 1) * HEAD_DIM] = o.astype(o_ref.dtype)


def attention_lat(z, cache_k, cache_v, rope_tabs, g_q, g_k):
    tq = 256
    nq = LAT_LEN // tq
    cos, slo, shi = rope_tabs
    tab = pl.BlockSpec((LAT_LEN, HEAD_DIM), lambda b, q: (0, 0))
    return pl.pallas_call(
        _attn_lat_body,
        grid=(N_LAT_SEQ, nq),
        in_specs=[
            pl.BlockSpec((tq, Q_COLS), lambda b, q: (b * nq + q, 0)),
            pl.BlockSpec((LAT_LEN, 2 * KV_COLS), lambda b, q: (b, COL_K // (2 * KV_COLS))),
            pl.BlockSpec((None, PAST_LEN, KV_COLS), lambda b, q: (b, 0, 0)),
            pl.BlockSpec((None, PAST_LEN, KV_COLS), lambda b, q: (b, 0, 0)),
            tab, tab, tab,
            pl.BlockSpec((1, HEAD_DIM), lambda b, q: (0, 0)),
            pl.BlockSpec((1, HEAD_DIM), lambda b, q: (0, 0)),
        ],
        out_specs=pl.BlockSpec((tq, Q_COLS), lambda b, q: (b * nq + q, 0)),
        out_shape=jax.ShapeDtypeStruct((N_LAT, Q_COLS), BF16),
        scratch_shapes=[pltpu.VMEM((LAT_LEN, KV_COLS), BF16)],
        compiler_params=_params(("arbitrary", "arbitrary")),
        name="attention_lat",
    )(z, z, cache_k, cache_v, cos, slo, shi, g_q.reshape(1, HEAD_DIM), g_k.reshape(1, HEAD_DIM))


def _rope_tables():
    t = np.arange(LAT_LEN)
    row = jnp.asarray(t // GRID_W, F32)
    col = jnp.asarray(t % GRID_W, F32)
    nf = HEAD_DIM // 4
    inv_freq = ROPE_THETA ** (-jnp.arange(nf, dtype=F32) / nf)
    ang_row = row[:, None] * inv_freq[None, :]
    ang_col = col[:, None] * inv_freq[None, :]
    ang = jnp.concatenate([ang_row, ang_row, ang_col, ang_col], axis=1)
    cos = jnp.cos(ang)
    sin = jnp.sin(ang)
    first = jnp.asarray((np.arange(HEAD_DIM) % (2 * nf)) < nf)[None, :]
    return cos, jnp.where(first, -sin, 0.0), jnp.where(first, 0.0, sin)


RNN_ROWS = 2048
RNN_COLS = 512
RNN_SUB = RNN_COLS // RNN_BLOCK_DIM


def _gelu_tanh(y):
    return 0.5 * y * (1.0 + jnp.tanh(0.7978845608028654 * (y + 0.044715 * (y * y * y))))


def _rglru_body(seq_len, xr_ref, yr_ref, cw_ref, cb_ref, wg_ref, bg_ref, lam_ref, h0f_ref, h0b_ref,
                o_ref, hf_ref, hb_ref, xs_ref, af_ref, bf_ref, ab_ref, bb_ref):
    n_seq = RNN_ROWS // seq_len
    for n in range(RNN_SUB):
        cols = slice(n * RNN_BLOCK_DIM, (n + 1) * RNN_BLOCK_DIM)
        for s in range(n_seq):
            xs_ref[n, pl.ds(s, seq_len, stride=n_seq), :] = xr_ref[s * seq_len:(s + 1) * seq_len, cols]

    row = lax.broadcasted_iota(jnp.int32, (RNN_ROWS, 1), 0)
    lam = lam_ref[...]
    softplus_neg = jnp.maximum(-lam, 0.0) + jnp.log(1.0 + jnp.exp(-jnp.abs(lam)))
    rate = softplus_neg * (-RG_C * 1.4426950408889634)
    for n in range(RNN_SUB):
        cols = slice(n * RNN_BLOCK_DIM, (n + 1) * RNN_BLOCK_DIM)
        x = xs_ref[n]
        x_m1 = jnp.where(row >= n_seq, pltpu.roll(x, n_seq, 0), 0.0)
        x_p1 = jnp.where(row < RNN_ROWS - n_seq, pltpu.roll(x, RNN_ROWS - n_seq, 0), 0.0)
        x_p2 = jnp.where(row < RNN_ROWS - 2 * n_seq, pltpu.roll(x, RNN_ROWS - 2 * n_seq, 0), 0.0)
        xn = (cb_ref[:, cols] + x_m1 * cw_ref[0:1, cols] + x * cw_ref[1:2, cols]
              + x_p1 * cw_ref[2:3, cols] + x_p2 * cw_ref[3:4, cols])
        pre = jnp.dot(xn.astype(BF16), wg_ref[n], preferred_element_type=F32) + bg_ref[n]
        for d, (a_ref, b_ref) in enumerate(((af_ref, bf_ref), (ab_ref, bb_ref))):
            r = 0.5 * jnp.tanh(0.5 * pre[:, (2 * d) * RNN_BLOCK_DIM:(2 * d + 1) * RNN_BLOCK_DIM]) + 0.5
            gate_in = 0.5 * jnp.tanh(
                0.5 * pre[:, (2 * d + 1) * RNN_BLOCK_DIM:(2 * d + 2) * RNN_BLOCK_DIM]) + 0.5
            a = jnp.exp2(r * rate[d:d + 1, cols])
            v = 1.0 - a * a
            a_ref[n] = a
            b_ref[n] = (v * lax.rsqrt(jnp.maximum(v, 1e-30))) * (gate_in * xn)

    def step(t, carry):
        rows_f = pl.ds(pl.multiple_of(t * n_seq, n_seq), n_seq)
        rows_b = pl.ds(pl.multiple_of((seq_len - 1 - t) * n_seq, n_seq), n_seq)
        out = []
        for n in range(RNN_SUB):
            hf = af_ref[n, rows_f, :] * carry[2 * n] + bf_ref[n, rows_f, :]
            hb = ab_ref[n, rows_b, :] * carry[2 * n + 1] + bb_ref[n, rows_b, :]
            bf_ref[n, rows_f, :] = hf
            bb_ref[n, rows_b, :] = hb
            out += [hf, hb]
        return tuple(out)

    init = []
    for n in range(RNN_SUB):
        cols = slice(n * RNN_BLOCK_DIM, (n + 1) * RNN_BLOCK_DIM)
        init += [h0f_ref[:, 0, cols], h0b_ref[:, 0, cols]]
    last = lax.fori_loop(0, seq_len, step, tuple(init), unroll=8)
    for n in range(RNN_SUB):
        cols = slice(n * RNN_BLOCK_DIM, (n + 1) * RNN_BLOCK_DIM)
        hf_ref[:, 0, cols] = last[2 * n]
        hb_ref[:, 0, cols] = last[2 * n + 1]
        bf_ref[n] = bf_ref[n] + bb_ref[n]
        for s in range(n_seq):
            rows = slice(s * seq_len, (s + 1) * seq_len)
            h_sum = bf_ref[n, pl.ds(s, seq_len, stride=n_seq), :]
            o_ref[rows, cols] = (h_sum * _gelu_tanh(yr_ref[rows, cols])).astype(o_ref.dtype)


def rglru_mixer(z, seq_len, conv_w, conv_b, w_gates, b_gates, lam, h0_f, h0_b):
    m = z.shape[0]
    n_seq_total = m // seq_len
    n_seq = RNN_ROWS // seq_len
    cblk = lambda base: (lambda r, c: (r, base // RNN_COLS + c))
    state_spec = pl.BlockSpec((n_seq, 1, RNN_COLS), lambda r, c: (r, 0, c))
    return pl.pallas_call(
        functools.partial(_rglru_body, seq_len),
        grid=(m // RNN_ROWS, D_MODEL // RNN_COLS),
        in_specs=[
            pl.BlockSpec((RNN_ROWS, RNN_COLS), cblk(COL_XR)),
            pl.BlockSpec((RNN_ROWS, RNN_COLS), cblk(COL_YR)),
            pl.BlockSpec((4, RNN_COLS), lambda r, c: (0, c)),
            pl.BlockSpec((1, RNN_COLS), lambda r, c: (0, c)),
            pl.BlockSpec((RNN_SUB, RNN_BLOCK_DIM, 4 * RNN_BLOCK_DIM), lambda r, c: (c, 0, 0)),
            pl.BlockSpec((RNN_SUB, 1, 4 * RNN_BLOCK_DIM), lambda r, c: (c, 0, 0)),
            pl.BlockSpec((2, RNN_COLS), lambda r, c: (0, c)),
            state_spec, state_spec,
        ],
        out_specs=[
            pl.BlockSpec((RNN_ROWS, RNN_COLS), lambda r, c: (r, c)),
            state_spec, state_spec,
        ],
        out_shape=[
            jax.ShapeDtypeStruct((m, D_MODEL), BF16),
            jax.ShapeDtypeStruct((n_seq_total, 1, D_MODEL), F32),
            jax.ShapeDtypeStruct((n_seq_total, 1, D_MODEL), F32),
        ],
        scratch_shapes=[pltpu.VMEM((RNN_SUB, RNN_ROWS, RNN_BLOCK_DIM), F32) for _ in range(5)],
        compiler_params=_params(("arbitrary", "arbitrary")),
        name="rglru_mixer_t%d" % seq_len,
    )(z, z, conv_w, conv_b.reshape(1, D_MODEL), w_gates, b_gates, lam, h0_f, h0_b)


def _merge_body(a_ref, r_ref, wa_ref, wr_ref, ga_ref, gr_ref, o_ref, wa_bf, wr_bf):
    @pl.when(pl.program_id(1) == 0)
    def _():
        wa_bf[...] = wa_ref[...].astype(BF16)
        wr_bf[...] = wr_ref[...].astype(BF16)

    pa = jnp.dot(a_ref[...], wa_bf[...], preferred_element_type=F32)
    pr = jnp.dot(r_ref[...], wr_bf[...], preferred_element_type=F32)
    o_ref[...] = (_sigmoid(ga_ref[...]) * pa + _sigmoid(gr_ref[...]) * pr).astype(o_ref.dtype)


def gated_merge(attn, rnn, z, w_o_attn, w_o_rnn):
    m = attn.shape[0]
    tm, tn = 1024, 512
    return pl.pallas_call(
        _merge_body,
        grid=(D_MODEL // tn, m // tm),
        in_specs=[
            pl.BlockSpec((tm, Q_COLS), lambda j, i: (i, 0)),
            pl.BlockSpec((tm, D_MODEL), lambda j, i: (i, 0)),
            pl.BlockSpec((Q_COLS, tn), lambda j, i: (0, j)),
            pl.BlockSpec((D_MODEL, tn), lambda j, i: (0, j)),
            pl.BlockSpec((tm, tn), lambda j, i: (i, COL_GA // tn + j)),
            pl.BlockSpec((tm, tn), lambda j, i: (i, COL_GR // tn + j)),
        ],
        out_specs=pl.BlockSpec((tm, tn), lambda j, i: (i, j)),
        out_shape=jax.ShapeDtypeStruct((m, D_MODEL), BF16),
        scratch_shapes=[pltpu.VMEM((Q_COLS, tn), BF16), pltpu.VMEM((D_MODEL, tn), BF16)],
        compiler_params=_params(("arbitrary", "arbitrary")),
        name="gated_merge",
    )(attn, rnn, w_o_attn, w_o_rnn, z, z)


POST_TM = 512
POST_SPLIT = 1
HALF_D = D_MODEL // 2
WORD_ROWS = HALF_D // 128
SUBLANES = 8
POST_CTX_BLOCKS = N_CTX // POST_TM
LAT_BLOCKS_PER_SEQ = LAT_LEN // POST_TM


def _post_group(i):
    return jnp.where(i < POST_CTX_BLOCKS, 0, 1 + (i - POST_CTX_BLOCKS) // LAT_BLOCKS_PER_SEQ)


def _postmix_body(mc_ref, ml_ref, xc_ref, xl_ref, wo_ref, gpm_ref, gt1_ref, gpf_ref, sh2_ref, sc2_ref,
                  wr_ref, br_ref, x1_ref, h2_ref, e_ref, gate_ref, rank_ref, cnt_ref, carry_ref):
    i = pl.program_id(0)
    tm = POST_TM

    @pl.when(i == 0)
    def _():
        carry_ref[...] = jnp.zeros_like(carry_ref)

    is_ctx = i < POST_CTX_BLOCKS
    th = tm // POST_SPLIT
    r_io = lax.broadcasted_iota(jnp.int32, (th, th), 0)
    c_io = lax.broadcasted_iota(jnp.int32, (th, th), 1)
    lower = jnp.where(c_io < r_io, 1.0, 0.0).astype(BF16)
    lane = lax.broadcasted_iota(jnp.int32, (th, N_EXPERTS), 1)
    lane_k = lax.broadcasted_iota(jnp.int32, (th, TOP_K), 1)
    wr = wr_ref[...].astype(BF16)
    carry = carry_ref[...]
    for part in range(POST_SPLIT):
        rows = slice(part * th, (part + 1) * th)
        merged = jnp.where(is_ctx, mc_ref[rows, :], ml_ref[rows, :])
        x = jnp.where(is_ctx, xc_ref[rows, :], xl_ref[rows, :])
        o = jnp.dot(merged, wo_ref[...], preferred_element_type=F32)
        x1 = x + gt1_ref[...] * (o * _rms_scale(o) * gpm_ref[...])
        x1_ref[rows, :] = x1
        h2 = (x1 * _rms_scale(x1) * gpf_ref[...]) * (1.0 + sc2_ref[...]) + sh2_ref[...]
        h2_bf = h2.astype(BF16)
        bits = lax.bitcast_convert_type(h2_bf.astype(F32), jnp.uint32)
        words = (lax.shift_right_logical(bits[:, :HALF_D], jnp.uint32(16))
                 | (bits[:, HALF_D:] & jnp.uint32(0xFFFF0000)))
        for c in range(WORD_ROWS):
            h2_ref[pl.ds(part * th * WORD_ROWS + c, th, stride=WORD_ROWS), :] = words[:, c * 128:(c + 1) * 128]

        logits = jnp.dot(h2_bf, wr, preferred_element_type=F32) + br_ref[...]
        work = logits
        chosen = jnp.zeros((th, N_EXPERTS), F32)
        sels, vals, idxs = [], [], []
        for _ in range(TOP_K):
            mx = jnp.max(work, axis=-1, keepdims=True)
            idx = jnp.min(jnp.where(work == mx, lane, N_EXPERTS), axis=-1, keepdims=True)
            sel = lane == idx
            work = jnp.where(sel, -jnp.inf, work)
            chosen = jnp.where(sel, 1.0, chosen)
            sels.append(sel)
            vals.append(mx)
            idxs.append(idx)
        exps = [jnp.exp(v - vals[0]) for v in vals]
        inv = 1.0 / (exps[0] + exps[1] + exps[2] + exps[3])

        before = jnp.dot(lower, chosen.astype(BF16), preferred_element_type=F32) + carry
        carry = carry + jnp.sum(chosen, axis=0, keepdims=True)

        e_out = jnp.zeros((th, TOP_K), jnp.int32)
        g_out = jnp.zeros((th, TOP_K), F32)
        r_out = jnp.zeros((th, TOP_K), jnp.int32)
        for k in range(TOP_K):
            rk = jnp.sum(jnp.where(sels[k], before, 0.0), axis=-1, keepdims=True).astype(jnp.int32)
            e_out = jnp.where(lane_k == k, idxs[k], e_out)
            g_out = jnp.where(lane_k == k, exps[k] * inv, g_out)
            r_out = jnp.where(lane_k == k, rk, r_out)
        e_ref[rows, :] = e_out
        gate_ref[rows, :] = g_out
        rank_ref[rows, :] = r_out
    carry_ref[...] = carry
    cnt_ref[...] = carry


def post_mix_router(merged_ctx, merged_lat, x_ctx, x_lat, w_out_bf, g_post_mix, gt1, g_pre_ffn, sh2, sc2,
                    w_router, b_router):
    tm = POST_TM
    ctx_map = lambda i: (jnp.minimum(i, POST_CTX_BLOCKS - 1), 0)
    lat_map = lambda i: (jnp.maximum(i - POST_CTX_BLOCKS, 0), 0)
    gmap = lambda i: (_post_group(i), 0, 0)
    row = lambda i: (i, 0)
    const = lambda i: (0, 0)
    vec = pl.BlockSpec((1, D_MODEL), const)
    gvec = pl.BlockSpec((None, 1, D_MODEL), gmap)
    return pl.pallas_call(
        _postmix_body,
        grid=(N_TOK // tm,),
        in_specs=[
            pl.BlockSpec((tm, D_MODEL), ctx_map),
            pl.BlockSpec((tm, D_MODEL), lat_map),
            pl.BlockSpec((tm, D_MODEL), ctx_map),
            pl.BlockSpec((tm, D_MODEL), lat_map),
            pl.BlockSpec((D_MODEL, D_MODEL), const),
            vec, gvec, vec, gvec, gvec,
            pl.BlockSpec((D_MODEL, N_EXPERTS), const),
            pl.BlockSpec((1, N_EXPERTS), const),
        ],
        out_specs=[
            pl.BlockSpec((tm, D_MODEL), row),
            pl.BlockSpec((tm * WORD_ROWS, 128), row),
            pl.BlockSpec((tm, TOP_K), row),
            pl.BlockSpec((tm, TOP_K), row),
            pl.BlockSpec((tm, TOP_K), row),
            pl.BlockSpec((1, N_EXPERTS), const),
        ],
        out_shape=[
            jax.ShapeDtypeStruct((N_TOK, D_MODEL), F32),
            jax.ShapeDtypeStruct((N_TOK * WORD_ROWS, 128), jnp.uint32),
            jax.ShapeDtypeStruct((N_TOK, TOP_K), jnp.int32),
            jax.ShapeDtypeStruct((N_TOK, TOP_K), F32),
            jax.ShapeDtypeStruct((N_TOK, TOP_K), jnp.int32),
            jax.ShapeDtypeStruct((1, N_EXPERTS), F32),
        ],
        scratch_shapes=[pltpu.VMEM((1, N_EXPERTS), F32)],
        compiler_params=_params(("arbitrary",)),
        name="post_mix_router",
    )(merged_ctx, merged_lat, x_ctx, x_lat, w_out_bf, g_post_mix.reshape(1, D_MODEL), gt1,
      g_pre_ffn.reshape(1, D_MODEL), sh2, sc2, w_router, b_router.reshape(1, N_EXPERTS))


GATHER_SHIFT = 4
GATHER_UNROLL = 1 << GATHER_SHIFT
GATHER_PRIORITY = 1


def _unpack_tile(xbuf_ref, slot, i):
    base = pl.multiple_of(i * (ROW_TILE * WORD_ROWS), ROW_TILE * WORD_ROWS)
    lo, hi = [], []
    for c in range(WORD_ROWS):
        words = xbuf_ref[slot, pl.ds(base + c, ROW_TILE, stride=WORD_ROWS), :]
        lo.append(lax.bitcast_convert_type(lax.shift_left(words, jnp.uint32(16)), F32).astype(BF16))
        hi.append(lax.bitcast_convert_type(words & jnp.uint32(0xFFFF0000), F32).astype(BF16))
    return jnp.concatenate(lo + hi, axis=1)


def _for_tiles(n_tiles, body):
    def one(i, _):
        body(i)
        return 0

    lax.fori_loop(0, n_tiles, one, 0)


def _moe_body(exp_ref, row_ref, nsub_ref, nzero_ref, npass_ref, rows_ref, tok_ref, h_ref, wg_ref, wl_ref, wd_ref, bg_ref,
              bl_ref, bd_ref, y_ref, xbuf_ref, act_ref, stage_ref, idx_ref, pend_ref,
              xsem, isem, ysem):
    s = pl.program_id(0)
    j = pl.program_id(1)
    n_pass = npass_ref[0]
    n_sub = nsub_ref[s]
    row_start = row_ref[s]

    def idx_copy(p):
        tile0 = pl.multiple_of(row_ref[p], ROW_TILE) // ROW_TILE
        return pltpu.make_async_copy(tok_ref.at[pl.ds(tile0, SUPER_TILES)], idx_ref.at[p % 2],
                                     isem.at[p % 2])

    def row_groups(p):
        return lax.shift_right_logical(rows_ref[p] + (GATHER_UNROLL - 1), GATHER_SHIFT)

    def gather_rows(p):
        slot = p % 2

        def issue(grp, _):
            first = grp * GATHER_UNROLL
            tile = lax.shift_right_logical(first, 8)
            col = jnp.bitwise_and(first, ROW_TILE - 1)
            for g in range(GATHER_UNROLL):
                t = idx_ref[slot, tile, 0, col + g]
                src = h_ref.at[t]
                dst = xbuf_ref.at[slot, pl.ds(pl.multiple_of((first + g) * WORD_ROWS, WORD_ROWS), WORD_ROWS), :]
                pltpu.make_async_copy(src, dst, xsem.at[slot]).start(priority=GATHER_PRIORITY)
            return 0

        lax.fori_loop(0, row_groups(p), issue, 0)

    def wait_rows(p):
        n_grp = row_groups(p)

        @pl.when(n_grp > 0)
        def _():
            n = pl.multiple_of(n_grp * (GATHER_UNROLL * WORD_ROWS), GATHER_UNROLL * WORD_ROWS)
            window = xbuf_ref.at[p % 2, pl.ds(0, n), :]
            pltpu.make_async_copy(window, window, xsem.at[p % 2]).wait()

    @pl.when(jnp.logical_and(s == 0, j == 0))
    def _():
        pend_ref[0] = 0
        pend_ref[1] = 0

    def drain_stage(slot):
        @pl.when(pend_ref[slot] == 1)
        def _():
            pltpu.make_async_copy(stage_ref.at[slot], stage_ref.at[slot], ysem.at[slot]).wait()
            pend_ref[slot] = 0

    @pl.when(jnp.logical_and(s == 0, j == 0))
    def _():
        xbuf_ref[...] = jnp.zeros(xbuf_ref.shape, xbuf_ref.dtype)
        idx_copy(0).start()
        idx_copy(0).wait()
        gather_rows(0)
        idx_copy(1).start()

    @pl.when(j == 0)
    def _():
        wait_rows(s)

    @pl.when(jnp.logical_and(j == 0, s + 1 < n_pass))
    def _():
        idx_copy(s + 1).wait()
        gather_rows(s + 1)

    @pl.when(jnp.logical_and(j == 0, s + 2 < n_pass))
    def _():
        idx_copy(s + 2).start()

    @pl.when(jnp.logical_and(j < N_FF_CHUNKS, n_sub > 0))
    def _():
        bg = bg_ref[...]
        bl = bl_ref[...]

        def up_tile(i):
            rows = pl.ds(pl.multiple_of(i * ROW_TILE, ROW_TILE), ROW_TILE)
            xt = _unpack_tile(xbuf_ref, s % 2, i)
            glu = jnp.minimum(jnp.dot(xt, wg_ref[...].astype(BF16), preferred_element_type=F32) + bg,
                              SWIGLU_LIMIT)
            lin = jnp.clip(jnp.dot(xt, wl_ref[...].astype(BF16), preferred_element_type=F32) + bl,
                           -SWIGLU_LIMIT, SWIGLU_LIMIT)
            act = glu * _sigmoid(SWIGLU_ALPHA * glu) * (lin + 1.0)
            act_ref[j, rows, :] = act.astype(BF16)

        _for_tiles(n_sub, up_tile)

    for cc in range(N_FF_CHUNKS):
        @pl.when(jnp.logical_and(j == N_FF_CHUNKS + cc, n_sub > 0))
        def _(cc=cc):
            bd = bd_ref[...]

            def out_copy(i, slot):
                dst = y_ref.at[pl.ds(pl.multiple_of(row_start + i * ROW_TILE, ROW_TILE), ROW_TILE),
                               cc * FF_CHUNK:(cc + 1) * FF_CHUNK]
                return pltpu.make_async_copy(stage_ref.at[slot], dst, ysem.at[slot])

            def down_tile(i):
                rows = pl.ds(pl.multiple_of(i * ROW_TILE, ROW_TILE), ROW_TILE)
                slot = i % 2
                drain_stage(slot)
                acc = bd
                for c in range(N_FF_CHUNKS):
                    acc = acc + jnp.dot(act_ref[c, rows, :],
                                        wd_ref[c * FF_CHUNK:(c + 1) * FF_CHUNK, :].astype(BF16),
                                        preferred_element_type=F32)
                stage_ref[slot] = acc
                out_copy(i, slot).start()
                pend_ref[slot] = 1

            _for_tiles(n_sub, down_tile)

    n_zero = nzero_ref[s]

    @pl.when(jnp.logical_and(j == 0, n_zero > 0))
    def _():
        drain_stage(0)
        stage_ref[0] = jnp.zeros((ROW_TILE, FF_CHUNK), F32)

        def zero_copy(i, cc):
            dst = y_ref.at[pl.ds(pl.multiple_of(row_start + i * ROW_TILE, ROW_TILE), ROW_TILE),
                           cc * FF_CHUNK:(cc + 1) * FF_CHUNK]
            return pltpu.make_async_copy(stage_ref.at[0], dst, ysem.at[0])

        def issue(i, _):
            for cc in range(N_FF_CHUNKS):
                zero_copy(i, cc).start()
            return 0

        def drain(i, _):
            for cc in range(N_FF_CHUNKS):
                zero_copy(i, cc).wait()
            return 0

        lax.fori_loop(0, n_zero, issue, 0)
        lax.fori_loop(0, n_zero, drain, 0)

    @pl.when(jnp.logical_and(s == n_pass - 1, j == 2 * N_FF_CHUNKS - 1))
    def _():
        drain_stage(0)
        drain_stage(1)


def expert_mlp(h_packed, tok_sorted, sched, w_gate_up, b_gate_up, w_down, b_down):
    exp_of, row_of, nsub_of, nzero_of, n_pass, rows_of = sched
    last = N_FF_CHUNKS - 1
    up_of = lambda s, j, n: jnp.where(n[s] > 0, jnp.minimum(j, last), last)
    up_chunk = lambda s, j, e, r, n, z, p, c: (e[s], 0, up_of(s, j, n))
    lin_chunk = lambda s, j, e, r, n, z, p, c: (e[s], 0, N_FF_CHUNKS + up_of(s, j, n))

    def down_chunk(s, j, e, r, n, z, p, c):
        in_down = jnp.logical_and(n[s] > 0, j >= N_FF_CHUNKS)
        expert = jnp.where(in_down, e[s], e[jnp.maximum(s - 1, 0)])
        return expert, 0, jnp.where(in_down, j - N_FF_CHUNKS, last)
    grid_spec = pltpu.PrefetchScalarGridSpec(
        num_scalar_prefetch=6,
        grid=(n_pass[0], 2 * N_FF_CHUNKS),
        in_specs=[
            pl.BlockSpec(memory_space=pl.ANY),
            pl.BlockSpec(memory_space=pl.ANY),
            pl.BlockSpec((None, D_MODEL, FF_CHUNK), up_chunk),
            pl.BlockSpec((None, D_MODEL, FF_CHUNK), lin_chunk),
            pl.BlockSpec((None, D_FF, FF_CHUNK), down_chunk),
            pl.BlockSpec((None, 1, FF_CHUNK), up_chunk),
            pl.BlockSpec((None, 1, FF_CHUNK), lin_chunk),
            pl.BlockSpec((None, 1, FF_CHUNK), down_chunk),
        ],
        out_specs=pl.BlockSpec(memory_space=pl.ANY),
        scratch_shapes=[
            pltpu.VMEM((2, SUPER_ROWS * WORD_ROWS, 128), jnp.uint32),
            pltpu.VMEM((N_FF_CHUNKS, SUPER_ROWS, FF_CHUNK), BF16),
            pltpu.VMEM((2, ROW_TILE, FF_CHUNK), F32),
            pltpu.SMEM((2, SUPER_TILES, 1, ROW_TILE), jnp.int32),
            pltpu.SMEM((2,), jnp.int32),
            pltpu.SemaphoreType.DMA((2,)),
            pltpu.SemaphoreType.DMA((2,)),
            pltpu.SemaphoreType.DMA((2,)),
        ],
    )
    tok_tiles = jnp.concatenate([tok_sorted.reshape(N_ROW_TILES, 1, ROW_TILE),
                                 jnp.zeros((SUPER_TILES, 1, ROW_TILE), jnp.int32)], axis=0)
    h_packed = h_packed.reshape(N_TOK, WORD_ROWS, 128)
    return pl.pallas_call(
        _moe_body,
        grid_spec=grid_spec,
        out_shape=jax.ShapeDtypeStruct((N_ROWS, D_MODEL), F32),
        compiler_params=_params(("arbitrary", "arbitrary"), vmem=EXPERT_VMEM_LIMIT),
        name="expert_mlp",
    )(exp_of, row_of, nsub_of, nzero_of, n_pass, rows_of, tok_tiles, h_packed, w_gate_up, w_gate_up, w_down,
      b_gate_up.reshape(N_EXPERTS, 1, 2 * D_FF), b_gate_up.reshape(N_EXPERTS, 1, 2 * D_FF),
      b_down.reshape(N_EXPERTS, 1, D_MODEL))


COMB_TB = 256


def _combine_start(y_ref, ybuf_ref, pos_ref, sem):
    def issue(t, _):
        for k in range(TOP_K):
            p = pos_ref[0, 0, t * TOP_K + k]
            pltpu.make_async_copy(y_ref.at[pl.ds(p, 1), :], ybuf_ref.at[k, pl.ds(t, 1), :], sem).start()
        return 0

    lax.fori_loop(0, COMB_TB, issue, 0, unroll=4)


def _combine_body(n, pos_ref, pos_next_ref, y_ref, gate_ref, x1_ref, gt2_ref, g_ref, o_ref, ybuf_ref, sem_ref):
    i = pl.program_id(0)
    slot = i % 2

    @pl.when(i == 0)
    def _():
        _combine_start(y_ref, ybuf_ref.at[0], pos_ref, sem_ref.at[0])

    @pl.when(i + 1 < n)
    def _():
        _combine_start(y_ref, ybuf_ref.at[1 - slot], pos_next_ref, sem_ref.at[1 - slot])

    for k in range(TOP_K):
        pltpu.make_async_copy(y_ref.at[pl.ds(0, COMB_TB), :], ybuf_ref.at[slot, k], sem_ref.at[slot]).wait()
    gates = gate_ref[...]
    ffn = gates[:, 0:1] * ybuf_ref[slot, 0]
    for k in range(1, TOP_K):
        ffn = ffn + gates[:, k:k + 1] * ybuf_ref[slot, k]
    o_ref[...] = x1_ref[...] + gt2_ref[...] * (ffn * _rms_scale(ffn) * g_ref[...])


def combine_residual(y_sorted, pos, gates, x1, gt2, g_post_ffn, row_offset, n_rows, group_of_block):
    tb = COMB_TB
    nblk = n_rows // tb
    off = row_offset // tb
    pos3 = pos.reshape(N_TOK // tb, 1, tb * TOP_K)
    smem_blk = lambda f: pl.BlockSpec((1, 1, tb * TOP_K), f, memory_space=pltpu.SMEM)
    return pl.pallas_call(
        functools.partial(_combine_body, nblk),
        grid=(nblk,),
        in_specs=[
            smem_blk(lambda i: (off + i, 0, 0)),
            smem_blk(lambda i: (off + jnp.minimum(i + 1, nblk - 1), 0, 0)),
            pl.BlockSpec(memory_space=pl.ANY),
            pl.BlockSpec((tb, TOP_K), lambda i: (off + i, 0)),
            pl.BlockSpec((tb, D_MODEL), lambda i: (off + i, 0)),
            pl.BlockSpec((None, 1, D_MODEL), lambda i: (group_of_block(i), 0, 0)),
            pl.BlockSpec((1, D_MODEL), lambda i: (0, 0)),
        ],
        out_specs=pl.BlockSpec((tb, D_MODEL), lambda i: (i, 0)),
        out_shape=jax.ShapeDtypeStruct((n_rows, D_MODEL), F32),
        scratch_shapes=[pltpu.VMEM((2, TOP_K, tb, D_MODEL), F32), pltpu.SemaphoreType.DMA((2,))],
        compiler_params=_params(("arbitrary",)),
        name="combine_residual",
    )(pos3, pos3, y_sorted, gates, x1, gt2, g_post_ffn.reshape(1, D_MODEL))


INV_CHUNK = 4096


def _row_tokens_body(pos_ref, zeros_ref, o_ref, sem):
    i = pl.program_id(0)

    @pl.when(i == 0)
    def _():
        cp = pltpu.make_async_copy(zeros_ref, o_ref, sem)
        cp.start()
        cp.wait()

    first_token = i * (INV_CHUNK // TOP_K)

    def put(t, _):
        for k in range(TOP_K):
            o_ref[pos_ref[t * TOP_K + k]] = first_token + t
        return 0

    lax.fori_loop(0, INV_CHUNK // TOP_K, put, 0, unroll=4)


def row_tokens(pos):
    return pl.pallas_call(
        _row_tokens_body,
        grid=(N_ASSIGN // INV_CHUNK,),
        in_specs=[
            pl.BlockSpec((INV_CHUNK,), lambda i: (i,), memory_space=pltpu.SMEM),
            pl.BlockSpec(memory_space=pl.ANY),
        ],
        out_specs=pl.BlockSpec(memory_space=pltpu.SMEM),
        out_shape=jax.ShapeDtypeStruct((N_ROWS,), jnp.int32),
        scratch_shapes=[pltpu.SemaphoreType.DMA(())],
        compiler_params=_params(("arbitrary",)),
        name="row_tokens",
    )(pos.reshape(N_ASSIGN), jnp.zeros((N_ROWS,), jnp.int32))


def _routing_tables(e_idx, rank, counts_f):
    counts = counts_f.reshape(N_EXPERTS).astype(jnp.int32)
    n_tiles = (counts + ROW_TILE - 1) // ROW_TILE
    padded = n_tiles * ROW_TILE
    pad_end = jnp.cumsum(padded)
    pad_start = pad_end - padded
    pos = (pad_start[e_idx] + rank).astype(jnp.int32)
    tok_sorted = row_tokens(pos)
    n_pass = (n_tiles + SUPER_TILES - 1) // SUPER_TILES
    pass_end = jnp.cumsum(n_pass)
    total = pass_end[-1]
    s = jnp.arange(N_SUPER, dtype=jnp.int32)
    s_eff = jnp.minimum(s, total - 1)
    e_of = jnp.minimum(jnp.searchsorted(pass_end, s_eff, side="right"), N_EXPERTS - 1).astype(jnp.int32)
    local = s_eff - (pass_end[e_of] - n_pass[e_of])
    row_of = pad_start[e_of] + local * SUPER_ROWS
    nsub = jnp.minimum(SUPER_TILES, n_tiles[e_of] - local * SUPER_TILES)
    nsub = jnp.where(s < total, nsub, 0).astype(jnp.int32)
    zero_row = pad_end[-1] + (s - total) * SUPER_ROWS
    nzero = jnp.clip((N_ROWS - zero_row) // ROW_TILE, 0, SUPER_TILES)
    nzero = jnp.where(s >= total, nzero, 0).astype(jnp.int32)
    row_of = jnp.where(s < total, row_of, jnp.minimum(zero_row, N_ROWS - ROW_TILE)).astype(jnp.int32)
    tail_tiles = (N_ROWS - pad_end[-1]) // ROW_TILE
    n_pass = jnp.minimum(total + jnp.maximum((tail_tiles + SUPER_TILES - 1) // SUPER_TILES, 1), N_SUPER)
    n_pass = n_pass.astype(jnp.int32).reshape(1)
    rows_of = jnp.clip(counts[e_of] - local * SUPER_ROWS, 0, SUPER_ROWS)
    rows_of = jnp.where(s < total, rows_of, 0).astype(jnp.int32)
    return pos.astype(jnp.int32), tok_sorted, (e_of, row_of, nsub, nzero, n_pass, rows_of)


def kernel(x_prompt, x_sample, cache_k, cache_v, state_rnn_fwd, state_rnn_bwd, c, c_ctx, w_mod, b_mod, g_pre_mix, w_in, g_q_norm, g_k_norm, conv_w, conv_b, rg_w_a, rg_b_a, rg_w_x, rg_b_x, rg_lambda, w_o_attn, w_o_rnn, w_out, g_post_mix, g_pre_ffn, w_router, b_router, w_gate_up, b_gate_up, w_down, b_down, g_post_ffn):
    l = 0
    x_ctx = x_prompt.reshape(N_CTX, D_MODEL)
    x_lat = x_sample.reshape(N_LAT, D_MODEL)

    cond8 = jnp.concatenate([c_ctx[None, :], c, jnp.zeros((8 - 1 - N_LAT_SEQ, D_MODEL), F32)], axis=0)
    mod = modulation(cond8, w_mod[l], b_mod[l])[:1 + N_LAT_SEQ].reshape(1 + N_LAT_SEQ, 6, 1, D_MODEL)
    sh1, sc1, gt1, sh2, sc2, gt2 = [mod[:, i] for i in range(6)]

    ctx_group = lambda i: 0
    lat_group_1024 = lambda i: 1 + i
    h_ctx = prenorm_modulate(x_ctx, g_pre_mix[l], sh1, sc1, ctx_group, 1024)
    h_lat = prenorm_modulate(x_lat, g_pre_mix[l], sh1, sc1, lat_group_1024, 1024)
    z_ctx = in_projection(h_ctx, w_in[l])
    z_lat = in_projection(h_lat, w_in[l])

    attn_ctx, k_new, v_new = attention_ctx(z_ctx, g_q_norm[l], g_k_norm[l])
    attn_lat = attention_lat(z_lat, cache_k[:, l].reshape(N_LAT_SEQ, PAST_LEN, KV_COLS),
                             cache_v[:, l].reshape(N_LAT_SEQ, PAST_LEN, KV_COLS),
                             _rope_tables(), g_q_norm[l], g_k_norm[l])

    def per_block(w):
        return w.reshape(2, RNN_BLOCKS, 1, RNN_BLOCK_DIM)

    w_gates = jnp.concatenate([rg_w_a[l, 0], rg_w_x[l, 0], rg_w_a[l, 1], rg_w_x[l, 1]], axis=-1).astype(BF16)
    ba, bx = per_block(rg_b_a[l]), per_block(rg_b_x[l])
    b_gates = jnp.concatenate([ba[0], bx[0], ba[1], bx[1]], axis=-1)
    zeros_state = jnp.zeros((N_CTX_SEQ, 1, D_MODEL), F32)
    rnn_ctx, hf_ctx, hb_ctx = rglru_mixer(z_ctx, CTX_LEN, conv_w[l], conv_b[l], w_gates, b_gates,
                                          rg_lambda[l], zeros_state, zeros_state)
    rnn_lat, _, _ = rglru_mixer(z_lat, LAT_LEN, conv_w[l], conv_b[l], w_gates, b_gates, rg_lambda[l],
                                state_rnn_fwd[:, l].reshape(N_LAT_SEQ, 1, D_MODEL),
                                state_rnn_bwd[:, l].reshape(N_LAT_SEQ, 1, D_MODEL))

    merged_ctx = gated_merge(attn_ctx, rnn_ctx, z_ctx, w_o_attn[l], w_o_rnn[l])
    merged_lat = gated_merge(attn_lat, rnn_lat, z_lat, w_o_attn[l], w_o_rnn[l])

    x1, h2, e_idx, gates, rank, counts = post_mix_router(
        merged_ctx, merged_lat, x_ctx, x_lat, w_out[l].astype(BF16), g_post_mix[l], gt1, g_pre_ffn[l],
        sh2, sc2, w_router[l], b_router[l])

    pos, tok_sorted, sched = _routing_tables(e_idx, rank, counts)
    y_sorted = expert_mlp(h2, tok_sorted, sched, w_gate_up[l], b_gate_up[l], w_down[l], b_down[l])

    y_ctx = combine_residual(y_sorted, pos, gates, x1, gt2, g_post_ffn[l], 0, N_CTX, ctx_group)
    y_lat = combine_residual(y_sorted, pos, gates, x1, gt2, g_post_ffn[l], N_CTX, N_LAT,
                             lambda i: 1 + i // (LAT_LEN // COMB_TB))

    return (y_ctx.reshape(N_CTX_SEQ, CTX_LEN, D_MODEL),
            y_lat.reshape(N_LAT_SEQ, LAT_LEN, D_MODEL),
            k_new.reshape(N_CTX_SEQ, 1, CTX_LEN, N_KV_HEADS, HEAD_DIM),
            v_new.reshape(N_CTX_SEQ, 1, CTX_LEN, N_KV_HEADS, HEAD_DIM),
            hf_ctx,
            hb_ctx)
```
